```python
import numpy as np
import jax
import jax.numpy as jnp
from jax import lax

D_MODEL = 1024
BATCH = 16
SEQ = 256
DEPTH = 4
DEC_BATCH = 4
DEC_SEQ = 2048
PAST_LEN = 512

GRID_W = 64
EPS = 1e-6
ROPE_BASE = 10000.0
NEG_INF = -1e30
Q_BLOCK = 128
HD_A = 64
N_HEADS_A = 8
N_KV_A = 2
GROUP_A = N_HEADS_A // N_KV_A
WINDOW = 128
N_HEADS_B = 8
NOPE_B = 64
ROPE_B = 32
V_B = 64
Q_RANK = 512
KV_RANK = 256
LRU_W = 512
LRU_HEADS = 8
LRU_BLOCK = LRU_W // LRU_HEADS
CONV_W = 4
LRU_C = 8.0
N_BRANCH = 3
IN_SIZES = (N_HEADS_A * HD_A, N_KV_A * HD_A, N_KV_A * HD_A, Q_RANK, KV_RANK, ROPE_B, LRU_W, LRU_W, N_BRANCH * D_MODEL)
IN_COLS = sum(IN_SIZES)
N_EXPERTS = 32
TOP_K = 4
D_FF = D_MODEL
SWIGLU_LIMIT = 7.0
SWIGLU_ALPHA = 1.702
MOE_BLOCK = 128

kernel_name = "hybrid_diffusion_prefix_step"


def rms_norm(x, g):
    xf = x.astype(jnp.float32)
    y = xf * lax.rsqrt(jnp.mean(xf * xf, axis=-1, keepdims=True) + EPS)
    return (y * g.astype(jnp.float32)).astype(x.dtype)


def modulate(h, shift, scale):
    return h * (1 + scale[:, None, :]) + shift[:, None, :]


def adaln(cond, w, b):
    return jnp.split(jax.nn.silu(cond) @ w + b, 6, axis=-1)


def axial_angles(n_tokens, d_rot):
    rows = n_tokens // GRID_W
    row_ids = jnp.repeat(jnp.arange(rows, dtype=jnp.float32), GRID_W)
    col_ids = jnp.tile(jnp.arange(GRID_W, dtype=jnp.float32), rows)
    d_axis = d_rot // 2
    inv_freq = ROPE_BASE ** (-jnp.arange(0, d_axis, 2, dtype=jnp.float32) / d_axis)
    return row_ids[:, None] * inv_freq, col_ids[:, None] * inv_freq


def rotate_axis(x, ang):
    cos = jnp.cos(ang)[:, None, :]
    sin = jnp.sin(ang)[:, None, :]
    x1, x2 = jnp.split(x.astype(jnp.float32), 2, axis=-1)
    return jnp.concatenate([x1 * cos - x2 * sin, x2 * cos + x1 * sin], axis=-1).astype(x.dtype)


def rope_2d(x, ang):
    ang_row, ang_col = ang
    x_row, x_col = jnp.split(x, 2, axis=-1)
    return jnp.concatenate([rotate_axis(x_row, ang_row), rotate_axis(x_col, ang_col)], axis=-1)


def project_in(u, w):
    cuts = [int(i) for i in np.cumsum(IN_SIZES)[:-1]]
    return jnp.split(u @ w, cuts, axis=-1)


def blocked_attention(q, k, v, sink):
    B, S, Hk, G, dk = q.shape
    nb = S // Q_BLOCK
    scale = dk ** -0.5
    qb = q.reshape(B, nb, Q_BLOCK, Hk, G, dk).swapaxes(0, 1)

    def block(qx):
        s = jnp.einsum('bqkgd,bskd->bkgqs', qx, k).astype(jnp.float32) * scale
        if sink is not None:
            sink_s = jnp.broadcast_to(sink.astype(jnp.float32)[None, :, :, None, None], (B, Hk, G, Q_BLOCK, 1))
            p = jax.nn.softmax(jnp.concatenate([s, sink_s], axis=-1), axis=-1)[..., :-1]
        else:
            p = jax.nn.softmax(s, axis=-1)
        return jnp.einsum('bkgqs,bskd->bqkgd', p.astype(v.dtype), v)

    out = lax.map(block, qb)
    return out.swapaxes(0, 1).reshape(B, S, Hk * G * v.shape[-1])


def window_attention(q, k, v, k_ctx, v_ctx, sink):
    B, N, Hk, G, hd = q.shape
    L = k_ctx.shape[1]
    nb = N // WINDOW
    pad = ((0, 0), (WINDOW, WINDOW), (0, 0), (0, 0))
    kp = jnp.pad(k, pad).reshape(B, nb + 2, WINDOW, Hk, hd)
    vp = jnp.pad(v, pad).reshape(B, nb + 2, WINDOW, Hk, hd)
    kb = jnp.concatenate([kp[:, :-2], kp[:, 1:-1], kp[:, 2:]], axis=2)
    vb = jnp.concatenate([vp[:, :-2], vp[:, 1:-1], vp[:, 2:]], axis=2)
    qi = jnp.arange(WINDOW)[:, None]
    kj = jnp.arange(3 * WINDOW)[None, :]
    key_pos = jnp.arange(nb)[:, None, None] * WINDOW - WINDOW + kj[None]
    mask = (jnp.abs(qi - kj + WINDOW) <= WINDOW)[None] & (key_pos >= 0) & (key_pos < N)
    qb = q.reshape(B, nb, WINDOW, Hk, G, hd)
    scale = hd ** -0.5
    n_loc = 3 * WINDOW
    sink_s = jnp.broadcast_to(sink.astype(jnp.float32)[None, :, :, None, None], (B, Hk, G, WINDOW, 1))

    def block(args):
        qx, kx, vx, m = args
        s_loc = jnp.einsum('bqkgd,bskd->bkgqs', qx, kx).astype(jnp.float32) * scale
        s_loc = jnp.where(m, s_loc, NEG_INF)
        s_ctx = jnp.einsum('bqkgd,bskd->bkgqs', qx, k_ctx).astype(jnp.float32) * scale
        p = jax.nn.softmax(jnp.concatenate([s_loc, s_ctx, sink_s], axis=-1), axis=-1).astype(vx.dtype)
        return (jnp.einsum('bkgqs,bskd->bqkgd', p[..., :n_loc], vx)
                + jnp.einsum('bkgqs,bskd->bqkgd', p[..., n_loc:n_loc + L], v_ctx))

    out = lax.map(block, (qb.swapaxes(0, 1), kb.swapaxes(0, 1), vb.swapaxes(0, 1), mask))
    return out.swapaxes(0, 1).reshape(B, N, Hk * G * hd)


def mla_queries(cq, lp):
    B, n, _ = cq.shape
    return (rms_norm(cq, lp['g_q']) @ lp['w_uq']).reshape(B, n, N_HEADS_B, NOPE_B + ROPE_B)


def mla_keys_values(ckv_n, k_rope, lp):
    B, n, _ = ckv_n.shape
    kv = (ckv_n @ lp['w_ukv']).reshape(B, n, N_HEADS_B, NOPE_B + V_B)
    kr = jnp.broadcast_to(k_rope[:, :, None, :], (B, n, N_HEADS_B, ROPE_B)).astype(kv.dtype)
    return jnp.concatenate([kv[..., :NOPE_B], kr], axis=-1), kv[..., NOPE_B:]


def dwconv(x, w, b):
    y = lax.conv_general_dilated(x, w[:, None, :].astype(x.dtype), (1,),
                                 [(CONV_W // 2, CONV_W - 1 - CONV_W // 2)],
                                 dimension_numbers=('NWC', 'WIO', 'NWC'),
                                 feature_group_count=x.shape[-1])
    return y + b


def lru_coeffs(xc, lp, d):
    B, n, _ = xc.shape
    xf = xc.astype(jnp.float32)
    xb = xf.reshape(B, n, LRU_HEADS, LRU_BLOCK)
    r = jax.nn.sigmoid(jnp.einsum('bnhi,hij->bnhj', xb, lp['lru_wa'][d]).reshape(B, n, LRU_W) + lp['lru_ba'][d])
    i = jax.nn.sigmoid(jnp.einsum('bnhi,hij->bnhj', xb, lp['lru_wx'][d]).reshape(B, n, LRU_W) + lp['lru_bx'][d])
    log_a = -LRU_C * r * jax.nn.softplus(-lp['lru_lam'][d].astype(jnp.float32))
    a = jnp.exp(log_a)
    b = jnp.sqrt(-jnp.expm1(2.0 * log_a)) * (i * xf)
    return a, b


def linear_scan(a, b, h0, reverse):
    def combine(e1, e2):
        a1, b1 = e1
        a2, b2 = e2
        return a1 * a2, a2 * b1 + b2
    a_cum, b_cum = lax.associative_scan(combine, (a, b), reverse=reverse, axis=1)
    return a_cum * h0[:, None, :] + b_cum


def recurrent_branch(xc, yc, h0, lp):
    xconv = dwconv(xc, lp['conv_w'], lp['conv_b'])
    a_f, b_f = lru_coeffs(xconv, lp, 0)
    a_b, b_b = lru_coeffs(xconv, lp, 1)
    h0 = h0.astype(jnp.float32)
    h_f = linear_scan(a_f, b_f, h0[:, 0], False)
    h_b = linear_scan(a_b, b_b, h0[:, 1], True)
    o = ((h_f + h_b) * jax.nn.gelu(yc.astype(jnp.float32))).astype(xc.dtype)
    return o, h_f, h_b


def merge_branches(o_a, o_b, o_c, gates, lp):
    B, n, _ = gates.shape
    g = jax.nn.sigmoid(gates.astype(jnp.float32)).astype(o_a.dtype).reshape(B, n, N_BRANCH, D_MODEL)
    m = (g[:, :, 0] * (o_a @ lp['w_pa']) + g[:, :, 1] * (o_b @ lp['w_pb'])
         + g[:, :, 2] * (o_c @ lp['w_pc']))
    return m @ lp['w_out']


def moe(h, lp):
    T, Dm = h.shape
    logits = (h @ lp['w_router']).astype(jnp.float32) + lp['b_router'].astype(jnp.float32)
    top_v, top_e = lax.top_k(logits, TOP_K)
    gate = jax.nn.softmax(top_v, axis=-1)
    n_slot = T * TOP_K
    slot_e = top_e.reshape(-1)
    slot_tok = jnp.arange(n_slot, dtype=jnp.int32) // TOP_K
    order = jnp.argsort(slot_e)
    e_sorted = slot_e[order]
    tok_sorted = slot_tok[order]
    counts = jnp.bincount(slot_e, length=N_EXPERTS)
    padded = (counts + MOE_BLOCK - 1) // MOE_BLOCK * MOE_BLOCK
    pad_end = jnp.cumsum(padded)
    pad_start = pad_end - padded
    grp_start = jnp.cumsum(counts) - counts
    dest = pad_start[e_sorted] + jnp.arange(n_slot, dtype=jnp.int32) - grp_start[e_sorted]
    n_blocks = -(-n_slot // MOE_BLOCK) + N_EXPERTS
    rows_tok = jnp.zeros((n_blocks * MOE_BLOCK,), jnp.int32).at[dest].set(tok_sorted)
    block_e = jnp.minimum(jnp.searchsorted(pad_end, jnp.arange(n_blocks) * MOE_BLOCK, side='right'), N_EXPERTS - 1)
    xb = h[rows_tok].reshape(n_blocks, MOE_BLOCK, Dm)

    def expert_block(args):
        xblk, e = args
        gu = xblk @ lp['w_gu'][e] + lp['b_gu'][e]
        glu, lin = jnp.split(gu, 2, axis=-1)
        glu = jnp.minimum(glu, SWIGLU_LIMIT)
        lin = jnp.clip(lin, -SWIGLU_LIMIT, SWIGLU_LIMIT)
        act = glu * jax.nn.sigmoid(SWIGLU_ALPHA * glu) * (lin + 1)
        return act @ lp['w_dn'][e] + lp['b_dn'][e]

    yb = lax.map(expert_block, (xb, block_e)).reshape(n_blocks * MOE_BLOCK, Dm)
    y_slot = yb[dest] * gate.reshape(-1)[order][:, None]
    return jax.ops.segment_sum(y_slot, tok_sorted, num_segments=T).astype(h.dtype)


def ffn_sublayer(x, shift, scale, gate, lp):
    B, n, Dm = x.shape
    h = modulate(rms_norm(x, lp['ln2']), shift, scale).reshape(B * n, Dm)
    return x + gate[:, None, :] * moe(h, lp).reshape(B, n, Dm)


def context_layer(x, ada, lp):
    shift1, scale1, gate1, shift2, scale2, gate2 = ada
    B, S, _ = x.shape
    u = modulate(rms_norm(x, lp['ln1']), shift1, scale1)
    qa, ka, va, cq, ckv, kr, xc, yc, gates = project_in(u, lp['w_in'])
    ka = ka.reshape(B, S, N_KV_A, HD_A)
    va = va.reshape(B, S, N_KV_A, HD_A)
    o_a = blocked_attention(qa.reshape(B, S, N_KV_A, GROUP_A, HD_A), ka, va,
                            lp['sink'].reshape(N_KV_A, GROUP_A))
    ckv_n = rms_norm(ckv, lp['g_kv'])
    k_b, v_b = mla_keys_values(ckv_n, kr, lp)
    o_b = blocked_attention(mla_queries(cq, lp)[:, :, :, None, :], k_b, v_b, None)
    o_c, h_f, h_b = recurrent_branch(xc, yc, jnp.zeros((B, 2, LRU_W), jnp.float32), lp)
    x = x + gate1[:, None, :] * merge_branches(o_a, o_b, o_c, gates, lp)
    x = ffn_sublayer(x, shift2, scale2, gate2, lp)
    lru_state = jnp.stack([h_f[:, -1], h_b[:, 0]], axis=1).astype(x.dtype)
    return x, ka, va, ckv_n, kr, lru_state


def latent_layer(x, ada, lp, k_ctx, v_ctx, ckv_ctx, kr_ctx, h_ctx, ang_a, ang_b):
    shift1, scale1, gate1, shift2, scale2, gate2 = ada
    B, N, _ = x.shape
    u = modulate(rms_norm(x, lp['ln1']), shift1, scale1)
    qa, ka, va, cq, ckv, kr, xc, yc, gates = project_in(u, lp['w_in'])
    qa = rope_2d(qa.reshape(B, N, N_HEADS_A, HD_A), ang_a).reshape(B, N, N_KV_A, GROUP_A, HD_A)
    ka = rope_2d(ka.reshape(B, N, N_KV_A, HD_A), ang_a)
    o_a = window_attention(qa, ka, va.reshape(B, N, N_KV_A, HD_A), k_ctx, v_ctx,
                           lp['sink'].reshape(N_KV_A, GROUP_A))
    q_b = mla_queries(cq, lp)
    q_b = jnp.concatenate([q_b[..., :NOPE_B], rope_2d(q_b[..., NOPE_B:], ang_b)], axis=-1)
    k_lat, v_lat = mla_keys_values(rms_norm(ckv, lp['g_kv']), rope_2d(kr[:, :, None, :], ang_b)[:, :, 0], lp)
    k_c, v_c = mla_keys_values(ckv_ctx, kr_ctx, lp)
    o_b = blocked_attention(q_b[:, :, :, None, :], jnp.concatenate([k_lat, k_c], axis=1),
                            jnp.concatenate([v_lat, v_c], axis=1), None)
    o_c, _, _ = recurrent_branch(xc, yc, h_ctx, lp)
    x = x + gate1[:, None, :] * merge_branches(o_a, o_b, o_c, gates, lp)
    return ffn_sublayer(x, shift2, scale2, gate2, lp)


def setup_inputs(seed: int = 0) -> dict:
    key = jax.random.key(seed)
    ks = iter(list(jax.random.split(key, 48)))
    D = D_MODEL

    def normal(shape, scale):
        return jax.random.normal(next(ks), shape, jnp.float32) * scale

    def gain(shape):
        return 1.0 + normal(shape, 0.02)

    a0 = jax.random.uniform(next(ks), (DEPTH, 2, LRU_W), jnp.float32, 0.9, 0.999)
    s = a0 ** (1.0 / LRU_C)
    lru_lam = jnp.log(s) - jnp.log1p(-s)
    return {
        'x_prompt': normal((BATCH, SEQ, D), 1.0),
        'x_sample': normal((DEC_BATCH, DEC_SEQ, D), 1.0),
        'cache_k_a': normal((DEC_BATCH, DEPTH, PAST_LEN, N_KV_A, HD_A), 1.0),
        'cache_v_a': normal((DEC_BATCH, DEPTH, PAST_LEN, N_KV_A, HD_A), 1.0),
        'cache_ckv': normal((DEC_BATCH, DEPTH, PAST_LEN, KV_RANK), 1.0),
        'cache_krope': normal((DEC_BATCH, DEPTH, PAST_LEN, ROPE_B), 1.0),
        'state_lru': normal((DEC_BATCH, DEPTH, 2, LRU_W), 0.5),
        'c': normal((DEC_BATCH, D), 1.0),
        'c_ctx': normal((D,), 1.0),
        'ln1_g': gain((DEPTH, D)),
        'ln2_g': gain((DEPTH, D)),
        'final_g': gain((D,)),
        'w_ada': normal((DEPTH, D, 6 * D), 0.5 * D ** -0.5),
        'b_ada': normal((DEPTH, 6 * D), 0.01),
        'w_in': normal((DEPTH, D, IN_COLS), D ** -0.5),
        'sink': normal((DEPTH, N_HEADS_A), 1.0),
        'g_q': gain((DEPTH, Q_RANK)),
        'w_uq': normal((DEPTH, Q_RANK, N_HEADS_B * (NOPE_B + ROPE_B)), Q_RANK ** -0.5),
        'g_kv': gain((DEPTH, KV_RANK)),
        'w_ukv': normal((DEPTH, KV_RANK, N_HEADS_B * (NOPE_B + V_B)), KV_RANK ** -0.5),
        'conv_w': normal((DEPTH, CONV_W, LRU_W), CONV_W ** -0.5),
        'conv_b': normal((DEPTH, LRU_W), 0.01),
        'lru_wa': normal((DEPTH, 2, LRU_HEADS, LRU_BLOCK, LRU_BLOCK), LRU_BLOCK ** -0.5),
        'lru_ba': normal((DEPTH, 2, LRU_W), 0.01),
        'lru_wx': normal((DEPTH, 2, LRU_HEADS, LRU_BLOCK, LRU_BLOCK), LRU_BLOCK ** -0.5),
        'lru_bx': normal((DEPTH, 2, LRU_W), 0.01),
        'lru_lam': lru_lam,
        'w_pa': normal((DEPTH, N_HEADS_A * HD_A, D), (N_HEADS_A * HD_A) ** -0.5),
        'w_pb': normal((DEPTH, N_HEADS_B * V_B, D), (N_HEADS_B * V_B) ** -0.5),
        'w_pc': normal((DEPTH, LRU_W, D), LRU_W ** -0.5),
        'w_out': normal((DEPTH, D, D), D ** -0.5),
        'w_router': normal((DEPTH, D, N_EXPERTS), D ** -0.5),
        'b_router': normal((DEPTH, N_EXPERTS), 0.01),
        'w_gu': normal((DEPTH, N_EXPERTS, D, 2 * D_FF), D ** -0.5),
        'b_gu': normal((DEPTH, N_EXPERTS, 2 * D_FF), 0.01),
        'w_dn': normal((DEPTH, N_EXPERTS, D_FF, D), D_FF ** -0.5),
        'b_dn': normal((DEPTH, N_EXPERTS, D), 0.01),
    }


def reference(x_prompt, x_sample, cache_k_a, cache_v_a, cache_ckv, cache_krope, state_lru, c, c_ctx,
              ln1_g, ln2_g, final_g, w_ada, b_ada, w_in, sink, g_q, w_uq, g_kv, w_ukv,
              conv_w, conv_b, lru_wa, lru_ba, lru_wx, lru_bx, lru_lam, w_pa, w_pb, w_pc, w_out,
              w_router, b_router, w_gu, b_gu, w_dn, b_dn):
    n_lat = x_sample.shape[1]
    ang_a = axial_angles(n_lat, HD_A)
    ang_b = axial_angles(n_lat, ROPE_B)
    xp = x_prompt
    xs = x_sample
    ks_a, vs_a, ckvs, krs, lrus = [], [], [], [], []
    for l in range(DEPTH):
        lp = dict(ln1=ln1_g[l], ln2=ln2_g[l], w_in=w_in[l], sink=sink[l], g_q=g_q[l], w_uq=w_uq[l],
                  g_kv=g_kv[l], w_ukv=w_ukv[l], conv_w=conv_w[l], conv_b=conv_b[l],
                  lru_wa=lru_wa[l], lru_ba=lru_ba[l], lru_wx=lru_wx[l], lru_bx=lru_bx[l], lru_lam=lru_lam[l],
                  w_pa=w_pa[l], w_pb=w_pb[l], w_pc=w_pc[l], w_out=w_out[l],
                  w_router=w_router[l], b_router=b_router[l], w_gu=w_gu[l], b_gu=b_gu[l],
                  w_dn=w_dn[l], b_dn=b_dn[l])
        ada_ctx = adaln(c_ctx[None, :], w_ada[l], b_ada[l])
        ada_lat = adaln(c, w_ada[l], b_ada[l])
        xp, k_a, v_a, ckv_n, k_r, lru_st = context_layer(xp, ada_ctx, lp)
        ks_a.append(k_a)
        vs_a.append(v_a)
        ckvs.append(ckv_n)
        krs.append(k_r)
        lrus.append(lru_st)
        xs = latent_layer(xs, ada_lat, lp, cache_k_a[:, l], cache_v_a[:, l], cache_ckv[:, l],
                          cache_krope[:, l], state_lru[:, l], ang_a, ang_b)
    y_prompt = rms_norm(xp, final_g)
    y_sample = rms_norm(xs, final_g)
    new_k_a = jnp.stack(ks_a, axis=1)
    new_v_a = jnp.stack(vs_a, axis=1)
    new_ckv = jnp.stack(ckvs, axis=1)
    new_krope = jnp.stack(krs, axis=1)
    new_lru_state = jnp.stack(lrus, axis=1)
    return (y_prompt, y_sample, new_k_a, new_v_a, new_ckv, new_krope, new_lru_state)
```

```python
import functools

import numpy as np
import jax
import jax.numpy as jnp
from jax import lax
from jax.experimental import pallas as pl
from jax.experimental.pallas import tpu as pltpu

F32 = jnp.float32
BF16 = jnp.bfloat16

D_MODEL = 1024
GRID_W = 64
EPS = 1e-6
ROPE_BASE = 10000.0
NEG_INF = -1e30
HD_A = 64
N_HEADS_A = 8
N_KV_A = 2
WINDOW = 128
N_HEADS_B = 8
NOPE_B = 64
ROPE_B = 32
V_B = 64
Q_RANK = 512
KV_RANK = 256
LRU_W = 512
LRU_HEADS = 8
CONV_W = 4
LRU_C = 8.0
N_EXPERTS = 32
TOP_K = 4
D_FF = D_MODEL
SWIGLU_LIMIT = 7.0
SWIGLU_ALPHA = 1.702

LANE = 128
SUBLANE = 8
TOKEN_TILE = 256
EXPERT_BLOCK = 256
HEAD_PAD_B = 128
LRU_HALF = LRU_W // 2
VMEM_LIMIT = 56 * 1024 * 1024

HEAD_PERM_A = (0, 4, 1, 5, 2, 6, 3, 7)

C_QA, C_KA, C_VA, C_CQ, C_CKV, C_XC, C_YC, C_KR, C_END = 0, 512, 640, 768, 1280, 1536, 2048, 2560, 2688


def _cparams(sem):
    return pltpu.CompilerParams(dimension_semantics=sem, vmem_limit_bytes=VMEM_LIMIT)


def _rms(x, g):
    return x * lax.rsqrt(jnp.mean(x * x, axis=-1, keepdims=True) + EPS) * g


def _dot(a, b):
    return jnp.dot(a, b, preferred_element_type=F32)


def _dot_t(a, b):
    return lax.dot_general(a, b, (((1,), (1,)), ((), ())), preferred_element_type=F32)


def _ada_kernel(c_ref, w_ref, b_ref, o_ref):
    c = c_ref[...]
    s = c * jax.nn.sigmoid(c)
    o_ref[...] = _dot(s.astype(BF16), w_ref[...].astype(BF16)) + b_ref[...]


def _ada_all(cond8, w_ada, b_ada):
    depth, d, n6 = w_ada.shape
    nb = 1536
    return pl.pallas_call(
        _ada_kernel,
        grid=(depth, n6 // nb),
        in_specs=[pl.BlockSpec((SUBLANE, d), lambda l, j: (0, 0)),
                  pl.BlockSpec((None, d, nb), lambda l, j: (l, 0, j)),
                  pl.BlockSpec((None, 1, nb), lambda l, j: (l, 0, j))],
        out_specs=pl.BlockSpec((None, SUBLANE, nb), lambda l, j: (l, 0, j)),
        out_shape=jax.ShapeDtypeStruct((depth, SUBLANE, n6), F32),
        compiler_params=_cparams(("arbitrary", "arbitrary")),
        name="ada",
    )(cond8, w_ada, b_ada.reshape(depth, 1, n6))


def _swap_halves(x, half):
    n = x.shape[-1]
    lane = lax.broadcasted_iota(jnp.int32, x.shape, x.ndim - 1)
    first = (lane % (2 * half)) < half
    return jnp.where(first, pltpu.roll(x, n - half, x.ndim - 1), pltpu.roll(x, half, x.ndim - 1))


def _rope(x, cos, sin_signed, half):
    reps = x.shape[-1] // cos.shape[-1]
    if reps > 1:
        cos = jnp.concatenate([cos] * reps, axis=-1)
        sin_signed = jnp.concatenate([sin_signed] * reps, axis=-1)
    return x * cos + _swap_halves(x, half) * sin_signed


def _in_kernel(x_ref, ada_ref, ln1_ref, w_ref, gq_ref, wuq_ref, gkv_ref, wuk_ref, wuv_ref,
               ca_ref, sa_ref, cb_ref, sb_ref,
               qa_ref, ka_ref, va_ref, qb_ref, ckvn_ref, kb_ref, vb_ref, kr_ref, xc_ref, yc_ref):
    d = D_MODEL
    x = x_ref[...]
    shift = ada_ref[:, 0:d]
    scale = ada_ref[:, d:2 * d]
    u = _rms(x, ln1_ref[...]) * (1.0 + scale) + shift
    p = _dot(u.astype(BF16), w_ref[...])
    ca, sa, cb, sb = ca_ref[...], sa_ref[...], cb_ref[...], sb_ref[...]

    qa = _rope(p[:, C_QA:C_KA], ca, sa, HD_A // 4) * (HD_A ** -0.5)
    qa_ref[...] = qa.astype(BF16)
    ka_ref[...] = _rope(p[:, C_KA:C_VA], ca, sa, HD_A // 4)
    va_ref[...] = p[:, C_VA:C_CQ]

    cq = _rms(p[:, C_CQ:C_CKV], gq_ref[...])
    qb = _dot(cq.astype(BF16), wuq_ref[...])
    qb = _rope(qb, cb, sb, ROPE_B // 4) * ((NOPE_B + ROPE_B) ** -0.5)
    qb_ref[...] = qb.astype(BF16)

    ckvn = _rms(p[:, C_CKV:C_XC], gkv_ref[...])
    ckvn_ref[...] = ckvn
    ckvn_b = ckvn.astype(BF16)
    kr = _rope(p[:, C_KR:C_END], cb, sb, ROPE_B // 4)
    kr_ref[...] = kr
    kb = _dot(ckvn_b, wuk_ref[...]) + jnp.concatenate([kr] * N_HEADS_B, axis=-1)
    kb_ref[...] = kb.astype(BF16)
    vb_ref[...] = _dot(ckvn_b, wuv_ref[...]).astype(BF16)

    xc_ref[...] = p[:, C_XC:C_YC]
    yc_ref[...] = p[:, C_YC:C_KR]


def _in_proj(x, ada3, ln1, w_main, g_q, w_uq, g_kv, w_uk, w_uv, tabs, seg, tab_blk):
    t, d = x.shape
    tm = TOKEN_TILE
    row = lambda n: pl.BlockSpec((tm, n), lambda i: (i, 0))
    full = lambda a: pl.BlockSpec(a.shape, lambda i: (0,) * a.ndim)
    tab = pl.BlockSpec((tm, LANE), lambda i: (tab_blk(i), 0))
    sds = lambda n, dt: jax.ShapeDtypeStruct((t, n), dt)
    nb = N_HEADS_B * HEAD_PAD_B
    return pl.pallas_call(
        _in_kernel,
        grid=(t // tm,),
        in_specs=[row(d), pl.BlockSpec((None, 1, 6 * d), lambda i: (seg(i), 0, 0)), full(ln1), full(w_main),
                  full(g_q), full(w_uq), full(g_kv), full(w_uk), full(w_uv), tab, tab, tab, tab],
        out_specs=[row(512), row(128), row(128), row(nb), row(KV_RANK), row(nb), row(512), row(128),
                   row(LRU_W), row(LRU_W)],
        out_shape=[sds(512, BF16), sds(128, F32), sds(128, F32), sds(nb, BF16), sds(KV_RANK, F32),
                   sds(nb, BF16), sds(512, BF16), sds(128, F32), sds(LRU_W, F32), sds(LRU_W, F32)],
        compiler_params=_cparams(("arbitrary",)),
        name="in_proj",
    )(x, ada3, ln1, w_main, g_q, w_uq, g_kv, w_uk, w_uv, *tabs)


def _gqa_heads(q, k_all, v_all, bias, sink_ref, o_ref):
    lane = lax.broadcasted_iota(jnp.int32, (q.shape[0], LANE), 1)
    low = lane < HD_A
    for pair in range(N_HEADS_A // 2):
        qp = q[:, pair * LANE:(pair + 1) * LANE]
        outs = []
        for half in range(2):
            qm = jnp.where(low if half == 0 else ~low, qp, jnp.zeros_like(qp))
            s = _dot_t(qm, k_all)
            if bias is not None:
                s = s + bias
            sink = sink_ref[2 * pair + half]
            m = jnp.maximum(jnp.max(s, axis=-1, keepdims=True), sink)
            e = jnp.exp(s - m)
            l = jnp.sum(e, axis=-1, keepdims=True) + jnp.exp(sink - m)
            outs.append(_dot(e.astype(BF16), v_all) / l)
        o_ref[:, pair * LANE:(pair + 1) * LANE] = jnp.where(low, outs[0], outs[1]).astype(o_ref.dtype)


def _gqa_ctx_kernel(sink_ref, q_ref, k_ref, v_ref, o_ref):
    _gqa_heads(q_ref[...], k_ref[...].astype(BF16), v_ref[...].astype(BF16), None, sink_ref, o_ref)


def _gqa_ctx(qa, ka, va, sink_p, n_seq, s_len):
    return pl.pallas_call(
        _gqa_ctx_kernel,
        grid=(n_seq,),
        in_specs=[pl.BlockSpec(memory_space=pltpu.SMEM),
                  pl.BlockSpec((s_len, 512), lambda b: (b, 0)),
                  pl.BlockSpec((s_len, LANE), lambda b: (b, 0)),
                  pl.BlockSpec((s_len, LANE), lambda b: (b, 0))],
        out_specs=pl.BlockSpec((s_len, 512), lambda b: (b, 0)),
        out_shape=jax.ShapeDtypeStruct((n_seq * s_len, 512), BF16),
        compiler_params=_cparams(("arbitrary",)),
        name="gqa_ctx",
    )(sink_p, qa, ka, va)


def _gqa_lat_kernel(sink_ref, q_ref, kp_ref, kc_ref, kn_ref, vp_ref, vc_ref, vn_ref, kx_ref, vx_ref, o_ref):
    j = pl.program_id(1)
    last = pl.num_programs(1) - 1
    w = WINDOW
    k_all = jnp.concatenate([kp_ref[...], kc_ref[...], kn_ref[...], kx_ref[...]], axis=0).astype(BF16)
    v_all = jnp.concatenate([vp_ref[...], vc_ref[...], vn_ref[...], vx_ref[...]], axis=0).astype(BF16)
    qi = lax.broadcasted_iota(jnp.int32, (w, w), 0)
    kj = lax.broadcasted_iota(jnp.int32, (w, w), 1)
    zero = jnp.zeros((w, w), F32)
    neg = jnp.full((w, w), NEG_INF, F32)
    b_prev = jnp.where(j > 0, jnp.where(kj >= qi, zero, neg), neg)
    b_next = jnp.where(j < last, jnp.where(kj <= qi, zero, neg), neg)
    bias = jnp.concatenate([b_prev, zero, b_next, jnp.zeros((w, kx_ref.shape[0]), F32)], axis=1)
    _gqa_heads(q_ref[...], k_all, v_all, bias, sink_ref, o_ref)


def _gqa_lat(qa, ka, va, kx, vx, sink_p, layer, n_seq, n_len, row0):
    w = WINDOW
    nqb = n_len // w
    blk0 = row0 // w
    past = kx.shape[2]
    cur = lambda b, j: (blk0 + b * nqb + j, 0)
    prev = lambda b, j: (blk0 + b * nqb + jnp.maximum(j - 1, 0), 0)
    nxt = lambda b, j: (blk0 + b * nqb + jnp.minimum(j + 1, nqb - 1), 0)
    cache = pl.BlockSpec((None, None, past, LANE), lambda b, j: (b, layer, 0, 0))
    return pl.pallas_call(
        _gqa_lat_kernel,
        grid=(n_seq, nqb),
        in_specs=[pl.BlockSpec(memory_space=pltpu.SMEM),
                  pl.BlockSpec((w, 512), cur),
                  pl.BlockSpec((w, LANE), prev), pl.BlockSpec((w, LANE), cur), pl.BlockSpec((w, LANE), nxt),
                  pl.BlockSpec((w, LANE), prev), pl.BlockSpec((w, LANE), cur), pl.BlockSpec((w, LANE), nxt),
                  cache, cache],
        out_specs=pl.BlockSpec((w, 512), lambda b, j: (b * nqb + j, 0)),
        out_shape=jax.ShapeDtypeStruct((n_seq * n_len, 512), BF16),
        compiler_params=_cparams(("arbitrary", "arbitrary")),
        name="gqa_lat",
    )(sink_p, qa, ka, ka, ka, va, va, va, kx, vx)


def _mla_kernel(*refs, n_src, chunk):
    q_ref = refs[0]
    kv_refs = refs[1:1 + 2 * n_src]
    o_ref = refs[1 + 2 * n_src]
    qb = q_ref.shape[0]
    lane = lax.broadcasted_iota(jnp.int32, (qb, LANE), 1)
    outs = []
    for hh in range(2):
        q = q_ref[:, hh * HEAD_PAD_B:(hh + 1) * HEAD_PAD_B]
        m = jnp.full((qb, 1), NEG_INF, F32)
        l = jnp.zeros((qb, 1), F32)
        acc = jnp.zeros((qb, LANE), F32)
        for s_i in range(n_src):
            k_ref, v_ref = kv_refs[2 * s_i], kv_refs[2 * s_i + 1]
            nk = k_ref.shape[0]
            for c in range(nk // chunk):
                k = k_ref[c * chunk:(c + 1) * chunk, hh * HEAD_PAD_B:(hh + 1) * HEAD_PAD_B]
                v = v_ref[c * chunk:(c + 1) * chunk, :]
                s = _dot_t(q, k)
                m_new = jnp.maximum(m, jnp.max(s, axis=-1, keepdims=True))
                alpha = jnp.exp(m - m_new)
                e = jnp.exp(s - m_new)
                l = alpha * l + jnp.sum(e, axis=-1, keepdims=True)
                acc = alpha * acc + _dot(e.astype(BF16), v)
                m = m_new
        outs.append(acc / l)
    o_ref[...] = jnp.where(lane < V_B, outs[0], outs[1]).astype(o_ref.dtype)


def _mla_ctx(qb, kb, vb, n_seq, s_len):
    npair = N_HEADS_B // 2
    return pl.pallas_call(
        functools.partial(_mla_kernel, n_src=1, chunk=s_len),
        grid=(n_seq, npair),
        in_specs=[pl.BlockSpec((s_len, 2 * HEAD_PAD_B), lambda b, p: (b, p)),
                  pl.BlockSpec((s_len, 2 * HEAD_PAD_B), lambda b, p: (b, p)),
                  pl.BlockSpec((s_len, LANE), lambda b, p: (b, p))],
        out_specs=pl.BlockSpec((s_len, LANE), lambda b, p: (b, p)),
        out_shape=jax.ShapeDtypeStruct((n_seq * s_len, 512), BF16),
        compiler_params=_cparams(("arbitrary", "arbitrary")),
        name="mla_ctx",
    )(qb, kb, vb)


def _mla_lat(qb, kb, vb, kbx, vbx, n_seq, n_len, row0):
    npair = N_HEADS_B // 2
    qblk = 256
    nqb = n_len // qblk
    past = kbx.shape[0] // n_seq
    return pl.pallas_call(
        functools.partial(_mla_kernel, n_src=2, chunk=512),
        grid=(n_seq, npair, nqb),
        in_specs=[pl.BlockSpec((qblk, 2 * HEAD_PAD_B), lambda b, p, j: (row0 // qblk + b * nqb + j, p)),
                  pl.BlockSpec((n_len, 2 * HEAD_PAD_B), lambda b, p, j: (row0 // n_len + b, p)),
                  pl.BlockSpec((n_len, LANE), lambda b, p, j: (row0 // n_len + b, p)),
                  pl.BlockSpec((past, 2 * HEAD_PAD_B), lambda b, p, j: (b, p)),
                  pl.BlockSpec((past, LANE), lambda b, p, j: (b, p))],
        out_specs=pl.BlockSpec((qblk, LANE), lambda b, p, j: (b * nqb + j, p)),
        out_shape=jax.ShapeDtypeStruct((n_seq * n_len, 512), BF16),
        compiler_params=_cparams(("arbitrary", "arbitrary", "arbitrary")),
        name="mla_lat",
    )(qb, kb, vb, kbx, vbx)


def _cache_kv_kernel(ckv_ref, kr_ref, wuk_ref, wuv_ref, kb_ref, vb_ref):
    c = ckv_ref[...].astype(BF16)
    kb = _dot(c, wuk_ref[...]) + jnp.concatenate([kr_ref[...]] * N_HEADS_B, axis=-1)
    kb_ref[...] = kb.astype(BF16)
    vb_ref[...] = _dot(c, wuv_ref[...]).astype(BF16)


def _cache_kv(cache_ckv, cache_kr_pad, w_uk, w_uv, layer):
    n_seq, _, past, _ = cache_ckv.shape
    nb = N_HEADS_B * HEAD_PAD_B
    return pl.pallas_call(
        _cache_kv_kernel,
        grid=(n_seq,),
        in_specs=[pl.BlockSpec((None, None, past, KV_RANK), lambda b: (b, layer, 0, 0)),
                  pl.BlockSpec((None, None, past, LANE), lambda b: (b, layer, 0, 0)),
                  pl.BlockSpec(w_uk.shape, lambda b: (0, 0)),
                  pl.BlockSpec(w_uv.shape, lambda b: (0, 0))],
        out_specs=[pl.BlockSpec((past, nb), lambda b: (b, 0)), pl.BlockSpec((past, 512), lambda b: (b, 0))],
        out_shape=[jax.ShapeDtypeStruct((n_seq * past, nb), BF16), jax.ShapeDtypeStruct((n_seq * past, 512), BF16)],
        compiler_params=_cparams(("arbitrary",)),
        name="cache_kv",
    )(cache_ckv, cache_kr_pad, w_uk, w_uv)


def _lru_kernel(xc_ref, yc_ref, h0_ref, cw_ref, cb_ref, wa_ref, ba_ref, wx_ref, bx_ref, sp_ref,
                o_ref, st_ref, pad_ref, xcv_ref, a_ref, b_ref, hs_ref, *, chunk):
    n = xc_ref.shape[0]
    halo = SUBLANE
    pad_ref[0:halo, :] = jnp.zeros((halo, LRU_W), F32)
    pad_ref[halo + n:2 * halo + n, :] = jnp.zeros((halo, LRU_W), F32)
    pad_ref[halo:halo + n, :] = xc_ref[...]
    left = CONV_W // 2
    for c in range(n // chunk):
        r0 = c * chunk
        acc = jnp.broadcast_to(cb_ref[...], (chunk, LRU_W))
        for j in range(CONV_W):
            off = halo + r0 + j - left
            acc = acc + cw_ref[j:j + 1, :] * pad_ref[off:off + chunk, :]
        xcv_ref[r0:r0 + chunk, :] = acc

    row = lax.broadcasted_iota(jnp.int32, (SUBLANE, LRU_W), 0)
    for d in range(2):
        for c in range(n // chunk):
            r0 = c * chunk
            xv = xcv_ref[r0:r0 + chunk, :]
            xb = xv.astype(BF16)
            for hf in range(2):
                cs = slice(hf * LRU_HALF, (hf + 1) * LRU_HALF)
                r = jax.nn.sigmoid(_dot(xb[:, cs], wa_ref[d, hf]) + ba_ref[d:d + 1, cs])
                i = jax.nn.sigmoid(_dot(xb[:, cs], wx_ref[d, hf]) + bx_ref[d:d + 1, cs])
                log_a = (-LRU_C) * r * sp_ref[d:d + 1, cs]
                a_ref[r0:r0 + chunk, cs] = jnp.exp(log_a)
                th = jnp.tanh(log_a)
                b_ref[r0:r0 + chunk, cs] = jnp.sqrt(-2.0 * th / (1.0 - th)) * (i * xv[:, cs])

        def body(g, h, d=d):
            grp = g if d == 0 else n // SUBLANE - 1 - g
            rows = pl.ds(pl.multiple_of(grp * SUBLANE, SUBLANE), SUBLANE)
            a = a_ref[rows, :]
            b = b_ref[rows, :]
            for sh in (1, 2, 4):
                if d == 0:
                    keep = row >= sh
                    a_s = jnp.where(keep, pltpu.roll(a, sh, 0), 1.0)
                    b_s = jnp.where(keep, pltpu.roll(b, sh, 0), 0.0)
                else:
                    keep = row < SUBLANE - sh
                    a_s = jnp.where(keep, pltpu.roll(a, SUBLANE - sh, 0), 1.0)
                    b_s = jnp.where(keep, pltpu.roll(b, SUBLANE - sh, 0), 0.0)
                b = a * b_s + b
                a = a * a_s
            hrows = a * h + b
            if d == 0:
                hs_ref[rows, :] = hrows
                return hrows[SUBLANE - 1:SUBLANE, :]
            hs_ref[rows, :] = hs_ref[rows, :] + hrows
            return hrows[0:1, :]

        h_fin = lax.fori_loop(0, n // SUBLANE, body, h0_ref[d:d + 1, :])
        st_ref[d:d + 1, :] = h_fin

    for c in range(n // chunk):
        r0 = c * chunk
        o_ref[r0:r0 + chunk, :] = (hs_ref[r0:r0 + chunk, :] * jax.nn.gelu(yc_ref[r0:r0 + chunk, :])).astype(o_ref.dtype)


def _lru(xc, yc, h0, conv_w, conv_b, wa, ba, wx, bx, sp, n_seq, n_len, row0):
    blk0 = row0 // n_len
    full = lambda a: pl.BlockSpec(a.shape, lambda b: (0,) * a.ndim)
    seq = pl.BlockSpec((n_len, LRU_W), lambda b: (blk0 + b, 0))
    return pl.pallas_call(
        functools.partial(_lru_kernel, chunk=min(n_len, 256)),
        grid=(n_seq,),
        in_specs=[seq, seq, pl.BlockSpec((None, 2, LRU_W), lambda b: (b, 0, 0)),
                  full(conv_w), full(conv_b), full(wa), full(ba), full(wx), full(bx), full(sp)],
        out_specs=[pl.BlockSpec((n_len, LRU_W), lambda b: (b, 0)),
                   pl.BlockSpec((None, 2, LRU_W), lambda b: (b, 0, 0))],
        out_shape=[jax.ShapeDtypeStruct((n_seq * n_len, LRU_W), BF16),
                   jax.ShapeDtypeStruct((n_seq, 2, LRU_W), F32)],
        scratch_shapes=[pltpu.VMEM((n_len + 2 * SUBLANE, LRU_W), F32), pltpu.VMEM((n_len, LRU_W), F32),
                        pltpu.VMEM((n_len, LRU_W), F32), pltpu.VMEM((n_len, LRU_W), F32),
                        pltpu.VMEM((n_len, LRU_W), F32)],
        compiler_params=_cparams(("arbitrary",)),
        name="lru",
    )(xc, yc, h0, conv_w, conv_b, wa, ba, wx, bx, sp)


def _merge_kernel(x_ref, ada_ref, oa_ref, ob_ref, oc_ref, ln1_ref, wg_ref, wpa_ref, wpb_ref, wpc_ref, wout_ref,
                  ln2_ref, wr_ref, br_ref,
                  x1_ref, h_ref, e4_ref, g4_ref, r4_ref, cnt_ref, carry_ref):
    d = D_MODEL
    tm = x_ref.shape[0]
    i = pl.program_id(0)

    @pl.when(i == 0)
    def _():
        carry_ref[...] = jnp.zeros_like(carry_ref)

    x = x_ref[...]
    ada = ada_ref[...]
    u = _rms(x, ln1_ref[...]) * (1.0 + ada[:, d:2 * d]) + ada[:, 0:d]
    g = jax.nn.sigmoid(_dot(u.astype(BF16), wg_ref[...]))
    m = (g[:, 0:d] * _dot(oa_ref[...], wpa_ref[...]) + g[:, d:2 * d] * _dot(ob_ref[...], wpb_ref[...])
         + g[:, 2 * d:3 * d] * _dot(oc_ref[...], wpc_ref[...]))
    x1 = x + ada[:, 2 * d:3 * d] * _dot(m.astype(BF16), wout_ref[...])
    x1_ref[...] = x1
    h = _rms(x1, ln2_ref[...]) * (1.0 + ada[:, 4 * d:5 * d]) + ada[:, 3 * d:4 * d]
    h_ref[...] = h.astype(BF16)

    logits = jnp.dot(h, wr_ref[...], preferred_element_type=F32, precision=lax.Precision.HIGHEST) + br_ref[...]
    col = lax.broadcasted_iota(jnp.int32, (tm, N_EXPERTS), 1).astype(F32)
    col4 = lax.broadcasted_iota(jnp.int32, (tm, TOP_K), 1)
    sel_any = jnp.zeros((tm, N_EXPERTS), F32)
    vals, idxs = [], []
    work = logits
    for _k in range(TOP_K):
        mx = jnp.max(work, axis=-1, keepdims=True)
        idx = jnp.min(jnp.where(work == mx, col, float(N_EXPERTS)), axis=-1, keepdims=True)
        sel = col == idx
        vals.append(mx)
        idxs.append(idx)
        sel_any = jnp.where(sel, 1.0, sel_any)
        work = jnp.where(sel, -jnp.inf, work)

    ri = lax.broadcasted_iota(jnp.int32, (tm, tm), 0)
    ci = lax.broadcasted_iota(jnp.int32, (tm, tm), 1)
    tri = jnp.where(ri > ci, 1.0, 0.0).astype(BF16)
    before = _dot(tri, sel_any.astype(BF16)) + carry_ref[...]
    carry = carry_ref[...] + jnp.sum(sel_any, axis=0, keepdims=True)
    carry_ref[...] = carry
    cnt_ref[...] = jnp.broadcast_to(carry, cnt_ref.shape)

    exps = [jnp.exp(v - vals[0]) for v in vals]
    den = exps[0] + exps[1] + exps[2] + exps[3]
    e4 = jnp.zeros((tm, TOP_K), jnp.int32)
    g4 = jnp.zeros((tm, TOP_K), F32)
    r4 = jnp.zeros((tm, TOP_K), jnp.int32)
    for k in range(TOP_K):
        rank = jnp.sum(jnp.where(col == idxs[k], before, 0.0), axis=-1, keepdims=True)
        e4 = jnp.where(col4 == k, idxs[k].astype(jnp.int32), e4)
        g4 = jnp.where(col4 == k, exps[k] / den, g4)
        r4 = jnp.where(col4 == k, rank.astype(jnp.int32), r4)
    e4_ref[...] = e4
    g4_ref[...] = g4
    r4_ref[...] = r4


def _merge(x, ada3, oa, ob, oc, ln1, w_g, w_pa, w_pb, w_pc, w_out, ln2, w_r, b_r, seg):
    t, d = x.shape
    tm = TOKEN_TILE
    row = lambda n: pl.BlockSpec((tm, n), lambda i: (i, 0))
    full = lambda a: pl.BlockSpec(a.shape, lambda i: (0,) * a.ndim)
    return pl.pallas_call(
        _merge_kernel,
        grid=(t // tm,),
        in_specs=[row(d), pl.BlockSpec((None, 1, 6 * d), lambda i: (seg(i), 0, 0)), row(512), row(512), row(512),
                  full(ln1), full(w_g), full(w_pa), full(w_pb), full(w_pc), full(w_out), full(ln2), full(w_r),
                  full(b_r)],
        out_specs=[row(d), row(d), row(TOP_K), row(TOP_K), row(TOP_K),
                   pl.BlockSpec((SUBLANE, N_EXPERTS), lambda i: (0, 0))],
        out_shape=[jax.ShapeDtypeStruct((t, d), F32), jax.ShapeDtypeStruct((t, d), BF16),
                   jax.ShapeDtypeStruct((t, TOP_K), jnp.int32), jax.ShapeDtypeStruct((t, TOP_K), F32),
                   jax.ShapeDtypeStruct((t, TOP_K), jnp.int32),
                   jax.ShapeDtypeStruct((SUBLANE, N_EXPERTS), F32)],
        scratch_shapes=[pltpu.VMEM((1, N_EXPERTS), F32)],
        compiler_params=_cparams(("arbitrary",)),
        name="merge",
    )(x, ada3, oa, ob, oc, ln1, w_g, w_pa, w_pb, w_pc, w_out, ln2, w_r, b_r)


def _expert_kernel(be_ref, nu_ref, x_ref, wgu_ref, bgu_ref, wdn_ref, bdn_ref, y_ref, *, n_chunk):
    b = pl.program_id(0)

    @pl.when(b < nu_ref[0])
    def _():
        x = x_ref[...]
        cw = D_FF // n_chunk
        acc = jnp.broadcast_to(bdn_ref[...], y_ref.shape)
        for c in range(n_chunk):
            glu = _dot(x, wgu_ref[:, c * cw:(c + 1) * cw].astype(BF16)) + bgu_ref[:, c * cw:(c + 1) * cw]
            lin = (_dot(x, wgu_ref[:, D_FF + c * cw:D_FF + (c + 1) * cw].astype(BF16))
                   + bgu_ref[:, D_FF + c * cw:D_FF + (c + 1) * cw])
            glu = jnp.minimum(glu, SWIGLU_LIMIT)
            lin = jnp.clip(lin, -SWIGLU_LIMIT, SWIGLU_LIMIT)
            act = glu * jax.nn.sigmoid(SWIGLU_ALPHA * glu) * (lin + 1.0)
            acc = acc + _dot(act.astype(BF16), wdn_ref[c * cw:(c + 1) * cw, :].astype(BF16))
        y_ref[...] = acc

    @pl.when(b >= nu_ref[0])
    def _():
        y_ref[...] = jnp.zeros_like(y_ref)


def _experts(xb, block_e, n_used, w_gu, b_gu, w_dn, b_dn):
    n_rows, d = xb.shape
    blk = EXPERT_BLOCK
    n_blocks = n_rows // blk
    grid_spec = pltpu.PrefetchScalarGridSpec(
        num_scalar_prefetch=2,
        grid=(n_blocks,),
        in_specs=[pl.BlockSpec((blk, d), lambda b, be, nu: (b, 0)),
                  pl.BlockSpec((None, d, 2 * D_FF), lambda b, be, nu: (be[b], 0, 0)),
                  pl.BlockSpec((None, 1, 2 * D_FF), lambda b, be, nu: (be[b], 0, 0)),
                  pl.BlockSpec((None, D_FF, d), lambda b, be, nu: (be[b], 0, 0)),
                  pl.BlockSpec((None, 1, d), lambda b, be, nu: (be[b], 0, 0))],
        out_specs=pl.BlockSpec((blk, d), lambda b, be, nu: (b, 0)),
    )
    return pl.pallas_call(
        functools.partial(_expert_kernel, n_chunk=4),
        grid_spec=grid_spec,
        out_shape=jax.ShapeDtypeStruct((n_rows, d), F32),
        compiler_params=_cparams(("arbitrary",)),
        name="experts",
    )(block_e, n_used, xb, w_gu, b_gu.reshape(N_EXPERTS, 1, 2 * D_FF), w_dn, b_dn.reshape(N_EXPERTS, 1, d))


def _combine_kernel(x1_ref, ada_ref, yg_ref, g4_ref, fg_ref, o_ref, *, final):
    d = D_MODEL
    g4 = g4_ref[...]
    y = g4[:, 0:1] * yg_ref[0]
    for k in range(1, TOP_K):
        y = y + g4[:, k:k + 1] * yg_ref[k]
    x2 = x1_ref[...] + ada_ref[:, 5 * d:6 * d] * y
    if final:
        x2 = _rms(x2, fg_ref[...])
    o_ref[...] = x2


def _combine(x1, ada3, yg, g4, final_g, seg, final):
    t, d = x1.shape
    tm = TOKEN_TILE
    return pl.pallas_call(
        functools.partial(_combine_kernel, final=final),
        grid=(t // tm,),
        in_specs=[pl.BlockSpec((tm, d), lambda i: (i, 0)),
                  pl.BlockSpec((None, 1, 6 * d), lambda i: (seg(i), 0, 0)),
                  pl.BlockSpec((TOP_K, tm, d), lambda i: (0, i, 0)),
                  pl.BlockSpec((tm, TOP_K), lambda i: (i, 0)),
                  pl.BlockSpec((1, d), lambda i: (0, 0))],
        out_specs=pl.BlockSpec((tm, d), lambda i: (i, 0)),
        out_shape=jax.ShapeDtypeStruct((t, d), F32),
        compiler_params=_cparams(("arbitrary",)),
        name="combine",
    )(x1, ada3, yg, g4, final_g)


def _rope_tables(n_lat):
    rows = n_lat // GRID_W
    row_ids = jnp.repeat(jnp.arange(rows, dtype=F32), GRID_W)
    col_ids = jnp.tile(jnp.arange(GRID_W, dtype=F32), rows)

    def table(d_rot, lane0):
        d_axis = d_rot // 2
        nf = d_axis // 2
        inv_freq = ROPE_BASE ** (-jnp.arange(0, d_axis, 2, dtype=F32) / d_axis)
        ang_r = row_ids[:, None] * inv_freq
        ang_c = col_ids[:, None] * inv_freq
        ang = jnp.concatenate([ang_r, ang_r, ang_c, ang_c], axis=-1)
        sign = jnp.tile(jnp.concatenate([-jnp.ones((nf,), F32), jnp.ones((nf,), F32)]), 2)
        cos = jnp.ones((n_lat, LANE), F32)
        sin = jnp.zeros((n_lat, LANE), F32)
        for l0 in lane0:
            cos = cos.at[:, l0:l0 + d_rot].set(jnp.cos(ang))
            sin = sin.at[:, l0:l0 + d_rot].set(jnp.sin(ang) * sign)
        ident = (jnp.ones((TOKEN_TILE, LANE), F32), jnp.zeros((TOKEN_TILE, LANE), F32))
        return jnp.concatenate([ident[0], cos], axis=0), jnp.concatenate([ident[1], sin], axis=0)

    ca, sa = table(HD_A, (0, HD_A))
    cb, sb = table(ROPE_B, (NOPE_B,))
    return ca, sa, cb, sb


def _prep_weights(w_in, sink, w_uq, w_ukv, lru_wa, lru_wx, lru_lam, w_pa):
    depth, d, _ = w_in.shape
    cuts = np.cumsum((512, 128, 128, Q_RANK, KV_RANK, ROPE_B, LRU_W, LRU_W, 3 * d))
    perm = np.array(HEAD_PERM_A)
    w_qa = w_in[:, :, :cuts[0]].reshape(depth, d, N_HEADS_A, HD_A)[:, :, perm].reshape(depth, d, 512)
    w_kr = jnp.pad(w_in[:, :, cuts[4]:cuts[5]], ((0, 0), (0, 0), (NOPE_B, LANE - NOPE_B - ROPE_B)))
    w_main = jnp.concatenate([w_qa, w_in[:, :, cuts[0]:cuts[4]], w_in[:, :, cuts[5]:cuts[7]], w_kr],
                             axis=-1).astype(BF16)
    w_g = w_in[:, :, cuts[7]:].astype(BF16)
    sink_p = sink[:, perm]
    w_pa_p = w_pa.reshape(depth, N_HEADS_A, HD_A, d)[:, perm].reshape(depth, 512, d).astype(BF16)
    hb = NOPE_B + ROPE_B
    w_uq_p = jnp.pad(w_uq.reshape(depth, Q_RANK, N_HEADS_B, hb),
                     ((0, 0), (0, 0), (0, 0), (0, HEAD_PAD_B - hb))).reshape(depth, Q_RANK, -1).astype(BF16)
    w_ukv4 = w_ukv.reshape(depth, KV_RANK, N_HEADS_B, NOPE_B + V_B)
    w_uk_p = jnp.pad(w_ukv4[..., :NOPE_B], ((0, 0), (0, 0), (0, 0), (0, HEAD_PAD_B - NOPE_B))
                     ).reshape(depth, KV_RANK, -1).astype(BF16)
    w_uv = w_ukv4[..., NOPE_B:].reshape(depth, KV_RANK, -1).astype(BF16)

    def block_diag(w):
        hpb = LRU_HEADS // 2
        blk = LRU_W // LRU_HEADS
        w = w.reshape(depth, 2, 2, hpb, blk, blk)
        eye = jnp.eye(hpb, dtype=w.dtype)
        out = jnp.einsum('ldghij,hk->ldghikj', w, eye)
        return out.reshape(depth, 2, 2, hpb * blk, hpb * blk).astype(BF16)

    sp = jax.nn.softplus(-lru_lam.astype(F32))
    return w_main, w_g, sink_p, w_pa_p, w_uq_p, w_uk_p, w_uv, block_diag(lru_wa), block_diag(lru_wx), sp


def kernel(x_prompt, x_sample, cache_k_a, cache_v_a, cache_ckv, cache_krope, state_lru, c, c_ctx, ln1_g, ln2_g,
           final_g, w_ada, b_ada, w_in, sink, g_q, w_uq, g_kv, w_ukv, conv_w, conv_b, lru_wa, lru_ba, lru_wx,
           lru_bx, lru_lam, w_pa, w_pb, w_pc, w_out, w_router, b_router, w_gu, b_gu, w_dn, b_dn):
    n_ctx, s_len, d = x_prompt.shape
    n_lat, n_len, _ = x_sample.shape
    depth = w_in.shape[0]
    past = cache_k_a.shape[2]
    t_ctx = n_ctx * s_len
    t_lat = n_lat * n_len
    t = t_ctx + t_lat
    tm = TOKEN_TILE
    ctx_tiles = t_ctx // tm
    lat_tiles = n_len // tm
    assert t_ctx % n_len == 0 and s_len % tm == 0 and n_len % tm == 0 and n_lat + 1 <= SUBLANE

    seg = lambda i: jnp.where(i < ctx_tiles, 0, 1 + (i - ctx_tiles) // lat_tiles)
    tab_blk = lambda i: jnp.where(i < ctx_tiles, 0, 1 + (i - ctx_tiles) % lat_tiles)

    cond8 = jnp.zeros((SUBLANE, d), F32).at[0].set(c_ctx).at[1:1 + n_lat].set(c)
    ada = _ada_all(cond8, w_ada, b_ada)
    tabs = _rope_tables(n_len)
    (w_main, w_g, sink_p, w_pa_p, w_uq_p, w_uk_p, w_uv, wa_bd, wx_bd, sp) = _prep_weights(
        w_in, sink, w_uq, w_ukv, lru_wa, lru_wx, lru_lam, w_pa)
    w_pb_b, w_pc_b, w_out_b = w_pb.astype(BF16), w_pc.astype(BF16), w_out.astype(BF16)
    cache_k2 = cache_k_a.reshape(n_lat, depth, past, N_KV_A * HD_A)
    cache_v2 = cache_v_a.reshape(n_lat, depth, past, N_KV_A * HD_A)
    cache_kr_pad = jnp.pad(cache_krope, ((0, 0), (0, 0), (0, 0), (NOPE_B, LANE - NOPE_B - ROPE_B)))
    h0_ctx = jnp.zeros((n_ctx, 2, LRU_W), F32)
    row2 = lambda a: a.reshape(1, -1)

    x = jnp.concatenate([x_prompt.reshape(t_ctx, d), x_sample.reshape(t_lat, d)], axis=0)
    ks_a, vs_a, ckvs, krs, lrus = [], [], [], [], []
    n_rows = t * TOP_K + N_EXPERTS * EXPERT_BLOCK
    n_blocks = n_rows // EXPERT_BLOCK
    for l in range(depth):
        ada3 = ada[l].reshape(SUBLANE, 1, 6 * d)
        qa, ka, va, qb, ckvn, kb, vb, kr, xc, yc = _in_proj(
            x, ada3, row2(ln1_g[l]), w_main[l], row2(g_q[l]), w_uq_p[l], row2(g_kv[l]), w_uk_p[l], w_uv[l],
            tabs, seg, tab_blk)
        ks_a.append(ka[:t_ctx].reshape(n_ctx, s_len, N_KV_A, HD_A))
        vs_a.append(va[:t_ctx].reshape(n_ctx, s_len, N_KV_A, HD_A))
        ckvs.append(ckvn[:t_ctx].reshape(n_ctx, s_len, KV_RANK))
        krs.append(kr[:t_ctx, NOPE_B:NOPE_B + ROPE_B].reshape(n_ctx, s_len, ROPE_B))

        oa_c = _gqa_ctx(qa, ka, va, sink_p[l], n_ctx, s_len)
        oa_l = _gqa_lat(qa, ka, va, cache_k2, cache_v2, sink_p[l], l, n_lat, n_len, t_ctx)
        kbx, vbx = _cache_kv(cache_ckv, cache_kr_pad, w_uk_p[l], w_uv[l], l)
        ob_c = _mla_ctx(qb, kb, vb, n_ctx, s_len)
        ob_l = _mla_lat(qb, kb, vb, kbx, vbx, n_lat, n_len, t_ctx)
        lru_args = (conv_w[l], row2(conv_b[l]), wa_bd[l], lru_ba[l], wx_bd[l], lru_bx[l], sp[l])
        oc_c, st_c = _lru(xc, yc, h0_ctx, *lru_args, n_ctx, s_len, 0)
        oc_l, _ = _lru(xc, yc, state_lru[:, l], *lru_args, n_lat, n_len, t_ctx)
        lrus.append(st_c)

        oa = jnp.concatenate([oa_c, oa_l], axis=0)
        ob = jnp.concatenate([ob_c, ob_l], axis=0)
        oc = jnp.concatenate([oc_c, oc_l], axis=0)
        x1, h, e4, g4, r4, cnt = _merge(x, ada3, oa, ob, oc, row2(ln1_g[l]), w_g[l], w_pa_p[l], w_pb_b[l],
                                        w_pc_b[l], w_out_b[l], row2(ln2_g[l]), w_router[l], row2(b_router[l]), seg)

        counts = cnt[0].astype(jnp.int32)
        padded = (counts + EXPERT_BLOCK - 1) // EXPERT_BLOCK * EXPERT_BLOCK
        pad_end = jnp.cumsum(padded)
        pad_start = pad_end - padded
        dest4 = pad_start[e4] + r4
        blk_row0 = jnp.arange(n_blocks, dtype=jnp.int32) * EXPERT_BLOCK
        block_e = jnp.minimum(jnp.sum((pad_end[None, :] <= blk_row0[:, None]).astype(jnp.int32), axis=1),
                              N_EXPERTS - 1)
        n_used = (pad_end[-1:] // EXPERT_BLOCK).astype(jnp.int32)
        rows_tok = jnp.zeros((n_rows,), jnp.int32).at[dest4.reshape(-1)].set(
            jnp.repeat(jnp.arange(t, dtype=jnp.int32), TOP_K))
        xb = jnp.take(h, rows_tok, axis=0)
        yb = _experts(xb, block_e, n_used, w_gu[l], b_gu[l], w_dn[l], b_dn[l])
        yg = jnp.take(yb, dest4.T, axis=0)
        x = _combine(x1, ada3, yg, g4, row2(final_g), seg, l == depth - 1)

    y_prompt = x[:t_ctx].reshape(n_ctx, s_len, d)
    y_sample = x[t_ctx:].reshape(n_lat, n_len, d)
    return (y_prompt, y_sample, jnp.stack(ks_a, axis=1), jnp.stack(vs_a, axis=1), jnp.stack(ckvs, axis=1),
            jnp.stack(krs, axis=1), jnp.stack(lrus, axis=1))
```

```python
import functools

import numpy as np
import jax
import jax.numpy as jnp
from jax import lax
from jax.experimental import pallas as pl
from jax.experimental.pallas import tpu as pltpu
from jax.experimental.pallas import tpu_sc as plsc

F32 = jnp.float32
BF16 = jnp.bfloat16

D_MODEL = 1024
GRID_W = 64
EPS = 1e-6
ROPE_BASE = 10000.0
NEG_INF = -1e30
HD_A = 64
N_HEADS_A = 8
N_KV_A = 2
WINDOW = 128
N_HEADS_B = 8
NOPE_B = 64
ROPE_B = 32
V_B = 64
Q_RANK = 512
KV_RANK = 256
LRU_W = 512
LRU_HEADS = 8
CONV_W = 4
LRU_C = 8.0
N_EXPERTS = 32
TOP_K = 4
D_FF = D_MODEL
SWIGLU_LIMIT = 7.0
SWIGLU_ALPHA = 1.702

LANE = 128
SUBLANE = 8
TOKEN_TILE = 256
EXPERT_BLOCK = 256
HEAD_PAD_B = 128
LRU_HALF = LRU_W // 2
VMEM_LIMIT = 56 * 1024 * 1024
SC_CORES = 2
SC_SUBCORES = 16
SC_CHUNK = 32

HEAD_PERM_A = (0, 4, 1, 5, 2, 6, 3, 7)

C_QA, C_KA, C_VA, C_CQ, C_CKV, C_XC, C_YC, C_KR, C_END = 0, 512, 640, 768, 1280, 1536, 2048, 2560, 2688


def _cparams(sem):
    return pltpu.CompilerParams(dimension_semantics=sem, vmem_limit_bytes=VMEM_LIMIT)


def _rms(x, g):
    return x * lax.rsqrt(jnp.mean(x * x, axis=-1, keepdims=True) + EPS) * g


def _dot(a, b):
    return jnp.dot(a, b, preferred_element_type=F32)


def _dot_t(a, b):
    return lax.dot_general(a, b, (((1,), (1,)), ((), ())), preferred_element_type=F32)


def _ada_kernel(c_ref, w_ref, b_ref, o_ref):
    c = c_ref[...]
    s = c * jax.nn.sigmoid(c)
    o_ref[...] = _dot(s.astype(BF16), w_ref[...].astype(BF16)) + b_ref[...]


def _ada_all(cond8, w_ada, b_ada):
    depth, d, n6 = w_ada.shape
    nb = 1536
    return pl.pallas_call(
        _ada_kernel,
        grid=(depth, n6 // nb),
        in_specs=[pl.BlockSpec((SUBLANE, d), lambda l, j: (0, 0)),
                  pl.BlockSpec((None, d, nb), lambda l, j: (l, 0, j)),
                  pl.BlockSpec((None, 1, nb), lambda l, j: (l, 0, j))],
        out_specs=pl.BlockSpec((None, SUBLANE, nb), lambda l, j: (l, 0, j)),
        out_shape=jax.ShapeDtypeStruct((depth, SUBLANE, n6), F32),
        compiler_params=_cparams(("arbitrary", "arbitrary")),
        name="ada",
    )(cond8, w_ada, b_ada.reshape(depth, 1, n6))


def _swap_halves(x, half):
    n = x.shape[-1]
    lane = lax.broadcasted_iota(jnp.int32, x.shape, x.ndim - 1)
    first = (lane % (2 * half)) < half
    return jnp.where(first, pltpu.roll(x, n - half, x.ndim - 1), pltpu.roll(x, half, x.ndim - 1))


def _rope(x, cos, sin_signed, half):
    reps = x.shape[-1] // cos.shape[-1]
    if reps > 1:
        cos = jnp.concatenate([cos] * reps, axis=-1)
        sin_signed = jnp.concatenate([sin_signed] * reps, axis=-1)
    return x * cos + _swap_halves(x, half) * sin_signed


def _in_kernel(x_ref, ada_ref, ln1_ref, w_ref, gq_ref, wuq_ref, gkv_ref, wuk_ref, wuv_ref,
               ca_ref, sa_ref, cb_ref, sb_ref,
               qa_ref, ka_ref, va_ref, qb_ref, ckvn_ref, kb_ref, vb_ref, kr_ref, xc_ref, yc_ref):
    d = D_MODEL
    x = x_ref[...]
    shift = ada_ref[:, 0:d]
    scale = ada_ref[:, d:2 * d]
    u = _rms(x, ln1_ref[...]) * (1.0 + scale) + shift
    p = _dot(u.astype(BF16), w_ref[...])
    ca, sa, cb, sb = ca_ref[...], sa_ref[...], cb_ref[...], sb_ref[...]

    qa = _rope(p[:, C_QA:C_KA], ca, sa, HD_A // 4) * (HD_A ** -0.5)
    qa_ref[...] = qa.astype(BF16)
    ka_ref[...] = _rope(p[:, C_KA:C_VA], ca, sa, HD_A // 4)
    va_ref[...] = p[:, C_VA:C_CQ]

    cq = _rms(p[:, C_CQ:C_CKV], gq_ref[...])
    qb = _dot(cq.astype(BF16), wuq_ref[...])
    qb = _rope(qb, cb, sb, ROPE_B // 4) * ((NOPE_B + ROPE_B) ** -0.5)
    qb_ref[...] = qb.astype(BF16)

    ckvn = _rms(p[:, C_CKV:C_XC], gkv_ref[...])
    ckvn_ref[...] = ckvn
    ckvn_b = ckvn.astype(BF16)
    kr = _rope(p[:, C_KR:C_END], cb, sb, ROPE_B // 4)
    kr_ref[...] = kr
    kb = _dot(ckvn_b, wuk_ref[...]) + jnp.concatenate([kr] * N_HEADS_B, axis=-1)
    kb_ref[...] = kb.astype(BF16)
    vb_ref[...] = _dot(ckvn_b, wuv_ref[...]).astype(BF16)

    xc_ref[...] = p[:, C_XC:C_YC]
    yc_ref[...] = p[:, C_YC:C_KR]


def _in_proj(x, ada3, ln1, w_main, g_q, w_uq, g_kv, w_uk, w_uv, tabs, seg, tab_blk):
    t, d = x.shape
    tm = TOKEN_TILE
    row = lambda n: pl.BlockSpec((tm, n), lambda i: (i, 0))
    full = lambda a: pl.BlockSpec(a.shape, lambda i: (0,) * a.ndim)
    tab = pl.BlockSpec((tm, LANE), lambda i: (tab_blk(i), 0))
    sds = lambda n, dt: jax.ShapeDtypeStruct((t, n), dt)
    nb = N_HEADS_B * HEAD_PAD_B
    return pl.pallas_call(
        _in_kernel,
        grid=(t // tm,),
        in_specs=[row(d), pl.BlockSpec((None, 1, 6 * d), lambda i: (seg(i), 0, 0)), full(ln1), full(w_main),
                  full(g_q), full(w_uq), full(g_kv), full(w_uk), full(w_uv), tab, tab, tab, tab],
        out_specs=[row(512), row(128), row(128), row(nb), row(KV_RANK), row(nb), row(512), row(128),
                   row(LRU_W), row(LRU_W)],
        out_shape=[sds(512, BF16), sds(128, F32), sds(128, F32), sds(nb, BF16), sds(KV_RANK, F32),
                   sds(nb, BF16), sds(512, BF16), sds(128, F32), sds(LRU_W, F32), sds(LRU_W, F32)],
        compiler_params=_cparams(("arbitrary",)),
        name="in_proj",
    )(x, ada3, ln1, w_main, g_q, w_uq, g_kv, w_uk, w_uv, *tabs)


def _gqa_heads(q, k_all, v_all, bias, sink_ref, o_ref):
    lane = lax.broadcasted_iota(jnp.int32, (q.shape[0], LANE), 1)
    low = lane < HD_A
    for pair in range(N_HEADS_A // 2):
        qp = q[:, pair * LANE:(pair + 1) * LANE]
        outs = []
        for half in range(2):
            qm = jnp.where(low if half == 0 else ~low, qp, jnp.zeros_like(qp))
            s = _dot_t(qm, k_all)
            if bias is not None:
                s = s + bias
            sink = sink_ref[2 * pair + half]
            m = jnp.maximum(jnp.max(s, axis=-1, keepdims=True), sink)
            e = jnp.exp(s - m)
            l = jnp.sum(e, axis=-1, keepdims=True) + jnp.exp(sink - m)
            outs.append(_dot(e.astype(BF16), v_all) / l)
        o_ref[:, pair * LANE:(pair + 1) * LANE] = jnp.where(low, outs[0], outs[1]).astype(o_ref.dtype)


def _gqa_ctx_kernel(sink_ref, q_ref, k_ref, v_ref, o_ref):
    _gqa_heads(q_ref[...], k_ref[...].astype(BF16), v_ref[...].astype(BF16), None, sink_ref, o_ref)


def _gqa_ctx(qa, ka, va, sink_p, n_seq, s_len):
    return pl.pallas_call(
        _gqa_ctx_kernel,
        grid=(n_seq,),
        in_specs=[pl.BlockSpec(memory_space=pltpu.SMEM),
                  pl.BlockSpec((s_len, 512), lambda b: (b, 0)),
                  pl.BlockSpec((s_len, LANE), lambda b: (b, 0)),
                  pl.BlockSpec((s_len, LANE), lambda b: (b, 0))],
        out_specs=pl.BlockSpec((s_len, 512), lambda b: (b, 0)),
        out_shape=jax.ShapeDtypeStruct((n_seq * s_len, 512), BF16),
        compiler_params=_cparams(("arbitrary",)),
        name="gqa_ctx",
    )(sink_p, qa, ka, va)


def _gqa_lat_kernel(sink_ref, q_ref, kp_ref, kc_ref, kn_ref, vp_ref, vc_ref, vn_ref, kx_ref, vx_ref, o_ref):
    j = pl.program_id(1)
    last = pl.num_programs(1) - 1
    w = WINDOW
    k_all = jnp.concatenate([kp_ref[...], kc_ref[...], kn_ref[...], kx_ref[...]], axis=0).astype(BF16)
    v_all = jnp.concatenate([vp_ref[...], vc_ref[...], vn_ref[...], vx_ref[...]], axis=0).astype(BF16)
    qi = lax.broadcasted_iota(jnp.int32, (w, w), 0)
    kj = lax.broadcasted_iota(jnp.int32, (w, w), 1)
    zero = jnp.zeros((w, w), F32)
    neg = jnp.full((w, w), NEG_INF, F32)
    b_prev = jnp.where(j > 0, jnp.where(kj >= qi, zero, neg), neg)
    b_next = jnp.where(j < last, jnp.where(kj <= qi, zero, neg), neg)
    bias = jnp.concatenate([b_prev, zero, b_next, jnp.zeros((w, kx_ref.shape[0]), F32)], axis=1)
    _gqa_heads(q_ref[...], k_all, v_all, bias, sink_ref, o_ref)


def _gqa_lat(qa, ka, va, kx, vx, sink_p, layer, n_seq, n_len, row0):
    w = WINDOW
    nqb = n_len // w
    blk0 = row0 // w
    past = kx.shape[2]
    cur = lambda b, j: (blk0 + b * nqb + j, 0)
    prev = lambda b, j: (blk0 + b * nqb + jnp.maximum(j - 1, 0), 0)
    nxt = lambda b, j: (blk0 + b * nqb + jnp.minimum(j + 1, nqb - 1), 0)
    cache = pl.BlockSpec((None, None, past, LANE), lambda b, j: (b, layer, 0, 0))
    return pl.pallas_call(
        _gqa_lat_kernel,
        grid=(n_seq, nqb),
        in_specs=[pl.BlockSpec(memory_space=pltpu.SMEM),
                  pl.BlockSpec((w, 512), cur),
                  pl.BlockSpec((w, LANE), prev), pl.BlockSpec((w, LANE), cur), pl.BlockSpec((w, LANE), nxt),
                  pl.BlockSpec((w, LANE), prev), pl.BlockSpec((w, LANE), cur), pl.BlockSpec((w, LANE), nxt),
                  cache, cache],
        out_specs=pl.BlockSpec((w, 512), lambda b, j: (b * nqb + j, 0)),
        out_shape=jax.ShapeDtypeStruct((n_seq * n_len, 512), BF16),
        compiler_params=_cparams(("arbitrary", "arbitrary")),
        name="gqa_lat",
    )(sink_p, qa, ka, ka, ka, va, va, va, kx, vx)


def _mla_kernel(*refs, n_src, chunk):
    q_ref = refs[0]
    kv_refs = refs[1:1 + 2 * n_src]
    o_ref = refs[1 + 2 * n_src]
    qb = q_ref.shape[0]
    lane = lax.broadcasted_iota(jnp.int32, (qb, LANE), 1)
    outs = []
    for hh in range(2):
        q = q_ref[:, hh * HEAD_PAD_B:(hh + 1) * HEAD_PAD_B]
        m = jnp.full((qb, 1), NEG_INF, F32)
        l = jnp.zeros((qb, 1), F32)
        acc = jnp.zeros((qb, LANE), F32)
        for s_i in range(n_src):
            k_ref, v_ref = kv_refs[2 * s_i], kv_refs[2 * s_i + 1]
            nk = k_ref.shape[0]
            for c in range(nk // chunk):
                k = k_ref[c * chunk:(c + 1) * chunk, hh * HEAD_PAD_B:(hh + 1) * HEAD_PAD_B]
                v = v_ref[c * chunk:(c + 1) * chunk, :]
                s = _dot_t(q, k)
                m_new = jnp.maximum(m, jnp.max(s, axis=-1, keepdims=True))
                alpha = jnp.exp(m - m_new)
                e = jnp.exp(s - m_new)
                l = alpha * l + jnp.sum(e, axis=-1, keepdims=True)
                acc = alpha * acc + _dot(e.astype(BF16), v)
                m = m_new
        outs.append(acc / l)
    o_ref[...] = jnp.where(lane < V_B, outs[0], outs[1]).astype(o_ref.dtype)


def _mla_ctx(qb, kb, vb, n_seq, s_len):
    npair = N_HEADS_B // 2
    return pl.pallas_call(
        functools.partial(_mla_kernel, n_src=1, chunk=s_len),
        grid=(n_seq, npair),
        in_specs=[pl.BlockSpec((s_len, 2 * HEAD_PAD_B), lambda b, p: (b, p)),
                  pl.BlockSpec((s_len, 2 * HEAD_PAD_B), lambda b, p: (b, p)),
                  pl.BlockSpec((s_len, LANE), lambda b, p: (b, p))],
        out_specs=pl.BlockSpec((s_len, LANE), lambda b, p: (b, p)),
        out_shape=jax.ShapeDtypeStruct((n_seq * s_len, 512), BF16),
        compiler_params=_cparams(("arbitrary", "arbitrary")),
        name="mla_ctx",
    )(qb, kb, vb)


def _mla_lat(qb, kb, vb, kbx, vbx, n_seq, n_len, row0):
    npair = N_HEADS_B // 2
    qblk = 256
    nqb = n_len // qblk
    past = kbx.shape[0] // n_seq
    return pl.pallas_call(
        functools.partial(_mla_kernel, n_src=2, chunk=512),
        grid=(n_seq, npair, nqb),
        in_specs=[pl.BlockSpec((qblk, 2 * HEAD_PAD_B), lambda b, p, j: (row0 // qblk + b * nqb + j, p)),
                  pl.BlockSpec((n_len, 2 * HEAD_PAD_B), lambda b, p, j: (row0 // n_len + b, p)),
                  pl.BlockSpec((n_len, LANE), lambda b, p, j: (row0 // n_len + b, p)),
                  pl.BlockSpec((past, 2 * HEAD_PAD_B), lambda b, p, j: (b, p)),
                  pl.BlockSpec((past, LANE), lambda b, p, j: (b, p))],
        out_specs=pl.BlockSpec((qblk, LANE), lambda b, p, j: (b * nqb + j, p)),
        out_shape=jax.ShapeDtypeStruct((n_seq * n_len, 512), BF16),
        compiler_params=_cparams(("arbitrary", "arbitrary", "arbitrary")),
        name="mla_lat",
    )(qb, kb, vb, kbx, vbx)


def _cache_kv_kernel(ckv_ref, kr_ref, wuk_ref, wuv_ref, kb_ref, vb_ref):
    c = ckv_ref[...].astype(BF16)
    kb = _dot(c, wuk_ref[...]) + jnp.concatenate([kr_ref[...]] * N_HEADS_B, axis=-1)
    kb_ref[...] = kb.astype(BF16)
    vb_ref[...] = _dot(c, wuv_ref[...]).astype(BF16)


def _cache_kv(cache_ckv, cache_kr_pad, w_uk, w_uv, layer):
    n_seq, _, past, _ = cache_ckv.shape
    nb = N_HEADS_B * HEAD_PAD_B
    return pl.pallas_call(
        _cache_kv_kernel,
        grid=(n_seq,),
        in_specs=[pl.BlockSpec((None, None, past, KV_RANK), lambda b: (b, layer, 0, 0)),
                  pl.BlockSpec((None, None, past, LANE), lambda b: (b, layer, 0, 0)),
                  pl.BlockSpec(w_uk.shape, lambda b: (0, 0)),
                  pl.BlockSpec(w_uv.shape, lambda b: (0, 0))],
        out_specs=[pl.BlockSpec((past, nb), lambda b: (b, 0)), pl.BlockSpec((past, 512), lambda b: (b, 0))],
        out_shape=[jax.ShapeDtypeStruct((n_seq * past, nb), BF16), jax.ShapeDtypeStruct((n_seq * past, 512), BF16)],
        compiler_params=_cparams(("arbitrary",)),
        name="cache_kv",
    )(cache_ckv, cache_kr_pad, w_uk, w_uv)


def _lru_kernel(xc_ref, yc_ref, h0_ref, cw_ref, cb_ref, wa_ref, ba_ref, wx_ref, bx_ref, sp_ref,
                o_ref, st_ref, pad_ref, xcv_ref, a_ref, b_ref, hs_ref, *, chunk):
    n = xc_ref.shape[0]
    halo = SUBLANE
    pad_ref[0:halo, :] = jnp.zeros((halo, LRU_W), F32)
    pad_ref[halo + n:2 * halo + n, :] = jnp.zeros((halo, LRU_W), F32)
    pad_ref[halo:halo + n, :] = xc_ref[...]
    left = CONV_W // 2
    for c in range(n // chunk):
        r0 = c * chunk
        acc = jnp.broadcast_to(cb_ref[...], (chunk, LRU_W))
        for j in range(CONV_W):
            off = halo + r0 + j - left
            acc = acc + cw_ref[j:j + 1, :] * pad_ref[off:off + chunk, :]
        xcv_ref[r0:r0 + chunk, :] = acc

    row = lax.broadcasted_iota(jnp.int32, (SUBLANE, LRU_W), 0)
    for d in range(2):
        for c in range(n // chunk):
            r0 = c * chunk
            xv = xcv_ref[r0:r0 + chunk, :]
            xb = xv.astype(BF16)
            for hf in range(2):
                cs = slice(hf * LRU_HALF, (hf + 1) * LRU_HALF)
                r = jax.nn.sigmoid(_dot(xb[:, cs], wa_ref[d, hf]) + ba_ref[d:d + 1, cs])
                i = jax.nn.sigmoid(_dot(xb[:, cs], wx_ref[d, hf]) + bx_ref[d:d + 1, cs])
                log_a = (-LRU_C) * r * sp_ref[d:d + 1, cs]
                a = jnp.exp(log_a)
                a_ref[r0:r0 + chunk, cs] = a
                b_ref[r0:r0 + chunk, cs] = jnp.sqrt(-jnp.tanh(log_a) * (a * a + 1.0)) * (i * xv[:, cs])

        def body(g, h, d=d):
            grp = g if d == 0 else n // SUBLANE - 1 - g
            rows = pl.ds(pl.multiple_of(grp * SUBLANE, SUBLANE), SUBLANE)
            a = a_ref[rows, :]
            b = b_ref[rows, :]
            for sh in (1, 2, 4):
                if d == 0:
                    keep = row >= sh
                    a_s = jnp.where(keep, pltpu.roll(a, sh, 0), 1.0)
                    b_s = jnp.where(keep, pltpu.roll(b, sh, 0), 0.0)
                else:
                    keep = row < SUBLANE - sh
                    a_s = jnp.where(keep, pltpu.roll(a, SUBLANE - sh, 0), 1.0)
                    b_s = jnp.where(keep, pltpu.roll(b, SUBLANE - sh, 0), 0.0)
                b = a * b_s + b
                a = a * a_s
            hrows = a * h + b
            if d == 0:
                hs_ref[rows, :] = hrows
                return hrows[SUBLANE - 1:SUBLANE, :]
            hs_ref[rows, :] = hs_ref[rows, :] + hrows
            return hrows[0:1, :]

        h_fin = lax.fori_loop(0, n // SUBLANE, body, h0_ref[d:d + 1, :])
        st_ref[d:d + 1, :] = h_fin

    for c in range(n // chunk):
        r0 = c * chunk
        o_ref[r0:r0 + chunk, :] = (hs_ref[r0:r0 + chunk, :] * jax.nn.gelu(yc_ref[r0:r0 + chunk, :])).astype(o_ref.dtype)


def _lru(xc, yc, h0, conv_w, conv_b, wa, ba, wx, bx, sp, n_seq, n_len, row0):
    blk0 = row0 // n_len
    full = lambda a: pl.BlockSpec(a.shape, lambda b: (0,) * a.ndim)
    seq = pl.BlockSpec((n_len, LRU_W), lambda b: (blk0 + b, 0))
    return pl.pallas_call(
        functools.partial(_lru_kernel, chunk=min(n_len, 256)),
        grid=(n_seq,),
        in_specs=[seq, seq, pl.BlockSpec((None, 2, LRU_W), lambda b: (b, 0, 0)),
                  full(conv_w), full(conv_b), full(wa), full(ba), full(wx), full(bx), full(sp)],
        out_specs=[pl.BlockSpec((n_len, LRU_W), lambda b: (b, 0)),
                   pl.BlockSpec((None, 2, LRU_W), lambda b: (b, 0, 0))],
        out_shape=[jax.ShapeDtypeStruct((n_seq * n_len, LRU_W), BF16),
                   jax.ShapeDtypeStruct((n_seq, 2, LRU_W), F32)],
        scratch_shapes=[pltpu.VMEM((n_len + 2 * SUBLANE, LRU_W), F32), pltpu.VMEM((n_len, LRU_W), F32),
                        pltpu.VMEM((n_len, LRU_W), F32), pltpu.VMEM((n_len, LRU_W), F32),
                        pltpu.VMEM((n_len, LRU_W), F32)],
        compiler_params=_cparams(("arbitrary",)),
        name="lru",
    )(xc, yc, h0, conv_w, conv_b, wa, ba, wx, bx, sp)


def _merge_kernel(x_ref, ada_ref, oa_ref, ob_ref, oc_ref, ln1_ref, wg_ref, wpa_ref, wpb_ref, wpc_ref, wout_ref,
                  ln2_ref, wr_ref, br_ref,
                  x1_ref, h_ref, e4_ref, g4_ref, r4_ref, cnt_ref, carry_ref):
    d = D_MODEL
    tm = x_ref.shape[0]
    i = pl.program_id(0)

    @pl.when(i == 0)
    def _():
        carry_ref[...] = jnp.zeros_like(carry_ref)

    x = x_ref[...]
    ada = ada_ref[...]
    u = _rms(x, ln1_ref[...]) * (1.0 + ada[:, d:2 * d]) + ada[:, 0:d]
    g = jax.nn.sigmoid(_dot(u.astype(BF16), wg_ref[...]))
    m = (g[:, 0:d] * _dot(oa_ref[...], wpa_ref[...]) + g[:, d:2 * d] * _dot(ob_ref[...], wpb_ref[...])
         + g[:, 2 * d:3 * d] * _dot(oc_ref[...], wpc_ref[...]))
    x1 = x + ada[:, 2 * d:3 * d] * _dot(m.astype(BF16), wout_ref[...])
    x1_ref[...] = x1
    h = _rms(x1, ln2_ref[...]) * (1.0 + ada[:, 4 * d:5 * d]) + ada[:, 3 * d:4 * d]
    h_ref[...] = h

    h_hi = h.astype(BF16)
    h_lo = (h - h_hi.astype(F32)).astype(BF16)
    hw = _dot(h_hi, wr_ref[...])
    logits = (hw[:, :N_EXPERTS] + hw[:, N_EXPERTS:] + _dot(h_lo, wr_ref[:, :N_EXPERTS])) + br_ref[...]
    col = lax.broadcasted_iota(jnp.int32, (tm, N_EXPERTS), 1).astype(F32)
    col4 = lax.broadcasted_iota(jnp.int32, (tm, TOP_K), 1)
    sel_any = jnp.zeros((tm, N_EXPERTS), F32)
    vals, idxs = [], []
    work = logits
    for _k in range(TOP_K):
        mx = jnp.max(work, axis=-1, keepdims=True)
        idx = jnp.min(jnp.where(work == mx, col, float(N_EXPERTS)), axis=-1, keepdims=True)
        sel = col == idx
        vals.append(mx)
        idxs.append(idx)
        sel_any = jnp.where(sel, 1.0, sel_any)
        work = jnp.where(sel, -jnp.inf, work)

    ri = lax.broadcasted_iota(jnp.int32, (tm, tm), 0)
    ci = lax.broadcasted_iota(jnp.int32, (tm, tm), 1)
    tri = jnp.where(ri > ci, 1.0, 0.0).astype(BF16)
    before = _dot(tri, sel_any.astype(BF16)) + carry_ref[...]
    carry = carry_ref[...] + jnp.sum(sel_any, axis=0, keepdims=True)
    carry_ref[...] = carry
    cnt_ref[...] = jnp.broadcast_to(carry, cnt_ref.shape)

    exps = [jnp.exp(v - vals[0]) for v in vals]
    den = exps[0] + exps[1] + exps[2] + exps[3]
    e4 = jnp.zeros((tm, TOP_K), jnp.int32)
    g4 = jnp.zeros((tm, TOP_K), F32)
    r4 = jnp.zeros((tm, TOP_K), jnp.int32)
    for k in range(TOP_K):
        rank = jnp.sum(jnp.where(col == idxs[k], before, 0.0), axis=-1, keepdims=True)
        e4 = jnp.where(col4 == k, idxs[k].astype(jnp.int32), e4)
        g4 = jnp.where(col4 == k, exps[k] / den, g4)
        r4 = jnp.where(col4 == k, rank.astype(jnp.int32), r4)
    e4_ref[...] = e4
    g4_ref[...] = g4
    r4_ref[...] = r4


def _merge(x, ada3, oa, ob, oc, ln1, w_g, w_pa, w_pb, w_pc, w_out, ln2, w_r, b_r, seg):
    t, d = x.shape
    tm = TOKEN_TILE
    row = lambda n: pl.BlockSpec((tm, n), lambda i: (i, 0))
    full = lambda a: pl.BlockSpec(a.shape, lambda i: (0,) * a.ndim)
    return pl.pallas_call(
        _merge_kernel,
        grid=(t // tm,),
        in_specs=[row(d), pl.BlockSpec((None, 1, 6 * d), lambda i: (seg(i), 0, 0)), row(512), row(512), row(512),
                  full(ln1), full(w_g), full(w_pa), full(w_pb), full(w_pc), full(w_out), full(ln2), full(w_r),
                  full(b_r)],
        out_specs=[row(d), row(d), row(TOP_K), row(TOP_K), row(TOP_K),
                   pl.BlockSpec((SUBLANE, N_EXPERTS), lambda i: (0, 0))],
        out_shape=[jax.ShapeDtypeStruct((t, d), F32), jax.ShapeDtypeStruct((t, d), F32),
                   jax.ShapeDtypeStruct((t, TOP_K), jnp.int32), jax.ShapeDtypeStruct((t, TOP_K), F32),
                   jax.ShapeDtypeStruct((t, TOP_K), jnp.int32),
                   jax.ShapeDtypeStruct((SUBLANE, N_EXPERTS), F32)],
        scratch_shapes=[pltpu.VMEM((1, N_EXPERTS), F32)],
        compiler_params=_cparams(("arbitrary",)),
        name="merge",
    )(x, ada3, oa, ob, oc, ln1, w_g, w_pa, w_pb, w_pc, w_out, ln2, w_r, b_r)


def _expert_kernel(be_ref, nu_ref, x_ref, wgu_ref, bgu_ref, wdn_ref, bdn_ref, y_ref, wgu_s, wdn_s, *, n_chunk):
    b = pl.program_id(0)
    used = b < nu_ref[0]
    new_expert = jnp.logical_or(b == 0, be_ref[b] != be_ref[jnp.maximum(b - 1, 0)])

    @pl.when(jnp.logical_and(used, new_expert))
    def _():
        wgu_s[...] = wgu_ref[...].astype(BF16)
        wdn_s[...] = wdn_ref[...].astype(BF16)

    @pl.when(used)
    def _():
        x = x_ref[...].astype(BF16)
        cw = D_FF // n_chunk
        acc = jnp.broadcast_to(bdn_ref[...], y_ref.shape)
        for c in range(n_chunk):
            glu = _dot(x, wgu_s[:, c * cw:(c + 1) * cw]) + bgu_ref[:, c * cw:(c + 1) * cw]
            lin = (_dot(x, wgu_s[:, D_FF + c * cw:D_FF + (c + 1) * cw])
                   + bgu_ref[:, D_FF + c * cw:D_FF + (c + 1) * cw])
            glu = jnp.minimum(glu, SWIGLU_LIMIT)
            lin = jnp.clip(lin, -SWIGLU_LIMIT, SWIGLU_LIMIT)
            act = glu * jax.nn.sigmoid(SWIGLU_ALPHA * glu) * (lin + 1.0)
            acc = acc + _dot(act.astype(BF16), wdn_s[c * cw:(c + 1) * cw, :])
        y_ref[...] = acc

    @pl.when(jnp.logical_not(used))
    def _():
        y_ref[...] = jnp.zeros_like(y_ref)


def _experts(xb, block_e, n_used, w_gu, b_gu, w_dn, b_dn, layer):
    n_rows, d = xb.shape
    depth = w_gu.shape[0]
    blk = EXPERT_BLOCK
    n_blocks = n_rows // blk
    wmap = lambda b, be, nu: (layer, be[b], 0, 0)
    grid_spec = pltpu.PrefetchScalarGridSpec(
        num_scalar_prefetch=2,
        grid=(n_blocks,),
        in_specs=[pl.BlockSpec((blk, d), lambda b, be, nu: (b, 0)),
                  pl.BlockSpec((None, None, d, 2 * D_FF), wmap),
                  pl.BlockSpec((None, None, 1, 2 * D_FF), wmap),
                  pl.BlockSpec((None, None, D_FF, d), wmap),
                  pl.BlockSpec((None, None, 1, d), wmap)],
        out_specs=pl.BlockSpec((blk, d), lambda b, be, nu: (b, 0)),
        scratch_shapes=[pltpu.VMEM((d, 2 * D_FF), BF16), pltpu.VMEM((D_FF, d), BF16)],
    )
    return pl.pallas_call(
        functools.partial(_expert_kernel, n_chunk=4),
        grid_spec=grid_spec,
        out_shape=jax.ShapeDtypeStruct((n_rows, d), F32),
        compiler_params=_cparams(("arbitrary",)),
        name="experts",
    )(block_e, n_used, xb, w_gu, b_gu.reshape(depth, N_EXPERTS, 1, 2 * D_FF), w_dn,
      b_dn.reshape(depth, N_EXPERTS, 1, d))


def _sc_gather(table, idx):
    n_idx = idx.shape[0]
    d = table.shape[1]
    n_workers = SC_CORES * SC_SUBCORES
    per_w = n_idx // n_workers
    assert n_idx % (n_workers * SC_CHUNK) == 0
    mesh = plsc.VectorSubcoreMesh(core_axis_name="c", subcore_axis_name="s")

    @functools.partial(
        pl.kernel, mesh=mesh, out_type=jax.ShapeDtypeStruct((n_idx, d), table.dtype),
        scratch_types=[pltpu.VMEM((SC_CHUNK,), jnp.int32), pltpu.VMEM((SC_CHUNK, d), table.dtype),
                       pltpu.SemaphoreType.DMA],
        name="sc_gather")
    def gather(table_hbm, idx_hbm, out_hbm, idx_v, rows_v, sem):
        base = (lax.axis_index("s") * SC_CORES + lax.axis_index("c")) * per_w

        @pl.loop(0, per_w // SC_CHUNK)
        def _(ci):
            off = pl.multiple_of(base + ci * SC_CHUNK, SC_CHUNK)
            pltpu.sync_copy(idx_hbm.at[pl.ds(off, SC_CHUNK)], idx_v)
            pltpu.async_copy(table_hbm.at[idx_v], rows_v, sem).wait()
            pltpu.sync_copy(rows_v, out_hbm.at[pl.ds(off, SC_CHUNK)])

    return gather(table, idx)


def _combine_kernel(x1_ref, ada_ref, yg_ref, g4_ref, fg_ref, o_ref, *, final):
    d = D_MODEL
    g4 = g4_ref[...]
    y = g4[:, 0:1] * yg_ref[0]
    for k in range(1, TOP_K):
        y = y + g4[:, k:k + 1] * yg_ref[k]
    x2 = x1_ref[...] + ada_ref[:, 5 * d:6 * d] * y
    if final:
        x2 = _rms(x2, fg_ref[...])
    o_ref[...] = x2


def _combine(x1, ada3, yg, g4, final_g, seg, final):
    t, d = x1.shape
    tm = TOKEN_TILE
    return pl.pallas_call(
        functools.partial(_combine_kernel, final=final),
        grid=(t // tm,),
        in_specs=[pl.BlockSpec((tm, d), lambda i: (i, 0)),
                  pl.BlockSpec((None, 1, 6 * d), lambda i: (seg(i), 0, 0)),
                  pl.BlockSpec((TOP_K, tm, d), lambda i: (0, i, 0)),
                  pl.BlockSpec((tm, TOP_K), lambda i: (i, 0)),
                  pl.BlockSpec((1, d), lambda i: (0, 0))],
        out_specs=pl.BlockSpec((tm, d), lambda i: (i, 0)),
        out_shape=jax.ShapeDtypeStruct((t, d), F32),
        compiler_params=_cparams(("arbitrary",)),
        name="combine",
    )(x1, ada3, yg, g4, final_g)


def _rope_tables(n_lat):
    rows = n_lat // GRID_W
    row_ids = jnp.repeat(jnp.arange(rows, dtype=F32), GRID_W)
    col_ids = jnp.tile(jnp.arange(GRID_W, dtype=F32), rows)

    def table(d_rot, lane0):
        d_axis = d_rot // 2
        nf = d_axis // 2
        inv_freq = ROPE_BASE ** (-jnp.arange(0, d_axis, 2, dtype=F32) / d_axis)
        ang_r = row_ids[:, None] * inv_freq
        ang_c = col_ids[:, None] * inv_freq
        ang = jnp.concatenate([ang_r, ang_r, ang_c, ang_c], axis=-1)
        sign = jnp.tile(jnp.concatenate([-jnp.ones((nf,), F32), jnp.ones((nf,), F32)]), 2)
        cos = jnp.ones((n_lat, LANE), F32)
        sin = jnp.zeros((n_lat, LANE), F32)
        for l0 in lane0:
            cos = cos.at[:, l0:l0 + d_rot].set(jnp.cos(ang))
            sin = sin.at[:, l0:l0 + d_rot].set(jnp.sin(ang) * sign)
        ident = (jnp.ones((TOKEN_TILE, LANE), F32), jnp.zeros((TOKEN_TILE, LANE), F32))
        return jnp.concatenate([ident[0], cos], axis=0), jnp.concatenate([ident[1], sin], axis=0)

    ca, sa = table(HD_A, (0, HD_A))
    cb, sb = table(ROPE_B, (NOPE_B,))
    return ca, sa, cb, sb


def _prep_weights(w_in, sink, w_uq, w_ukv, lru_wa, lru_wx, lru_lam, w_pa):
    depth, d, _ = w_in.shape
    cuts = np.cumsum((512, 128, 128, Q_RANK, KV_RANK, ROPE_B, LRU_W, LRU_W, 3 * d))
    perm = np.array(HEAD_PERM_A)
    w_qa = w_in[:, :, :cuts[0]].reshape(depth, d, N_HEADS_A, HD_A)[:, :, perm].reshape(depth, d, 512)
    w_kr = jnp.pad(w_in[:, :, cuts[4]:cuts[5]], ((0, 0), (0, 0), (NOPE_B, LANE - NOPE_B - ROPE_B)))
    w_main = jnp.concatenate([w_qa, w_in[:, :, cuts[0]:cuts[4]], w_in[:, :, cuts[5]:cuts[7]], w_kr],
                             axis=-1).astype(BF16)
    w_g = w_in[:, :, cuts[7]:].astype(BF16)
    sink_p = sink[:, perm]
    w_pa_p = w_pa.reshape(depth, N_HEADS_A, HD_A, d)[:, perm].reshape(depth, 512, d).astype(BF16)
    hb = NOPE_B + ROPE_B
    w_uq_p = jnp.pad(w_uq.reshape(depth, Q_RANK, N_HEADS_B, hb),
                     ((0, 0), (0, 0), (0, 0), (0, HEAD_PAD_B - hb))).reshape(depth, Q_RANK, -1).astype(BF16)
    w_ukv4 = w_ukv.reshape(depth, KV_RANK, N_HEADS_B, NOPE_B + V_B)
    w_uk_p = jnp.pad(w_ukv4[..., :NOPE_B], ((0, 0), (0, 0), (0, 0), (0, HEAD_PAD_B - NOPE_B))
                     ).reshape(depth, KV_RANK, -1).astype(BF16)
    w_uv = w_ukv4[..., NOPE_B:].reshape(depth, KV_RANK, -1).astype(BF16)

    def block_diag(w):
        hpb = LRU_HEADS // 2
        blk = LRU_W // LRU_HEADS
        w = w.reshape(depth, 2, 2, hpb, blk, blk)
        eye = jnp.eye(hpb, dtype=w.dtype)
        out = jnp.einsum('ldghij,hk->ldghikj', w, eye)
        return out.reshape(depth, 2, 2, hpb * blk, hpb * blk).astype(BF16)

    sp = jax.nn.softplus(-lru_lam.astype(F32))
    return w_main, w_g, sink_p, w_pa_p, w_uq_p, w_uk_p, w_uv, block_diag(lru_wa), block_diag(lru_wx), sp


def kernel(x_prompt, x_sample, cache_k_a, cache_v_a, cache_ckv, cache_krope, state_lru, c, c_ctx, ln1_g, ln2_g,
           final_g, w_ada, b_ada, w_in, sink, g_q, w_uq, g_kv, w_ukv, conv_w, conv_b, lru_wa, lru_ba, lru_wx,
           lru_bx, lru_lam, w_pa, w_pb, w_pc, w_out, w_router, b_router, w_gu, b_gu, w_dn, b_dn):
    n_ctx, s_len, d = x_prompt.shape
    n_lat, n_len, _ = x_sample.shape
    depth = w_in.shape[0]
    past = cache_k_a.shape[2]
    t_ctx = n_ctx * s_len
    t_lat = n_lat * n_len
    t = t_ctx + t_lat
    tm = TOKEN_TILE
    ctx_tiles = t_ctx // tm
    lat_tiles = n_len // tm
    assert t_ctx % n_len == 0 and s_len % tm == 0 and n_len % tm == 0 and n_lat + 1 <= SUBLANE

    seg = lambda i: jnp.where(i < ctx_tiles, 0, 1 + (i - ctx_tiles) // lat_tiles)
    tab_blk = lambda i: jnp.where(i < ctx_tiles, 0, 1 + (i - ctx_tiles) % lat_tiles)

    cond8 = jnp.zeros((SUBLANE, d), F32).at[0].set(c_ctx).at[1:1 + n_lat].set(c)
    ada = _ada_all(cond8, w_ada, b_ada)
    tabs = _rope_tables(n_len)
    (w_main, w_g, sink_p, w_pa_p, w_uq_p, w_uk_p, w_uv, wa_bd, wx_bd, sp) = _prep_weights(
        w_in, sink, w_uq, w_ukv, lru_wa, lru_wx, lru_lam, w_pa)
    w_pb_b, w_pc_b, w_out_b = w_pb.astype(BF16), w_pc.astype(BF16), w_out.astype(BF16)
    w_r_hi = w_router.astype(BF16)
    w_r_cat = jnp.concatenate([w_r_hi, (w_router - w_r_hi.astype(F32)).astype(BF16)], axis=-1)
    cache_k2 = cache_k_a.reshape(n_lat, depth, past, N_KV_A * HD_A)
    cache_v2 = cache_v_a.reshape(n_lat, depth, past, N_KV_A * HD_A)
    cache_kr_pad = jnp.pad(cache_krope, ((0, 0), (0, 0), (0, 0), (NOPE_B, LANE - NOPE_B - ROPE_B)))
    h0_ctx = jnp.zeros((n_ctx, 2, LRU_W), F32)
    row2 = lambda a: a.reshape(1, -1)

    x = jnp.concatenate([x_prompt.reshape(t_ctx, d), x_sample.reshape(t_lat, d)], axis=0)
    ks_a, vs_a, ckvs, krs, lrus = [], [], [], [], []
    n_rows = t * TOP_K + N_EXPERTS * EXPERT_BLOCK
    n_blocks = n_rows // EXPERT_BLOCK
    for l in range(depth):
        ada3 = ada[l].reshape(SUBLANE, 1, 6 * d)
        qa, ka, va, qb, ckvn, kb, vb, kr, xc, yc = _in_proj(
            x, ada3, row2(ln1_g[l]), w_main[l], row2(g_q[l]), w_uq_p[l], row2(g_kv[l]), w_uk_p[l], w_uv[l],
            tabs, seg, tab_blk)
        ks_a.append(ka[:t_ctx].reshape(n_ctx, s_len, N_KV_A, HD_A))
        vs_a.append(va[:t_ctx].reshape(n_ctx, s_len, N_KV_A, HD_A))
        ckvs.append(ckvn[:t_ctx].reshape(n_ctx, s_len, KV_RANK))
        krs.append(kr[:t_ctx, NOPE_B:NOPE_B + ROPE_B].reshape(n_ctx, s_len, ROPE_B))

        oa_c = _gqa_ctx(qa, ka, va, sink_p[l], n_ctx, s_len)
        oa_l = _gqa_lat(qa, ka, va, cache_k2, cache_v2, sink_p[l], l, n_lat, n_len, t_ctx)
        kbx, vbx = _cache_kv(cache_ckv, cache_kr_pad, w_uk_p[l], w_uv[l], l)
        ob_c = _mla_ctx(qb, kb, vb, n_ctx, s_len)
        ob_l = _mla_lat(qb, kb, vb, kbx, vbx, n_lat, n_len, t_ctx)
        lru_args = (conv_w[l], row2(conv_b[l]), wa_bd[l], lru_ba[l], wx_bd[l], lru_bx[l], sp[l])
        oc_c, st_c = _lru(xc, yc, h0_ctx, *lru_args, n_ctx, s_len, 0)
        oc_l, _ = _lru(xc, yc, state_lru[:, l], *lru_args, n_lat, n_len, t_ctx)
        lrus.append(st_c)

        oa = jnp.concatenate([oa_c, oa_l], axis=0)
        ob = jnp.concatenate([ob_c, ob_l], axis=0)
        oc = jnp.concatenate([oc_c, oc_l], axis=0)
        x1, h, e4, g4, r4, cnt = _merge(x, ada3, oa, ob, oc, row2(ln1_g[l]), w_g[l], w_pa_p[l], w_pb_b[l],
                                        w_pc_b[l], w_out_b[l], row2(ln2_g[l]), w_r_cat[l], row2(b_router[l]), seg)

        counts = cnt[0].astype(jnp.int32)
        padded = (counts + EXPERT_BLOCK - 1) // EXPERT_BLOCK * EXPERT_BLOCK
        pad_end = jnp.cumsum(padded)
        pad_start = pad_end - padded
        dest4 = pad_start[e4] + r4
        blk_row0 = jnp.arange(n_blocks, dtype=jnp.int32) * EXPERT_BLOCK
        block_e = jnp.minimum(jnp.sum((pad_end[None, :] <= blk_row0[:, None]).astype(jnp.int32), axis=1),
                              N_EXPERTS - 1)
        n_used = (pad_end[-1:] // EXPERT_BLOCK).astype(jnp.int32)
        rows_tok = jnp.zeros((n_rows,), jnp.int32).at[dest4.reshape(-1)].set(
            jnp.repeat(jnp.arange(t, dtype=jnp.int32), TOP_K), mode='promise_in_bounds', unique_indices=True)
        xb = _sc_gather(h, rows_tok)
        yb = _experts(xb, block_e, n_used, w_gu, b_gu, w_dn, b_dn, l)
        yg = _sc_gather(yb, dest4.T.reshape(-1)).reshape(TOP_K, t, d)
        x = _combine(x1, ada3, yg, g4, row2(final_g), seg, l == depth - 1)

    y_prompt = x[:t_ctx].reshape(n_ctx, s_len, d)
    y_sample = x[t_ctx:].reshape(n_lat, n_len, d)
    return (y_prompt, y_sample, jnp.stack(ks_a, axis=1), jnp.stack(vs_a, axis=1), jnp.stack(ckvs, axis=1),
            jnp.stack(krs, axis=1), jnp.stack(lrus, axis=1))
```

```python
import functools

import numpy as np
import jax
import jax.numpy as jnp
from jax import lax
from jax.experimental import pallas as pl
from jax.experimental.pallas import tpu as pltpu
from jax.experimental.pallas import tpu_sc as plsc

F32 = jnp.float32
BF16 = jnp.bfloat16

D_MODEL = 1024
GRID_W = 64
EPS = 1e-6
ROPE_BASE = 10000.0
NEG_INF = -1e30
HD_A = 64
N_HEADS_A = 8
N_KV_A = 2
WINDOW = 128
N_HEADS_B = 8
NOPE_B = 64
ROPE_B = 32
V_B = 64
Q_RANK = 512
KV_RANK = 256
LRU_W = 512
LRU_HEADS = 8
CONV_W = 4
LRU_C = 8.0
N_EXPERTS = 32
TOP_K = 4
D_FF = D_MODEL
SWIGLU_LIMIT = 7.0
SWIGLU_ALPHA = 1.702

LANE = 128
SUBLANE = 8
TOKEN_TILE = 256
EXPERT_BLOCK = 512
HEAD_PAD_B = 128
LRU_HALF = LRU_W // 2
VMEM_LIMIT = 56 * 1024 * 1024
SC_CORES = 2
SC_SUBCORES = 16
SC_LANES = 16
SC_CHUNK = 32

HEAD_PERM_A = (0, 4, 1, 5, 2, 6, 3, 7)

C_QA, C_KA, C_VA, C_CQ, C_CKV, C_XC, C_YC, C_KR, C_END = 0, 512, 640, 768, 1280, 1536, 2048, 2560, 2688


def _cparams(sem):
    return pltpu.CompilerParams(dimension_semantics=sem, vmem_limit_bytes=VMEM_LIMIT)


def _rms(x, g):
    return x * lax.rsqrt(jnp.mean(x * x, axis=-1, keepdims=True) + EPS) * g


def _dot(a, b):
    return jnp.dot(a, b, preferred_element_type=F32)


def _dot_t(a, b):
    return lax.dot_general(a, b, (((1,), (1,)), ((), ())), preferred_element_type=F32)


def _ada_kernel(c_ref, w_ref, b_ref, o_ref):
    c = c_ref[...]
    s = c * jax.nn.sigmoid(c)
    o_ref[...] = _dot(s.astype(BF16), w_ref[...].astype(BF16)) + b_ref[...]


def _ada_all(cond8, w_ada, b_ada):
    depth, d, n6 = w_ada.shape
    nb = 1536
    return pl.pallas_call(
        _ada_kernel,
        grid=(depth, n6 // nb),
        in_specs=[pl.BlockSpec((SUBLANE, d), lambda l, j: (0, 0)),
                  pl.BlockSpec((None, d, nb), lambda l, j: (l, 0, j)),
                  pl.BlockSpec((None, 1, nb), lambda l, j: (l, 0, j))],
        out_specs=pl.BlockSpec((None, SUBLANE, nb), lambda l, j: (l, 0, j)),
        out_shape=jax.ShapeDtypeStruct((depth, SUBLANE, n6), F32),
        compiler_params=_cparams(("arbitrary", "arbitrary")),
        name="ada",
    )(cond8, w_ada, b_ada.reshape(depth, 1, n6))


def _swap_halves(x, half):
    n = x.shape[-1]
    lane = lax.broadcasted_iota(jnp.int32, x.shape, x.ndim - 1)
    first = (lane % (2 * half)) < half
    return jnp.where(first, pltpu.roll(x, n - half, x.ndim - 1), pltpu.roll(x, half, x.ndim - 1))


def _rope(x, cos, sin_signed, half):
    reps = x.shape[-1] // cos.shape[-1]
    if reps > 1:
        cos = jnp.concatenate([cos] * reps, axis=-1)
        sin_signed = jnp.concatenate([sin_signed] * reps, axis=-1)
    return x * cos + _swap_halves(x, half) * sin_signed


def _in_kernel(x_ref, ada_ref, ln1_ref, w_ref, gq_ref, wuq_ref, gkv_ref, wuk_ref, wuv_ref,
               ca_ref, sa_ref, cb_ref, sb_ref,
               qa_ref, ka_ref, va_ref, qb_ref, ckvn_ref, kb_ref, vb_ref, kr_ref, xc_ref, yc_ref):
    d = D_MODEL
    x = x_ref[...]
    shift = ada_ref[:, 0:d]
    scale = ada_ref[:, d:2 * d]
    u = _rms(x, ln1_ref[...]) * (1.0 + scale) + shift
    p = _dot(u.astype(BF16), w_ref[...])
    ca, sa, cb, sb = ca_ref[...], sa_ref[...], cb_ref[...], sb_ref[...]

    qa = _rope(p[:, C_QA:C_KA], ca, sa, HD_A // 4) * (HD_A ** -0.5)
    qa_ref[...] = qa.astype(BF16)
    ka_ref[...] = _rope(p[:, C_KA:C_VA], ca, sa, HD_A // 4)
    va_ref[...] = p[:, C_VA:C_CQ]

    cq = _rms(p[:, C_CQ:C_CKV], gq_ref[...])
    qb = _dot(cq.astype(BF16), wuq_ref[...])
    qb = _rope(qb, cb, sb, ROPE_B // 4) * ((NOPE_B + ROPE_B) ** -0.5)
    qb_ref[...] = qb.astype(BF16)

    ckvn = _rms(p[:, C_CKV:C_XC], gkv_ref[...])
    ckvn_ref[...] = ckvn
    ckvn_b = ckvn.astype(BF16)
    kr = _rope(p[:, C_KR:C_END], cb, sb, ROPE_B // 4)
    kr_ref[...] = kr
    kb = _dot(ckvn_b, wuk_ref[...]) + jnp.concatenate([kr] * N_HEADS_B, axis=-1)
    kb_ref[...] = kb.astype(BF16)
    vb_ref[...] = _dot(ckvn_b, wuv_ref[...]).astype(BF16)

    xc_ref[...] = p[:, C_XC:C_YC]
    yc_ref[...] = p[:, C_YC:C_KR]


def _in_proj(x, ada3, ln1, w_main, g_q, w_uq, g_kv, w_uk, w_uv, tabs, seg, tab_blk):
    t, d = x.shape
    tm = TOKEN_TILE
    row = lambda n: pl.BlockSpec((tm, n), lambda i: (i, 0))
    full = lambda a: pl.BlockSpec(a.shape, lambda i: (0,) * a.ndim)
    tab = pl.BlockSpec((tm, LANE), lambda i: (tab_blk(i), 0))
    sds = lambda n, dt: jax.ShapeDtypeStruct((t, n), dt)
    nb = N_HEADS_B * HEAD_PAD_B
    return pl.pallas_call(
        _in_kernel,
        grid=(t // tm,),
        in_specs=[row(d), pl.BlockSpec((None, 1, 6 * d), lambda i: (seg(i), 0, 0)), full(ln1), full(w_main),
                  full(g_q), full(w_uq), full(g_kv), full(w_uk), full(w_uv), tab, tab, tab, tab],
        out_specs=[row(512), row(128), row(128), row(nb), row(KV_RANK), row(nb), row(512), row(128),
                   row(LRU_W), row(LRU_W)],
        out_shape=[sds(512, BF16), sds(128, F32), sds(128, F32), sds(nb, BF16), sds(KV_RANK, F32),
                   sds(nb, BF16), sds(512, BF16), sds(128, F32), sds(LRU_W, F32), sds(LRU_W, F32)],
        compiler_params=_cparams(("arbitrary",)),
        name="in_proj",
    )(x, ada3, ln1, w_main, g_q, w_uq, g_kv, w_uk, w_uv, *tabs)


def _gqa_heads(q, k_all, v_all, bias, sink_ref, o_ref):
    lane = lax.broadcasted_iota(jnp.int32, (q.shape[0], LANE), 1)
    low = lane < HD_A
    for pair in range(N_HEADS_A // 2):
        qp = q[:, pair * LANE:(pair + 1) * LANE]
        outs = []
        for half in range(2):
            qm = jnp.where(low if half == 0 else ~low, qp, jnp.zeros_like(qp))
            s = _dot_t(qm, k_all)
            if bias is not None:
                s = s + bias
            sink = sink_ref[2 * pair + half]
            m = jnp.maximum(jnp.max(s, axis=-1, keepdims=True), sink)
            e = jnp.exp(s - m)
            l = jnp.sum(e, axis=-1, keepdims=True) + jnp.exp(sink - m)
            outs.append(_dot(e.astype(BF16), v_all) / l)
        o_ref[:, pair * LANE:(pair + 1) * LANE] = jnp.where(low, outs[0], outs[1]).astype(o_ref.dtype)


def _gqa_ctx_kernel(sink_ref, q_ref, k_ref, v_ref, o_ref):
    _gqa_heads(q_ref[...], k_ref[...].astype(BF16), v_ref[...].astype(BF16), None, sink_ref, o_ref)


def _gqa_ctx(qa, ka, va, sink_p, n_seq, s_len):
    return pl.pallas_call(
        _gqa_ctx_kernel,
        grid=(n_seq,),
        in_specs=[pl.BlockSpec(memory_space=pltpu.SMEM),
                  pl.BlockSpec((s_len, 512), lambda b: (b, 0)),
                  pl.BlockSpec((s_len, LANE), lambda b: (b, 0)),
                  pl.BlockSpec((s_len, LANE), lambda b: (b, 0))],
        out_specs=pl.BlockSpec((s_len, 512), lambda b: (b, 0)),
        out_shape=jax.ShapeDtypeStruct((n_seq * s_len, 512), BF16),
        compiler_params=_cparams(("arbitrary",)),
        name="gqa_ctx",
    )(sink_p, qa, ka, va)


def _gqa_lat_kernel(sink_ref, q_ref, kp_ref, kc_ref, kn_ref, vp_ref, vc_ref, vn_ref, kx_ref, vx_ref, o_ref):
    j = pl.program_id(1)
    last = pl.num_programs(1) - 1
    w = WINDOW
    k_all = jnp.concatenate([kp_ref[...], kc_ref[...], kn_ref[...], kx_ref[...]], axis=0).astype(BF16)
    v_all = jnp.concatenate([vp_ref[...], vc_ref[...], vn_ref[...], vx_ref[...]], axis=0).astype(BF16)
    qi = lax.broadcasted_iota(jnp.int32, (w, w), 0)
    kj = lax.broadcasted_iota(jnp.int32, (w, w), 1)
    zero = jnp.zeros((w, w), F32)
    neg = jnp.full((w, w), NEG_INF, F32)
    b_prev = jnp.where(j > 0, jnp.where(kj >= qi, zero, neg), neg)
    b_next = jnp.where(j < last, jnp.where(kj <= qi, zero, neg), neg)
    bias = jnp.concatenate([b_prev, zero, b_next, jnp.zeros((w, kx_ref.shape[0]), F32)], axis=1)
    _gqa_heads(q_ref[...], k_all, v_all, bias, sink_ref, o_ref)


def _gqa_lat(qa, ka, va, kx, vx, sink_p, layer, n_seq, n_len, row0):
    w = WINDOW
    nqb = n_len // w
    blk0 = row0 // w
    past = kx.shape[2]
    cur = lambda b, j: (blk0 + b * nqb + j, 0)
    prev = lambda b, j: (blk0 + b * nqb + jnp.maximum(j - 1, 0), 0)
    nxt = lambda b, j: (blk0 + b * nqb + jnp.minimum(j + 1, nqb - 1), 0)
    cache = pl.BlockSpec((None, None, past, LANE), lambda b, j: (b, layer, 0, 0))
    return pl.pallas_call(
        _gqa_lat_kernel,
        grid=(n_seq, nqb),
        in_specs=[pl.BlockSpec(memory_space=pltpu.SMEM),
                  pl.BlockSpec((w, 512), cur),
                  pl.BlockSpec((w, LANE), prev), pl.BlockSpec((w, LANE), cur), pl.BlockSpec((w, LANE), nxt),
                  pl.BlockSpec((w, LANE), prev), pl.BlockSpec((w, LANE), cur), pl.BlockSpec((w, LANE), nxt),
                  cache, cache],
        out_specs=pl.BlockSpec((w, 512), lambda b, j: (b * nqb + j, 0)),
        out_shape=jax.ShapeDtypeStruct((n_seq * n_len, 512), BF16),
        compiler_params=_cparams(("arbitrary", "arbitrary")),
        name="gqa_lat",
    )(sink_p, qa, ka, ka, ka, va, va, va, kx, vx)


def _mla_kernel(*refs, n_src, chunk):
    q_ref = refs[0]
    kv_refs = refs[1:1 + 2 * n_src]
    o_ref = refs[-1]
    qb = q_ref.shape[0]
    lane = lax.broadcasted_iota(jnp.int32, (qb, LANE), 1)
    outs = []
    for hh in range(2):
        q = q_ref[:, hh * HEAD_PAD_B:(hh + 1) * HEAD_PAD_B]
        m = jnp.full((qb, 1), NEG_INF, F32)
        l = jnp.zeros((qb, 1), F32)
        acc = jnp.zeros((qb, LANE), F32)
        for s_i in range(n_src):
            k_ref, v_ref = kv_refs[2 * s_i], kv_refs[2 * s_i + 1]
            nk = k_ref.shape[0]
            for c in range(nk // chunk):
                k = k_ref[c * chunk:(c + 1) * chunk, hh * HEAD_PAD_B:(hh + 1) * HEAD_PAD_B]
                v = v_ref[c * chunk:(c + 1) * chunk, :]
                s = _dot_t(q, k)
                m_new = jnp.maximum(m, jnp.max(s, axis=-1, keepdims=True))
                alpha = jnp.exp(m - m_new)
                e = jnp.exp(s - m_new)
                l = alpha * l + jnp.sum(e, axis=-1, keepdims=True)
                acc = alpha * acc + _dot(e.astype(BF16), v)
                m = m_new
        outs.append(acc / l)
    o_ref[...] = jnp.where(lane < V_B, outs[0], outs[1]).astype(o_ref.dtype)


def _mla_ctx(qb, kb, vb, n_seq, s_len):
    npair = N_HEADS_B // 2
    return pl.pallas_call(
        functools.partial(_mla_kernel, n_src=1, chunk=s_len),
        grid=(n_seq, npair),
        in_specs=[pl.BlockSpec((s_len, 2 * HEAD_PAD_B), lambda b, p: (b, p)),
                  pl.BlockSpec((s_len, 2 * HEAD_PAD_B), lambda b, p: (b, p)),
                  pl.BlockSpec((s_len, LANE), lambda b, p: (b, p))],
        out_specs=pl.BlockSpec((s_len, LANE), lambda b, p: (b, p)),
        out_shape=jax.ShapeDtypeStruct((n_seq * s_len, 512), BF16),
        compiler_params=_cparams(("arbitrary", "arbitrary")),
        name="mla_ctx",
    )(qb, kb, vb)


def _mla_lat(qb, kb, vb, kbx, vbx, n_seq, n_len, row0):
    npair = N_HEADS_B // 2
    qblk = 256
    nqb = n_len // qblk
    past = kbx.shape[0] // n_seq
    qmap = lambda b, p, j: (row0 // qblk + b * nqb + j, p)
    return pl.pallas_call(
        functools.partial(_mla_kernel, n_src=2, chunk=512),
        grid=(n_seq, npair, nqb),
        in_specs=[pl.BlockSpec((qblk, 2 * HEAD_PAD_B), qmap),
                  pl.BlockSpec((n_len, 2 * HEAD_PAD_B), lambda b, p, j: (row0 // n_len + b, p)),
                  pl.BlockSpec((n_len, LANE), lambda b, p, j: (row0 // n_len + b, p)),
                  pl.BlockSpec((past, 2 * HEAD_PAD_B), lambda b, p, j: (b, p)),
                  pl.BlockSpec((past, LANE), lambda b, p, j: (b, p))],
        out_specs=pl.BlockSpec((qblk, LANE), lambda b, p, j: (b * nqb + j, p)),
        out_shape=jax.ShapeDtypeStruct((n_seq * n_len, 512), BF16),
        compiler_params=_cparams(("arbitrary", "arbitrary", "arbitrary")),
        name="mla_lat",
    )(qb, kb, vb, kbx, vbx)


def _cache_kv_kernel(ckv_ref, kr_ref, wuk_ref, wuv_ref, kb_ref, vb_ref):
    c = ckv_ref[...].astype(BF16)
    kb = _dot(c, wuk_ref[...]) + jnp.concatenate([kr_ref[...]] * N_HEADS_B, axis=-1)
    kb_ref[...] = kb.astype(BF16)
    vb_ref[...] = _dot(c, wuv_ref[...]).astype(BF16)


def _cache_kv(cache_ckv, cache_kr_pad, w_uk, w_uv, layer):
    n_seq, _, past, _ = cache_ckv.shape
    nb = N_HEADS_B * HEAD_PAD_B
    return pl.pallas_call(
        _cache_kv_kernel,
        grid=(n_seq,),
        in_specs=[pl.BlockSpec((None, None, past, KV_RANK), lambda b: (b, layer, 0, 0)),
                  pl.BlockSpec((None, None, past, LANE), lambda b: (b, layer, 0, 0)),
                  pl.BlockSpec(w_uk.shape, lambda b: (0, 0)),
                  pl.BlockSpec(w_uv.shape, lambda b: (0, 0))],
        out_specs=[pl.BlockSpec((past, nb), lambda b: (b, 0)), pl.BlockSpec((past, 512), lambda b: (b, 0))],
        out_shape=[jax.ShapeDtypeStruct((n_seq * past, nb), BF16), jax.ShapeDtypeStruct((n_seq * past, 512), BF16)],
        compiler_params=_cparams(("arbitrary",)),
        name="cache_kv",
    )(cache_ckv, cache_kr_pad, w_uk, w_uv)


def _lru_kernel(xc_ref, yc_ref, h0_ref, cw_ref, cb_ref, wa_ref, ba_ref, wx_ref, bx_ref, sp_ref,
                o_ref, st_ref, pad_ref, xcv_ref, a_ref, b_ref, hs_ref, *, chunk):
    n = xc_ref.shape[0]
    halo = SUBLANE
    pad_ref[0:halo, :] = jnp.zeros((halo, LRU_W), F32)
    pad_ref[halo + n:2 * halo + n, :] = jnp.zeros((halo, LRU_W), F32)
    pad_ref[halo:halo + n, :] = xc_ref[...]
    left = CONV_W // 2
    for c in range(n // chunk):
        r0 = c * chunk
        acc = jnp.broadcast_to(cb_ref[...], (chunk, LRU_W))
        for j in range(CONV_W):
            off = halo + r0 + j - left
            acc = acc + cw_ref[j:j + 1, :] * pad_ref[off:off + chunk, :]
        xcv_ref[r0:r0 + chunk, :] = acc

    row = lax.broadcasted_iota(jnp.int32, (SUBLANE, LRU_W), 0)
    for d in range(2):
        for c in range(n // chunk):
            r0 = c * chunk
            xv = xcv_ref[r0:r0 + chunk, :]
            xb = xv.astype(BF16)
            for hf in range(2):
                cs = slice(hf * LRU_HALF, (hf + 1) * LRU_HALF)
                r = jax.nn.sigmoid(_dot(xb[:, cs], wa_ref[d, hf]) + ba_ref[d:d + 1, cs])
                i = jax.nn.sigmoid(_dot(xb[:, cs], wx_ref[d, hf]) + bx_ref[d:d + 1, cs])
                log_a = (-LRU_C) * r * sp_ref[d:d + 1, cs]
                a = jnp.exp(log_a)
                a_ref[r0:r0 + chunk, cs] = a
                b_ref[r0:r0 + chunk, cs] = jnp.sqrt(-jnp.tanh(log_a) * (a * a + 1.0)) * (i * xv[:, cs])

        def body(g, h, d=d):
            grp = g if d == 0 else n // SUBLANE - 1 - g
            rows = pl.ds(pl.multiple_of(grp * SUBLANE, SUBLANE), SUBLANE)
            a = a_ref[rows, :]
            b = b_ref[rows, :]
            for sh in (1, 2, 4):
                if d == 0:
                    keep = row >= sh
                    a_s = jnp.where(keep, pltpu.roll(a, sh, 0), 1.0)
                    b_s = jnp.where(keep, pltpu.roll(b, sh, 0), 0.0)
                else:
                    keep = row < SUBLANE - sh
                    a_s = jnp.where(keep, pltpu.roll(a, SUBLANE - sh, 0), 1.0)
                    b_s = jnp.where(keep, pltpu.roll(b, SUBLANE - sh, 0), 0.0)
                b = a * b_s + b
                a = a * a_s
            hrows = a * h + b
            if d == 0:
                hs_ref[rows, :] = hrows
                return hrows[SUBLANE - 1:SUBLANE, :]
            hs_ref[rows, :] = hs_ref[rows, :] + hrows
            return hrows[0:1, :]

        h_fin = lax.fori_loop(0, n // SUBLANE, body, h0_ref[d:d + 1, :])
        st_ref[d:d + 1, :] = h_fin

    for c in range(n // chunk):
        r0 = c * chunk
        o_ref[r0:r0 + chunk, :] = (hs_ref[r0:r0 + chunk, :] * jax.nn.gelu(yc_ref[r0:r0 + chunk, :])).astype(o_ref.dtype)


def _lru(xc, yc, h0, conv_w, conv_b, wa, ba, wx, bx, sp, n_seq, n_len, row0):
    blk0 = row0 // n_len
    full = lambda a: pl.BlockSpec(a.shape, lambda b: (0,) * a.ndim)
    seq = pl.BlockSpec((n_len, LRU_W), lambda b: (blk0 + b, 0))
    return pl.pallas_call(
        functools.partial(_lru_kernel, chunk=min(n_len, 256)),
        grid=(n_seq,),
        in_specs=[seq, seq, pl.BlockSpec((None, 2, LRU_W), lambda b: (b, 0, 0)),
                  full(conv_w), full(conv_b), full(wa), full(ba), full(wx), full(bx), full(sp)],
        out_specs=[pl.BlockSpec((n_len, LRU_W), lambda b: (b, 0)),
                   pl.BlockSpec((None, 2, LRU_W), lambda b: (b, 0, 0))],
        out_shape=[jax.ShapeDtypeStruct((n_seq * n_len, LRU_W), BF16),
                   jax.ShapeDtypeStruct((n_seq, 2, LRU_W), F32)],
        scratch_shapes=[pltpu.VMEM((n_len + 2 * SUBLANE, LRU_W), F32), pltpu.VMEM((n_len, LRU_W), F32),
                        pltpu.VMEM((n_len, LRU_W), F32), pltpu.VMEM((n_len, LRU_W), F32),
                        pltpu.VMEM((n_len, LRU_W), F32)],
        compiler_params=_cparams(("arbitrary",)),
        name="lru",
    )(xc, yc, h0, conv_w, conv_b, wa, ba, wx, bx, sp)


def _merge_kernel(x_ref, ada_ref, oac_ref, oal_ref, obc_ref, obl_ref, occ_ref, ocl_ref, ln1_ref, wg_ref, wpa_ref,
                  wpb_ref, wpc_ref, wout_ref, ln2_ref, wr_ref, br_ref,
                  x1_ref, h_ref, e4_ref, g4_ref, r4_ref, cnt_ref, carry_ref, *, ctx_tiles):
    d = D_MODEL
    tm = x_ref.shape[0]
    i = pl.program_id(0)
    is_ctx = i < ctx_tiles
    oa = jnp.where(is_ctx, oac_ref[...], oal_ref[...])
    ob = jnp.where(is_ctx, obc_ref[...], obl_ref[...])
    oc = jnp.where(is_ctx, occ_ref[...], ocl_ref[...])

    @pl.when(i == 0)
    def _():
        carry_ref[...] = jnp.zeros_like(carry_ref)

    x = x_ref[...]
    ada = ada_ref[...]
    u = _rms(x, ln1_ref[...]) * (1.0 + ada[:, d:2 * d]) + ada[:, 0:d]
    g = jax.nn.sigmoid(_dot(u.astype(BF16), wg_ref[...]))
    m = (g[:, 0:d] * _dot(oa, wpa_ref[...]) + g[:, d:2 * d] * _dot(ob, wpb_ref[...])
         + g[:, 2 * d:3 * d] * _dot(oc, wpc_ref[...]))
    x1 = x + ada[:, 2 * d:3 * d] * _dot(m.astype(BF16), wout_ref[...])
    x1_ref[...] = x1
    h = _rms(x1, ln2_ref[...]) * (1.0 + ada[:, 4 * d:5 * d]) + ada[:, 3 * d:4 * d]
    h_ref[...] = h

    h_hi = h.astype(BF16)
    h_lo = (h - h_hi.astype(F32)).astype(BF16)
    hw = _dot(h_hi, wr_ref[...])
    logits = (hw[:, :N_EXPERTS] + hw[:, N_EXPERTS:] + _dot(h_lo, wr_ref[:, :N_EXPERTS])) + br_ref[...]
    col = lax.broadcasted_iota(jnp.int32, (tm, N_EXPERTS), 1).astype(F32)
    col4 = lax.broadcasted_iota(jnp.int32, (tm, TOP_K), 1)
    sel_any = jnp.zeros((tm, N_EXPERTS), F32)
    vals, idxs = [], []
    work = logits
    for _k in range(TOP_K):
        mx = jnp.max(work, axis=-1, keepdims=True)
        idx = jnp.min(jnp.where(work == mx, col, float(N_EXPERTS)), axis=-1, keepdims=True)
        sel = col == idx
        vals.append(mx)
        idxs.append(idx)
        sel_any = jnp.where(sel, 1.0, sel_any)
        work = jnp.where(sel, -jnp.inf, work)

    ri = lax.broadcasted_iota(jnp.int32, (tm, tm), 0)
    ci = lax.broadcasted_iota(jnp.int32, (tm, tm), 1)
    tri = jnp.where(ri > ci, 1.0, 0.0).astype(BF16)
    before = _dot(tri, sel_any.astype(BF16)) + carry_ref[...]
    carry = carry_ref[...] + jnp.sum(sel_any, axis=0, keepdims=True)
    carry_ref[...] = carry
    cnt_ref[...] = jnp.broadcast_to(carry, cnt_ref.shape)

    exps = [jnp.exp(v - vals[0]) for v in vals]
    den = exps[0] + exps[1] + exps[2] + exps[3]
    e4 = jnp.zeros((tm, TOP_K), jnp.int32)
    g4 = jnp.zeros((tm, TOP_K), F32)
    r4 = jnp.zeros((tm, TOP_K), jnp.int32)
    for k in range(TOP_K):
        rank = jnp.sum(jnp.where(col == idxs[k], before, 0.0), axis=-1, keepdims=True)
        e4 = jnp.where(col4 == k, idxs[k].astype(jnp.int32), e4)
        g4 = jnp.where(col4 == k, exps[k] / den, g4)
        r4 = jnp.where(col4 == k, rank.astype(jnp.int32), r4)
    e4_ref[...] = e4
    g4_ref[...] = g4
    r4_ref[...] = r4


def _merge(x, ada3, o_ctx, o_lat, ln1, w_g, w_pa, w_pb, w_pc, w_out, ln2, w_r, b_r, seg, ctx_tiles):
    t, d = x.shape
    tm = TOKEN_TILE
    row = lambda n: pl.BlockSpec((tm, n), lambda i: (i, 0))
    full = lambda a: pl.BlockSpec(a.shape, lambda i: (0,) * a.ndim)
    ctx = pl.BlockSpec((tm, 512), lambda i: (jnp.minimum(i, ctx_tiles - 1), 0))
    lat = pl.BlockSpec((tm, 512), lambda i: (jnp.maximum(i - ctx_tiles, 0), 0))
    return pl.pallas_call(
        functools.partial(_merge_kernel, ctx_tiles=ctx_tiles),
        grid=(t // tm,),
        in_specs=[row(d), pl.BlockSpec((None, 1, 6 * d), lambda i: (seg(i), 0, 0)), ctx, lat, ctx, lat, ctx, lat,
                  full(ln1), full(w_g), full(w_pa), full(w_pb), full(w_pc), full(w_out), full(ln2), full(w_r),
                  full(b_r)],
        out_specs=[row(d), row(d), row(TOP_K), row(TOP_K), row(TOP_K),
                   pl.BlockSpec((SUBLANE, N_EXPERTS), lambda i: (0, 0))],
        out_shape=[jax.ShapeDtypeStruct((t, d), F32), jax.ShapeDtypeStruct((t, d), F32),
                   jax.ShapeDtypeStruct((t, TOP_K), jnp.int32), jax.ShapeDtypeStruct((t, TOP_K), F32),
                   jax.ShapeDtypeStruct((t, TOP_K), jnp.int32),
                   jax.ShapeDtypeStruct((SUBLANE, N_EXPERTS), F32)],
        scratch_shapes=[pltpu.VMEM((1, N_EXPERTS), F32)],
        compiler_params=_cparams(("arbitrary",)),
        name="merge",
    )(x, ada3, o_ctx[0], o_lat[0], o_ctx[1], o_lat[1], o_ctx[2], o_lat[2], ln1, w_g, w_pa, w_pb, w_pc, w_out, ln2,
      w_r, b_r)


def _expert_kernel(be_ref, nu_ref, x_ref, wgu_ref, bgu_ref, wdn_ref, bdn_ref, y_ref, wgu_s, wdn_s, *, n_chunk):
    b = pl.program_id(0)
    used = b < nu_ref[0]
    new_expert = jnp.logical_or(b == 0, be_ref[b] != be_ref[jnp.maximum(b - 1, 0)])

    @pl.when(jnp.logical_and(used, new_expert))
    def _():
        wgu_s[...] = wgu_ref[...].astype(BF16)
        wdn_s[...] = wdn_ref[...].astype(BF16)

    @pl.when(used)
    def _():
        x = x_ref[...].astype(BF16)
        cw = D_FF // n_chunk
        acc = jnp.broadcast_to(bdn_ref[...], y_ref.shape)
        for c in range(n_chunk):
            glu = _dot(x, wgu_s[:, c * cw:(c + 1) * cw]) + bgu_ref[:, c * cw:(c + 1) * cw]
            lin = (_dot(x, wgu_s[:, D_FF + c * cw:D_FF + (c + 1) * cw])
                   + bgu_ref[:, D_FF + c * cw:D_FF + (c + 1) * cw])
            glu = jnp.minimum(glu, SWIGLU_LIMIT)
            lin = jnp.clip(lin, -SWIGLU_LIMIT, SWIGLU_LIMIT)
            act = glu * jax.nn.sigmoid(SWIGLU_ALPHA * glu) * (lin + 1.0)
            acc = acc + _dot(act.astype(BF16), wdn_s[c * cw:(c + 1) * cw, :])
        y_ref[...] = acc

    @pl.when(jnp.logical_not(used))
    def _():
        y_ref[...] = jnp.zeros_like(y_ref)


def _experts(xb, block_e, n_used, w_gu, b_gu, w_dn, b_dn, layer):
    n_rows, d = xb.shape
    depth = w_gu.shape[0]
    blk = EXPERT_BLOCK
    n_blocks = n_rows // blk
    wmap = lambda b, be, nu: (layer, be[b], 0, 0)
    grid_spec = pltpu.PrefetchScalarGridSpec(
        num_scalar_prefetch=2,
        grid=(n_blocks,),
        in_specs=[pl.BlockSpec((blk, d), lambda b, be, nu: (b, 0)),
                  pl.BlockSpec((None, None, d, 2 * D_FF), wmap),
                  pl.BlockSpec((None, None, 1, 2 * D_FF), wmap),
                  pl.BlockSpec((None, None, D_FF, d), wmap),
                  pl.BlockSpec((None, None, 1, d), wmap)],
        out_specs=pl.BlockSpec((blk, d), lambda b, be, nu: (b, 0)),
        scratch_shapes=[pltpu.VMEM((d, 2 * D_FF), BF16), pltpu.VMEM((D_FF, d), BF16)],
    )
    return pl.pallas_call(
        functools.partial(_expert_kernel, n_chunk=4),
        grid_spec=grid_spec,
        out_shape=jax.ShapeDtypeStruct((n_rows, d), F32),
        compiler_params=_cparams(("arbitrary",)),
        name="experts",
    )(block_e, n_used, xb, w_gu, b_gu.reshape(depth, N_EXPERTS, 1, 2 * D_FF), w_dn,
      b_dn.reshape(depth, N_EXPERTS, 1, d))


def _sc_worker_base(per_w):
    return (lax.axis_index("s") * SC_CORES + lax.axis_index("c")) * per_w


def _sc_gather_rows(table_hbm, out_hbm, idx_v, rows_v, gsem, osem, base, n_chunks):
    ch = SC_CHUNK

    def gather(c, slot):
        return pltpu.make_async_copy(table_hbm.at[idx_v.at[pl.ds(c * ch, ch)]], rows_v.at[slot], gsem.at[slot])

    def put(c, slot):
        return pltpu.make_async_copy(rows_v.at[slot], out_hbm.at[pl.ds(base + c * ch, ch)], osem.at[slot])

    gather(0, 0).start()

    @pl.loop(0, n_chunks // 2)
    def _(i):
        c = 2 * i
        gather(c, 0).wait()
        put(c, 0).start()

        @pl.when(i > 0)
        def _():
            put(c - 1, 1).wait()

        gather(c + 1, 1).start()
        gather(c + 1, 1).wait()
        put(c + 1, 1).start()
        put(c, 0).wait()

        @pl.when(c + 2 < n_chunks)
        def _():
            gather(c + 2, 0).start()

    put(n_chunks - 1, 1).wait()


def _sc_scratch(per_w, d, dtype):
    return [pltpu.VMEM((per_w,), jnp.int32), pltpu.VMEM((2, SC_CHUNK, d), dtype),
            pltpu.SemaphoreType.DMA((2,)), pltpu.SemaphoreType.DMA((2,))]


def _sc_gather(table, idx):
    n_idx = idx.shape[0]
    d = table.shape[1]
    n_workers = SC_CORES * SC_SUBCORES
    per_w = n_idx // n_workers
    assert n_idx % (n_workers * SC_CHUNK * 2) == 0
    mesh = plsc.VectorSubcoreMesh(core_axis_name="c", subcore_axis_name="s")

    @functools.partial(
        pl.kernel, mesh=mesh, out_type=jax.ShapeDtypeStruct((n_idx, d), table.dtype),
        scratch_types=_sc_scratch(per_w, d, table.dtype), name="sc_gather")
    def gather(table_hbm, idx_hbm, out_hbm, idx_v, rows_v, gsem, osem):
        base = _sc_worker_base(per_w)
        pltpu.sync_copy(idx_hbm.at[pl.ds(base, per_w)], idx_v)
        _sc_gather_rows(table_hbm, out_hbm, idx_v, rows_v, gsem, osem, base, per_w // SC_CHUNK)

    return gather(table, idx)


def _sc_dispatch(h, dest, n_rows):
    t, d = h.shape
    n_slots = dest.shape[0]
    n_workers = SC_CORES * SC_SUBCORES
    per_w = n_rows // n_workers
    piece = 4096
    fill_shift = 3
    assert n_rows % (n_workers * SC_CHUNK * 2) == 0 and n_slots % piece == 0 and (n_rows >> fill_shift) <= t
    mesh = plsc.VectorSubcoreMesh(core_axis_name="c", subcore_axis_name="s")

    @functools.partial(
        pl.kernel, mesh=mesh, out_type=jax.ShapeDtypeStruct((n_rows, d), h.dtype),
        scratch_types=[pltpu.VMEM((piece,), jnp.int32)] + _sc_scratch(per_w, d, h.dtype),
        compiler_params=pltpu.CompilerParams(needs_layout_passes=False), name="sc_dispatch")
    def dispatch(h_hbm, dest_hbm, out_hbm, dest_v, idx_v, rows_v, gsem, osem):
        base = _sc_worker_base(per_w)
        lane = lax.iota(jnp.int32, SC_LANES)

        @pl.loop(0, per_w // SC_LANES)
        def _(j):
            idx_v[pl.ds(j * SC_LANES, SC_LANES)] = (base + j * SC_LANES + lane) >> fill_shift

        @pl.loop(0, n_slots // piece)
        def _(p):
            pltpu.sync_copy(dest_hbm.at[pl.ds(p * piece, piece)], dest_v)

            @pl.loop(0, piece // SC_LANES)
            def _(j):
                loc = dest_v[pl.ds(j * SC_LANES, SC_LANES)] - base
                mine = (loc >= 0) & (loc < per_w)
                tok = (p * piece + j * SC_LANES + lane) // TOP_K
                plsc.store_scatter(idx_v, [loc], tok, mask=mine)

        _sc_gather_rows(h_hbm, out_hbm, idx_v, rows_v, gsem, osem, base, per_w // SC_CHUNK)

    return dispatch(h, dest)


def _combine_kernel(x1_ref, ada_ref, yg_ref, g4_ref, fg_ref, o_ref, *, final):
    d = D_MODEL
    g4 = g4_ref[...]
    y = g4[:, 0:1] * yg_ref[0]
    for k in range(1, TOP_K):
        y = y + g4[:, k:k + 1] * yg_ref[k]
    x2 = x1_ref[...] + ada_ref[:, 5 * d:6 * d] * y
    if final:
        x2 = _rms(x2, fg_ref[...])
    o_ref[...] = x2


def _combine(x1, ada3, yg, g4, final_g, seg, final):
    t, d = x1.shape
    tm = TOKEN_TILE
    return pl.pallas_call(
        functools.partial(_combine_kernel, final=final),
        grid=(t // tm,),
        in_specs=[pl.BlockSpec((tm, d), lambda i: (i, 0)),
                  pl.BlockSpec((None, 1, 6 * d), lambda i: (seg(i), 0, 0)),
                  pl.BlockSpec((TOP_K, tm, d), lambda i: (0, i, 0)),
                  pl.BlockSpec((tm, TOP_K), lambda i: (i, 0)),
                  pl.BlockSpec((1, d), lambda i: (0, 0))],
        out_specs=pl.BlockSpec((tm, d), lambda i: (i, 0)),
        out_shape=jax.ShapeDtypeStruct((t, d), F32),
        compiler_params=_cparams(("arbitrary",)),
        name="combine",
    )(x1, ada3, yg, g4, final_g)


def _rope_tables(n_lat):
    rows = n_lat // GRID_W
    row_ids = jnp.repeat(jnp.arange(rows, dtype=F32), GRID_W)
    col_ids = jnp.tile(jnp.arange(GRID_W, dtype=F32), rows)

    def table(d_rot, lane0):
        d_axis = d_rot // 2
        nf = d_axis // 2
        inv_freq = ROPE_BASE ** (-jnp.arange(0, d_axis, 2, dtype=F32) / d_axis)
        ang_r = row_ids[:, None] * inv_freq
        ang_c = col_ids[:, None] * inv_freq
        ang = jnp.concatenate([ang_r, ang_r, ang_c, ang_c], axis=-1)
        sign = jnp.tile(jnp.concatenate([-jnp.ones((nf,), F32), jnp.ones((nf,), F32)]), 2)
        cos = jnp.ones((n_lat, LANE), F32)
        sin = jnp.zeros((n_lat, LANE), F32)
        for l0 in lane0:
            cos = cos.at[:, l0:l0 + d_rot].set(jnp.cos(ang))
            sin = sin.at[:, l0:l0 + d_rot].set(jnp.sin(ang) * sign)
        ident = (jnp.ones((TOKEN_TILE, LANE), F32), jnp.zeros((TOKEN_TILE, LANE), F32))
        return jnp.concatenate([ident[0], cos], axis=0), jnp.concatenate([ident[1], sin], axis=0)

    ca, sa = table(HD_A, (0, HD_A))
    cb, sb = table(ROPE_B, (NOPE_B,))
    return ca, sa, cb, sb


def _prep_weights(w_in, sink, w_uq, w_ukv, lru_wa, lru_wx, lru_lam, w_pa):
    depth, d, _ = w_in.shape
    cuts = np.cumsum((512, 128, 128, Q_RANK, KV_RANK, ROPE_B, LRU_W, LRU_W, 3 * d))
    perm = np.array(HEAD_PERM_A)
    w_qa = w_in[:, :, :cuts[0]].reshape(depth, d, N_HEADS_A, HD_A)[:, :, perm].reshape(depth, d, 512)
    w_kr = jnp.pad(w_in[:, :, cuts[4]:cuts[5]], ((0, 0), (0, 0), (NOPE_B, LANE - NOPE_B - ROPE_B)))
    w_main = jnp.concatenate([w_qa, w_in[:, :, cuts[0]:cuts[4]], w_in[:, :, cuts[5]:cuts[7]], w_kr],
                             axis=-1).astype(BF16)
    w_g = w_in[:, :, cuts[7]:].astype(BF16)
    sink_p = sink[:, perm]
    w_pa_p = w_pa.reshape(depth, N_HEADS_A, HD_A, d)[:, perm].reshape(depth, 512, d).astype(BF16)
    hb = NOPE_B + ROPE_B
    w_uq_p = jnp.pad(w_uq.reshape(depth, Q_RANK, N_HEADS_B, hb),
                     ((0, 0), (0, 0), (0, 0), (0, HEAD_PAD_B - hb))).reshape(depth, Q_RANK, -1).astype(BF16)
    w_ukv4 = w_ukv.reshape(depth, KV_RANK, N_HEADS_B, NOPE_B + V_B)
    w_uk_p = jnp.pad(w_ukv4[..., :NOPE_B], ((0, 0), (0, 0), (0, 0), (0, HEAD_PAD_B - NOPE_B))
                     ).reshape(depth, KV_RANK, -1).astype(BF16)
    w_uv = w_ukv4[..., NOPE_B:].reshape(depth, KV_RANK, -1).astype(BF16)

    def block_diag(w):
        hpb = LRU_HEADS // 2
        blk = LRU_W // LRU_HEADS
        w = w.reshape(depth, 2, 2, hpb, blk, blk)
        eye = jnp.eye(hpb, dtype=w.dtype)
        out = jnp.einsum('ldghij,hk->ldghikj', w, eye)
        return out.reshape(depth, 2, 2, hpb * blk, hpb * blk).astype(BF16)

    sp = jax.nn.softplus(-lru_lam.astype(F32))
    return w_main, w_g, sink_p, w_pa_p, w_uq_p, w_uk_p, w_uv, block_diag(lru_wa), block_diag(lru_wx), sp


def kernel(x_prompt, x_sample, cache_k_a, cache_v_a, cache_ckv, cache_krope, state_lru, c, c_ctx, ln1_g, ln2_g,
           final_g, w_ada, b_ada, w_in, sink, g_q, w_uq, g_kv, w_ukv, conv_w, conv_b, lru_wa, lru_ba, lru_wx,
           lru_bx, lru_lam, w_pa, w_pb, w_pc, w_out, w_router, b_router, w_gu, b_gu, w_dn, b_dn):
    n_ctx, s_len, d = x_prompt.shape
    n_lat, n_len, _ = x_sample.shape
    depth = w_in.shape[0]
    past = cache_k_a.shape[2]
    t_ctx = n_ctx * s_len
    t_lat = n_lat * n_len
    t = t_ctx + t_lat
    tm = TOKEN_TILE
    ctx_tiles = t_ctx // tm
    lat_tiles = n_len // tm
    assert t_ctx % n_len == 0 and s_len % tm == 0 and n_len % tm == 0 and n_lat + 1 <= SUBLANE

    seg = lambda i: jnp.where(i < ctx_tiles, 0, 1 + (i - ctx_tiles) // lat_tiles)
    tab_blk = lambda i: jnp.where(i < ctx_tiles, 0, 1 + (i - ctx_tiles) % lat_tiles)

    cond8 = jnp.zeros((SUBLANE, d), F32).at[0].set(c_ctx).at[1:1 + n_lat].set(c)
    ada = _ada_all(cond8, w_ada, b_ada)
    tabs = _rope_tables(n_len)
    (w_main, w_g, sink_p, w_pa_p, w_uq_p, w_uk_p, w_uv, wa_bd, wx_bd, sp) = _prep_weights(
        w_in, sink, w_uq, w_ukv, lru_wa, lru_wx, lru_lam, w_pa)
    w_pb_b, w_pc_b, w_out_b = w_pb.astype(BF16), w_pc.astype(BF16), w_out.astype(BF16)
    w_r_hi = w_router.astype(BF16)
    w_r_cat = jnp.concatenate([w_r_hi, (w_router - w_r_hi.astype(F32)).astype(BF16)], axis=-1)
    cache_k2 = cache_k_a.reshape(n_lat, depth, past, N_KV_A * HD_A)
    cache_v2 = cache_v_a.reshape(n_lat, depth, past, N_KV_A * HD_A)
    cache_kr_pad = jnp.pad(cache_krope, ((0, 0), (0, 0), (0, 0), (NOPE_B, LANE - NOPE_B - ROPE_B)))
    h0_ctx = jnp.zeros((n_ctx, 2, LRU_W), F32)
    row2 = lambda a: a.reshape(1, -1)

    x = jnp.concatenate([x_prompt.reshape(t_ctx, d), x_sample.reshape(t_lat, d)], axis=0)
    ks_a, vs_a, ckvs, krs, lrus = [], [], [], [], []
    n_rows = t * TOP_K + N_EXPERTS * EXPERT_BLOCK
    n_blocks = n_rows // EXPERT_BLOCK
    for l in range(depth):
        ada3 = ada[l].reshape(SUBLANE, 1, 6 * d)
        qa, ka, va, qb, ckvn, kb, vb, kr, xc, yc = _in_proj(
            x, ada3, row2(ln1_g[l]), w_main[l], row2(g_q[l]), w_uq_p[l], row2(g_kv[l]), w_uk_p[l], w_uv[l],
            tabs, seg, tab_blk)
        ks_a.append(ka[:t_ctx].reshape(n_ctx, s_len, N_KV_A, HD_A))
        vs_a.append(va[:t_ctx].reshape(n_ctx, s_len, N_KV_A, HD_A))
        ckvs.append(ckvn[:t_ctx].reshape(n_ctx, s_len, KV_RANK))
        krs.append(kr[:t_ctx, NOPE_B:NOPE_B + ROPE_B].reshape(n_ctx, s_len, ROPE_B))

        oa_c = _gqa_ctx(qa, ka, va, sink_p[l], n_ctx, s_len)
        oa_l = _gqa_lat(qa, ka, va, cache_k2, cache_v2, sink_p[l], l, n_lat, n_len, t_ctx)
        kbx, vbx = _cache_kv(cache_ckv, cache_kr_pad, w_uk_p[l], w_uv[l], l)
        ob_c = _mla_ctx(qb, kb, vb, n_ctx, s_len)
        ob_l = _mla_lat(qb, kb, vb, kbx, vbx, n_lat, n_len, t_ctx)
        lru_args = (conv_w[l], row2(conv_b[l]), wa_bd[l], lru_ba[l], wx_bd[l], lru_bx[l], sp[l])
        oc_c, st_c = _lru(xc, yc, h0_ctx, *lru_args, n_ctx, s_len, 0)
        oc_l, _ = _lru(xc, yc, state_lru[:, l], *lru_args, n_lat, n_len, t_ctx)
        lrus.append(st_c)
        x1, h, e4, g4, r4, cnt = _merge(x, ada3, (oa_c, ob_c, oc_c), (oa_l, ob_l, oc_l), row2(ln1_g[l]), w_g[l],
                                        w_pa_p[l], w_pb_b[l], w_pc_b[l], w_out_b[l], row2(ln2_g[l]), w_r_cat[l],
                                        row2(b_router[l]), seg, ctx_tiles)

        counts = cnt[0].astype(jnp.int32)
        padded = (counts + EXPERT_BLOCK - 1) // EXPERT_BLOCK * EXPERT_BLOCK
        pad_end = jnp.cumsum(padded)
        pad_start = pad_end - padded
        dest4 = pad_start[e4] + r4
        blk_row0 = jnp.arange(n_blocks, dtype=jnp.int32) * EXPERT_BLOCK
        block_e = jnp.minimum(jnp.sum((pad_end[None, :] <= blk_row0[:, None]).astype(jnp.int32), axis=1),
                              N_EXPERTS - 1)
        n_used = (pad_end[-1:] // EXPERT_BLOCK).astype(jnp.int32)
        xb = _sc_dispatch(h, dest4.reshape(-1), n_rows)
        yb = _experts(xb, block_e, n_used, w_gu, b_gu, w_dn, b_dn, l)
        yg = _sc_gather(yb, dest4.T.reshape(-1)).reshape(TOP_K, t, d)
        x = _combine(x1, ada3, yg, g4, row2(final_g), seg, l == depth - 1)

    y_prompt = x[:t_ctx].reshape(n_ctx, s_len, d)
    y_sample = x[t_ctx:].reshape(n_lat, n_len, d)
    return (y_prompt, y_sample, jnp.stack(ks_a, axis=1), jnp.stack(vs_a, axis=1), jnp.stack(ckvs, axis=1),
            jnp.stack(krs, axis=1), jnp.stack(lrus, axis=1))
```

```python
import functools

import numpy as np
import jax
import jax.numpy as jnp
from jax import lax
from jax.experimental import pallas as pl
from jax.experimental.pallas import tpu as pltpu
from jax.experimental.pallas import tpu_sc as plsc

F32 = jnp.float32
BF16 = jnp.bfloat16

D_MODEL = 1024
GRID_W = 64
EPS = 1e-6
ROPE_BASE = 10000.0
NEG_INF = -1e30
HD_A = 64
N_HEADS_A = 8
N_KV_A = 2
WINDOW = 128
N_HEADS_B = 8
NOPE_B = 64
ROPE_B = 32
V_B = 64
Q_RANK = 512
KV_RANK = 256
LRU_W = 512
LRU_HEADS = 8
CONV_W = 4
LRU_C = 8.0
N_EXPERTS = 32
TOP_K = 4
D_FF = D_MODEL
SWIGLU_LIMIT = 7.0
SWIGLU_ALPHA = 1.702

LANE = 128
SUBLANE = 8
TOKEN_TILE = 512
MLA_Q_BLOCK = 1024
EXPERT_BLOCK = 512
HEAD_PAD_B = 128
LRU_HALF = LRU_W // 2
VMEM_LIMIT = 56 * 1024 * 1024
SC_CORES = 2
SC_SUBCORES = 16
SC_LANES = 16
SC_CHUNK = 64

HEAD_PERM_A = (0, 4, 1, 5, 2, 6, 3, 7)

C_QA, C_KA, C_VA, C_CQ, C_CKV, C_XC, C_YC, C_KR, C_END = 0, 512, 640, 768, 1280, 1536, 2048, 2560, 2688


def _cparams(sem):
    return pltpu.CompilerParams(dimension_semantics=sem, vmem_limit_bytes=VMEM_LIMIT)


def _rms(x, g):
    return x * lax.rsqrt(jnp.mean(x * x, axis=-1, keepdims=True) + EPS) * g


def _dot(a, b):
    return jnp.dot(a, b, preferred_element_type=F32)


def _dot_t(a, b):
    return lax.dot_general(a, b, (((1,), (1,)), ((), ())), preferred_element_type=F32)


def _pack_bf16_pairs(x):
    n = x.shape[1] // 2
    bits = lambda v: pltpu.bitcast(v.astype(BF16).astype(F32), jnp.uint32)
    return pltpu.bitcast((bits(x[:, :n]) >> 16) | (bits(x[:, n:]) & jnp.uint32(0xFFFF0000)), jnp.int32)


def _unpack_bf16_pairs(p):
    p = pltpu.bitcast(p, jnp.uint32)
    lo = pltpu.bitcast(p << 16, F32)
    hi = pltpu.bitcast(p & jnp.uint32(0xFFFF0000), F32)
    return jnp.concatenate([lo, hi], axis=1)


def _ada_kernel(c_ref, w_ref, b_ref, o_ref):
    c = c_ref[...]
    s = c * jax.nn.sigmoid(c)
    o_ref[...] = _dot(s.astype(BF16), w_ref[...].astype(BF16)) + b_ref[...]


def _ada_all(cond8, w_ada, b_ada):
    depth, d, n6 = w_ada.shape
    nb = 1536
    return pl.pallas_call(
        _ada_kernel,
        grid=(depth, n6 // nb),
        in_specs=[pl.BlockSpec((SUBLANE, d), lambda l, j: (0, 0)),
                  pl.BlockSpec((None, d, nb), lambda l, j: (l, 0, j)),
                  pl.BlockSpec((None, 1, nb), lambda l, j: (l, 0, j))],
        out_specs=pl.BlockSpec((None, SUBLANE, nb), lambda l, j: (l, 0, j)),
        out_shape=jax.ShapeDtypeStruct((depth, SUBLANE, n6), F32),
        compiler_params=_cparams(("arbitrary", "arbitrary")),
        name="ada",
    )(cond8, w_ada, b_ada.reshape(depth, 1, n6))


def _swap_halves(x, half):
    n = x.shape[-1]
    lane = lax.broadcasted_iota(jnp.int32, x.shape, x.ndim - 1)
    first = (lane % (2 * half)) < half
    return jnp.where(first, pltpu.roll(x, n - half, x.ndim - 1), pltpu.roll(x, half, x.ndim - 1))


def _rope(x, cos, sin_signed, half):
    reps = x.shape[-1] // cos.shape[-1]
    if reps > 1:
        cos = jnp.concatenate([cos] * reps, axis=-1)
        sin_signed = jnp.concatenate([sin_signed] * reps, axis=-1)
    return x * cos + _swap_halves(x, half) * sin_signed


def _in_kernel(x_ref, ada_ref, ln1_ref, w_ref, gq_ref, wuq_ref, gkv_ref, wuk_ref, wuv_ref,
               ca_ref, sa_ref, cb_ref, sb_ref,
               qa_ref, ka_ref, va_ref, qb_ref, ckvn_ref, kb_ref, vb_ref, kr_ref, xc_ref, yc_ref):
    d = D_MODEL
    x = x_ref[...]
    shift = ada_ref[:, 0:d]
    scale = ada_ref[:, d:2 * d]
    u = _rms(x, ln1_ref[...]) * (1.0 + scale) + shift
    p = _dot(u.astype(BF16), w_ref[...])
    ca, sa, cb, sb = ca_ref[...], sa_ref[...], cb_ref[...], sb_ref[...]

    qa = _rope(p[:, C_QA:C_KA], ca, sa, HD_A // 4) * (HD_A ** -0.5)
    qa_ref[...] = qa.astype(BF16)
    ka_ref[...] = _rope(p[:, C_KA:C_VA], ca, sa, HD_A // 4)
    va_ref[...] = p[:, C_VA:C_CQ]

    cq = _rms(p[:, C_CQ:C_CKV], gq_ref[...])
    qb = _dot(cq.astype(BF16), wuq_ref[...])
    qb = _rope(qb, cb, sb, ROPE_B // 4) * ((NOPE_B + ROPE_B) ** -0.5)
    qb_ref[...] = qb.astype(BF16)

    ckvn = _rms(p[:, C_CKV:C_XC], gkv_ref[...])
    ckvn_ref[...] = ckvn
    ckvn_b = ckvn.astype(BF16)
    kr = _rope(p[:, C_KR:C_END], cb, sb, ROPE_B // 4)
    kr_ref[...] = kr
    kb = _dot(ckvn_b, wuk_ref[...]) + jnp.concatenate([kr] * N_HEADS_B, axis=-1)
    kb_ref[...] = kb.astype(BF16)
    vb_ref[...] = _dot(ckvn_b, wuv_ref[...]).astype(BF16)

    xc_ref[...] = p[:, C_XC:C_YC]
    yc_ref[...] = p[:, C_YC:C_KR]


def _in_proj(x, ada3, ln1, w_main, g_q, w_uq, g_kv, w_uk, w_uv, tabs, seg, tab_blk):
    t, d = x.shape
    tm = TOKEN_TILE
    row = lambda n: pl.BlockSpec((tm, n), lambda i: (i, 0))
    full = lambda a: pl.BlockSpec(a.shape, lambda i: (0,) * a.ndim)
    tab = pl.BlockSpec((tm, LANE), lambda i: (tab_blk(i), 0))
    sds = lambda n, dt: jax.ShapeDtypeStruct((t, n), dt)
    nb = N_HEADS_B * HEAD_PAD_B
    return pl.pallas_call(
        _in_kernel,
        grid=(t // tm,),
        in_specs=[row(d), pl.BlockSpec((None, 1, 6 * d), lambda i: (seg(i), 0, 0)), full(ln1), full(w_main),
                  full(g_q), full(w_uq), full(g_kv), full(w_uk), full(w_uv), tab, tab, tab, tab],
        out_specs=[row(512), row(128), row(128), row(nb), row(KV_RANK), row(nb), row(512), row(128),
                   row(LRU_W), row(LRU_W)],
        out_shape=[sds(512, BF16), sds(128, F32), sds(128, F32), sds(nb, BF16), sds(KV_RANK, F32),
                   sds(nb, BF16), sds(512, BF16), sds(128, F32), sds(LRU_W, F32), sds(LRU_W, F32)],
        compiler_params=_cparams(("arbitrary",)),
        name="in_proj",
    )(x, ada3, ln1, w_main, g_q, w_uq, g_kv, w_uk, w_uv, *tabs)


def _gqa_heads(q, k_all, v_all, bias, sink_ref, o_ref):
    qb = q.shape[0]
    npair = N_HEADS_A // 2
    lane = lax.broadcasted_iota(jnp.int32, (qb, LANE), 1)
    low = lane < HD_A
    pairs = [q[:, p * LANE:(p + 1) * LANE] for p in range(npair)]
    outs = []
    for g in range(N_KV_A):
        keep = low if g == 0 else ~low
        qg = jnp.concatenate([jnp.where(keep, qp, jnp.zeros_like(qp)) for qp in pairs], axis=0)
        s = _dot_t(qg, k_all)
        if bias is not None:
            s = s + jnp.concatenate([bias] * npair, axis=0)
        sink = jnp.concatenate([jnp.full((qb, 1), sink_ref[2 * p + g], F32) for p in range(npair)], axis=0)
        m = jnp.maximum(jnp.max(s, axis=-1, keepdims=True), sink)
        e = jnp.exp(s - m)
        l = jnp.sum(e, axis=-1, keepdims=True) + jnp.exp(sink - m)
        outs.append(_dot(e.astype(BF16), v_all) / l)
    for p in range(npair):
        rows = slice(p * qb, (p + 1) * qb)
        o_ref[:, p * LANE:(p + 1) * LANE] = jnp.where(low, outs[0][rows], outs[1][rows]).astype(o_ref.dtype)


def _gqa_ctx_kernel(sink_ref, q_ref, k_ref, v_ref, o_ref):
    _gqa_heads(q_ref[...], k_ref[...].astype(BF16), v_ref[...].astype(BF16), None, sink_ref, o_ref)


def _gqa_ctx(qa, ka, va, sink_p, n_seq, s_len):
    return pl.pallas_call(
        _gqa_ctx_kernel,
        grid=(n_seq,),
        in_specs=[pl.BlockSpec(memory_space=pltpu.SMEM),
                  pl.BlockSpec((s_len, 512), lambda b: (b, 0)),
                  pl.BlockSpec((s_len, LANE), lambda b: (b, 0)),
                  pl.BlockSpec((s_len, LANE), lambda b: (b, 0))],
        out_specs=pl.BlockSpec((s_len, 512), lambda b: (b, 0)),
        out_shape=jax.ShapeDtypeStruct((n_seq * s_len, 512), BF16),
        compiler_params=_cparams(("arbitrary",)),
        name="gqa_ctx",
    )(sink_p, qa, ka, va)


def _gqa_lat_kernel(sink_ref, q_ref, kp_ref, kc_ref, kn_ref, vp_ref, vc_ref, vn_ref, kx_ref, vx_ref, o_ref):
    j = pl.program_id(1)
    last = pl.num_programs(1) - 1
    w = WINDOW
    k_all = jnp.concatenate([kp_ref[...], kc_ref[...], kn_ref[...], kx_ref[...]], axis=0).astype(BF16)
    v_all = jnp.concatenate([vp_ref[...], vc_ref[...], vn_ref[...], vx_ref[...]], axis=0).astype(BF16)
    qi = lax.broadcasted_iota(jnp.int32, (w, w), 0)
    kj = lax.broadcasted_iota(jnp.int32, (w, w), 1)
    zero = jnp.zeros((w, w), F32)
    neg = jnp.full((w, w), NEG_INF, F32)
    b_prev = jnp.where(j > 0, jnp.where(kj >= qi, zero, neg), neg)
    b_next = jnp.where(j < last, jnp.where(kj <= qi, zero, neg), neg)
    bias = jnp.concatenate([b_prev, zero, b_next, jnp.zeros((w, kx_ref.shape[0]), F32)], axis=1)
    _gqa_heads(q_ref[...], k_all, v_all, bias, sink_ref, o_ref)


def _gqa_lat(qa, ka, va, kx, vx, sink_p, layer, n_seq, n_len, row0):
    w = WINDOW
    nqb = n_len // w
    blk0 = row0 // w
    past = kx.shape[2]
    cur = lambda b, j: (blk0 + b * nqb + j, 0)
    prev = lambda b, j: (blk0 + b * nqb + jnp.maximum(j - 1, 0), 0)
    nxt = lambda b, j: (blk0 + b * nqb + jnp.minimum(j + 1, nqb - 1), 0)
    cache = pl.BlockSpec((None, None, past, LANE), lambda b, j: (b, layer, 0, 0))
    return pl.pallas_call(
        _gqa_lat_kernel,
        grid=(n_seq, nqb),
        in_specs=[pl.BlockSpec(memory_space=pltpu.SMEM),
                  pl.BlockSpec((w, 512), cur),
                  pl.BlockSpec((w, LANE), prev), pl.BlockSpec((w, LANE), cur), pl.BlockSpec((w, LANE), nxt),
                  pl.BlockSpec((w, LANE), prev), pl.BlockSpec((w, LANE), cur), pl.BlockSpec((w, LANE), nxt),
                  cache, cache],
        out_specs=pl.BlockSpec((w, 512), lambda b, j: (b * nqb + j, 0)),
        out_shape=jax.ShapeDtypeStruct((n_seq * n_len, 512), BF16),
        compiler_params=_cparams(("arbitrary", "arbitrary")),
        name="gqa_lat",
    )(sink_p, qa, ka, ka, ka, va, va, va, kx, vx)


def _mla_kernel(*refs, n_src, chunk):
    q_ref = refs[0]
    kv_refs = refs[1:1 + 2 * n_src]
    o_ref = refs[-1]
    qb = q_ref.shape[0]
    lane = lax.broadcasted_iota(jnp.int32, (qb, LANE), 1)
    outs = []
    for hh in range(2):
        q = q_ref[:, hh * HEAD_PAD_B:(hh + 1) * HEAD_PAD_B]
        m = jnp.full((qb, 1), NEG_INF, F32)
        l = jnp.zeros((qb, 1), F32)
        acc = jnp.zeros((qb, LANE), F32)
        for s_i in range(n_src):
            k_ref, v_ref = kv_refs[2 * s_i], kv_refs[2 * s_i + 1]
            nk = k_ref.shape[0]
            for c in range(nk // chunk):
                k = k_ref[c * chunk:(c + 1) * chunk, hh * HEAD_PAD_B:(hh + 1) * HEAD_PAD_B]
                v = v_ref[c * chunk:(c + 1) * chunk, :]
                s = _dot_t(q, k)
                m_new = jnp.maximum(m, jnp.max(s, axis=-1, keepdims=True))
                alpha = jnp.exp(m - m_new)
                e = jnp.exp(s - m_new)
                l = alpha * l + jnp.sum(e, axis=-1, keepdims=True)
                acc = alpha * acc + _dot(e.astype(BF16), v)
                m = m_new
        outs.append(acc / l)
    o_ref[...] = jnp.where(lane < V_B, outs[0], outs[1]).astype(o_ref.dtype)


def _mla_ctx(qb, kb, vb, n_seq, s_len):
    npair = N_HEADS_B // 2
    return pl.pallas_call(
        functools.partial(_mla_kernel, n_src=1, chunk=s_len),
        grid=(n_seq, npair),
        in_specs=[pl.BlockSpec((s_len, 2 * HEAD_PAD_B), lambda b, p: (b, p)),
                  pl.BlockSpec((s_len, 2 * HEAD_PAD_B), lambda b, p: (b, p)),
                  pl.BlockSpec((s_len, LANE), lambda b, p: (b, p))],
        out_specs=pl.BlockSpec((s_len, LANE), lambda b, p: (b, p)),
        out_shape=jax.ShapeDtypeStruct((n_seq * s_len, 512), BF16),
        compiler_params=_cparams(("arbitrary", "arbitrary")),
        name="mla_ctx",
    )(qb, kb, vb)


def _mla_lat(qb, kb, vb, kbx, vbx, n_seq, n_len, row0):
    npair = N_HEADS_B // 2
    qblk = min(MLA_Q_BLOCK, n_len)
    nqb = n_len // qblk
    past = kbx.shape[0] // n_seq
    qmap = lambda b, p, j: (row0 // qblk + b * nqb + j, p)
    return pl.pallas_call(
        functools.partial(_mla_kernel, n_src=2, chunk=512),
        grid=(n_seq, npair, nqb),
        in_specs=[pl.BlockSpec((qblk, 2 * HEAD_PAD_B), qmap),
                  pl.BlockSpec((n_len, 2 * HEAD_PAD_B), lambda b, p, j: (row0 // n_len + b, p)),
                  pl.BlockSpec((n_len, LANE), lambda b, p, j: (row0 // n_len + b, p)),
                  pl.BlockSpec((past, 2 * HEAD_PAD_B), lambda b, p, j: (b, p)),
                  pl.BlockSpec((past, LANE), lambda b, p, j: (b, p))],
        out_specs=pl.BlockSpec((qblk, LANE), lambda b, p, j: (b * nqb + j, p)),
        out_shape=jax.ShapeDtypeStruct((n_seq * n_len, 512), BF16),
        compiler_params=_cparams(("arbitrary", "arbitrary", "arbitrary")),
        name="mla_lat",
    )(qb, kb, vb, kbx, vbx)


def _cache_kv_kernel(ckv_ref, kr_ref, wuk_ref, wuv_ref, kb_ref, vb_ref):
    c = ckv_ref[...].astype(BF16)
    kb = _dot(c, wuk_ref[...]) + jnp.concatenate([kr_ref[...]] * N_HEADS_B, axis=-1)
    kb_ref[...] = kb.astype(BF16)
    vb_ref[...] = _dot(c, wuv_ref[...]).astype(BF16)


def _cache_kv(cache_ckv, cache_kr_pad, w_uk, w_uv, layer):
    n_seq, _, past, _ = cache_ckv.shape
    nb = N_HEADS_B * HEAD_PAD_B
    return pl.pallas_call(
        _cache_kv_kernel,
        grid=(n_seq,),
        in_specs=[pl.BlockSpec((None, None, past, KV_RANK), lambda b: (b, layer, 0, 0)),
                  pl.BlockSpec((None, None, past, LANE), lambda b: (b, layer, 0, 0)),
                  pl.BlockSpec(w_uk.shape, lambda b: (0, 0)),
                  pl.BlockSpec(w_uv.shape, lambda b: (0, 0))],
        out_specs=[pl.BlockSpec((past, nb), lambda b: (b, 0)), pl.BlockSpec((past, 512), lambda b: (b, 0))],
        out_shape=[jax.ShapeDtypeStruct((n_seq * past, nb), BF16), jax.ShapeDtypeStruct((n_seq * past, 512), BF16)],
        compiler_params=_cparams(("arbitrary",)),
        name="cache_kv",
    )(cache_ckv, cache_kr_pad, w_uk, w_uv)


def _lru_kernel(xc_ref, yc_ref, h0_ref, cw_ref, cb_ref, wa_ref, ba_ref, wx_ref, bx_ref, sp_ref,
                o_ref, st_ref, pad_ref, xcv_ref, a_ref, b_ref, hs_ref, *, chunk):
    n = xc_ref.shape[0]
    halo = SUBLANE
    pad_ref[0:halo, :] = jnp.zeros((halo, LRU_W), F32)
    pad_ref[halo + n:2 * halo + n, :] = jnp.zeros((halo, LRU_W), F32)
    pad_ref[halo:halo + n, :] = xc_ref[...]
    left = CONV_W // 2
    for c in range(n // chunk):
        r0 = c * chunk
        acc = jnp.broadcast_to(cb_ref[...], (chunk, LRU_W))
        for j in range(CONV_W):
            off = halo + r0 + j - left
            acc = acc + cw_ref[j:j + 1, :] * pad_ref[off:off + chunk, :]
        xcv_ref[r0:r0 + chunk, :] = acc

    row = lax.broadcasted_iota(jnp.int32, (SUBLANE, LRU_W), 0)
    for d in range(2):
        for c in range(n // chunk):
            r0 = c * chunk
            xv = xcv_ref[r0:r0 + chunk, :]
            xb = xv.astype(BF16)
            for hf in range(2):
                cs = slice(hf * LRU_HALF, (hf + 1) * LRU_HALF)
                r = jax.nn.sigmoid(_dot(xb[:, cs], wa_ref[d, hf]) + ba_ref[d:d + 1, cs])
                i = jax.nn.sigmoid(_dot(xb[:, cs], wx_ref[d, hf]) + bx_ref[d:d + 1, cs])
                log_a = (-LRU_C) * r * sp_ref[d:d + 1, cs]
                a = jnp.exp(log_a)
                a_ref[r0:r0 + chunk, cs] = a
                b_ref[r0:r0 + chunk, cs] = jnp.sqrt(-jnp.tanh(log_a) * (a * a + 1.0)) * (i * xv[:, cs])

        def body(g, h, d=d):
            grp = g if d == 0 else n // SUBLANE - 1 - g
            rows = pl.ds(pl.multiple_of(grp * SUBLANE, SUBLANE), SUBLANE)
            a = a_ref[rows, :]
            b = b_ref[rows, :]
            for sh in (1, 2, 4):
                if d == 0:
                    keep = row >= sh
                    a_s = jnp.where(keep, pltpu.roll(a, sh, 0), 1.0)
                    b_s = jnp.where(keep, pltpu.roll(b, sh, 0), 0.0)
                else:
                    keep = row < SUBLANE - sh
                    a_s = jnp.where(keep, pltpu.roll(a, SUBLANE - sh, 0), 1.0)
                    b_s = jnp.where(keep, pltpu.roll(b, SUBLANE - sh, 0), 0.0)
                b = a * b_s + b
                a = a * a_s
            hrows = a * h + b
            if d == 0:
                hs_ref[rows, :] = hrows
                return hrows[SUBLANE - 1:SUBLANE, :]
            hs_ref[rows, :] = hs_ref[rows, :] + hrows
            return hrows[0:1, :]

        h_fin = lax.fori_loop(0, n // SUBLANE, body, h0_ref[d:d + 1, :])
        st_ref[d:d + 1, :] = h_fin

    for c in range(n // chunk):
        r0 = c * chunk
        o_ref[r0:r0 + chunk, :] = (hs_ref[r0:r0 + chunk, :] * jax.nn.gelu(yc_ref[r0:r0 + chunk, :])).astype(o_ref.dtype)


def _lru(xc, yc, h0, conv_w, conv_b, wa, ba, wx, bx, sp, n_seq, n_len, row0):
    blk0 = row0 // n_len
    full = lambda a: pl.BlockSpec(a.shape, lambda b: (0,) * a.ndim)
    seq = pl.BlockSpec((n_len, LRU_W), lambda b: (blk0 + b, 0))
    return pl.pallas_call(
        functools.partial(_lru_kernel, chunk=min(n_len, 256)),
        grid=(n_seq,),
        in_specs=[seq, seq, pl.BlockSpec((None, 2, LRU_W), lambda b: (b, 0, 0)),
                  full(conv_w), full(conv_b), full(wa), full(ba), full(wx), full(bx), full(sp)],
        out_specs=[pl.BlockSpec((n_len, LRU_W), lambda b: (b, 0)),
                   pl.BlockSpec((None, 2, LRU_W), lambda b: (b, 0, 0))],
        out_shape=[jax.ShapeDtypeStruct((n_seq * n_len, LRU_W), BF16),
                   jax.ShapeDtypeStruct((n_seq, 2, LRU_W), F32)],
        scratch_shapes=[pltpu.VMEM((n_len + 2 * SUBLANE, LRU_W), F32), pltpu.VMEM((n_len, LRU_W), F32),
                        pltpu.VMEM((n_len, LRU_W), F32), pltpu.VMEM((n_len, LRU_W), F32),
                        pltpu.VMEM((n_len, LRU_W), F32)],
        compiler_params=_cparams(("arbitrary",)),
        name="lru",
    )(xc, yc, h0, conv_w, conv_b, wa, ba, wx, bx, sp)


def _merge_kernel(x_ref, ada_ref, oac_ref, oal_ref, obc_ref, obl_ref, occ_ref, ocl_ref, ln1_ref, wg_ref, wpa_ref,
                  wpb_ref, wpc_ref, wout_ref, ln2_ref, wr_ref, br_ref,
                  x1_ref, h_ref, e4_ref, g4_ref, r4_ref, cnt_ref, carry_ref, *, ctx_tiles):
    d = D_MODEL
    tm = x_ref.shape[0]
    i = pl.program_id(0)
    is_ctx = i < ctx_tiles
    oa = jnp.where(is_ctx, oac_ref[...], oal_ref[...])
    ob = jnp.where(is_ctx, obc_ref[...], obl_ref[...])
    oc = jnp.where(is_ctx, occ_ref[...], ocl_ref[...])

    @pl.when(i == 0)
    def _():
        carry_ref[...] = jnp.zeros_like(carry_ref)

    x = x_ref[...]
    ada = ada_ref[...]
    u = _rms(x, ln1_ref[...]) * (1.0 + ada[:, d:2 * d]) + ada[:, 0:d]
    g = jax.nn.sigmoid(_dot(u.astype(BF16), wg_ref[...]))
    m = (g[:, 0:d] * _dot(oa, wpa_ref[...]) + g[:, d:2 * d] * _dot(ob, wpb_ref[...])
         + g[:, 2 * d:3 * d] * _dot(oc, wpc_ref[...]))
    x1 = x + ada[:, 2 * d:3 * d] * _dot(m.astype(BF16), wout_ref[...])
    x1_ref[...] = x1
    h = _rms(x1, ln2_ref[...]) * (1.0 + ada[:, 4 * d:5 * d]) + ada[:, 3 * d:4 * d]
    h_ref[...] = _pack_bf16_pairs(h)

    h_hi = h.astype(BF16)
    h_lo = (h - h_hi.astype(F32)).astype(BF16)
    hw = _dot(h_hi, wr_ref[...])
    logits = (hw[:, :N_EXPERTS] + hw[:, N_EXPERTS:] + _dot(h_lo, wr_ref[:, :N_EXPERTS])) + br_ref[...]
    col = lax.broadcasted_iota(jnp.int32, (tm, N_EXPERTS), 1).astype(F32)
    col4 = lax.broadcasted_iota(jnp.int32, (tm, TOP_K), 1)
    sel_any = jnp.zeros((tm, N_EXPERTS), F32)
    vals, idxs = [], []
    work = logits
    for _k in range(TOP_K):
        mx = jnp.max(work, axis=-1, keepdims=True)
        idx = jnp.min(jnp.where(work == mx, col, float(N_EXPERTS)), axis=-1, keepdims=True)
        sel = col == idx
        vals.append(mx)
        idxs.append(idx)
        sel_any = jnp.where(sel, 1.0, sel_any)
        work = jnp.where(sel, -jnp.inf, work)

    ri = lax.broadcasted_iota(jnp.int32, (tm, tm), 0)
    ci = lax.broadcasted_iota(jnp.int32, (tm, tm), 1)
    tri = jnp.where(ri > ci, 1.0, 0.0).astype(BF16)
    before = _dot(tri, sel_any.astype(BF16)) + carry_ref[...]
    carry = carry_ref[...] + jnp.sum(sel_any, axis=0, keepdims=True)
    carry_ref[...] = carry
    cnt_ref[...] = jnp.broadcast_to(carry, cnt_ref.shape)

    exps = [jnp.exp(v - vals[0]) for v in vals]
    den = exps[0] + exps[1] + exps[2] + exps[3]
    e4 = jnp.zeros((tm, TOP_K), jnp.int32)
    g4 = jnp.zeros((tm, TOP_K), F32)
    r4 = jnp.zeros((tm, TOP_K), jnp.int32)
    for k in range(TOP_K):
        rank = jnp.sum(jnp.where(col == idxs[k], before, 0.0), axis=-1, keepdims=True)
        e4 = jnp.where(col4 == k, idxs[k].astype(jnp.int32), e4)
        g4 = jnp.where(col4 == k, exps[k] / den, g4)
        r4 = jnp.where(col4 == k, rank.astype(jnp.int32), r4)
    e4_ref[...] = e4
    g4_ref[...] = g4
    r4_ref[...] = r4


def _merge(x, ada3, o_ctx, o_lat, ln1, w_g, w_pa, w_pb, w_pc, w_out, ln2, w_r, b_r, seg, ctx_tiles):
    t, d = x.shape
    tm = TOKEN_TILE
    row = lambda n: pl.BlockSpec((tm, n), lambda i: (i, 0))
    full = lambda a: pl.BlockSpec(a.shape, lambda i: (0,) * a.ndim)
    ctx = pl.BlockSpec((tm, 512), lambda i: (jnp.minimum(i, ctx_tiles - 1), 0))
    lat = pl.BlockSpec((tm, 512), lambda i: (jnp.maximum(i - ctx_tiles, 0), 0))
    return pl.pallas_call(
        functools.partial(_merge_kernel, ctx_tiles=ctx_tiles),
        grid=(t // tm,),
        in_specs=[row(d), pl.BlockSpec((None, 1, 6 * d), lambda i: (seg(i), 0, 0)), ctx, lat, ctx, lat, ctx, lat,
                  full(ln1), full(w_g), full(w_pa), full(w_pb), full(w_pc), full(w_out), full(ln2), full(w_r),
                  full(b_r)],
        out_specs=[row(d), row(d // 2), row(TOP_K), row(TOP_K), row(TOP_K),
                   pl.BlockSpec((SUBLANE, N_EXPERTS), lambda i: (0, 0))],
        out_shape=[jax.ShapeDtypeStruct((t, d), F32), jax.ShapeDtypeStruct((t, d // 2), jnp.int32),
                   jax.ShapeDtypeStruct((t, TOP_K), jnp.int32), jax.ShapeDtypeStruct((t, TOP_K), F32),
                   jax.ShapeDtypeStruct((t, TOP_K), jnp.int32),
                   jax.ShapeDtypeStruct((SUBLANE, N_EXPERTS), F32)],
        scratch_shapes=[pltpu.VMEM((1, N_EXPERTS), F32)],
        compiler_params=_cparams(("arbitrary",)),
        name="merge",
    )(x, ada3, o_ctx[0], o_lat[0], o_ctx[1], o_lat[1], o_ctx[2], o_lat[2], ln1, w_g, w_pa, w_pb, w_pc, w_out, ln2,
      w_r, b_r)


def _expert_kernel(be_ref, nu_ref, x_ref, wgu_ref, bgu_ref, wdn_ref, bdn_ref, y_ref, wgu_s, wdn_s, *, n_chunk):
    b = pl.program_id(0)
    used = b < nu_ref[0]
    new_expert = jnp.logical_or(b == 0, be_ref[b] != be_ref[jnp.maximum(b - 1, 0)])

    @pl.when(jnp.logical_and(used, new_expert))
    def _():
        wgu_s[...] = wgu_ref[...].astype(BF16)
        wdn_s[...] = wdn_ref[...].astype(BF16)

    @pl.when(used)
    def _():
        x = _unpack_bf16_pairs(x_ref[...]).astype(BF16)
        cw = D_FF // n_chunk
        acc = jnp.broadcast_to(bdn_ref[...], x.shape)
        for c in range(n_chunk):
            glu = _dot(x, wgu_s[:, c * cw:(c + 1) * cw]) + bgu_ref[:, c * cw:(c + 1) * cw]
            lin = (_dot(x, wgu_s[:, D_FF + c * cw:D_FF + (c + 1) * cw])
                   + bgu_ref[:, D_FF + c * cw:D_FF + (c + 1) * cw])
            glu = jnp.minimum(glu, SWIGLU_LIMIT)
            lin = jnp.clip(lin, -SWIGLU_LIMIT, SWIGLU_LIMIT)
            act = glu * jax.nn.sigmoid(SWIGLU_ALPHA * glu) * (lin + 1.0)
            acc = acc + _dot(act.astype(BF16), wdn_s[c * cw:(c + 1) * cw, :])
        y_ref[...] = _pack_bf16_pairs(acc)

    @pl.when(jnp.logical_not(used))
    def _():
        y_ref[...] = jnp.zeros_like(y_ref)


def _experts(xb, block_e, n_used, w_gu, b_gu, w_dn, b_dn, layer):
    n_rows, dp = xb.shape
    d = 2 * dp
    depth = w_gu.shape[0]
    blk = EXPERT_BLOCK
    n_blocks = n_rows // blk
    wmap = lambda b, be, nu: (layer, be[b], 0, 0)
    grid_spec = pltpu.PrefetchScalarGridSpec(
        num_scalar_prefetch=2,
        grid=(n_blocks,),
        in_specs=[pl.BlockSpec((blk, dp), lambda b, be, nu: (b, 0)),
                  pl.BlockSpec((None, None, d, 2 * D_FF), wmap),
                  pl.BlockSpec((None, None, 1, 2 * D_FF), wmap),
                  pl.BlockSpec((None, None, D_FF, d), wmap),
                  pl.BlockSpec((None, None, 1, d), wmap)],
        out_specs=pl.BlockSpec((blk, dp), lambda b, be, nu: (b, 0)),
        scratch_shapes=[pltpu.VMEM((d, 2 * D_FF), BF16), pltpu.VMEM((D_FF, d), BF16)],
    )
    return pl.pallas_call(
        functools.partial(_expert_kernel, n_chunk=4),
        grid_spec=grid_spec,
        out_shape=jax.ShapeDtypeStruct((n_rows, dp), jnp.int32),
        compiler_params=_cparams(("arbitrary",)),
        name="experts",
    )(block_e, n_used, xb, w_gu, b_gu.reshape(depth, N_EXPERTS, 1, 2 * D_FF), w_dn,
      b_dn.reshape(depth, N_EXPERTS, 1, d))


def _sc_worker_base(per_w):
    return (lax.axis_index("s") * SC_CORES + lax.axis_index("c")) * per_w


def _sc_gather_rows(table_hbm, out_hbm, idx_v, rows_v, gsem, osem, base, n_chunks):
    ch = SC_CHUNK

    def gather(c, slot):
        return pltpu.make_async_copy(table_hbm.at[idx_v.at[pl.ds(c * ch, ch)]], rows_v.at[slot], gsem.at[slot])

    def put(c, slot):
        return pltpu.make_async_copy(rows_v.at[slot], out_hbm.at[pl.ds(base + c * ch, ch)], osem.at[slot])

    gather(0, 0).start()

    @pl.loop(0, n_chunks // 2)
    def _(i):
        c = 2 * i
        gather(c, 0).wait()
        put(c, 0).start()

        @pl.when(i > 0)
        def _():
            put(c - 1, 1).wait()

        gather(c + 1, 1).start()
        gather(c + 1, 1).wait()
        put(c + 1, 1).start()
        put(c, 0).wait()

        @pl.when(c + 2 < n_chunks)
        def _():
            gather(c + 2, 0).start()

    put(n_chunks - 1, 1).wait()


def _sc_scratch(per_w, d, dtype):
    return [pltpu.VMEM((per_w,), jnp.int32), pltpu.VMEM((2, SC_CHUNK, d), dtype),
            pltpu.SemaphoreType.DMA((2,)), pltpu.SemaphoreType.DMA((2,))]


def _sc_gather(table, idx):
    n_idx = idx.shape[0]
    d = table.shape[1]
    n_workers = SC_CORES * SC_SUBCORES
    per_w = n_idx // n_workers
    assert n_idx % (n_workers * SC_CHUNK * 2) == 0
    mesh = plsc.VectorSubcoreMesh(core_axis_name="c", subcore_axis_name="s")

    @functools.partial(
        pl.kernel, mesh=mesh, out_type=jax.ShapeDtypeStruct((n_idx, d), table.dtype),
        scratch_types=_sc_scratch(per_w, d, table.dtype), name="sc_gather")
    def gather(table_hbm, idx_hbm, out_hbm, idx_v, rows_v, gsem, osem):
        base = _sc_worker_base(per_w)
        pltpu.sync_copy(idx_hbm.at[pl.ds(base, per_w)], idx_v)
        _sc_gather_rows(table_hbm, out_hbm, idx_v, rows_v, gsem, osem, base, per_w // SC_CHUNK)

    return gather(table, idx)


def _sc_dispatch(h, dest, n_rows):
    t, d = h.shape
    n_slots = dest.shape[0]
    n_workers = SC_CORES * SC_SUBCORES
    per_w = n_rows // n_workers
    piece = 4096
    fill_shift = 3
    assert n_rows % (n_workers * SC_CHUNK * 2) == 0 and n_slots % piece == 0 and (n_rows >> fill_shift) <= t
    mesh = plsc.VectorSubcoreMesh(core_axis_name="c", subcore_axis_name="s")

    @functools.partial(
        pl.kernel, mesh=mesh, out_type=jax.ShapeDtypeStruct((n_rows, d), h.dtype),
        scratch_types=[pltpu.VMEM((piece,), jnp.int32)] + _sc_scratch(per_w, d, h.dtype),
        compiler_params=pltpu.CompilerParams(needs_layout_passes=False), name="sc_dispatch")
    def dispatch(h_hbm, dest_hbm, out_hbm, dest_v, idx_v, rows_v, gsem, osem):
        base = _sc_worker_base(per_w)
        lane = lax.iota(jnp.int32, SC_LANES)

        @pl.loop(0, per_w // SC_LANES)
        def _(j):
            idx_v[pl.ds(j * SC_LANES, SC_LANES)] = (base + j * SC_LANES + lane) >> fill_shift

        @pl.loop(0, n_slots // piece)
        def _(p):
            pltpu.sync_copy(dest_hbm.at[pl.ds(p * piece, piece)], dest_v)

            @pl.loop(0, piece // SC_LANES)
            def _(j):
                loc = dest_v[pl.ds(j * SC_LANES, SC_LANES)] - base
                mine = (loc >= 0) & (loc < per_w)
                tok = (p * piece + j * SC_LANES + lane) // TOP_K
                plsc.store_scatter(idx_v, [loc], tok, mask=mine)

        _sc_gather_rows(h_hbm, out_hbm, idx_v, rows_v, gsem, osem, base, per_w // SC_CHUNK)

    return dispatch(h, dest)


def _combine_kernel(x1_ref, ada_ref, yg_ref, g4_ref, fg_ref, o_ref, *, final):
    d = D_MODEL
    g4 = g4_ref[...]
    y = g4[:, 0:1] * _unpack_bf16_pairs(yg_ref[0])
    for k in range(1, TOP_K):
        y = y + g4[:, k:k + 1] * _unpack_bf16_pairs(yg_ref[k])
    x2 = x1_ref[...] + ada_ref[:, 5 * d:6 * d] * y
    if final:
        x2 = _rms(x2, fg_ref[...])
    o_ref[...] = x2


def _combine(x1, ada3, yg, g4, final_g, seg, final):
    t, d = x1.shape
    tm = TOKEN_TILE
    return pl.pallas_call(
        functools.partial(_combine_kernel, final=final),
        grid=(t // tm,),
        in_specs=[pl.BlockSpec((tm, d), lambda i: (i, 0)),
                  pl.BlockSpec((None, 1, 6 * d), lambda i: (seg(i), 0, 0)),
                  pl.BlockSpec((TOP_K, tm, d // 2), lambda i: (0, i, 0)),
                  pl.BlockSpec((tm, TOP_K), lambda i: (i, 0)),
                  pl.BlockSpec((1, d), lambda i: (0, 0))],
        out_specs=pl.BlockSpec((tm, d), lambda i: (i, 0)),
        out_shape=jax.ShapeDtypeStruct((t, d), F32),
        compiler_params=_cparams(("arbitrary",)),
        name="combine",
    )(x1, ada3, yg, g4, final_g)


def _rope_tables(n_lat):
    rows = n_lat // GRID_W
    row_ids = jnp.repeat(jnp.arange(rows, dtype=F32), GRID_W)
    col_ids = jnp.tile(jnp.arange(GRID_W, dtype=F32), rows)

    def table(d_rot, lane0):
        d_axis = d_rot // 2
        nf = d_axis // 2
        inv_freq = ROPE_BASE ** (-jnp.arange(0, d_axis, 2, dtype=F32) / d_axis)
        ang_r = row_ids[:, None] * inv_freq
        ang_c = col_ids[:, None] * inv_freq
        ang = jnp.concatenate([ang_r, ang_r, ang_c, ang_c], axis=-1)
        sign = jnp.tile(jnp.concatenate([-jnp.ones((nf,), F32), jnp.ones((nf,), F32)]), 2)
        cos = jnp.ones((n_lat, LANE), F32)
        sin = jnp.zeros((n_lat, LANE), F32)
        for l0 in lane0:
            cos = cos.at[:, l0:l0 + d_rot].set(jnp.cos(ang))
            sin = sin.at[:, l0:l0 + d_rot].set(jnp.sin(ang) * sign)
        ident = (jnp.ones((TOKEN_TILE, LANE), F32), jnp.zeros((TOKEN_TILE, LANE), F32))
        return jnp.concatenate([ident[0], cos], axis=0), jnp.concatenate([ident[1], sin], axis=0)

    ca, sa = table(HD_A, (0, HD_A))
    cb, sb = table(ROPE_B, (NOPE_B,))
    return ca, sa, cb, sb


def _prep_weights(w_in, sink, w_uq, w_ukv, lru_wa, lru_wx, lru_lam, w_pa):
    depth, d, _ = w_in.shape
    cuts = np.cumsum((512, 128, 128, Q_RANK, KV_RANK, ROPE_B, LRU_W, LRU_W, 3 * d))
    perm = np.array(HEAD_PERM_A)
    w_qa = w_in[:, :, :cuts[0]].reshape(depth, d, N_HEADS_A, HD_A)[:, :, perm].reshape(depth, d, 512)
    w_kr = jnp.pad(w_in[:, :, cuts[4]:cuts[5]], ((0, 0), (0, 0), (NOPE_B, LANE - NOPE_B - ROPE_B)))
    w_main = jnp.concatenate([w_qa, w_in[:, :, cuts[0]:cuts[4]], w_in[:, :, cuts[5]:cuts[7]], w_kr],
                             axis=-1).astype(BF16)
    w_g = w_in[:, :, cuts[7]:].astype(BF16)
    sink_p = sink[:, perm]
    w_pa_p = w_pa.reshape(depth, N_HEADS_A, HD_A, d)[:, perm].reshape(depth, 512, d).astype(BF16)
    hb = NOPE_B + ROPE_B
    w_uq_p = jnp.pad(w_uq.reshape(depth, Q_RANK, N_HEADS_B, hb),
                     ((0, 0), (0, 0), (0, 0), (0, HEAD_PAD_B - hb))).reshape(depth, Q_RANK, -1).astype(BF16)
    w_ukv4 = w_ukv.reshape(depth, KV_RANK, N_HEADS_B, NOPE_B + V_B)
    w_uk_p = jnp.pad(w_ukv4[..., :NOPE_B], ((0, 0), (0, 0), (0, 0), (0, HEAD_PAD_B - NOPE_B))
                     ).reshape(depth, KV_RANK, -1).astype(BF16)
    w_uv = w_ukv4[..., NOPE_B:].reshape(depth, KV_RANK, -1).astype(BF16)

    def block_diag(w):
        hpb = LRU_HEADS // 2
        blk = LRU_W // LRU_HEADS
        w = w.reshape(depth, 2, 2, hpb, blk, blk)
        eye = jnp.eye(hpb, dtype=w.dtype)
        out = jnp.einsum('ldghij,hk->ldghikj', w, eye)
        return out.reshape(depth, 2, 2, hpb * blk, hpb * blk).astype(BF16)

    sp = jax.nn.softplus(-lru_lam.astype(F32))
    return w_main, w_g, sink_p, w_pa_p, w_uq_p, w_uk_p, w_uv, block_diag(lru_wa), block_diag(lru_wx), sp


def kernel(x_prompt, x_sample, cache_k_a, cache_v_a, cache_ckv, cache_krope, state_lru, c, c_ctx, ln1_g, ln2_g,
           final_g, w_ada, b_ada, w_in, sink, g_q, w_uq, g_kv, w_ukv, conv_w, conv_b, lru_wa, lru_ba, lru_wx,
           lru_bx, lru_lam, w_pa, w_pb, w_pc, w_out, w_router, b_router, w_gu, b_gu, w_dn, b_dn):
    n_ctx, s_len, d = x_prompt.shape
    n_lat, n_len, _ = x_sample.shape
    depth = w_in.shape[0]
    past = cache_k_a.shape[2]
    t_ctx = n_ctx * s_len
    t_lat = n_lat * n_len
    t = t_ctx + t_lat
    tm = TOKEN_TILE
    ctx_tiles = t_ctx // tm
    lat_tiles = n_len // tm
    assert t_ctx % n_len == 0 and t_ctx % tm == 0 and n_len % tm == 0 and n_lat + 1 <= SUBLANE

    seg = lambda i: jnp.where(i < ctx_tiles, 0, 1 + (i - ctx_tiles) // lat_tiles)
    tab_blk = lambda i: jnp.where(i < ctx_tiles, 0, 1 + (i - ctx_tiles) % lat_tiles)

    cond8 = jnp.zeros((SUBLANE, d), F32).at[0].set(c_ctx).at[1:1 + n_lat].set(c)
    ada = _ada_all(cond8, w_ada, b_ada)
    tabs = _rope_tables(n_len)
    (w_main, w_g, sink_p, w_pa_p, w_uq_p, w_uk_p, w_uv, wa_bd, wx_bd, sp) = _prep_weights(
        w_in, sink, w_uq, w_ukv, lru_wa, lru_wx, lru_lam, w_pa)
    w_pb_b, w_pc_b, w_out_b = w_pb.astype(BF16), w_pc.astype(BF16), w_out.astype(BF16)
    w_r_hi = w_router.astype(BF16)
    w_r_cat = jnp.concatenate([w_r_hi, (w_router - w_r_hi.astype(F32)).astype(BF16)], axis=-1)
    cache_k2 = cache_k_a.reshape(n_lat, depth, past, N_KV_A * HD_A)
    cache_v2 = cache_v_a.reshape(n_lat, depth, past, N_KV_A * HD_A)
    cache_kr_pad = jnp.pad(cache_krope, ((0, 0), (0, 0), (0, 0), (NOPE_B, LANE - NOPE_B - ROPE_B)))
    h0_ctx = jnp.zeros((n_ctx, 2, LRU_W), F32)
    row2 = lambda a: a.reshape(1, -1)

    x = jnp.concatenate([x_prompt.reshape(t_ctx, d), x_sample.reshape(t_lat, d)], axis=0)
    ks_a, vs_a, ckvs, krs, lrus = [], [], [], [], []
    n_rows = t * TOP_K + N_EXPERTS * EXPERT_BLOCK
    n_blocks = n_rows // EXPERT_BLOCK
    for l in range(depth):
        ada3 = ada[l].reshape(SUBLANE, 1, 6 * d)
        qa, ka, va, qb, ckvn, kb, vb, kr, xc, yc = _in_proj(
            x, ada3, row2(ln1_g[l]), w_main[l], row2(g_q[l]), w_uq_p[l], row2(g_kv[l]), w_uk_p[l], w_uv[l],
            tabs, seg, tab_blk)
        ks_a.append(ka[:t_ctx].reshape(n_ctx, s_len, N_KV_A, HD_A))
        vs_a.append(va[:t_ctx].reshape(n_ctx, s_len, N_KV_A, HD_A))
        ckvs.append(ckvn[:t_ctx].reshape(n_ctx, s_len, KV_RANK))
        krs.append(kr[:t_ctx, NOPE_B:NOPE_B + ROPE_B].reshape(n_ctx, s_len, ROPE_B))

        oa_c = _gqa_ctx(qa, ka, va, sink_p[l], n_ctx, s_len)
        oa_l = _gqa_lat(qa, ka, va, cache_k2, cache_v2, sink_p[l], l, n_lat, n_len, t_ctx)
        kbx, vbx = _cache_kv(cache_ckv, cache_kr_pad, w_uk_p[l], w_uv[l], l)
        ob_c = _mla_ctx(qb, kb, vb, n_ctx, s_len)
        ob_l = _mla_lat(qb, kb, vb, kbx, vbx, n_lat, n_len, t_ctx)
        lru_args = (conv_w[l], row2(conv_b[l]), wa_bd[l], lru_ba[l], wx_bd[l], lru_bx[l], sp[l])
        oc_c, st_c = _lru(xc, yc, h0_ctx, *lru_args, n_ctx, s_len, 0)
        oc_l, _ = _lru(xc, yc, state_lru[:, l], *lru_args, n_lat, n_len, t_ctx)
        lrus.append(st_c)
        x1, h, e4, g4, r4, cnt = _merge(x, ada3, (oa_c, ob_c, oc_c), (oa_l, ob_l, oc_l), row2(ln1_g[l]), w_g[l],
                                        w_pa_p[l], w_pb_b[l], w_pc_b[l], w_out_b[l], row2(ln2_g[l]), w_r_cat[l],
                                        row2(b_router[l]), seg, ctx_tiles)

        counts = cnt[0].astype(jnp.int32)
        padded = (counts + EXPERT_BLOCK - 1) // EXPERT_BLOCK * EXPERT_BLOCK
        pad_end = jnp.cumsum(padded)
        pad_start = pad_end - padded
        dest4 = pad_start[e4] + r4
        blk_row0 = jnp.arange(n_blocks, dtype=jnp.int32) * EXPERT_BLOCK
        block_e = jnp.minimum(jnp.sum((pad_end[None, :] <= blk_row0[:, None]).astype(jnp.int32), axis=1),
                              N_EXPERTS - 1)
        n_used = (pad_end[-1:] // EXPERT_BLOCK).astype(jnp.int32)
        xb = _sc_dispatch(h, dest4.reshape(-1), n_rows)
        yb = _experts(xb, block_e, n_used, w_gu, b_gu, w_dn, b_dn, l)
        yg = _sc_gather(yb, dest4.T.reshape(-1)).reshape(TOP_K, t, d // 2)
        x = _combine(x1, ada3, yg, g4, row2(final_g), seg, l == depth - 1)

    y_prompt = x[:t_ctx].reshape(n_ctx, s_len, d)
    y_sample = x[t_ctx:].reshape(n_lat, n_len, d)
    return (y_prompt, y_sample, jnp.stack(ks_a, axis=1), jnp.stack(vs_a, axis=1), jnp.stack(ckvs, axis=1),
            jnp.stack(krs, axis=1), jnp.stack(lrus, axis=1))
```

```python
import functools

import numpy as np
import jax
import jax.numpy as jnp
from jax import lax
from jax.experimental import pallas as pl
from jax.experimental.pallas import tpu as pltpu
from jax.experimental.pallas import tpu_sc as plsc

F32 = jnp.float32
BF16 = jnp.bfloat16

D_MODEL = 1024
GRID_W = 64
EPS = 1e-6
ROPE_BASE = 10000.0
NEG_INF = -1e30
HD_A = 64
N_HEADS_A = 8
N_KV_A = 2
WINDOW = 128
N_HEADS_B = 8
NOPE_B = 64
ROPE_B = 32
V_B = 64
Q_RANK = 512
KV_RANK = 256
LRU_W = 512
LRU_HEADS = 8
CONV_W = 4
LRU_C = 8.0
N_EXPERTS = 32
TOP_K = 4
D_FF = D_MODEL
SWIGLU_LIMIT = 7.0
SWIGLU_ALPHA = 1.702

LANE = 128
SUBLANE = 8
TOKEN_TILE = 512
MLA_Q_BLOCK = 1024
GQA_Q_BLOCK = 256
MERGE_TILE = 512
EXPERT_BLOCK = 512
HEAD_PAD_B = 128
LRU_HALF = LRU_W // 2
VMEM_LIMIT = 56 * 1024 * 1024
SC_CORES = 2
SC_SUBCORES = 16
SC_LANES = 16
SC_CHUNK = 64

HEAD_PERM_A = (0, 4, 1, 5, 2, 6, 3, 7)

C_QA, C_KA, C_VA, C_CQ, C_CKV, C_XC, C_YC, C_KR, C_END = 0, 512, 640, 768, 1280, 1536, 2048, 2560, 2688


def _cparams(sem):
    return pltpu.CompilerParams(dimension_semantics=sem, vmem_limit_bytes=VMEM_LIMIT)


def _rms(x, g):
    return x * lax.rsqrt(jnp.mean(x * x, axis=-1, keepdims=True) + EPS) * g


def _dot(a, b):
    return jnp.dot(a, b, preferred_element_type=F32)


def _dot_t(a, b):
    return lax.dot_general(a, b, (((1,), (1,)), ((), ())), preferred_element_type=F32)


def _pack_bf16_pairs(x):
    n = x.shape[1] // 2
    bits = lambda v: pltpu.bitcast(v.astype(BF16).astype(F32), jnp.uint32)
    return pltpu.bitcast((bits(x[:, :n]) >> 16) | (bits(x[:, n:]) & jnp.uint32(0xFFFF0000)), jnp.int32)


def _unpack_bf16_pairs(p):
    p = pltpu.bitcast(p, jnp.uint32)
    lo = pltpu.bitcast(p << 16, F32)
    hi = pltpu.bitcast(p & jnp.uint32(0xFFFF0000), F32)
    return jnp.concatenate([lo, hi], axis=1)


def _ada_kernel(c_ref, w_ref, b_ref, o_ref):
    c = c_ref[...]
    s = c * jax.nn.sigmoid(c)
    o_ref[...] = _dot(s.astype(BF16), w_ref[...].astype(BF16)) + b_ref[...]


def _ada_all(cond8, w_ada, b_ada):
    depth, d, n6 = w_ada.shape
    nb = 1536
    return pl.pallas_call(
        _ada_kernel,
        grid=(depth, n6 // nb),
        in_specs=[pl.BlockSpec((SUBLANE, d), lambda l, j: (0, 0)),
                  pl.BlockSpec((None, d, nb), lambda l, j: (l, 0, j)),
                  pl.BlockSpec((None, 1, nb), lambda l, j: (l, 0, j))],
        out_specs=pl.BlockSpec((None, SUBLANE, nb), lambda l, j: (l, 0, j)),
        out_shape=jax.ShapeDtypeStruct((depth, SUBLANE, n6), F32),
        compiler_params=_cparams(("arbitrary", "arbitrary")),
        name="ada",
    )(cond8, w_ada, b_ada.reshape(depth, 1, n6))


def _swap_halves(x, half):
    n = x.shape[-1]
    lane = lax.broadcasted_iota(jnp.int32, x.shape, x.ndim - 1)
    first = (lane % (2 * half)) < half
    return jnp.where(first, pltpu.roll(x, n - half, x.ndim - 1), pltpu.roll(x, half, x.ndim - 1))


def _rope(x, cos, sin_signed, half):
    reps = x.shape[-1] // cos.shape[-1]
    if reps > 1:
        cos = jnp.concatenate([cos] * reps, axis=-1)
        sin_signed = jnp.concatenate([sin_signed] * reps, axis=-1)
    return x * cos + _swap_halves(x, half) * sin_signed


def _in_kernel(x_ref, ada_ref, ln1_ref, w_ref, gq_ref, wuq_ref, gkv_ref, wuk_ref, wuv_ref,
               ca_ref, sa_ref, cb_ref, sb_ref,
               qa_ref, ka_ref, va_ref, qb_ref, ckvn_ref, kb_ref, vb_ref, kr_ref, xc_ref, yc_ref):
    d = D_MODEL
    x = x_ref[...]
    shift = ada_ref[:, 0:d]
    scale = ada_ref[:, d:2 * d]
    u = _rms(x, ln1_ref[...]) * (1.0 + scale) + shift
    p = _dot(u.astype(BF16), w_ref[...])
    ca, sa, cb, sb = ca_ref[...], sa_ref[...], cb_ref[...], sb_ref[...]

    qa = _rope(p[:, C_QA:C_KA], ca, sa, HD_A // 4) * (HD_A ** -0.5)
    qa_ref[...] = qa.astype(BF16)
    ka_ref[...] = _rope(p[:, C_KA:C_VA], ca, sa, HD_A // 4)
    va_ref[...] = p[:, C_VA:C_CQ]

    cq = _rms(p[:, C_CQ:C_CKV], gq_ref[...])
    qb = _dot(cq.astype(BF16), wuq_ref[...])
    qb = _rope(qb, cb, sb, ROPE_B // 4) * ((NOPE_B + ROPE_B) ** -0.5)
    qb_ref[...] = qb.astype(BF16)

    ckvn = _rms(p[:, C_CKV:C_XC], gkv_ref[...])
    ckvn_ref[...] = ckvn
    ckvn_b = ckvn.astype(BF16)
    kr = _rope(p[:, C_KR:C_END], cb, sb, ROPE_B // 4)
    kr_ref[...] = kr
    kb = _dot(ckvn_b, wuk_ref[...]) + jnp.concatenate([kr] * N_HEADS_B, axis=-1)
    kb_ref[...] = kb.astype(BF16)
    vb_ref[...] = _dot(ckvn_b, wuv_ref[...]).astype(BF16)

    xc_ref[...] = p[:, C_XC:C_YC]
    yc_ref[...] = p[:, C_YC:C_KR]


def _in_proj(x, ada3, ln1, w_main, g_q, w_uq, g_kv, w_uk, w_uv, tabs, seg, tab_blk):
    t, d = x.shape
    tm = TOKEN_TILE
    row = lambda n: pl.BlockSpec((tm, n), lambda i: (i, 0))
    full = lambda a: pl.BlockSpec(a.shape, lambda i: (0,) * a.ndim)
    tab = pl.BlockSpec((tm, LANE), lambda i: (tab_blk(i), 0))
    sds = lambda n, dt: jax.ShapeDtypeStruct((t, n), dt)
    nb = N_HEADS_B * HEAD_PAD_B
    return pl.pallas_call(
        _in_kernel,
        grid=(t // tm,),
        in_specs=[row(d), pl.BlockSpec((None, 1, 6 * d), lambda i: (seg(i), 0, 0)), full(ln1), full(w_main),
                  full(g_q), full(w_uq), full(g_kv), full(w_uk), full(w_uv), tab, tab, tab, tab],
        out_specs=[row(512), row(128), row(128), row(nb), row(KV_RANK), row(nb), row(512), row(128),
                   row(LRU_W), row(LRU_W)],
        out_shape=[sds(512, BF16), sds(128, F32), sds(128, F32), sds(nb, BF16), sds(KV_RANK, F32),
                   sds(nb, BF16), sds(512, BF16), sds(128, F32), sds(LRU_W, F32), sds(LRU_W, F32)],
        compiler_params=_cparams(("arbitrary",)),
        name="in_proj",
    )(x, ada3, ln1, w_main, g_q, w_uq, g_kv, w_uk, w_uv, *tabs)


def _gqa_heads(q, k_all, v_all, bias, sink_ref, o_ref):
    lane = lax.broadcasted_iota(jnp.int32, (q.shape[0], LANE), 1)
    low = lane < HD_A
    for pair in range(N_HEADS_A // 2):
        qp = q[:, pair * LANE:(pair + 1) * LANE]
        outs = []
        for g in range(N_KV_A):
            qm = jnp.where(low if g == 0 else ~low, qp, jnp.zeros_like(qp))
            s = _dot_t(qm, k_all)
            if bias is not None:
                s = s + bias
            sink = sink_ref[2 * pair + g]
            m = jnp.maximum(jnp.max(s, axis=-1, keepdims=True), sink)
            e = jnp.exp(s - m)
            l = jnp.sum(e, axis=-1, keepdims=True) + jnp.exp(sink - m)
            outs.append(_dot(e.astype(BF16), v_all) / l)
        o_ref[:, pair * LANE:(pair + 1) * LANE] = jnp.where(low, outs[0], outs[1]).astype(o_ref.dtype)


def _gqa_ctx_kernel(sink_ref, q_ref, k_ref, v_ref, o_ref):
    _gqa_heads(q_ref[...], k_ref[...].astype(BF16), v_ref[...].astype(BF16), None, sink_ref, o_ref)


def _gqa_ctx(qa, ka, va, sink_p, n_seq, s_len):
    return pl.pallas_call(
        _gqa_ctx_kernel,
        grid=(n_seq,),
        in_specs=[pl.BlockSpec(memory_space=pltpu.SMEM),
                  pl.BlockSpec((s_len, 512), lambda b: (b, 0)),
                  pl.BlockSpec((s_len, LANE), lambda b: (b, 0)),
                  pl.BlockSpec((s_len, LANE), lambda b: (b, 0))],
        out_specs=pl.BlockSpec((s_len, 512), lambda b: (b, 0)),
        out_shape=jax.ShapeDtypeStruct((n_seq * s_len, 512), BF16),
        compiler_params=_cparams(("arbitrary",)),
        name="gqa_ctx",
    )(sink_p, qa, ka, va)


def _gqa_lat_kernel(sink_ref, q_ref, kp_ref, kc_ref, kn_ref, vp_ref, vc_ref, vn_ref, kx_ref, vx_ref, o_ref):
    j = pl.program_id(1)
    last = pl.num_programs(1) - 1
    w = WINDOW
    qb = q_ref.shape[0]
    k_all = jnp.concatenate([kp_ref[...], kc_ref[...], kn_ref[...], kx_ref[...]], axis=0).astype(BF16)
    v_all = jnp.concatenate([vp_ref[...], vc_ref[...], vn_ref[...], vx_ref[...]], axis=0).astype(BF16)
    n_loc = qb + 2 * w
    qi = lax.broadcasted_iota(jnp.int32, (qb, n_loc), 0)
    col = lax.broadcasted_iota(jnp.int32, (qb, n_loc), 1)
    rel = col - w - qi
    zero = jnp.zeros((qb, n_loc), F32)
    neg = jnp.full((qb, n_loc), NEG_INF, F32)
    band = jnp.where(jnp.abs(rel) <= w, zero, neg)
    before = jnp.where(col < w, jnp.where(j > 0, zero, neg), zero)
    after = jnp.where(col >= qb + w, jnp.where(j < last, zero, neg), zero)
    bias = jnp.concatenate([band + before + after, jnp.zeros((qb, kx_ref.shape[0]), F32)], axis=1)
    _gqa_heads(q_ref[...], k_all, v_all, bias, sink_ref, o_ref)


def _gqa_lat(qa, ka, va, kx, vx, sink_p, layer, n_seq, n_len, row0):
    w = WINDOW
    qb = GQA_Q_BLOCK
    r = qb // w
    nqb = n_len // qb
    nwb = n_len // w
    cur = lambda b, j: (row0 // qb + b * nqb + j, 0)
    prev = lambda b, j: (row0 // w + b * nwb + jnp.maximum(r * j - 1, 0), 0)
    nxt = lambda b, j: (row0 // w + b * nwb + jnp.minimum(r * j + r, nwb - 1), 0)
    past = kx.shape[2]
    cache = pl.BlockSpec((None, None, past, LANE), lambda b, j: (b, layer, 0, 0))
    return pl.pallas_call(
        _gqa_lat_kernel,
        grid=(n_seq, nqb),
        in_specs=[pl.BlockSpec(memory_space=pltpu.SMEM),
                  pl.BlockSpec((qb, 512), cur),
                  pl.BlockSpec((w, LANE), prev), pl.BlockSpec((qb, LANE), cur), pl.BlockSpec((w, LANE), nxt),
                  pl.BlockSpec((w, LANE), prev), pl.BlockSpec((qb, LANE), cur), pl.BlockSpec((w, LANE), nxt),
                  cache, cache],
        out_specs=pl.BlockSpec((qb, 512), lambda b, j: (b * nqb + j, 0)),
        out_shape=jax.ShapeDtypeStruct((n_seq * n_len, 512), BF16),
        compiler_params=_cparams(("arbitrary", "arbitrary")),
        name="gqa_lat",
    )(sink_p, qa, ka, ka, ka, va, va, va, kx, vx)


def _mla_kernel(*refs, n_src, chunk):
    q_ref = refs[0]
    kv_refs = refs[1:1 + 2 * n_src]
    o_ref = refs[-1]
    qb = q_ref.shape[0]
    lane = lax.broadcasted_iota(jnp.int32, (qb, LANE), 1)
    outs = []
    for hh in range(2):
        q = q_ref[:, hh * HEAD_PAD_B:(hh + 1) * HEAD_PAD_B]
        m = jnp.full((qb, 1), NEG_INF, F32)
        l = jnp.zeros((qb, 1), F32)
        acc = jnp.zeros((qb, LANE), F32)
        for s_i in range(n_src):
            k_ref, v_ref = kv_refs[2 * s_i], kv_refs[2 * s_i + 1]
            nk = k_ref.shape[0]
            for c in range(nk // chunk):
                k = k_ref[c * chunk:(c + 1) * chunk, hh * HEAD_PAD_B:(hh + 1) * HEAD_PAD_B]
                v = v_ref[c * chunk:(c + 1) * chunk, :]
                s = _dot_t(q, k)
                m_new = jnp.maximum(m, jnp.max(s, axis=-1, keepdims=True))
                alpha = jnp.exp(m - m_new)
                e = jnp.exp(s - m_new)
                l = alpha * l + jnp.sum(e, axis=-1, keepdims=True)
                acc = alpha * acc + _dot(e.astype(BF16), v)
                m = m_new
        outs.append(acc / l)
    o_ref[...] = jnp.where(lane < V_B, outs[0], outs[1]).astype(o_ref.dtype)


def _mla_ctx(qb, kb, vb, n_seq, s_len):
    npair = N_HEADS_B // 2
    return pl.pallas_call(
        functools.partial(_mla_kernel, n_src=1, chunk=s_len),
        grid=(n_seq, npair),
        in_specs=[pl.BlockSpec((s_len, 2 * HEAD_PAD_B), lambda b, p: (b, p)),
                  pl.BlockSpec((s_len, 2 * HEAD_PAD_B), lambda b, p: (b, p)),
                  pl.BlockSpec((s_len, LANE), lambda b, p: (b, p))],
        out_specs=pl.BlockSpec((s_len, LANE), lambda b, p: (b, p)),
        out_shape=jax.ShapeDtypeStruct((n_seq * s_len, 512), BF16),
        compiler_params=_cparams(("arbitrary", "arbitrary")),
        name="mla_ctx",
    )(qb, kb, vb)


def _mla_lat(qb, kb, vb, kbx, vbx, n_seq, n_len, row0):
    npair = N_HEADS_B // 2
    qblk = min(MLA_Q_BLOCK, n_len)
    nqb = n_len // qblk
    past = kbx.shape[0] // n_seq
    qmap = lambda b, p, j: (row0 // qblk + b * nqb + j, p)
    return pl.pallas_call(
        functools.partial(_mla_kernel, n_src=2, chunk=512),
        grid=(n_seq, npair, nqb),
        in_specs=[pl.BlockSpec((qblk, 2 * HEAD_PAD_B), qmap),
                  pl.BlockSpec((n_len, 2 * HEAD_PAD_B), lambda b, p, j: (row0 // n_len + b, p)),
                  pl.BlockSpec((n_len, LANE), lambda b, p, j: (row0 // n_len + b, p)),
                  pl.BlockSpec((past, 2 * HEAD_PAD_B), lambda b, p, j: (b, p)),
                  pl.BlockSpec((past, LANE), lambda b, p, j: (b, p))],
        out_specs=pl.BlockSpec((qblk, LANE), lambda b, p, j: (b * nqb + j, p)),
        out_shape=jax.ShapeDtypeStruct((n_seq * n_len, 512), BF16),
        compiler_params=_cparams(("arbitrary", "arbitrary", "arbitrary")),
        name="mla_lat",
    )(qb, kb, vb, kbx, vbx)


def _cache_kv_kernel(ckv_ref, kr_ref, wuk_ref, wuv_ref, kb_ref, vb_ref):
    c = ckv_ref[...].astype(BF16)
    kb = _dot(c, wuk_ref[...]) + jnp.concatenate([kr_ref[...]] * N_HEADS_B, axis=-1)
    kb_ref[...] = kb.astype(BF16)
    vb_ref[...] = _dot(c, wuv_ref[...]).astype(BF16)


def _cache_kv(cache_ckv, cache_kr_pad, w_uk, w_uv, layer):
    n_seq, _, past, _ = cache_ckv.shape
    nb = N_HEADS_B * HEAD_PAD_B
    return pl.pallas_call(
        _cache_kv_kernel,
        grid=(n_seq,),
        in_specs=[pl.BlockSpec((None, None, past, KV_RANK), lambda b: (b, layer, 0, 0)),
                  pl.BlockSpec((None, None, past, LANE), lambda b: (b, layer, 0, 0)),
                  pl.BlockSpec(w_uk.shape, lambda b: (0, 0)),
                  pl.BlockSpec(w_uv.shape, lambda b: (0, 0))],
        out_specs=[pl.BlockSpec((past, nb), lambda b: (b, 0)), pl.BlockSpec((past, 512), lambda b: (b, 0))],
        out_shape=[jax.ShapeDtypeStruct((n_seq * past, nb), BF16), jax.ShapeDtypeStruct((n_seq * past, 512), BF16)],
        compiler_params=_cparams(("arbitrary",)),
        name="cache_kv",
    )(cache_ckv, cache_kr_pad, w_uk, w_uv)


def _lru_kernel(xc_ref, yc_ref, h0_ref, cw_ref, cb_ref, wa_ref, ba_ref, wx_ref, bx_ref, sp_ref,
                o_ref, st_ref, pad_ref, xcv_ref, a_ref, b_ref, *, chunk):
    n = xc_ref.shape[0]
    halo = SUBLANE
    pad_ref[0:halo, :] = jnp.zeros((halo, LRU_W), F32)
    pad_ref[halo + n:2 * halo + n, :] = jnp.zeros((halo, LRU_W), F32)
    pad_ref[halo:halo + n, :] = xc_ref[...]
    left = CONV_W // 2
    for c in range(n // chunk):
        r0 = c * chunk
        acc = jnp.broadcast_to(cb_ref[...], (chunk, LRU_W))
        for j in range(CONV_W):
            off = halo + r0 + j - left
            acc = acc + cw_ref[j:j + 1, :] * pad_ref[off:off + chunk, :]
        xcv_ref[r0:r0 + chunk, :] = acc

    row = lax.broadcasted_iota(jnp.int32, (SUBLANE, LRU_W), 0)
    for d in range(2):
        for c in range(n // chunk):
            r0 = c * chunk
            xv = xcv_ref[r0:r0 + chunk, :]
            xb = xv.astype(BF16)
            for hf in range(2):
                cs = slice(hf * LRU_HALF, (hf + 1) * LRU_HALF)
                r = jax.nn.sigmoid(_dot(xb[:, cs], wa_ref[d, hf]) + ba_ref[d:d + 1, cs])
                i = jax.nn.sigmoid(_dot(xb[:, cs], wx_ref[d, hf]) + bx_ref[d:d + 1, cs])
                log_a = (-LRU_C) * r * sp_ref[d:d + 1, cs]
                a = jnp.exp(log_a)
                a_ref[d, r0:r0 + chunk, cs] = a
                b_ref[d, r0:r0 + chunk, cs] = jnp.sqrt(-jnp.tanh(log_a) * (a * a + 1.0)) * (i * xv[:, cs])

    def scan_group(d, grp, h):
        rows = pl.ds(pl.multiple_of(grp * SUBLANE, SUBLANE), SUBLANE)
        a = a_ref[d, rows, :]
        b = b_ref[d, rows, :]
        for sh in (1, 2, 4):
            if d == 0:
                keep = row >= sh
                a_s = jnp.where(keep, pltpu.roll(a, sh, 0), 1.0)
                b_s = jnp.where(keep, pltpu.roll(b, sh, 0), 0.0)
            else:
                keep = row < SUBLANE - sh
                a_s = jnp.where(keep, pltpu.roll(a, SUBLANE - sh, 0), 1.0)
                b_s = jnp.where(keep, pltpu.roll(b, SUBLANE - sh, 0), 0.0)
            b = a * b_s + b
            a = a * a_s
        hrows = a * h + b
        b_ref[d, rows, :] = hrows
        return hrows[SUBLANE - 1:SUBLANE, :] if d == 0 else hrows[0:1, :]

    n_grp = n // SUBLANE

    def body(g, hs):
        return scan_group(0, g, hs[0]), scan_group(1, n_grp - 1 - g, hs[1])

    h_f, h_b = lax.fori_loop(0, n_grp, body, (h0_ref[0:1, :], h0_ref[1:2, :]))
    st_ref[0:1, :] = h_f
    st_ref[1:2, :] = h_b

    for c in range(n // chunk):
        rs = slice(c * chunk, (c + 1) * chunk)
        o_ref[rs, :] = ((b_ref[0, rs, :] + b_ref[1, rs, :]) * jax.nn.gelu(yc_ref[rs, :])).astype(o_ref.dtype)


def _lru(xc, yc, h0, conv_w, conv_b, wa, ba, wx, bx, sp, n_seq, n_len, row0):
    blk0 = row0 // n_len
    full = lambda a: pl.BlockSpec(a.shape, lambda b: (0,) * a.ndim)
    seq = pl.BlockSpec((n_len, LRU_W), lambda b: (blk0 + b, 0))
    return pl.pallas_call(
        functools.partial(_lru_kernel, chunk=min(n_len, 256)),
        grid=(n_seq,),
        in_specs=[seq, seq, pl.BlockSpec((None, 2, LRU_W), lambda b: (b, 0, 0)),
                  full(conv_w), full(conv_b), full(wa), full(ba), full(wx), full(bx), full(sp)],
        out_specs=[pl.BlockSpec((n_len, LRU_W), lambda b: (b, 0)),
                   pl.BlockSpec((None, 2, LRU_W), lambda b: (b, 0, 0))],
        out_shape=[jax.ShapeDtypeStruct((n_seq * n_len, LRU_W), BF16),
                   jax.ShapeDtypeStruct((n_seq, 2, LRU_W), F32)],
        scratch_shapes=[pltpu.VMEM((n_len + 2 * SUBLANE, LRU_W), F32), pltpu.VMEM((n_len, LRU_W), F32),
                        pltpu.VMEM((2, n_len, LRU_W), F32), pltpu.VMEM((2, n_len, LRU_W), F32)],
        compiler_params=_cparams(("arbitrary",)),
        name="lru",
    )(xc, yc, h0, conv_w, conv_b, wa, ba, wx, bx, sp)


def _merge_kernel(x_ref, ada_ref, oac_ref, oal_ref, obc_ref, obl_ref, occ_ref, ocl_ref, ln1_ref, wg_ref, wpa_ref,
                  wpb_ref, wpc_ref, wout_ref, ln2_ref, wr_ref, br_ref,
                  x1_ref, h_ref, e4_ref, g4_ref, r4_ref, cnt_ref, carry_ref, *, ctx_tiles):
    d = D_MODEL
    tm = x_ref.shape[0]
    i = pl.program_id(0)
    is_ctx = i < ctx_tiles
    oa = jnp.where(is_ctx, oac_ref[...], oal_ref[...])
    ob = jnp.where(is_ctx, obc_ref[...], obl_ref[...])
    oc = jnp.where(is_ctx, occ_ref[...], ocl_ref[...])

    @pl.when(i == 0)
    def _():
        carry_ref[...] = jnp.zeros_like(carry_ref)

    x = x_ref[...]
    ada = ada_ref[...]
    u = _rms(x, ln1_ref[...]) * (1.0 + ada[:, d:2 * d]) + ada[:, 0:d]
    g = jax.nn.sigmoid(_dot(u.astype(BF16), wg_ref[...]))
    m = (g[:, 0:d] * _dot(oa, wpa_ref[...]) + g[:, d:2 * d] * _dot(ob, wpb_ref[...])
         + g[:, 2 * d:3 * d] * _dot(oc, wpc_ref[...]))
    x1 = x + ada[:, 2 * d:3 * d] * _dot(m.astype(BF16), wout_ref[...])
    x1_ref[...] = x1
    h = _rms(x1, ln2_ref[...]) * (1.0 + ada[:, 4 * d:5 * d]) + ada[:, 3 * d:4 * d]
    h_ref[...] = _pack_bf16_pairs(h)

    h_hi = h.astype(BF16)
    h_lo = (h - h_hi.astype(F32)).astype(BF16)
    hw = _dot(h_hi, wr_ref[...])
    logits = (hw[:, :N_EXPERTS] + hw[:, N_EXPERTS:] + _dot(h_lo, wr_ref[:, :N_EXPERTS])) + br_ref[...]
    col = lax.broadcasted_iota(jnp.int32, (tm, N_EXPERTS), 1).astype(F32)
    col4 = lax.broadcasted_iota(jnp.int32, (tm, TOP_K), 1)
    sel_any = jnp.zeros((tm, N_EXPERTS), F32)
    vals, idxs = [], []
    work = logits
    for _k in range(TOP_K):
        mx = jnp.max(work, axis=-1, keepdims=True)
        idx = jnp.min(jnp.where(work == mx, col, float(N_EXPERTS)), axis=-1, keepdims=True)
        sel = col == idx
        vals.append(mx)
        idxs.append(idx)
        sel_any = jnp.where(sel, 1.0, sel_any)
        work = jnp.where(sel, -jnp.inf, work)

    ri = lax.broadcasted_iota(jnp.int32, (tm, tm), 0)
    ci = lax.broadcasted_iota(jnp.int32, (tm, tm), 1)
    tri = jnp.where(ri > ci, 1.0, 0.0).astype(BF16)
    before = _dot(tri, sel_any.astype(BF16)) + carry_ref[...]
    carry = carry_ref[...] + jnp.sum(sel_any, axis=0, keepdims=True)
    carry_ref[...] = carry
    cnt_ref[...] = jnp.broadcast_to(carry, cnt_ref.shape)

    exps = [jnp.exp(v - vals[0]) for v in vals]
    den = exps[0] + exps[1] + exps[2] + exps[3]
    e4 = jnp.zeros((tm, TOP_K), jnp.int32)
    g4 = jnp.zeros((tm, TOP_K), F32)
    r4 = jnp.zeros((tm, TOP_K), jnp.int32)
    for k in range(TOP_K):
        rank = jnp.sum(jnp.where(col == idxs[k], before, 0.0), axis=-1, keepdims=True)
        e4 = jnp.where(col4 == k, idxs[k].astype(jnp.int32), e4)
        g4 = jnp.where(col4 == k, exps[k] / den, g4)
        r4 = jnp.where(col4 == k, rank.astype(jnp.int32), r4)
    e4_ref[...] = e4
    g4_ref[...] = g4
    r4_ref[...] = r4


def _merge(x, ada3, o_ctx, o_lat, ln1, w_g, w_pa, w_pb, w_pc, w_out, ln2, w_r, b_r, seg, tm):
    t, d = x.shape
    ctx_tiles = o_ctx[0].shape[0] // tm
    row = lambda n: pl.BlockSpec((tm, n), lambda i: (i, 0))
    full = lambda a: pl.BlockSpec(a.shape, lambda i: (0,) * a.ndim)
    ctx = pl.BlockSpec((tm, 512), lambda i: (jnp.minimum(i, ctx_tiles - 1), 0))
    lat = pl.BlockSpec((tm, 512), lambda i: (jnp.maximum(i - ctx_tiles, 0), 0))
    return pl.pallas_call(
        functools.partial(_merge_kernel, ctx_tiles=ctx_tiles),
        grid=(t // tm,),
        in_specs=[row(d), pl.BlockSpec((None, 1, 6 * d), lambda i: (seg(i), 0, 0)), ctx, lat, ctx, lat, ctx, lat,
                  full(ln1), full(w_g), full(w_pa), full(w_pb), full(w_pc), full(w_out), full(ln2), full(w_r),
                  full(b_r)],
        out_specs=[row(d), row(d // 2), row(TOP_K), row(TOP_K), row(TOP_K),
                   pl.BlockSpec((SUBLANE, N_EXPERTS), lambda i: (0, 0))],
        out_shape=[jax.ShapeDtypeStruct((t, d), F32), jax.ShapeDtypeStruct((t, d // 2), jnp.int32),
                   jax.ShapeDtypeStruct((t, TOP_K), jnp.int32), jax.ShapeDtypeStruct((t, TOP_K), F32),
                   jax.ShapeDtypeStruct((t, TOP_K), jnp.int32),
                   jax.ShapeDtypeStruct((SUBLANE, N_EXPERTS), F32)],
        scratch_shapes=[pltpu.VMEM((1, N_EXPERTS), F32)],
        compiler_params=_cparams(("arbitrary",)),
        name="merge",
    )(x, ada3, o_ctx[0], o_lat[0], o_ctx[1], o_lat[1], o_ctx[2], o_lat[2], ln1, w_g, w_pa, w_pb, w_pc, w_out, ln2,
      w_r, b_r)


def _expert_kernel(be_ref, nu_ref, x_ref, wgu_ref, bgu_ref, wdn_ref, bdn_ref, y_ref, wgu_s, wdn_s, *, n_chunk):
    b = pl.program_id(0)
    used = b < nu_ref[0]
    new_expert = jnp.logical_or(b == 0, be_ref[b] != be_ref[jnp.maximum(b - 1, 0)])

    @pl.when(jnp.logical_and(used, new_expert))
    def _():
        wgu_s[...] = wgu_ref[...].astype(BF16)
        wdn_s[...] = wdn_ref[...].astype(BF16)

    @pl.when(used)
    def _():
        x = _unpack_bf16_pairs(x_ref[...]).astype(BF16)
        cw = D_FF // n_chunk
        acc = jnp.broadcast_to(bdn_ref[...], x.shape)
        for c in range(n_chunk):
            glu = _dot(x, wgu_s[:, c * cw:(c + 1) * cw]) + bgu_ref[:, c * cw:(c + 1) * cw]
            lin = (_dot(x, wgu_s[:, D_FF + c * cw:D_FF + (c + 1) * cw])
                   + bgu_ref[:, D_FF + c * cw:D_FF + (c + 1) * cw])
            glu = jnp.minimum(glu, SWIGLU_LIMIT)
            lin = jnp.clip(lin, -SWIGLU_LIMIT, SWIGLU_LIMIT)
            act = glu * jax.nn.sigmoid(SWIGLU_ALPHA * glu) * (lin + 1.0)
            acc = acc + _dot(act.astype(BF16), wdn_s[c * cw:(c + 1) * cw, :])
        y_ref[...] = _pack_bf16_pairs(acc)

    @pl.when(jnp.logical_not(used))
    def _():
        y_ref[...] = jnp.zeros_like(y_ref)


def _experts(xb, block_e, n_used, w_gu, b_gu, w_dn, b_dn, layer):
    n_rows, dp = xb.shape
    d = 2 * dp
    depth = w_gu.shape[0]
    blk = EXPERT_BLOCK
    n_blocks = n_rows // blk
    wmap = lambda b, be, nu: (layer, be[b], 0, 0)
    grid_spec = pltpu.PrefetchScalarGridSpec(
        num_scalar_prefetch=2,
        grid=(n_blocks,),
        in_specs=[pl.BlockSpec((blk, dp), lambda b, be, nu: (b, 0)),
                  pl.BlockSpec((None, None, d, 2 * D_FF), wmap),
                  pl.BlockSpec((None, None, 1, 2 * D_FF), wmap),
                  pl.BlockSpec((None, None, D_FF, d), wmap),
                  pl.BlockSpec((None, None, 1, d), wmap)],
        out_specs=pl.BlockSpec((blk, dp), lambda b, be, nu: (b, 0)),
        scratch_shapes=[pltpu.VMEM((d, 2 * D_FF), BF16), pltpu.VMEM((D_FF, d), BF16)],
    )
    return pl.pallas_call(
        functools.partial(_expert_kernel, n_chunk=4),
        grid_spec=grid_spec,
        out_shape=jax.ShapeDtypeStruct((n_rows, dp), jnp.int32),
        compiler_params=_cparams(("arbitrary",)),
        name="experts",
    )(block_e, n_used, xb, w_gu, b_gu.reshape(depth, N_EXPERTS, 1, 2 * D_FF), w_dn,
      b_dn.reshape(depth, N_EXPERTS, 1, d))


def _sc_worker_base(per_w):
    return (lax.axis_index("s") * SC_CORES + lax.axis_index("c")) * per_w


def _sc_gather_rows(table_hbm, out_hbm, idx_v, rows_v, gsem, osem, base, n_chunks):
    ch = SC_CHUNK

    def gather(c, slot):
        return pltpu.make_async_copy(table_hbm.at[idx_v.at[pl.ds(c * ch, ch)]], rows_v.at[slot], gsem.at[slot])

    def put(c, slot):
        return pltpu.make_async_copy(rows_v.at[slot], out_hbm.at[pl.ds(base + c * ch, ch)], osem.at[slot])

    gather(0, 0).start()

    @pl.loop(0, n_chunks // 2)
    def _(i):
        c = 2 * i
        gather(c, 0).wait()
        put(c, 0).start()

        @pl.when(i > 0)
        def _():
            put(c - 1, 1).wait()

        gather(c + 1, 1).start()
        gather(c + 1, 1).wait()
        put(c + 1, 1).start()
        put(c, 0).wait()

        @pl.when(c + 2 < n_chunks)
        def _():
            gather(c + 2, 0).start()

    put(n_chunks - 1, 1).wait()


def _sc_scratch(per_w, d, dtype):
    return [pltpu.VMEM((per_w,), jnp.int32), pltpu.VMEM((2, SC_CHUNK, d), dtype),
            pltpu.SemaphoreType.DMA((2,)), pltpu.SemaphoreType.DMA((2,))]


def _sc_gather(table, idx):
    n_idx = idx.shape[0]
    d = table.shape[1]
    n_workers = SC_CORES * SC_SUBCORES
    per_w = n_idx // n_workers
    assert n_idx % (n_workers * SC_CHUNK * 2) == 0
    mesh = plsc.VectorSubcoreMesh(core_axis_name="c", subcore_axis_name="s")

    @functools.partial(
        pl.kernel, mesh=mesh, out_type=jax.ShapeDtypeStruct((n_idx, d), table.dtype),
        scratch_types=_sc_scratch(per_w, d, table.dtype), name="sc_gather")
    def gather(table_hbm, idx_hbm, out_hbm, idx_v, rows_v, gsem, osem):
        base = _sc_worker_base(per_w)
        pltpu.sync_copy(idx_hbm.at[pl.ds(base, per_w)], idx_v)
        _sc_gather_rows(table_hbm, out_hbm, idx_v, rows_v, gsem, osem, base, per_w // SC_CHUNK)

    return gather(table, idx)


def _sc_dispatch(h, dest, n_rows):
    t, d = h.shape
    n_slots = dest.shape[0]
    n_workers = SC_CORES * SC_SUBCORES
    per_w = n_rows // n_workers
    piece = 4096
    fill_shift = 3
    assert n_rows % (n_workers * SC_CHUNK * 2) == 0 and n_slots % piece == 0 and (n_rows >> fill_shift) <= t
    mesh = plsc.VectorSubcoreMesh(core_axis_name="c", subcore_axis_name="s")

    @functools.partial(
        pl.kernel, mesh=mesh, out_type=jax.ShapeDtypeStruct((n_rows, d), h.dtype),
        scratch_types=[pltpu.VMEM((piece,), jnp.int32)] + _sc_scratch(per_w, d, h.dtype),
        compiler_params=pltpu.CompilerParams(needs_layout_passes=False), name="sc_dispatch")
    def dispatch(h_hbm, dest_hbm, out_hbm, dest_v, idx_v, rows_v, gsem, osem):
        base = _sc_worker_base(per_w)
        lane = lax.iota(jnp.int32, SC_LANES)

        @pl.loop(0, per_w // SC_LANES)
        def _(j):
            idx_v[pl.ds(j * SC_LANES, SC_LANES)] = (base + j * SC_LANES + lane) >> fill_shift

        @pl.loop(0, n_slots // piece)
        def _(p):
            pltpu.sync_copy(dest_hbm.at[pl.ds(p * piece, piece)], dest_v)

            @pl.loop(0, piece // SC_LANES)
            def _(j):
                loc = dest_v[pl.ds(j * SC_LANES, SC_LANES)] - base
                mine = (loc >= 0) & (loc < per_w)
                tok = (p * piece + j * SC_LANES + lane) // TOP_K
                plsc.store_scatter(idx_v, [loc], tok, mask=mine)

        _sc_gather_rows(h_hbm, out_hbm, idx_v, rows_v, gsem, osem, base, per_w // SC_CHUNK)

    return dispatch(h, dest)


def _combine_kernel(x1_ref, ada_ref, yg_ref, g4_ref, fg_ref, o_ref, *, final):
    d = D_MODEL
    g4 = g4_ref[...]
    y = g4[:, 0:1] * _unpack_bf16_pairs(yg_ref[0])
    for k in range(1, TOP_K):
        y = y + g4[:, k:k + 1] * _unpack_bf16_pairs(yg_ref[k])
    x2 = x1_ref[...] + ada_ref[:, 5 * d:6 * d] * y
    if final:
        x2 = _rms(x2, fg_ref[...])
    o_ref[...] = x2


def _combine(x1, ada3, yg, g4, final_g, seg, final):
    t, d = x1.shape
    tm = TOKEN_TILE
    return pl.pallas_call(
        functools.partial(_combine_kernel, final=final),
        grid=(t // tm,),
        in_specs=[pl.BlockSpec((tm, d), lambda i: (i, 0)),
                  pl.BlockSpec((None, 1, 6 * d), lambda i: (seg(i), 0, 0)),
                  pl.BlockSpec((TOP_K, tm, d // 2), lambda i: (0, i, 0)),
                  pl.BlockSpec((tm, TOP_K), lambda i: (i, 0)),
                  pl.BlockSpec((1, d), lambda i: (0, 0))],
        out_specs=pl.BlockSpec((tm, d), lambda i: (i, 0)),
        out_shape=jax.ShapeDtypeStruct((t, d), F32),
        compiler_params=_cparams(("arbitrary",)),
        name="combine",
    )(x1, ada3, yg, g4, final_g)


def _rope_tables(n_lat):
    rows = n_lat // GRID_W
    row_ids = jnp.repeat(jnp.arange(rows, dtype=F32), GRID_W)
    col_ids = jnp.tile(jnp.arange(GRID_W, dtype=F32), rows)

    def table(d_rot, lane0):
        d_axis = d_rot // 2
        nf = d_axis // 2
        inv_freq = ROPE_BASE ** (-jnp.arange(0, d_axis, 2, dtype=F32) / d_axis)
        ang_r = row_ids[:, None] * inv_freq
        ang_c = col_ids[:, None] * inv_freq
        ang = jnp.concatenate([ang_r, ang_r, ang_c, ang_c], axis=-1)
        sign = jnp.tile(jnp.concatenate([-jnp.ones((nf,), F32), jnp.ones((nf,), F32)]), 2)
        cos = jnp.ones((n_lat, LANE), F32)
        sin = jnp.zeros((n_lat, LANE), F32)
        for l0 in lane0:
            cos = cos.at[:, l0:l0 + d_rot].set(jnp.cos(ang))
            sin = sin.at[:, l0:l0 + d_rot].set(jnp.sin(ang) * sign)
        ident = (jnp.ones((TOKEN_TILE, LANE), F32), jnp.zeros((TOKEN_TILE, LANE), F32))
        return jnp.concatenate([ident[0], cos], axis=0), jnp.concatenate([ident[1], sin], axis=0)

    ca, sa = table(HD_A, (0, HD_A))
    cb, sb = table(ROPE_B, (NOPE_B,))
    return ca, sa, cb, sb


def _prep_weights(w_in, sink, w_uq, w_ukv, lru_wa, lru_wx, lru_lam, w_pa):
    depth, d, _ = w_in.shape
    cuts = np.cumsum((512, 128, 128, Q_RANK, KV_RANK, ROPE_B, LRU_W, LRU_W, 3 * d))
    perm = np.array(HEAD_PERM_A)
    w_qa = w_in[:, :, :cuts[0]].reshape(depth, d, N_HEADS_A, HD_A)[:, :, perm].reshape(depth, d, 512)
    w_kr = jnp.pad(w_in[:, :, cuts[4]:cuts[5]], ((0, 0), (0, 0), (NOPE_B, LANE - NOPE_B - ROPE_B)))
    w_main = jnp.concatenate([w_qa, w_in[:, :, cuts[0]:cuts[4]], w_in[:, :, cuts[5]:cuts[7]], w_kr],
                             axis=-1).astype(BF16)
    w_g = w_in[:, :, cuts[7]:].astype(BF16)
    sink_p = sink[:, perm]
    w_pa_p = w_pa.reshape(depth, N_HEADS_A, HD_A, d)[:, perm].reshape(depth, 512, d).astype(BF16)
    hb = NOPE_B + ROPE_B
    w_uq_p = jnp.pad(w_uq.reshape(depth, Q_RANK, N_HEADS_B, hb),
                     ((0, 0), (0, 0), (0, 0), (0, HEAD_PAD_B - hb))).reshape(depth, Q_RANK, -1).astype(BF16)
    w_ukv4 = w_ukv.reshape(depth, KV_RANK, N_HEADS_B, NOPE_B + V_B)
    w_uk_p = jnp.pad(w_ukv4[..., :NOPE_B], ((0, 0), (0, 0), (0, 0), (0, HEAD_PAD_B - NOPE_B))
                     ).reshape(depth, KV_RANK, -1).astype(BF16)
    w_uv = w_ukv4[..., NOPE_B:].reshape(depth, KV_RANK, -1).astype(BF16)

    def block_diag(w):
        hpb = LRU_HEADS // 2
        blk = LRU_W // LRU_HEADS
        w = w.reshape(depth, 2, 2, hpb, blk, blk)
        eye = jnp.eye(hpb, dtype=w.dtype)
        out = jnp.einsum('ldghij,hk->ldghikj', w, eye)
        return out.reshape(depth, 2, 2, hpb * blk, hpb * blk).astype(BF16)

    sp = jax.nn.softplus(-lru_lam.astype(F32))
    return w_main, w_g, sink_p, w_pa_p, w_uq_p, w_uk_p, w_uv, block_diag(lru_wa), block_diag(lru_wx), sp


def kernel(x_prompt, x_sample, cache_k_a, cache_v_a, cache_ckv, cache_krope, state_lru, c, c_ctx, ln1_g, ln2_g,
           final_g, w_ada, b_ada, w_in, sink, g_q, w_uq, g_kv, w_ukv, conv_w, conv_b, lru_wa, lru_ba, lru_wx,
           lru_bx, lru_lam, w_pa, w_pb, w_pc, w_out, w_router, b_router, w_gu, b_gu, w_dn, b_dn):
    n_ctx, s_len, d = x_prompt.shape
    n_lat, n_len, _ = x_sample.shape
    depth = w_in.shape[0]
    past = cache_k_a.shape[2]
    t_ctx = n_ctx * s_len
    t_lat = n_lat * n_len
    t = t_ctx + t_lat
    tm = TOKEN_TILE
    ctx_tiles = t_ctx // tm
    lat_tiles = n_len // tm
    merge_tile = min(MERGE_TILE, n_len)
    assert t_ctx % n_len == 0 and t_ctx % tm == 0 and n_len % tm == 0 and n_lat + 1 <= SUBLANE
    assert t_ctx % merge_tile == 0 and n_len % merge_tile == 0

    seg = lambda i: jnp.where(i < ctx_tiles, 0, 1 + (i - ctx_tiles) // lat_tiles)
    seg_m = lambda i: jnp.where(i < t_ctx // merge_tile, 0, 1 + (i - t_ctx // merge_tile) // (n_len // merge_tile))
    tab_blk = lambda i: jnp.where(i < ctx_tiles, 0, 1 + (i - ctx_tiles) % lat_tiles)

    cond8 = jnp.zeros((SUBLANE, d), F32).at[0].set(c_ctx).at[1:1 + n_lat].set(c)
    ada = _ada_all(cond8, w_ada, b_ada)
    tabs = _rope_tables(n_len)
    (w_main, w_g, sink_p, w_pa_p, w_uq_p, w_uk_p, w_uv, wa_bd, wx_bd, sp) = _prep_weights(
        w_in, sink, w_uq, w_ukv, lru_wa, lru_wx, lru_lam, w_pa)
    w_pb_b, w_pc_b, w_out_b = w_pb.astype(BF16), w_pc.astype(BF16), w_out.astype(BF16)
    w_r_hi = w_router.astype(BF16)
    w_r_cat = jnp.concatenate([w_r_hi, (w_router - w_r_hi.astype(F32)).astype(BF16)], axis=-1)
    cache_k2 = cache_k_a.reshape(n_lat, depth, past, N_KV_A * HD_A)
    cache_v2 = cache_v_a.reshape(n_lat, depth, past, N_KV_A * HD_A)
    cache_kr_pad = jnp.pad(cache_krope, ((0, 0), (0, 0), (0, 0), (NOPE_B, LANE - NOPE_B - ROPE_B)))
    h0_ctx = jnp.zeros((n_ctx, 2, LRU_W), F32)
    row2 = lambda a: a.reshape(1, -1)

    x = jnp.concatenate([x_prompt.reshape(t_ctx, d), x_sample.reshape(t_lat, d)], axis=0)
    ks_a, vs_a, ckvs, krs, lrus = [], [], [], [], []
    n_rows = t * TOP_K + N_EXPERTS * EXPERT_BLOCK
    n_blocks = n_rows // EXPERT_BLOCK
    for l in range(depth):
        ada3 = ada[l].reshape(SUBLANE, 1, 6 * d)
        qa, ka, va, qb, ckvn, kb, vb, kr, xc, yc = _in_proj(
            x, ada3, row2(ln1_g[l]), w_main[l], row2(g_q[l]), w_uq_p[l], row2(g_kv[l]), w_uk_p[l], w_uv[l],
            tabs, seg, tab_blk)
        ks_a.append(ka[:t_ctx].reshape(n_ctx, s_len, N_KV_A, HD_A))
        vs_a.append(va[:t_ctx].reshape(n_ctx, s_len, N_KV_A, HD_A))
        ckvs.append(ckvn[:t_ctx].reshape(n_ctx, s_len, KV_RANK))
        krs.append(kr[:t_ctx, NOPE_B:NOPE_B + ROPE_B].reshape(n_ctx, s_len, ROPE_B))

        oa_c = _gqa_ctx(qa, ka, va, sink_p[l], n_ctx, s_len)
        oa_l = _gqa_lat(qa, ka, va, cache_k2, cache_v2, sink_p[l], l, n_lat, n_len, t_ctx)
        kbx, vbx = _cache_kv(cache_ckv, cache_kr_pad, w_uk_p[l], w_uv[l], l)
        ob_c = _mla_ctx(qb, kb, vb, n_ctx, s_len)
        ob_l = _mla_lat(qb, kb, vb, kbx, vbx, n_lat, n_len, t_ctx)
        lru_args = (conv_w[l], row2(conv_b[l]), wa_bd[l], lru_ba[l], wx_bd[l], lru_bx[l], sp[l])
        oc_c, st_c = _lru(xc, yc, h0_ctx, *lru_args, n_ctx, s_len, 0)
        oc_l, _ = _lru(xc, yc, state_lru[:, l], *lru_args, n_lat, n_len, t_ctx)
        lrus.append(st_c)
        x1, h, e4, g4, r4, cnt = _merge(x, ada3, (oa_c, ob_c, oc_c), (oa_l, ob_l, oc_l), row2(ln1_g[l]), w_g[l],
                                        w_pa_p[l], w_pb_b[l], w_pc_b[l], w_out_b[l], row2(ln2_g[l]), w_r_cat[l],
                                        row2(b_router[l]), seg_m, merge_tile)

        counts = cnt[0].astype(jnp.int32)
        padded = (counts + EXPERT_BLOCK - 1) // EXPERT_BLOCK * EXPERT_BLOCK
        pad_end = jnp.cumsum(padded)
        pad_start = pad_end - padded
        dest4 = pad_start[e4] + r4
        blk_row0 = jnp.arange(n_blocks, dtype=jnp.int32) * EXPERT_BLOCK
        block_e = jnp.minimum(jnp.sum((pad_end[None, :] <= blk_row0[:, None]).astype(jnp.int32), axis=1),
                              N_EXPERTS - 1)
        n_used = (pad_end[-1:] // EXPERT_BLOCK).astype(jnp.int32)
        xb = _sc_dispatch(h, dest4.reshape(-1), n_rows)
        yb = _experts(xb, block_e, n_used, w_gu, b_gu, w_dn, b_dn, l)
        yg = _sc_gather(yb, dest4.T.reshape(-1)).reshape(TOP_K, t, d // 2)
        x = _combine(x1, ada3, yg, g4, row2(final_g), seg, l == depth - 1)

    y_prompt = x[:t_ctx].reshape(n_ctx, s_len, d)
    y_sample = x[t_ctx:].reshape(n_lat, n_len, d)
    return (y_prompt, y_sample, jnp.stack(ks_a, axis=1), jnp.stack(vs_a, axis=1), jnp.stack(ckvs, axis=1),
            jnp.stack(krs, axis=1), jnp.stack(lrus, axis=1))
```

```python
import functools

import numpy as np
import jax
import jax.numpy as jnp
from jax import lax
from jax.experimental import pallas as pl
from jax.experimental.pallas import tpu as pltpu
from jax.experimental.pallas import tpu_sc as plsc

F32 = jnp.float32
BF16 = jnp.bfloat16

D_MODEL = 1024
GRID_W = 64
EPS = 1e-6
ROPE_BASE = 10000.0
NEG_INF = -1e30
HD_A = 64
N_HEADS_A = 8
N_KV_A = 2
WINDOW = 128
N_HEADS_B = 8
NOPE_B = 64
ROPE_B = 32
V_B = 64
Q_RANK = 512
KV_RANK = 256
LRU_W = 512
LRU_HEADS = 8
CONV_W = 4
LRU_C = 8.0
N_EXPERTS = 32
TOP_K = 4
D_FF = D_MODEL
SWIGLU_LIMIT = 7.0
SWIGLU_ALPHA = 1.702

LANE = 128
SUBLANE = 8
TOKEN_TILE = 512
MLA_Q_BLOCK = 1024
GQA_Q_BLOCK = 256
MERGE_TILE = 512
EXPERT_BLOCK = 512
SUPER_BLOCKS = 3
HEAD_PAD_B = 128
LRU_HALF = LRU_W // 2
VMEM_LIMIT = 56 * 1024 * 1024
SC_CORES = 2
SC_SUBCORES = 16
SC_LANES = 16
SC_CHUNK = 64

HEAD_PERM_A = (0, 4, 1, 5, 2, 6, 3, 7)

C_QA, C_KA, C_VA, C_CQ, C_CKV, C_XC, C_YC, C_KR, C_END = 0, 512, 640, 768, 1280, 1536, 2048, 2560, 2688


def _cparams(sem):
    return pltpu.CompilerParams(dimension_semantics=sem, vmem_limit_bytes=VMEM_LIMIT)


def _layer_spec(a, layer):
    zeros = (0,) * (a.ndim - 1)
    return pl.BlockSpec((None,) + a.shape[1:], lambda *_: (layer,) + zeros)


def _ada_spec(ada, layer, seg):
    return pl.BlockSpec((None, None, 1, ada.shape[-1]), lambda i: (layer, seg(i), 0, 0))


def _rms(x, g):
    return x * lax.rsqrt(jnp.mean(x * x, axis=-1, keepdims=True) + EPS) * g


def _dot(a, b):
    return jnp.dot(a, b, preferred_element_type=F32)


def _dot_t(a, b):
    return lax.dot_general(a, b, (((1,), (1,)), ((), ())), preferred_element_type=F32)


def _pack_bf16_pairs(x):
    n = x.shape[1] // 2
    bits = lambda v: pltpu.bitcast(v.astype(BF16).astype(F32), jnp.uint32)
    return pltpu.bitcast((bits(x[:, :n]) >> 16) | (bits(x[:, n:]) & jnp.uint32(0xFFFF0000)), jnp.int32)


def _unpack_bf16_pairs(p):
    p = pltpu.bitcast(p, jnp.uint32)
    lo = pltpu.bitcast(p << 16, F32)
    hi = pltpu.bitcast(p & jnp.uint32(0xFFFF0000), F32)
    return jnp.concatenate([lo, hi], axis=1)


def _ada_kernel(c_ref, w_ref, b_ref, o_ref):
    c = c_ref[...]
    s = c * jax.nn.sigmoid(c)
    o_ref[...] = _dot(s.astype(BF16), w_ref[...].astype(BF16)) + b_ref[...]


def _ada_all(cond8, w_ada, b_ada):
    depth, d, n6 = w_ada.shape
    nb = 1536
    return pl.pallas_call(
        _ada_kernel,
        grid=(depth, n6 // nb),
        in_specs=[pl.BlockSpec((SUBLANE, d), lambda l, j: (0, 0)),
                  pl.BlockSpec((None, d, nb), lambda l, j: (l, 0, j)),
                  pl.BlockSpec((None, 1, nb), lambda l, j: (l, 0, j))],
        out_specs=pl.BlockSpec((None, SUBLANE, nb), lambda l, j: (l, 0, j)),
        out_shape=jax.ShapeDtypeStruct((depth, SUBLANE, n6), F32),
        compiler_params=_cparams(("arbitrary", "arbitrary")),
        name="ada",
    )(cond8, w_ada, b_ada.reshape(depth, 1, n6))


def _swap_halves(x, half):
    n = x.shape[-1]
    lane = lax.broadcasted_iota(jnp.int32, x.shape, x.ndim - 1)
    first = (lane % (2 * half)) < half
    return jnp.where(first, pltpu.roll(x, n - half, x.ndim - 1), pltpu.roll(x, half, x.ndim - 1))


def _rope(x, cos, sin_signed, half):
    reps = x.shape[-1] // cos.shape[-1]
    if reps > 1:
        cos = jnp.concatenate([cos] * reps, axis=-1)
        sin_signed = jnp.concatenate([sin_signed] * reps, axis=-1)
    return x * cos + _swap_halves(x, half) * sin_signed


def _in_kernel(x_ref, ada_ref, ln1_ref, w_ref, gq_ref, wuq_ref, gkv_ref, wuk_ref, wuv_ref,
               ca_ref, sa_ref, cb_ref, sb_ref,
               qa_ref, ka_ref, va_ref, qb_ref, ckvn_ref, kb_ref, vb_ref, kr_ref, xc_ref, yc_ref):
    d = D_MODEL
    x = x_ref[...]
    shift = ada_ref[:, 0:d]
    scale = ada_ref[:, d:2 * d]
    u = _rms(x, ln1_ref[...]) * (1.0 + scale) + shift
    p = _dot(u.astype(BF16), w_ref[...])
    ca, sa, cb, sb = ca_ref[...], sa_ref[...], cb_ref[...], sb_ref[...]

    qa = _rope(p[:, C_QA:C_KA], ca, sa, HD_A // 4) * (HD_A ** -0.5)
    qa_ref[...] = qa.astype(BF16)
    ka_ref[...] = _rope(p[:, C_KA:C_VA], ca, sa, HD_A // 4)
    va_ref[...] = p[:, C_VA:C_CQ]

    cq = _rms(p[:, C_CQ:C_CKV], gq_ref[...])
    qb = _dot(cq.astype(BF16), wuq_ref[...])
    qb = _rope(qb, cb, sb, ROPE_B // 4) * ((NOPE_B + ROPE_B) ** -0.5)
    qb_ref[...] = qb.astype(BF16)

    ckvn = _rms(p[:, C_CKV:C_XC], gkv_ref[...])
    ckvn_ref[...] = ckvn
    ckvn_b = ckvn.astype(BF16)
    kr = _rope(p[:, C_KR:C_END], cb, sb, ROPE_B // 4)
    kr_ref[...] = kr
    kb = _dot(ckvn_b, wuk_ref[...]) + jnp.concatenate([kr] * N_HEADS_B, axis=-1)
    kb_ref[...] = kb.astype(BF16)
    vb_ref[...] = _dot(ckvn_b, wuv_ref[...]).astype(BF16)

    xc_ref[...] = p[:, C_XC:C_YC]
    yc_ref[...] = p[:, C_YC:C_KR]


def _in_proj(x, ada, ln1, w_main, g_q, w_uq, g_kv, w_uk, w_uv, tabs, seg, tab_blk, layer):
    t, d = x.shape
    tm = TOKEN_TILE
    row = lambda n: pl.BlockSpec((tm, n), lambda i: (i, 0))
    full = lambda a: _layer_spec(a, layer)
    tab = pl.BlockSpec((tm, LANE), lambda i: (tab_blk(i), 0))
    sds = lambda n, dt: jax.ShapeDtypeStruct((t, n), dt)
    nb = N_HEADS_B * HEAD_PAD_B
    return pl.pallas_call(
        _in_kernel,
        grid=(t // tm,),
        in_specs=[row(d), _ada_spec(ada, layer, seg), full(ln1), full(w_main),
                  full(g_q), full(w_uq), full(g_kv), full(w_uk), full(w_uv), tab, tab, tab, tab],
        out_specs=[row(512), row(128), row(128), row(nb), row(KV_RANK), row(nb), row(512), row(128),
                   row(LRU_W), row(LRU_W)],
        out_shape=[sds(512, BF16), sds(128, F32), sds(128, F32), sds(nb, BF16), sds(KV_RANK, F32),
                   sds(nb, BF16), sds(512, BF16), sds(128, F32), sds(LRU_W, F32), sds(LRU_W, F32)],
        compiler_params=_cparams(("arbitrary",)),
        name="in_proj",
    )(x, ada, ln1, w_main, g_q, w_uq, g_kv, w_uk, w_uv, *tabs)


def _gqa_heads(q, k_all, v_all, bias, sink_ref, layer, o_ref):
    lane = lax.broadcasted_iota(jnp.int32, (q.shape[0], LANE), 1)
    low = lane < HD_A
    for pair in range(N_HEADS_A // 2):
        qp = q[:, pair * LANE:(pair + 1) * LANE]
        outs = []
        for g in range(N_KV_A):
            qm = jnp.where(low if g == 0 else ~low, qp, jnp.zeros_like(qp))
            s = _dot_t(qm, k_all)
            if bias is not None:
                s = s + bias
            sink = sink_ref[layer, 2 * pair + g]
            m = jnp.maximum(jnp.max(s, axis=-1, keepdims=True), sink)
            e = jnp.exp(s - m)
            l = jnp.sum(e, axis=-1, keepdims=True) + jnp.exp(sink - m)
            outs.append(_dot(e.astype(BF16), v_all) / l)
        o_ref[:, pair * LANE:(pair + 1) * LANE] = jnp.where(low, outs[0], outs[1]).astype(o_ref.dtype)


def _gqa_ctx_kernel(sink_ref, q_ref, k_ref, v_ref, o_ref, *, layer):
    _gqa_heads(q_ref[...], k_ref[...].astype(BF16), v_ref[...].astype(BF16), None, sink_ref, layer, o_ref)


def _gqa_ctx(qa, ka, va, sink_p, layer, n_seq, s_len):
    return pl.pallas_call(
        functools.partial(_gqa_ctx_kernel, layer=layer),
        grid=(n_seq,),
        in_specs=[pl.BlockSpec(memory_space=pltpu.SMEM),
                  pl.BlockSpec((s_len, 512), lambda b: (b, 0)),
                  pl.BlockSpec((s_len, LANE), lambda b: (b, 0)),
                  pl.BlockSpec((s_len, LANE), lambda b: (b, 0))],
        out_specs=pl.BlockSpec((s_len, 512), lambda b: (b, 0)),
        out_shape=jax.ShapeDtypeStruct((n_seq * s_len, 512), BF16),
        compiler_params=_cparams(("arbitrary",)),
        name="gqa_ctx",
    )(sink_p, qa, ka, va)


def _gqa_lat_kernel(sink_ref, q_ref, kp_ref, kc_ref, kn_ref, vp_ref, vc_ref, vn_ref, kx_ref, vx_ref, o_ref, *,
                    layer):
    j = pl.program_id(1)
    last = pl.num_programs(1) - 1
    w = WINDOW
    qb = q_ref.shape[0]
    k_all = jnp.concatenate([kp_ref[...], kc_ref[...], kn_ref[...], kx_ref[...]], axis=0).astype(BF16)
    v_all = jnp.concatenate([vp_ref[...], vc_ref[...], vn_ref[...], vx_ref[...]], axis=0).astype(BF16)
    n_loc = qb + 2 * w
    qi = lax.broadcasted_iota(jnp.int32, (qb, n_loc), 0)
    col = lax.broadcasted_iota(jnp.int32, (qb, n_loc), 1)
    rel = col - w - qi
    zero = jnp.zeros((qb, n_loc), F32)
    neg = jnp.full((qb, n_loc), NEG_INF, F32)
    band = jnp.where(jnp.abs(rel) <= w, zero, neg)
    before = jnp.where(col < w, jnp.where(j > 0, zero, neg), zero)
    after = jnp.where(col >= qb + w, jnp.where(j < last, zero, neg), zero)
    bias = jnp.concatenate([band + before + after, jnp.zeros((qb, kx_ref.shape[0]), F32)], axis=1)
    _gqa_heads(q_ref[...], k_all, v_all, bias, sink_ref, layer, o_ref)


def _gqa_lat(qa, ka, va, kx, vx, sink_p, layer, n_seq, n_len, row0):
    w = WINDOW
    qb = GQA_Q_BLOCK
    r = qb // w
    nqb = n_len // qb
    nwb = n_len // w
    cur = lambda b, j: (row0 // qb + b * nqb + j, 0)
    prev = lambda b, j: (row0 // w + b * nwb + jnp.maximum(r * j - 1, 0), 0)
    nxt = lambda b, j: (row0 // w + b * nwb + jnp.minimum(r * j + r, nwb - 1), 0)
    past = kx.shape[2]
    cache = pl.BlockSpec((None, None, past, LANE), lambda b, j: (b, layer, 0, 0))
    return pl.pallas_call(
        functools.partial(_gqa_lat_kernel, layer=layer),
        grid=(n_seq, nqb),
        in_specs=[pl.BlockSpec(memory_space=pltpu.SMEM),
                  pl.BlockSpec((qb, 512), cur),
                  pl.BlockSpec((w, LANE), prev), pl.BlockSpec((qb, LANE), cur), pl.BlockSpec((w, LANE), nxt),
                  pl.BlockSpec((w, LANE), prev), pl.BlockSpec((qb, LANE), cur), pl.BlockSpec((w, LANE), nxt),
                  cache, cache],
        out_specs=pl.BlockSpec((qb, 512), lambda b, j: (b * nqb + j, 0)),
        out_shape=jax.ShapeDtypeStruct((n_seq * n_len, 512), BF16),
        compiler_params=_cparams(("arbitrary", "arbitrary")),
        name="gqa_lat",
    )(sink_p, qa, ka, ka, ka, va, va, va, kx, vx)


def _mla_kernel(*refs, n_src, chunk):
    q_ref = refs[0]
    kv_refs = refs[1:1 + 2 * n_src]
    o_ref = refs[-1]
    qb = q_ref.shape[0]
    lane = lax.broadcasted_iota(jnp.int32, (qb, LANE), 1)
    outs = []
    for hh in range(2):
        q = q_ref[:, hh * HEAD_PAD_B:(hh + 1) * HEAD_PAD_B]
        m = jnp.full((qb, 1), NEG_INF, F32)
        l = jnp.zeros((qb, 1), F32)
        acc = jnp.zeros((qb, LANE), F32)
        for s_i in range(n_src):
            k_ref, v_ref = kv_refs[2 * s_i], kv_refs[2 * s_i + 1]
            nk = k_ref.shape[0]
            for c in range(nk // chunk):
                k = k_ref[c * chunk:(c + 1) * chunk, hh * HEAD_PAD_B:(hh + 1) * HEAD_PAD_B]
                v = v_ref[c * chunk:(c + 1) * chunk, :]
                s = _dot_t(q, k)
                m_new = jnp.maximum(m, jnp.max(s, axis=-1, keepdims=True))
                alpha = jnp.exp(m - m_new)
                e = jnp.exp(s - m_new)
                l = alpha * l + jnp.sum(e, axis=-1, keepdims=True)
                acc = alpha * acc + _dot(e.astype(BF16), v)
                m = m_new
        outs.append(acc / l)
    o_ref[...] = jnp.where(lane < V_B, outs[0], outs[1]).astype(o_ref.dtype)


def _mla_ctx(qb, kb, vb, n_seq, s_len):
    npair = N_HEADS_B // 2
    return pl.pallas_call(
        functools.partial(_mla_kernel, n_src=1, chunk=s_len),
        grid=(n_seq, npair),
        in_specs=[pl.BlockSpec((s_len, 2 * HEAD_PAD_B), lambda b, p: (b, p)),
                  pl.BlockSpec((s_len, 2 * HEAD_PAD_B), lambda b, p: (b, p)),
                  pl.BlockSpec((s_len, LANE), lambda b, p: (b, p))],
        out_specs=pl.BlockSpec((s_len, LANE), lambda b, p: (b, p)),
        out_shape=jax.ShapeDtypeStruct((n_seq * s_len, 512), BF16),
        compiler_params=_cparams(("arbitrary", "arbitrary")),
        name="mla_ctx",
    )(qb, kb, vb)


def _mla_lat(qb, kb, vb, kbx, vbx, n_seq, n_len, row0):
    npair = N_HEADS_B // 2
    qblk = min(MLA_Q_BLOCK, n_len)
    nqb = n_len // qblk
    past = kbx.shape[0] // n_seq
    qmap = lambda b, p, j: (row0 // qblk + b * nqb + j, p)
    return pl.pallas_call(
        functools.partial(_mla_kernel, n_src=2, chunk=512),
        grid=(n_seq, npair, nqb),
        in_specs=[pl.BlockSpec((qblk, 2 * HEAD_PAD_B), qmap),
                  pl.BlockSpec((n_len, 2 * HEAD_PAD_B), lambda b, p, j: (row0 // n_len + b, p)),
                  pl.BlockSpec((n_len, LANE), lambda b, p, j: (row0 // n_len + b, p)),
                  pl.BlockSpec((past, 2 * HEAD_PAD_B), lambda b, p, j: (b, p)),
                  pl.BlockSpec((past, LANE), lambda b, p, j: (b, p))],
        out_specs=pl.BlockSpec((qblk, LANE), lambda b, p, j: (b * nqb + j, p)),
        out_shape=jax.ShapeDtypeStruct((n_seq * n_len, 512), BF16),
        compiler_params=_cparams(("arbitrary", "arbitrary", "arbitrary")),
        name="mla_lat",
    )(qb, kb, vb, kbx, vbx)


def _cache_kv_kernel(ckv_ref, kr_ref, wuk_ref, wuv_ref, kb_ref, vb_ref):
    c = ckv_ref[...].astype(BF16)
    kb = _dot(c, wuk_ref[...]) + jnp.concatenate([kr_ref[...]] * N_HEADS_B, axis=-1)
    kb_ref[...] = kb.astype(BF16)
    vb_ref[...] = _dot(c, wuv_ref[...]).astype(BF16)


def _cache_kv(cache_ckv, cache_kr_pad, w_uk, w_uv, layer):
    n_seq, _, past, _ = cache_ckv.shape
    nb = N_HEADS_B * HEAD_PAD_B
    return pl.pallas_call(
        _cache_kv_kernel,
        grid=(n_seq,),
        in_specs=[pl.BlockSpec((None, None, past, KV_RANK), lambda b: (b, layer, 0, 0)),
                  pl.BlockSpec((None, None, past, LANE), lambda b: (b, layer, 0, 0)),
                  _layer_spec(w_uk, layer), _layer_spec(w_uv, layer)],
        out_specs=[pl.BlockSpec((past, nb), lambda b: (b, 0)), pl.BlockSpec((past, 512), lambda b: (b, 0))],
        out_shape=[jax.ShapeDtypeStruct((n_seq * past, nb), BF16), jax.ShapeDtypeStruct((n_seq * past, 512), BF16)],
        compiler_params=_cparams(("arbitrary",)),
        name="cache_kv",
    )(cache_ckv, cache_kr_pad, w_uk, w_uv)


def _lru_kernel(xc_ref, yc_ref, h0_ref, cw_ref, cb_ref, wa_ref, ba_ref, wx_ref, bx_ref, sp_ref,
                o_ref, st_ref, pad_ref, xcv_ref, a_ref, b_ref, *, chunk):
    n = xc_ref.shape[0]
    halo = SUBLANE
    pad_ref[0:halo, :] = jnp.zeros((halo, LRU_W), F32)
    pad_ref[halo + n:2 * halo + n, :] = jnp.zeros((halo, LRU_W), F32)
    pad_ref[halo:halo + n, :] = xc_ref[...]
    left = CONV_W // 2
    for c in range(n // chunk):
        r0 = c * chunk
        acc = jnp.broadcast_to(cb_ref[...], (chunk, LRU_W))
        for j in range(CONV_W):
            off = halo + r0 + j - left
            acc = acc + cw_ref[j:j + 1, :] * pad_ref[off:off + chunk, :]
        xcv_ref[r0:r0 + chunk, :] = acc

    row = lax.broadcasted_iota(jnp.int32, (SUBLANE, LRU_W), 0)
    for d in range(2):
        for c in range(n // chunk):
            r0 = c * chunk
            xv = xcv_ref[r0:r0 + chunk, :]
            xb = xv.astype(BF16)
            for hf in range(2):
                cs = slice(hf * LRU_HALF, (hf + 1) * LRU_HALF)
                r = jax.nn.sigmoid(_dot(xb[:, cs], wa_ref[d, hf]) + ba_ref[d:d + 1, cs])
                i = jax.nn.sigmoid(_dot(xb[:, cs], wx_ref[d, hf]) + bx_ref[d:d + 1, cs])
                log_a = (-LRU_C) * r * sp_ref[d:d + 1, cs]
                a = jnp.exp(log_a)
                a_ref[d, r0:r0 + chunk, cs] = a
                b_ref[d, r0:r0 + chunk, cs] = jnp.sqrt(-jnp.tanh(log_a) * (a * a + 1.0)) * (i * xv[:, cs])

    def scan_group(d, grp, h):
        rows = pl.ds(pl.multiple_of(grp * SUBLANE, SUBLANE), SUBLANE)
        a = a_ref[d, rows, :]
        b = b_ref[d, rows, :]
        for sh in (1, 2, 4):
            if d == 0:
                keep = row >= sh
                a_s = jnp.where(keep, pltpu.roll(a, sh, 0), 1.0)
                b_s = jnp.where(keep, pltpu.roll(b, sh, 0), 0.0)
            else:
                keep = row < SUBLANE - sh
                a_s = jnp.where(keep, pltpu.roll(a, SUBLANE - sh, 0), 1.0)
                b_s = jnp.where(keep, pltpu.roll(b, SUBLANE - sh, 0), 0.0)
            b = a * b_s + b
            a = a * a_s
        hrows = a * h + b
        b_ref[d, rows, :] = hrows
        return hrows[SUBLANE - 1:SUBLANE, :] if d == 0 else hrows[0:1, :]

    n_grp = n // SUBLANE

    def body(g, hs):
        return scan_group(0, g, hs[0]), scan_group(1, n_grp - 1 - g, hs[1])

    h_f, h_b = lax.fori_loop(0, n_grp, body, (h0_ref[0:1, :], h0_ref[1:2, :]))
    st_ref[0:1, :] = h_f
    st_ref[1:2, :] = h_b

    for c in range(n // chunk):
        rs = slice(c * chunk, (c + 1) * chunk)
        o_ref[rs, :] = ((b_ref[0, rs, :] + b_ref[1, rs, :]) * jax.nn.gelu(yc_ref[rs, :])).astype(o_ref.dtype)


def _lru(xc, yc, h0, h0_layer, conv_w, conv_b, wa, ba, wx, bx, sp, layer, n_seq, n_len, row0):
    blk0 = row0 // n_len
    full = lambda a: _layer_spec(a, layer)
    seq = pl.BlockSpec((n_len, LRU_W), lambda b: (blk0 + b, 0))
    return pl.pallas_call(
        functools.partial(_lru_kernel, chunk=min(n_len, 256)),
        grid=(n_seq,),
        in_specs=[seq, seq, pl.BlockSpec((None, None, 2, LRU_W), lambda b: (b, h0_layer, 0, 0)),
                  full(conv_w), full(conv_b), full(wa), full(ba), full(wx), full(bx), full(sp)],
        out_specs=[pl.BlockSpec((n_len, LRU_W), lambda b: (b, 0)),
                   pl.BlockSpec((None, 2, LRU_W), lambda b: (b, 0, 0))],
        out_shape=[jax.ShapeDtypeStruct((n_seq * n_len, LRU_W), BF16),
                   jax.ShapeDtypeStruct((n_seq, 2, LRU_W), F32)],
        scratch_shapes=[pltpu.VMEM((n_len + 2 * SUBLANE, LRU_W), F32), pltpu.VMEM((n_len, LRU_W), F32),
                        pltpu.VMEM((2, n_len, LRU_W), F32), pltpu.VMEM((2, n_len, LRU_W), F32)],
        compiler_params=_cparams(("arbitrary",)),
        name="lru",
    )(xc, yc, h0, conv_w, conv_b, wa, ba, wx, bx, sp)


def _merge_kernel(x_ref, ada_ref, oac_ref, oal_ref, obc_ref, obl_ref, occ_ref, ocl_ref, ln1_ref, wg_ref, wpa_ref,
                  wpb_ref, wpc_ref, wout_ref, ln2_ref, wr_ref, br_ref,
                  x1_ref, h_ref, e4_ref, g4_ref, r4_ref, cnt_ref, carry_ref, *, ctx_tiles):
    d = D_MODEL
    tm = x_ref.shape[0]
    i = pl.program_id(0)
    is_ctx = i < ctx_tiles
    oa = jnp.where(is_ctx, oac_ref[...], oal_ref[...])
    ob = jnp.where(is_ctx, obc_ref[...], obl_ref[...])
    oc = jnp.where(is_ctx, occ_ref[...], ocl_ref[...])

    @pl.when(i == 0)
    def _():
        carry_ref[...] = jnp.zeros_like(carry_ref)

    x = x_ref[...]
    ada = ada_ref[...]
    u = _rms(x, ln1_ref[...]) * (1.0 + ada[:, d:2 * d]) + ada[:, 0:d]
    g = jax.nn.sigmoid(_dot(u.astype(BF16), wg_ref[...]))
    m = (g[:, 0:d] * _dot(oa, wpa_ref[...]) + g[:, d:2 * d] * _dot(ob, wpb_ref[...])
         + g[:, 2 * d:3 * d] * _dot(oc, wpc_ref[...]))
    x1 = x + ada[:, 2 * d:3 * d] * _dot(m.astype(BF16), wout_ref[...])
    x1_ref[...] = x1
    h = _rms(x1, ln2_ref[...]) * (1.0 + ada[:, 4 * d:5 * d]) + ada[:, 3 * d:4 * d]
    h_ref[...] = _pack_bf16_pairs(h)

    h_hi = h.astype(BF16)
    h_lo = (h - h_hi.astype(F32)).astype(BF16)
    hw = _dot(h_hi, wr_ref[...])
    logits = (hw[:, :N_EXPERTS] + hw[:, N_EXPERTS:] + _dot(h_lo, wr_ref[:, :N_EXPERTS])) + br_ref[...]
    col = lax.broadcasted_iota(jnp.int32, (tm, N_EXPERTS), 1).astype(F32)
    col4 = lax.broadcasted_iota(jnp.int32, (tm, TOP_K), 1)
    sel_any = jnp.zeros((tm, N_EXPERTS), F32)
    vals, idxs = [], []
    work = logits
    for _k in range(TOP_K):
        mx = jnp.max(work, axis=-1, keepdims=True)
        idx = jnp.min(jnp.where(work == mx, col, float(N_EXPERTS)), axis=-1, keepdims=True)
        sel = col == idx
        vals.append(mx)
        idxs.append(idx)
        sel_any = jnp.where(sel, 1.0, sel_any)
        work = jnp.where(sel, -jnp.inf, work)

    ri = lax.broadcasted_iota(jnp.int32, (tm, tm), 0)
    ci = lax.broadcasted_iota(jnp.int32, (tm, tm), 1)
    tri = jnp.where(ri > ci, 1.0, 0.0).astype(BF16)
    before = _dot(tri, sel_any.astype(BF16)) + carry_ref[...]
    carry = carry_ref[...] + jnp.sum(sel_any, axis=0, keepdims=True)
    carry_ref[...] = carry
    cnt_ref[...] = jnp.broadcast_to(carry, cnt_ref.shape)

    exps = [jnp.exp(v - vals[0]) for v in vals]
    den = exps[0] + exps[1] + exps[2] + exps[3]
    e4 = jnp.zeros((tm, TOP_K), jnp.int32)
    g4 = jnp.zeros((tm, TOP_K), F32)
    r4 = jnp.zeros((tm, TOP_K), jnp.int32)
    for k in range(TOP_K):
        rank = jnp.sum(jnp.where(col == idxs[k], before, 0.0), axis=-1, keepdims=True)
        e4 = jnp.where(col4 == k, idxs[k].astype(jnp.int32), e4)
        g4 = jnp.where(col4 == k, exps[k] / den, g4)
        r4 = jnp.where(col4 == k, rank.astype(jnp.int32), r4)
    e4_ref[...] = e4
    g4_ref[...] = g4
    r4_ref[...] = r4


def _merge(x, ada, o_ctx, o_lat, ln1, w_g, w_pa, w_pb, w_pc, w_out, ln2, w_r, b_r, seg, tm, layer):
    t, d = x.shape
    ctx_tiles = o_ctx[0].shape[0] // tm
    row = lambda n: pl.BlockSpec((tm, n), lambda i: (i, 0))
    full = lambda a: _layer_spec(a, layer)
    ctx = pl.BlockSpec((tm, 512), lambda i: (jnp.minimum(i, ctx_tiles - 1), 0))
    lat = pl.BlockSpec((tm, 512), lambda i: (jnp.maximum(i - ctx_tiles, 0), 0))
    return pl.pallas_call(
        functools.partial(_merge_kernel, ctx_tiles=ctx_tiles),
        grid=(t // tm,),
        in_specs=[row(d), _ada_spec(ada, layer, seg), ctx, lat, ctx, lat, ctx, lat,
                  full(ln1), full(w_g), full(w_pa), full(w_pb), full(w_pc), full(w_out), full(ln2), full(w_r),
                  full(b_r)],
        out_specs=[row(d), row(d // 2), row(TOP_K), row(TOP_K), row(TOP_K),
                   pl.BlockSpec((SUBLANE, N_EXPERTS), lambda i: (0, 0))],
        out_shape=[jax.ShapeDtypeStruct((t, d), F32), jax.ShapeDtypeStruct((t, d // 2), jnp.int32),
                   jax.ShapeDtypeStruct((t, TOP_K), jnp.int32), jax.ShapeDtypeStruct((t, TOP_K), F32),
                   jax.ShapeDtypeStruct((t, TOP_K), jnp.int32),
                   jax.ShapeDtypeStruct((SUBLANE, N_EXPERTS), F32)],
        scratch_shapes=[pltpu.VMEM((1, N_EXPERTS), F32)],
        compiler_params=_cparams(("arbitrary",)),
        name="merge",
    )(x, ada, o_ctx[0], o_lat[0], o_ctx[1], o_lat[1], o_ctx[2], o_lat[2], ln1, w_g, w_pa, w_pb, w_pc, w_out, ln2,
      w_r, b_r)


def _expert_kernel(se_ref, sr_ref, sn_ref, so_ref, x_hbm, wgu_ref, bgu_ref, wdn_ref, bdn_ref, y_hbm,
                   wgu_s, wdn_s, xbuf, xb16, acc_ref, ybuf, xsem, ysem, *, n_chunk):
    s = pl.program_id(0)
    n_steps = pl.num_programs(0)
    blk = EXPERT_BLOCK
    slot = lax.rem(s, 2)
    nb = sn_ref[s]
    no = so_ref[s]

    def x_copy(step, b, slt):
        row = pl.multiple_of(sr_ref[step] + b * blk, blk)
        return pltpu.make_async_copy(x_hbm.at[pl.ds(row, blk)], xbuf.at[slt, pl.ds(b * blk, blk)], xsem.at[slt])

    def y_copy(step, b):
        row = pl.multiple_of(sr_ref[step] + b * blk, blk)
        return pltpu.make_async_copy(ybuf.at[pl.ds(b * blk, blk)], y_hbm.at[pl.ds(row, blk)], ysem.at[0])

    def for_blocks(count, fn):
        for b in range(SUPER_BLOCKS):
            pl.when(b < count)(functools.partial(fn, b))

    @pl.when(s == 0)
    def _():
        for_blocks(sn_ref[0], lambda b: x_copy(0, b, 0).start())

    @pl.when(s + 1 < n_steps)
    def _():
        for_blocks(sn_ref[s + 1], lambda b: x_copy(s + 1, b, 1 - slot).start())

    new_expert = jnp.logical_or(s == 0, se_ref[s] != se_ref[jnp.maximum(s - 1, 0)])

    @pl.when(jnp.logical_and(nb > 0, new_expert))
    def _():
        wgu_s[...] = wgu_ref[...].astype(BF16)
        wdn_s[...] = wdn_ref[...].astype(BF16)

    for_blocks(nb, lambda b: x_copy(s, b, slot).wait())

    cw = D_FF // n_chunk
    for m in range(1, SUPER_BLOCKS + 1):
        @pl.when(nb == m)
        def _(m=m):
            rows = m * blk
            xb16[0:rows, :] = _unpack_bf16_pairs(xbuf[slot, 0:rows, :]).astype(BF16)
            acc_ref[0:rows, :] = jnp.broadcast_to(bdn_ref[...], (rows, D_MODEL))
            for c in range(n_chunk):
                x = xb16[0:rows, :]
                glu = _dot(x, wgu_s[:, c * cw:(c + 1) * cw]) + bgu_ref[:, c * cw:(c + 1) * cw]
                lin = (_dot(x, wgu_s[:, D_FF + c * cw:D_FF + (c + 1) * cw])
                       + bgu_ref[:, D_FF + c * cw:D_FF + (c + 1) * cw])
                glu = jnp.minimum(glu, SWIGLU_LIMIT)
                lin = jnp.clip(lin, -SWIGLU_LIMIT, SWIGLU_LIMIT)
                act = glu * jax.nn.sigmoid(SWIGLU_ALPHA * glu) * (lin + 1.0)
                acc_ref[0:rows, :] += _dot(act.astype(BF16), wdn_s[c * cw:(c + 1) * cw, :])

    @pl.when(s > 0)
    def _():
        for_blocks(so_ref[s - 1], lambda b: y_copy(s - 1, b).wait())

    for m in range(1, SUPER_BLOCKS + 1):
        @pl.when(nb == m)
        def _(m=m):
            ybuf[0:m * blk, :] = _pack_bf16_pairs(acc_ref[0:m * blk, :])

    @pl.when(jnp.logical_and(nb == 0, no > 0))
    def _():
        ybuf[...] = jnp.zeros_like(ybuf)

    for_blocks(no, lambda b: y_copy(s, b).start())

    @pl.when(s == n_steps - 1)
    def _():
        for_blocks(no, lambda b: y_copy(s, b).wait())


def _expert_steps(pad_start, padded, n_rows):
    blk, sb = EXPERT_BLOCK, SUPER_BLOCKS
    n_steps = (n_rows // blk + (N_EXPERTS + 1) * (sb - 1)) // sb + 1
    nb_e = padded // blk
    n_e = (nb_e + sb - 1) // sb
    end = jnp.cumsum(n_e)
    start = end - n_e
    s_idx = jnp.arange(n_steps, dtype=jnp.int32)
    e_of_s = jnp.minimum(jnp.sum((end[None, :] <= s_idx[:, None]).astype(jnp.int32), axis=1), N_EXPERTS - 1)
    onehot = (e_of_s[:, None] == jnp.arange(N_EXPERTS, dtype=jnp.int32)[None, :]).astype(jnp.int32)
    pick = lambda v: jnp.sum(onehot * v[None, :], axis=1)
    k = s_idx - pick(start)
    is_c = s_idx < end[-1]
    sn = jnp.where(is_c, jnp.clip(pick(nb_e) - sb * k, 0, sb), 0)
    row_c = pick(pad_start) + k * (sb * blk)
    row_z = pad_start[-1] + padded[-1] + (s_idx - end[-1]) * (sb * blk)
    zn = jnp.where(is_c, 0, jnp.clip((n_rows - row_z) // blk, 0, sb))
    sr = jnp.where(is_c, row_c, jnp.clip(row_z, 0, n_rows - blk))
    i32 = lambda a: a.astype(jnp.int32)
    return i32(e_of_s), i32(sr), i32(sn), i32(sn + zn)


def _experts(xb, steps, w_gu, b_gu, w_dn, b_dn, layer):
    n_rows, dp = xb.shape
    d = 2 * dp
    depth = w_gu.shape[0]
    rows = SUPER_BLOCKS * EXPERT_BLOCK
    wmap = lambda s, se, sr, sn, so: (layer, se[s], 0, 0)
    grid_spec = pltpu.PrefetchScalarGridSpec(
        num_scalar_prefetch=4,
        grid=(steps[0].shape[0],),
        in_specs=[pl.BlockSpec(memory_space=pl.ANY),
                  pl.BlockSpec((None, None, d, 2 * D_FF), wmap),
                  pl.BlockSpec((None, None, 1, 2 * D_FF), wmap),
                  pl.BlockSpec((None, None, D_FF, d), wmap),
                  pl.BlockSpec((None, None, 1, d), wmap)],
        out_specs=pl.BlockSpec(memory_space=pl.ANY),
        scratch_shapes=[pltpu.VMEM((d, 2 * D_FF), BF16), pltpu.VMEM((D_FF, d), BF16),
                        pltpu.VMEM((2, rows, dp), jnp.int32), pltpu.VMEM((rows, d), BF16),
                        pltpu.VMEM((rows, d), F32), pltpu.VMEM((rows, dp), jnp.int32),
                        pltpu.SemaphoreType.DMA((2,)), pltpu.SemaphoreType.DMA((1,))],
    )
    return pl.pallas_call(
        functools.partial(_expert_kernel, n_chunk=4),
        grid_spec=grid_spec,
        out_shape=jax.ShapeDtypeStruct((n_rows, dp), jnp.int32),
        compiler_params=_cparams(("arbitrary",)),
        name="experts",
    )(*steps, xb, w_gu, b_gu.reshape(depth, N_EXPERTS, 1, 2 * D_FF), w_dn, b_dn.reshape(depth, N_EXPERTS, 1, d))


def _sc_worker_base(per_w):
    return (lax.axis_index("s") * SC_CORES + lax.axis_index("c")) * per_w


def _sc_gather_rows(table_hbm, out_hbm, idx_v, rows_v, gsem, osem, base, n_chunks):
    ch = SC_CHUNK

    def gather(c, slot):
        return pltpu.make_async_copy(table_hbm.at[idx_v.at[pl.ds(c * ch, ch)]], rows_v.at[slot], gsem.at[slot])

    def put(c, slot):
        return pltpu.make_async_copy(rows_v.at[slot], out_hbm.at[pl.ds(base + c * ch, ch)], osem.at[slot])

    gather(0, 0).start()

    @pl.loop(0, n_chunks // 2)
    def _(i):
        c = 2 * i
        gather(c, 0).wait()
        put(c, 0).start()

        @pl.when(i > 0)
        def _():
            put(c - 1, 1).wait()

        gather(c + 1, 1).start()
        gather(c + 1, 1).wait()
        put(c + 1, 1).start()
        put(c, 0).wait()

        @pl.when(c + 2 < n_chunks)
        def _():
            gather(c + 2, 0).start()

    put(n_chunks - 1, 1).wait()


def _sc_scratch(per_w, d, dtype):
    return [pltpu.VMEM((per_w,), jnp.int32), pltpu.VMEM((2, SC_CHUNK, d), dtype),
            pltpu.SemaphoreType.DMA((2,)), pltpu.SemaphoreType.DMA((2,))]


def _sc_gather(table, idx):
    n_idx = idx.shape[0]
    d = table.shape[1]
    n_workers = SC_CORES * SC_SUBCORES
    per_w = n_idx // n_workers
    assert n_idx % (n_workers * SC_CHUNK * 2) == 0
    mesh = plsc.VectorSubcoreMesh(core_axis_name="c", subcore_axis_name="s")

    @functools.partial(
        pl.kernel, mesh=mesh, out_type=jax.ShapeDtypeStruct((n_idx, d), table.dtype),
        scratch_types=_sc_scratch(per_w, d, table.dtype), name="sc_gather")
    def gather(table_hbm, idx_hbm, out_hbm, idx_v, rows_v, gsem, osem):
        base = _sc_worker_base(per_w)
        pltpu.sync_copy(idx_hbm.at[pl.ds(base, per_w)], idx_v)
        _sc_gather_rows(table_hbm, out_hbm, idx_v, rows_v, gsem, osem, base, per_w // SC_CHUNK)

    return gather(table, idx)


def _sc_dispatch(h, dest, n_rows):
    t, d = h.shape
    n_slots = dest.shape[0]
    n_workers = SC_CORES * SC_SUBCORES
    per_w = n_rows // n_workers
    piece = 16384
    unroll = 8
    fill_shift = 3
    assert n_rows % (n_workers * SC_CHUNK * 2) == 0 and n_slots % piece == 0 and (n_rows >> fill_shift) <= t
    mesh = plsc.VectorSubcoreMesh(core_axis_name="c", subcore_axis_name="s")

    @functools.partial(
        pl.kernel, mesh=mesh, out_type=jax.ShapeDtypeStruct((n_rows, d), h.dtype),
        scratch_types=[pltpu.VMEM((piece,), jnp.int32)] + _sc_scratch(per_w, d, h.dtype),
        compiler_params=pltpu.CompilerParams(needs_layout_passes=False), name="sc_dispatch")
    def dispatch(h_hbm, dest_hbm, out_hbm, dest_v, idx_v, rows_v, gsem, osem):
        base = _sc_worker_base(per_w)
        lane = lax.iota(jnp.int32, SC_LANES)

        @pl.loop(0, per_w // SC_LANES)
        def _(j):
            idx_v[pl.ds(j * SC_LANES, SC_LANES)] = (base + j * SC_LANES + lane) >> fill_shift

        @pl.loop(0, n_slots // piece)
        def _(p):
            pltpu.sync_copy(dest_hbm.at[pl.ds(p * piece, piece)], dest_v)

            @pl.loop(0, piece // (SC_LANES * unroll))
            def _(j):
                for u in range(unroll):
                    s0 = (j * unroll + u) * SC_LANES
                    loc = dest_v[pl.ds(s0, SC_LANES)] - base
                    mine = (loc >= 0) & (loc < per_w)
                    tok = (p * piece + s0 + lane) // TOP_K
                    plsc.store_scatter(idx_v, [loc], tok, mask=mine)

        _sc_gather_rows(h_hbm, out_hbm, idx_v, rows_v, gsem, osem, base, per_w // SC_CHUNK)

    return dispatch(h, dest)


def _combine_kernel(x1_ref, ada_ref, yg_ref, g4_ref, fg_ref, o_ref, *, final):
    d = D_MODEL
    g4 = g4_ref[...]
    y = g4[:, 0:1] * _unpack_bf16_pairs(yg_ref[0])
    for k in range(1, TOP_K):
        y = y + g4[:, k:k + 1] * _unpack_bf16_pairs(yg_ref[k])
    x2 = x1_ref[...] + ada_ref[:, 5 * d:6 * d] * y
    if final:
        x2 = _rms(x2, fg_ref[...])
    o_ref[...] = x2


def _combine(x1, ada, yg, g4, final_g, seg, layer, final):
    t, d = x1.shape
    tm = TOKEN_TILE
    return pl.pallas_call(
        functools.partial(_combine_kernel, final=final),
        grid=(t // tm,),
        in_specs=[pl.BlockSpec((tm, d), lambda i: (i, 0)),
                  _ada_spec(ada, layer, seg),
                  pl.BlockSpec((TOP_K, tm, d // 2), lambda i: (0, i, 0)),
                  pl.BlockSpec((tm, TOP_K), lambda i: (i, 0)),
                  pl.BlockSpec((1, d), lambda i: (0, 0))],
        out_specs=pl.BlockSpec((tm, d), lambda i: (i, 0)),
        out_shape=jax.ShapeDtypeStruct((t, d), F32),
        compiler_params=_cparams(("arbitrary",)),
        name="combine",
    )(x1, ada, yg, g4, final_g)


def _rope_tables(n_lat):
    rows = n_lat // GRID_W
    row_ids = jnp.repeat(jnp.arange(rows, dtype=F32), GRID_W)
    col_ids = jnp.tile(jnp.arange(GRID_W, dtype=F32), rows)

    def table(d_rot, lane0):
        d_axis = d_rot // 2
        nf = d_axis // 2
        inv_freq = ROPE_BASE ** (-jnp.arange(0, d_axis, 2, dtype=F32) / d_axis)
        ang_r = row_ids[:, None] * inv_freq
        ang_c = col_ids[:, None] * inv_freq
        ang = jnp.concatenate([ang_r, ang_r, ang_c, ang_c], axis=-1)
        sign = jnp.tile(jnp.concatenate([-jnp.ones((nf,), F32), jnp.ones((nf,), F32)]), 2)
        cos = jnp.ones((n_lat, LANE), F32)
        sin = jnp.zeros((n_lat, LANE), F32)
        for l0 in lane0:
            cos = cos.at[:, l0:l0 + d_rot].set(jnp.cos(ang))
            sin = sin.at[:, l0:l0 + d_rot].set(jnp.sin(ang) * sign)
        ident = (jnp.ones((TOKEN_TILE, LANE), F32), jnp.zeros((TOKEN_TILE, LANE), F32))
        return jnp.concatenate([ident[0], cos], axis=0), jnp.concatenate([ident[1], sin], axis=0)

    ca, sa = table(HD_A, (0, HD_A))
    cb, sb = table(ROPE_B, (NOPE_B,))
    return ca, sa, cb, sb


def _prep_weights(w_in, sink, w_uq, w_ukv, lru_wa, lru_wx, lru_lam, w_pa):
    depth, d, _ = w_in.shape
    cuts = np.cumsum((512, 128, 128, Q_RANK, KV_RANK, ROPE_B, LRU_W, LRU_W, 3 * d))
    perm = np.array(HEAD_PERM_A)
    w_qa = w_in[:, :, :cuts[0]].reshape(depth, d, N_HEADS_A, HD_A)[:, :, perm].reshape(depth, d, 512)
    w_kr = jnp.pad(w_in[:, :, cuts[4]:cuts[5]], ((0, 0), (0, 0), (NOPE_B, LANE - NOPE_B - ROPE_B)))
    w_main = jnp.concatenate([w_qa, w_in[:, :, cuts[0]:cuts[4]], w_in[:, :, cuts[5]:cuts[7]], w_kr],
                             axis=-1).astype(BF16)
    w_g = w_in[:, :, cuts[7]:].astype(BF16)
    sink_p = sink[:, perm]
    w_pa_p = w_pa.reshape(depth, N_HEADS_A, HD_A, d)[:, perm].reshape(depth, 512, d).astype(BF16)
    hb = NOPE_B + ROPE_B
    w_uq_p = jnp.pad(w_uq.reshape(depth, Q_RANK, N_HEADS_B, hb),
                     ((0, 0), (0, 0), (0, 0), (0, HEAD_PAD_B - hb))).reshape(depth, Q_RANK, -1).astype(BF16)
    w_ukv4 = w_ukv.reshape(depth, KV_RANK, N_HEADS_B, NOPE_B + V_B)
    w_uk_p = jnp.pad(w_ukv4[..., :NOPE_B], ((0, 0), (0, 0), (0, 0), (0, HEAD_PAD_B - NOPE_B))
                     ).reshape(depth, KV_RANK, -1).astype(BF16)
    w_uv = w_ukv4[..., NOPE_B:].reshape(depth, KV_RANK, -1).astype(BF16)

    def block_diag(w):
        hpb = LRU_HEADS // 2
        blk = LRU_W // LRU_HEADS
        w = w.reshape(depth, 2, 2, hpb, blk, blk)
        eye = jnp.eye(hpb, dtype=w.dtype)
        out = jnp.einsum('ldghij,hk->ldghikj', w, eye)
        return out.reshape(depth, 2, 2, hpb * blk, hpb * blk).astype(BF16)

    sp = jax.nn.softplus(-lru_lam.astype(F32))
    return w_main, w_g, sink_p, w_pa_p, w_uq_p, w_uk_p, w_uv, block_diag(lru_wa), block_diag(lru_wx), sp


def kernel(x_prompt, x_sample, cache_k_a, cache_v_a, cache_ckv, cache_krope, state_lru, c, c_ctx, ln1_g, ln2_g,
           final_g, w_ada, b_ada, w_in, sink, g_q, w_uq, g_kv, w_ukv, conv_w, conv_b, lru_wa, lru_ba, lru_wx,
           lru_bx, lru_lam, w_pa, w_pb, w_pc, w_out, w_router, b_router, w_gu, b_gu, w_dn, b_dn):
    n_ctx, s_len, d = x_prompt.shape
    n_lat, n_len, _ = x_sample.shape
    depth = w_in.shape[0]
    past = cache_k_a.shape[2]
    t_ctx = n_ctx * s_len
    t_lat = n_lat * n_len
    t = t_ctx + t_lat
    tm = TOKEN_TILE
    ctx_tiles = t_ctx // tm
    lat_tiles = n_len // tm
    merge_tile = min(MERGE_TILE, n_len)
    assert t_ctx % n_len == 0 and t_ctx % tm == 0 and n_len % tm == 0 and n_lat + 1 <= SUBLANE
    assert t_ctx % merge_tile == 0 and n_len % merge_tile == 0

    seg = lambda i: jnp.where(i < ctx_tiles, 0, 1 + (i - ctx_tiles) // lat_tiles)
    seg_m = lambda i: jnp.where(i < t_ctx // merge_tile, 0, 1 + (i - t_ctx // merge_tile) // (n_len // merge_tile))
    tab_blk = lambda i: jnp.where(i < ctx_tiles, 0, 1 + (i - ctx_tiles) % lat_tiles)

    cond8 = jnp.zeros((SUBLANE, d), F32).at[0].set(c_ctx).at[1:1 + n_lat].set(c)
    ada = _ada_all(cond8, w_ada, b_ada)
    tabs = _rope_tables(n_len)
    (w_main, w_g, sink_p, w_pa_p, w_uq_p, w_uk_p, w_uv, wa_bd, wx_bd, sp) = _prep_weights(
        w_in, sink, w_uq, w_ukv, lru_wa, lru_wx, lru_lam, w_pa)
    w_pb_b, w_pc_b, w_out_b = w_pb.astype(BF16), w_pc.astype(BF16), w_out.astype(BF16)
    w_r_hi = w_router.astype(BF16)
    w_r_cat = jnp.concatenate([w_r_hi, (w_router - w_r_hi.astype(F32)).astype(BF16)], axis=-1)
    cache_k2 = cache_k_a.reshape(n_lat, depth, past, N_KV_A * HD_A)
    cache_v2 = cache_v_a.reshape(n_lat, depth, past, N_KV_A * HD_A)
    cache_kr_pad = jnp.pad(cache_krope, ((0, 0), (0, 0), (0, 0), (NOPE_B, LANE - NOPE_B - ROPE_B)))
    h0_ctx = jnp.zeros((n_ctx, 1, 2, LRU_W), F32)
    rows3 = lambda a: a.reshape(depth, 1, -1)
    ada = ada.reshape(depth, SUBLANE, 1, 6 * d)
    ln1_3, ln2_3, g_q3, g_kv3, conv_b3, b_r3 = (rows3(a) for a in (ln1_g, ln2_g, g_q, g_kv, conv_b, b_router))

    x = jnp.concatenate([x_prompt.reshape(t_ctx, d), x_sample.reshape(t_lat, d)], axis=0)
    ks_a, vs_a, ckvs, krs, lrus = [], [], [], [], []
    n_rows = t * TOP_K + N_EXPERTS * EXPERT_BLOCK
    for l in range(depth):
        qa, ka, va, qb, ckvn, kb, vb, kr, xc, yc = _in_proj(
            x, ada, ln1_3, w_main, g_q3, w_uq_p, g_kv3, w_uk_p, w_uv, tabs, seg, tab_blk, l)
        ks_a.append(ka[:t_ctx].reshape(n_ctx, s_len, N_KV_A, HD_A))
        vs_a.append(va[:t_ctx].reshape(n_ctx, s_len, N_KV_A, HD_A))
        ckvs.append(ckvn[:t_ctx].reshape(n_ctx, s_len, KV_RANK))
        krs.append(kr[:t_ctx, NOPE_B:NOPE_B + ROPE_B].reshape(n_ctx, s_len, ROPE_B))

        oa_c = _gqa_ctx(qa, ka, va, sink_p, l, n_ctx, s_len)
        oa_l = _gqa_lat(qa, ka, va, cache_k2, cache_v2, sink_p, l, n_lat, n_len, t_ctx)
        kbx, vbx = _cache_kv(cache_ckv, cache_kr_pad, w_uk_p, w_uv, l)
        ob_c = _mla_ctx(qb, kb, vb, n_ctx, s_len)
        ob_l = _mla_lat(qb, kb, vb, kbx, vbx, n_lat, n_len, t_ctx)
        lru_args = (conv_w, conv_b3, wa_bd, lru_ba, wx_bd, lru_bx, sp, l)
        oc_c, st_c = _lru(xc, yc, h0_ctx, 0, *lru_args, n_ctx, s_len, 0)
        oc_l, _ = _lru(xc, yc, state_lru, l, *lru_args, n_lat, n_len, t_ctx)
        lrus.append(st_c)
        x1, h, e4, g4, r4, cnt = _merge(x, ada, (oa_c, ob_c, oc_c), (oa_l, ob_l, oc_l), ln1_3, w_g, w_pa_p, w_pb_b,
                                        w_pc_b, w_out_b, ln2_3, w_r_cat, b_r3, seg_m, merge_tile, l)

        counts = cnt[0].astype(jnp.int32)
        padded = (counts + EXPERT_BLOCK - 1) // EXPERT_BLOCK * EXPERT_BLOCK
        pad_end = jnp.cumsum(padded)
        pad_start = pad_end - padded
        dest4 = pad_start[e4] + r4
        xb = _sc_dispatch(h, dest4.reshape(-1), n_rows)
        yb = _experts(xb, _expert_steps(pad_start, padded, n_rows), w_gu, b_gu, w_dn, b_dn, l)
        yg = _sc_gather(yb, dest4.T.reshape(-1)).reshape(TOP_K, t, d // 2)
        x = _combine(x1, ada, yg, g4, final_g.reshape(1, d), seg, l, l == depth - 1)

    y_prompt = x[:t_ctx].reshape(n_ctx, s_len, d)
    y_sample = x[t_ctx:].reshape(n_lat, n_len, d)
    return (y_prompt, y_sample, jnp.stack(ks_a, axis=1), jnp.stack(vs_a, axis=1), jnp.stack(ckvs, axis=1),
            jnp.stack(krs, axis=1), jnp.stack(lrus, axis=1))
```

```python
import functools

import numpy as np
import jax
import jax.numpy as jnp
from jax import lax
from jax.experimental import pallas as pl
from jax.experimental.pallas import tpu as pltpu
from jax.experimental.pallas import tpu_sc as plsc

F32 = jnp.float32
BF16 = jnp.bfloat16

D_MODEL = 1024
GRID_W = 64
EPS = 1e-6
ROPE_BASE = 10000.0
NEG_INF = -1e30
HD_A = 64
N_HEADS_A = 8
N_KV_A = 2
WINDOW = 128
N_HEADS_B = 8
NOPE_B = 64
ROPE_B = 32
V_B = 64
Q_RANK = 512
KV_RANK = 256
LRU_W = 512
LRU_HEADS = 8
CONV_W = 4
LRU_C = 8.0
N_EXPERTS = 32
TOP_K = 4
D_FF = D_MODEL
SWIGLU_LIMIT = 7.0
SWIGLU_ALPHA = 1.702

LANE = 128
SUBLANE = 8
TOKEN_TILE = 512
MLA_Q_BLOCK = 1024
GQA_Q_BLOCK = 256
MERGE_TILE = 512
EXPERT_BLOCK = 512
HEAD_PAD_B = 128
LRU_HALF = LRU_W // 2
VMEM_LIMIT = 56 * 1024 * 1024
SC_CORES = 2
SC_SUBCORES = 16
SC_LANES = 16
SC_CHUNK = 64

HEAD_PERM_A = (0, 4, 1, 5, 2, 6, 3, 7)

C_QA, C_KA, C_VA, C_CQ, C_CKV, C_XC, C_YC, C_KR, C_END = 0, 512, 640, 768, 1280, 1536, 2048, 2560, 2688


def _cparams(sem):
    return pltpu.CompilerParams(dimension_semantics=sem, vmem_limit_bytes=VMEM_LIMIT)


def _layer_spec(a, layer):
    zeros = (0,) * (a.ndim - 1)
    return pl.BlockSpec((None,) + a.shape[1:], lambda *_: (layer,) + zeros)


def _ada_spec(ada, layer, seg):
    return pl.BlockSpec((None, None, 1, ada.shape[-1]), lambda i: (layer, seg(i), 0, 0))


def _stream_specs(x_ctx, x_lat, t_ctx, tm):
    ctx_tiles = t_ctx // tm
    lat0 = 0 if x_lat is not x_ctx else ctx_tiles
    d = x_ctx.shape[1]
    return (pl.BlockSpec((tm, d), lambda i: (jnp.minimum(i, ctx_tiles - 1), 0)),
            pl.BlockSpec((tm, d), lambda i: (jnp.maximum(i - ctx_tiles, 0) + lat0, 0)))


def _rms(x, g):
    return x * lax.rsqrt(jnp.mean(x * x, axis=-1, keepdims=True) + EPS) * g


def _dot(a, b):
    return jnp.dot(a, b, preferred_element_type=F32)


def _dot_t(a, b):
    return lax.dot_general(a, b, (((1,), (1,)), ((), ())), preferred_element_type=F32)


def _sigmoid(x):
    return 0.5 * jnp.tanh(0.5 * x) + 0.5


def _pack_bf16_pairs(x):
    n = x.shape[1] // 2
    bits = lambda v: pltpu.bitcast(v.astype(BF16).astype(F32), jnp.uint32)
    return pltpu.bitcast((bits(x[:, :n]) >> 16) | (bits(x[:, n:]) & jnp.uint32(0xFFFF0000)), jnp.int32)


def _unpack_bf16_pairs(p):
    p = pltpu.bitcast(p, jnp.uint32)
    lo = pltpu.bitcast(p << 16, F32)
    hi = pltpu.bitcast(p & jnp.uint32(0xFFFF0000), F32)
    return jnp.concatenate([lo, hi], axis=1)


def _ada_kernel(c_ref, w_ref, b_ref, o_ref):
    c = c_ref[...]
    s = c * jax.nn.sigmoid(c)
    o_ref[...] = _dot(s.astype(BF16), w_ref[...].astype(BF16)) + b_ref[...]


def _ada_all(cond8, w_ada, b_ada):
    depth, d, n6 = w_ada.shape
    nb = 1536
    return pl.pallas_call(
        _ada_kernel,
        grid=(depth, n6 // nb),
        in_specs=[pl.BlockSpec((SUBLANE, d), lambda l, j: (0, 0)),
                  pl.BlockSpec((None, d, nb), lambda l, j: (l, 0, j)),
                  pl.BlockSpec((None, 1, nb), lambda l, j: (l, 0, j))],
        out_specs=pl.BlockSpec((None, SUBLANE, nb), lambda l, j: (l, 0, j)),
        out_shape=jax.ShapeDtypeStruct((depth, SUBLANE, n6), F32),
        compiler_params=_cparams(("arbitrary", "arbitrary")),
        name="ada",
    )(cond8, w_ada, b_ada.reshape(depth, 1, n6))


def _swap_halves(x, half):
    n = x.shape[-1]
    lane = lax.broadcasted_iota(jnp.int32, x.shape, x.ndim - 1)
    first = (lane % (2 * half)) < half
    return jnp.where(first, pltpu.roll(x, n - half, x.ndim - 1), pltpu.roll(x, half, x.ndim - 1))


def _rope(x, cos, sin_signed, half):
    reps = x.shape[-1] // cos.shape[-1]
    if reps > 1:
        cos = jnp.concatenate([cos] * reps, axis=-1)
        sin_signed = jnp.concatenate([sin_signed] * reps, axis=-1)
    return x * cos + _swap_halves(x, half) * sin_signed


def _in_kernel(xc_in_ref, xl_in_ref, ada_ref, ln1_ref, w_ref, gq_ref, wuq_ref, gkv_ref, wuk_ref, wuv_ref,
               ca_ref, sa_ref, cb_ref, sb_ref,
               qa_ref, ka_ref, va_ref, qb_ref, ckvn_ref, kb_ref, vb_ref, kr_ref, xc_ref, yc_ref, *, ctx_tiles):
    d = D_MODEL
    x = jnp.where(pl.program_id(0) < ctx_tiles, xc_in_ref[...], xl_in_ref[...])
    shift = ada_ref[:, 0:d]
    scale = ada_ref[:, d:2 * d]
    u = _rms(x, ln1_ref[...]) * (1.0 + scale) + shift
    p = _dot(u.astype(BF16), w_ref[...])
    ca, sa, cb, sb = ca_ref[...], sa_ref[...], cb_ref[...], sb_ref[...]

    qa = _rope(p[:, C_QA:C_KA], ca, sa, HD_A // 4) * (HD_A ** -0.5)
    qa_ref[...] = qa.astype(BF16)
    ka_ref[...] = _rope(p[:, C_KA:C_VA], ca, sa, HD_A // 4)
    va_ref[...] = p[:, C_VA:C_CQ]

    cq = _rms(p[:, C_CQ:C_CKV], gq_ref[...])
    qb = _dot(cq.astype(BF16), wuq_ref[...])
    qb = _rope(qb, cb, sb, ROPE_B // 4) * ((NOPE_B + ROPE_B) ** -0.5)
    qb_ref[...] = qb.astype(BF16)

    ckvn = _rms(p[:, C_CKV:C_XC], gkv_ref[...])
    ckvn_ref[...] = ckvn
    ckvn_b = ckvn.astype(BF16)
    kr = _rope(p[:, C_KR:C_END], cb, sb, ROPE_B // 4)
    kr_ref[...] = kr
    kb = _dot(ckvn_b, wuk_ref[...]) + jnp.concatenate([kr] * N_HEADS_B, axis=-1)
    kb_ref[...] = kb.astype(BF16)
    vb_ref[...] = _dot(ckvn_b, wuv_ref[...]).astype(BF16)

    xc_ref[...] = p[:, C_XC:C_YC]
    yc_ref[...] = p[:, C_YC:C_KR]


def _in_proj(x_ctx, x_lat, t_ctx, t, ada, ln1, w_main, g_q, w_uq, g_kv, w_uk, w_uv, tabs, seg, tab_blk, layer):
    tm = TOKEN_TILE
    row = lambda n: pl.BlockSpec((tm, n), lambda i: (i, 0))
    full = lambda a: _layer_spec(a, layer)
    tab = pl.BlockSpec((tm, LANE), lambda i: (tab_blk(i), 0))
    sds = lambda n, dt: jax.ShapeDtypeStruct((t, n), dt)
    nb = N_HEADS_B * HEAD_PAD_B
    return pl.pallas_call(
        functools.partial(_in_kernel, ctx_tiles=t_ctx // tm),
        grid=(t // tm,),
        in_specs=[*_stream_specs(x_ctx, x_lat, t_ctx, tm), _ada_spec(ada, layer, seg), full(ln1), full(w_main),
                  full(g_q), full(w_uq), full(g_kv), full(w_uk), full(w_uv), tab, tab, tab, tab],
        out_specs=[row(512), row(128), row(128), row(nb), row(KV_RANK), row(nb), row(512), row(128),
                   row(LRU_W), row(LRU_W)],
        out_shape=[sds(512, BF16), sds(128, F32), sds(128, F32), sds(nb, BF16), sds(KV_RANK, F32),
                   sds(nb, BF16), sds(512, BF16), sds(128, F32), sds(LRU_W, F32), sds(LRU_W, F32)],
        compiler_params=_cparams(("arbitrary",)),
        name="in_proj",
    )(x_ctx, x_lat, ada, ln1, w_main, g_q, w_uq, g_kv, w_uk, w_uv, *tabs)


def _gqa_heads(q, k_all, v_all, bias, sink_ref, layer, o_ref):
    lane = lax.broadcasted_iota(jnp.int32, (q.shape[0], LANE), 1)
    low = lane < HD_A
    for pair in range(N_HEADS_A // 2):
        qp = q[:, pair * LANE:(pair + 1) * LANE]
        outs = []
        for g in range(N_KV_A):
            qm = jnp.where(low if g == 0 else ~low, qp, jnp.zeros_like(qp))
            s = _dot_t(qm, k_all)
            if bias is not None:
                s = s + bias
            sink = sink_ref[layer, 2 * pair + g]
            m = jnp.maximum(jnp.max(s, axis=-1, keepdims=True), sink)
            e = jnp.exp(s - m)
            l = jnp.sum(e, axis=-1, keepdims=True) + jnp.exp(sink - m)
            outs.append(_dot(e.astype(BF16), v_all) / l)
        o_ref[:, pair * LANE:(pair + 1) * LANE] = jnp.where(low, outs[0], outs[1]).astype(o_ref.dtype)


def _gqa_ctx_kernel(sink_ref, q_ref, k_ref, v_ref, o_ref, *, layer):
    _gqa_heads(q_ref[...], k_ref[...].astype(BF16), v_ref[...].astype(BF16), None, sink_ref, layer, o_ref)


def _gqa_ctx(qa, ka, va, sink_p, layer, n_seq, s_len):
    return pl.pallas_call(
        functools.partial(_gqa_ctx_kernel, layer=layer),
        grid=(n_seq,),
        in_specs=[pl.BlockSpec(memory_space=pltpu.SMEM),
                  pl.BlockSpec((s_len, 512), lambda b: (b, 0)),
                  pl.BlockSpec((s_len, LANE), lambda b: (b, 0)),
                  pl.BlockSpec((s_len, LANE), lambda b: (b, 0))],
        out_specs=pl.BlockSpec((s_len, 512), lambda b: (b, 0)),
        out_shape=jax.ShapeDtypeStruct((n_seq * s_len, 512), BF16),
        compiler_params=_cparams(("arbitrary",)),
        name="gqa_ctx",
    )(sink_p, qa, ka, va)


def _gqa_lat_kernel(sink_ref, q_ref, kp_ref, kc_ref, kn_ref, vp_ref, vc_ref, vn_ref, kx_ref, vx_ref, o_ref, *,
                    layer):
    j = pl.program_id(1)
    last = pl.num_programs(1) - 1
    w = WINDOW
    qb = q_ref.shape[0]
    k_all = jnp.concatenate([kp_ref[...], kc_ref[...], kn_ref[...], kx_ref[...]], axis=0).astype(BF16)
    v_all = jnp.concatenate([vp_ref[...], vc_ref[...], vn_ref[...], vx_ref[...]], axis=0).astype(BF16)
    n_loc = qb + 2 * w
    qi = lax.broadcasted_iota(jnp.int32, (qb, n_loc), 0)
    col = lax.broadcasted_iota(jnp.int32, (qb, n_loc), 1)
    rel = col - w - qi
    zero = jnp.zeros((qb, n_loc), F32)
    neg = jnp.full((qb, n_loc), NEG_INF, F32)
    band = jnp.where(jnp.abs(rel) <= w, zero, neg)
    before = jnp.where(col < w, jnp.where(j > 0, zero, neg), zero)
    after = jnp.where(col >= qb + w, jnp.where(j < last, zero, neg), zero)
    bias = jnp.concatenate([band + before + after, jnp.zeros((qb, kx_ref.shape[0]), F32)], axis=1)
    _gqa_heads(q_ref[...], k_all, v_all, bias, sink_ref, layer, o_ref)


def _gqa_lat(qa, ka, va, kx, vx, sink_p, layer, n_seq, n_len, row0):
    w = WINDOW
    qb = GQA_Q_BLOCK
    r = qb // w
    nqb = n_len // qb
    nwb = n_len // w
    cur = lambda b, j: (row0 // qb + b * nqb + j, 0)
    prev = lambda b, j: (row0 // w + b * nwb + jnp.maximum(r * j - 1, 0), 0)
    nxt = lambda b, j: (row0 // w + b * nwb + jnp.minimum(r * j + r, nwb - 1), 0)
    past = kx.shape[2]
    cache = pl.BlockSpec((None, None, past, LANE), lambda b, j: (b, layer, 0, 0))
    return pl.pallas_call(
        functools.partial(_gqa_lat_kernel, layer=layer),
        grid=(n_seq, nqb),
        in_specs=[pl.BlockSpec(memory_space=pltpu.SMEM),
                  pl.BlockSpec((qb, 512), cur),
                  pl.BlockSpec((w, LANE), prev), pl.BlockSpec((qb, LANE), cur), pl.BlockSpec((w, LANE), nxt),
                  pl.BlockSpec((w, LANE), prev), pl.BlockSpec((qb, LANE), cur), pl.BlockSpec((w, LANE), nxt),
                  cache, cache],
        out_specs=pl.BlockSpec((qb, 512), lambda b, j: (b * nqb + j, 0)),
        out_shape=jax.ShapeDtypeStruct((n_seq * n_len, 512), BF16),
        compiler_params=_cparams(("arbitrary", "arbitrary")),
        name="gqa_lat",
    )(sink_p, qa, ka, ka, ka, va, va, va, kx, vx)


def _mla_kernel(*refs, n_src, chunk):
    q_ref = refs[0]
    kv_refs = refs[1:1 + 2 * n_src]
    o_ref = refs[-1]
    qb = q_ref.shape[0]
    lane = lax.broadcasted_iota(jnp.int32, (qb, LANE), 1)
    for pair in range(q_ref.shape[1] // (2 * HEAD_PAD_B)):
        outs = []
        for hh in range(2):
            hs = slice((2 * pair + hh) * HEAD_PAD_B, (2 * pair + hh + 1) * HEAD_PAD_B)
            q = q_ref[:, hs]
            m = jnp.full((qb, 1), NEG_INF, F32)
            l = jnp.zeros((qb, 1), F32)
            acc = jnp.zeros((qb, LANE), F32)
            for s_i in range(n_src):
                k_ref, v_ref = kv_refs[2 * s_i], kv_refs[2 * s_i + 1]
                nk = k_ref.shape[0]
                for c in range(nk // chunk):
                    k = k_ref[c * chunk:(c + 1) * chunk, hs]
                    v = v_ref[c * chunk:(c + 1) * chunk, pair * LANE:(pair + 1) * LANE]
                    s = _dot_t(q, k)
                    m_new = jnp.maximum(m, jnp.max(s, axis=-1, keepdims=True))
                    alpha = jnp.exp(m - m_new)
                    e = jnp.exp(s - m_new)
                    l = alpha * l + jnp.sum(e, axis=-1, keepdims=True)
                    acc = alpha * acc + _dot(e.astype(BF16), v)
                    m = m_new
            outs.append(acc / l)
        o_ref[:, pair * LANE:(pair + 1) * LANE] = jnp.where(lane < V_B, outs[0], outs[1]).astype(o_ref.dtype)


def _mla_ctx(qb, kb, vb, n_seq, s_len):
    nq = N_HEADS_B * HEAD_PAD_B
    nv = N_HEADS_B * V_B
    return pl.pallas_call(
        functools.partial(_mla_kernel, n_src=1, chunk=s_len),
        grid=(n_seq,),
        in_specs=[pl.BlockSpec((s_len, nq), lambda b: (b, 0)),
                  pl.BlockSpec((s_len, nq), lambda b: (b, 0)),
                  pl.BlockSpec((s_len, nv), lambda b: (b, 0))],
        out_specs=pl.BlockSpec((s_len, nv), lambda b: (b, 0)),
        out_shape=jax.ShapeDtypeStruct((n_seq * s_len, nv), BF16),
        compiler_params=_cparams(("arbitrary",)),
        name="mla_ctx",
    )(qb, kb, vb)


def _mla_lat(qb, kb, vb, kbx, vbx, n_seq, n_len, row0):
    npair = N_HEADS_B // 2
    qblk = min(MLA_Q_BLOCK, n_len)
    nqb = n_len // qblk
    past = kbx.shape[0] // n_seq
    qmap = lambda b, p, j: (row0 // qblk + b * nqb + j, p)
    return pl.pallas_call(
        functools.partial(_mla_kernel, n_src=2, chunk=512),
        grid=(n_seq, npair, nqb),
        in_specs=[pl.BlockSpec((qblk, 2 * HEAD_PAD_B), qmap),
                  pl.BlockSpec((n_len, 2 * HEAD_PAD_B), lambda b, p, j: (row0 // n_len + b, p)),
                  pl.BlockSpec((n_len, LANE), lambda b, p, j: (row0 // n_len + b, p)),
                  pl.BlockSpec((past, 2 * HEAD_PAD_B), lambda b, p, j: (b, p)),
                  pl.BlockSpec((past, LANE), lambda b, p, j: (b, p))],
        out_specs=pl.BlockSpec((qblk, LANE), lambda b, p, j: (b * nqb + j, p)),
        out_shape=jax.ShapeDtypeStruct((n_seq * n_len, 512), BF16),
        compiler_params=_cparams(("arbitrary", "arbitrary", "arbitrary")),
        name="mla_lat",
    )(qb, kb, vb, kbx, vbx)


def _cache_kv_kernel(ckv_ref, kr_ref, wuk_ref, wuv_ref, kb_ref, vb_ref):
    c = ckv_ref[...].astype(BF16)
    kb = _dot(c, wuk_ref[...]) + jnp.concatenate([kr_ref[...]] * N_HEADS_B, axis=-1)
    kb_ref[...] = kb.astype(BF16)
    vb_ref[...] = _dot(c, wuv_ref[...]).astype(BF16)


def _cache_kv(cache_ckv, cache_kr_pad, w_uk, w_uv, layer):
    n_seq, _, past, _ = cache_ckv.shape
    nb = N_HEADS_B * HEAD_PAD_B
    return pl.pallas_call(
        _cache_kv_kernel,
        grid=(n_seq,),
        in_specs=[pl.BlockSpec((None, None, past, KV_RANK), lambda b: (b, layer, 0, 0)),
                  pl.BlockSpec((None, None, past, LANE), lambda b: (b, layer, 0, 0)),
                  _layer_spec(w_uk, layer), _layer_spec(w_uv, layer)],
        out_specs=[pl.BlockSpec((past, nb), lambda b: (b, 0)), pl.BlockSpec((past, 512), lambda b: (b, 0))],
        out_shape=[jax.ShapeDtypeStruct((n_seq * past, nb), BF16), jax.ShapeDtypeStruct((n_seq * past, 512), BF16)],
        compiler_params=_cparams(("arbitrary",)),
        name="cache_kv",
    )(cache_ckv, cache_kr_pad, w_uk, w_uv)


def _lru_kernel(xc_ref, yc_ref, h0_ref, cw_ref, cb_ref, wa_ref, ba_ref, wx_ref, bx_ref, sp_ref,
                o_ref, st_ref, pad_ref, xcv_ref, a_ref, b_ref, *, chunk):
    n = xc_ref.shape[0]
    halo = SUBLANE
    pad_ref[0:halo, :] = jnp.zeros((halo, LRU_W), F32)
    pad_ref[halo + n:2 * halo + n, :] = jnp.zeros((halo, LRU_W), F32)
    pad_ref[halo:halo + n, :] = xc_ref[...]
    left = CONV_W // 2
    for c in range(n // chunk):
        r0 = c * chunk
        acc = jnp.broadcast_to(cb_ref[...], (chunk, LRU_W))
        for j in range(CONV_W):
            off = halo + r0 + j - left
            acc = acc + cw_ref[j:j + 1, :] * pad_ref[off:off + chunk, :]
        xcv_ref[r0:r0 + chunk, :] = acc

    row = lax.broadcasted_iota(jnp.int32, (SUBLANE, LRU_W), 0)
    for d in range(2):
        for c in range(n // chunk):
            r0 = c * chunk
            xv = xcv_ref[r0:r0 + chunk, :]
            xb = xv.astype(BF16)
            for hf in range(2):
                cs = slice(hf * LRU_HALF, (hf + 1) * LRU_HALF)
                r = _sigmoid(_dot(xb[:, cs], wa_ref[d, hf]) + ba_ref[d:d + 1, cs])
                i = _sigmoid(_dot(xb[:, cs], wx_ref[d, hf]) + bx_ref[d:d + 1, cs])
                log_a = (-LRU_C) * r * sp_ref[d:d + 1, cs]
                a = jnp.exp(log_a)
                a_ref[d, r0:r0 + chunk, cs] = a
                b_ref[d, r0:r0 + chunk, cs] = jnp.sqrt(-jnp.tanh(log_a) * (a * a + 1.0)) * (i * xv[:, cs])

    def scan_group(d, grp, h):
        rows = pl.ds(pl.multiple_of(grp * SUBLANE, SUBLANE), SUBLANE)
        a = a_ref[d, rows, :]
        b = b_ref[d, rows, :]
        for sh in (1, 2, 4):
            if d == 0:
                keep = row >= sh
                a_s = jnp.where(keep, pltpu.roll(a, sh, 0), 1.0)
                b_s = jnp.where(keep, pltpu.roll(b, sh, 0), 0.0)
            else:
                keep = row < SUBLANE - sh
                a_s = jnp.where(keep, pltpu.roll(a, SUBLANE - sh, 0), 1.0)
                b_s = jnp.where(keep, pltpu.roll(b, SUBLANE - sh, 0), 0.0)
            b = a * b_s + b
            a = a * a_s
        hrows = a * h + b
        b_ref[d, rows, :] = hrows
        return hrows[SUBLANE - 1:SUBLANE, :] if d == 0 else hrows[0:1, :]

    n_grp = n // SUBLANE

    def body(g, hs):
        return scan_group(0, g, hs[0]), scan_group(1, n_grp - 1 - g, hs[1])

    h_f, h_b = lax.fori_loop(0, n_grp, body, (h0_ref[0:1, :], h0_ref[1:2, :]))
    st_ref[0:1, :] = h_f
    st_ref[1:2, :] = h_b

    for c in range(n // chunk):
        rs = slice(c * chunk, (c + 1) * chunk)
        o_ref[rs, :] = ((b_ref[0, rs, :] + b_ref[1, rs, :]) * jax.nn.gelu(yc_ref[rs, :])).astype(o_ref.dtype)


def _lru(xc, yc, h0, h0_layer, conv_w, conv_b, wa, ba, wx, bx, sp, layer, n_seq, n_len, row0):
    blk0 = row0 // n_len
    full = lambda a: _layer_spec(a, layer)
    seq = pl.BlockSpec((n_len, LRU_W), lambda b: (blk0 + b, 0))
    return pl.pallas_call(
        functools.partial(_lru_kernel, chunk=min(n_len, 256)),
        grid=(n_seq,),
        in_specs=[seq, seq, pl.BlockSpec((None, None, 2, LRU_W), lambda b: (b, h0_layer, 0, 0)),
                  full(conv_w), full(conv_b), full(wa), full(ba), full(wx), full(bx), full(sp)],
        out_specs=[pl.BlockSpec((n_len, LRU_W), lambda b: (b, 0)),
                   pl.BlockSpec((None, 2, LRU_W), lambda b: (b, 0, 0))],
        out_shape=[jax.ShapeDtypeStruct((n_seq * n_len, LRU_W), BF16),
                   jax.ShapeDtypeStruct((n_seq, 2, LRU_W), F32)],
        scratch_shapes=[pltpu.VMEM((n_len + 2 * SUBLANE, LRU_W), F32), pltpu.VMEM((n_len, LRU_W), F32),
                        pltpu.VMEM((2, n_len, LRU_W), F32), pltpu.VMEM((2, n_len, LRU_W), F32)],
        compiler_params=_cparams(("arbitrary",)),
        name="lru",
    )(xc, yc, h0, conv_w, conv_b, wa, ba, wx, bx, sp)


def _merge_kernel(xc_in_ref, xl_in_ref, ada_ref, oac_ref, oal_ref, obc_ref, obl_ref, occ_ref, ocl_ref, ln1_ref,
                  wg_ref, wpa_ref, wpb_ref, wpc_ref, wout_ref, ln2_ref, wr_ref, br_ref,
                  x1_ref, h_ref, e4_ref, g4_ref, r4_ref, cnt_ref, carry_ref, *, ctx_tiles):
    d = D_MODEL
    tm = xc_in_ref.shape[0]
    i = pl.program_id(0)
    is_ctx = i < ctx_tiles
    x = jnp.where(is_ctx, xc_in_ref[...], xl_in_ref[...])
    oa = jnp.where(is_ctx, oac_ref[...], oal_ref[...])
    ob = jnp.where(is_ctx, obc_ref[...], obl_ref[...])
    oc = jnp.where(is_ctx, occ_ref[...], ocl_ref[...])

    @pl.when(i == 0)
    def _():
        carry_ref[...] = jnp.zeros_like(carry_ref)

    ada = ada_ref[...]
    u = _rms(x, ln1_ref[...]) * (1.0 + ada[:, d:2 * d]) + ada[:, 0:d]
    g = jax.nn.sigmoid(_dot(u.astype(BF16), wg_ref[...]))
    m = (g[:, 0:d] * _dot(oa, wpa_ref[...]) + g[:, d:2 * d] * _dot(ob, wpb_ref[...])
         + g[:, 2 * d:3 * d] * _dot(oc, wpc_ref[...]))
    x1 = x + ada[:, 2 * d:3 * d] * _dot(m.astype(BF16), wout_ref[...])
    x1_ref[...] = x1
    h = _rms(x1, ln2_ref[...]) * (1.0 + ada[:, 4 * d:5 * d]) + ada[:, 3 * d:4 * d]
    h_ref[...] = _pack_bf16_pairs(h)

    h_hi = h.astype(BF16)
    h_lo = (h - h_hi.astype(F32)).astype(BF16)
    hw = _dot(h_hi, wr_ref[...])
    logits = (hw[:, :N_EXPERTS] + hw[:, N_EXPERTS:] + _dot(h_lo, wr_ref[:, :N_EXPERTS])) + br_ref[...]
    col = lax.broadcasted_iota(jnp.int32, (tm, N_EXPERTS), 1).astype(F32)
    col4 = lax.broadcasted_iota(jnp.int32, (tm, TOP_K), 1)
    sel_any = jnp.zeros((tm, N_EXPERTS), F32)
    vals, idxs = [], []
    work = logits
    for _k in range(TOP_K):
        mx = jnp.max(work, axis=-1, keepdims=True)
        idx = jnp.min(jnp.where(work == mx, col, float(N_EXPERTS)), axis=-1, keepdims=True)
        sel = col == idx
        vals.append(mx)
        idxs.append(idx)
        sel_any = jnp.where(sel, 1.0, sel_any)
        work = jnp.where(sel, -jnp.inf, work)

    ri = lax.broadcasted_iota(jnp.int32, (tm, tm), 0)
    ci = lax.broadcasted_iota(jnp.int32, (tm, tm), 1)
    tri = jnp.where(ri > ci, 1.0, 0.0).astype(BF16)
    before = _dot(tri, sel_any.astype(BF16)) + carry_ref[...]
    carry = carry_ref[...] + jnp.sum(sel_any, axis=0, keepdims=True)
    carry_ref[...] = carry
    cnt_ref[...] = jnp.broadcast_to(carry, cnt_ref.shape)

    exps = [jnp.exp(v - vals[0]) for v in vals]
    den = exps[0] + exps[1] + exps[2] + exps[3]
    e4 = jnp.zeros((tm, TOP_K), jnp.int32)
    g4 = jnp.zeros((tm, TOP_K), F32)
    r4 = jnp.zeros((tm, TOP_K), jnp.int32)
    for k in range(TOP_K):
        rank = jnp.sum(jnp.where(col == idxs[k], before, 0.0), axis=-1, keepdims=True)
        e4 = jnp.where(col4 == k, idxs[k].astype(jnp.int32), e4)
        g4 = jnp.where(col4 == k, exps[k] / den, g4)
        r4 = jnp.where(col4 == k, rank.astype(jnp.int32), r4)
    e4_ref[...] = e4
    g4_ref[...] = g4
    r4_ref[...] = r4


def _merge(x_ctx, x_lat, t, ada, o_ctx, o_lat, ln1, w_g, w_pa, w_pb, w_pc, w_out, ln2, w_r, b_r, seg, tm, layer):
    d = x_ctx.shape[1]
    t_ctx = o_ctx[0].shape[0]
    ctx_tiles = t_ctx // tm
    row = lambda n: pl.BlockSpec((tm, n), lambda i: (i, 0))
    full = lambda a: _layer_spec(a, layer)
    ctx = pl.BlockSpec((tm, 512), lambda i: (jnp.minimum(i, ctx_tiles - 1), 0))
    lat = pl.BlockSpec((tm, 512), lambda i: (jnp.maximum(i - ctx_tiles, 0), 0))
    return pl.pallas_call(
        functools.partial(_merge_kernel, ctx_tiles=ctx_tiles),
        grid=(t // tm,),
        in_specs=[*_stream_specs(x_ctx, x_lat, t_ctx, tm), _ada_spec(ada, layer, seg), ctx, lat, ctx, lat, ctx, lat,
                  full(ln1), full(w_g), full(w_pa), full(w_pb), full(w_pc), full(w_out), full(ln2), full(w_r),
                  full(b_r)],
        out_specs=[row(d), row(d // 2), row(TOP_K), row(TOP_K), row(TOP_K),
                   pl.BlockSpec((SUBLANE, N_EXPERTS), lambda i: (0, 0))],
        out_shape=[jax.ShapeDtypeStruct((t, d), F32), jax.ShapeDtypeStruct((t, d // 2), jnp.int32),
                   jax.ShapeDtypeStruct((t, TOP_K), jnp.int32), jax.ShapeDtypeStruct((t, TOP_K), F32),
                   jax.ShapeDtypeStruct((t, TOP_K), jnp.int32),
                   jax.ShapeDtypeStruct((SUBLANE, N_EXPERTS), F32)],
        scratch_shapes=[pltpu.VMEM((1, N_EXPERTS), F32)],
        compiler_params=_cparams(("arbitrary",)),
        name="merge",
    )(x_ctx, x_lat, ada, o_ctx[0], o_lat[0], o_ctx[1], o_lat[1], o_ctx[2], o_lat[2], ln1, w_g, w_pa, w_pb, w_pc,
      w_out, ln2, w_r, b_r)


def _expert_kernel(be_ref, nu_ref, x_ref, wgu_ref, bgu_ref, wdn_ref, bdn_ref, y_ref, wgu_s, wdn_s, *, n_chunk):
    b = pl.program_id(0)
    used = b < nu_ref[0]
    new_expert = jnp.logical_or(b == 0, be_ref[b] != be_ref[jnp.maximum(b - 1, 0)])

    @pl.when(jnp.logical_and(used, new_expert))
    def _():
        wgu_s[...] = wgu_ref[...].astype(BF16)
        wdn_s[...] = wdn_ref[...].astype(BF16)

    @pl.when(used)
    def _():
        x = _unpack_bf16_pairs(x_ref[...]).astype(BF16)
        cw = D_FF // n_chunk
        acc = jnp.broadcast_to(bdn_ref[...], x.shape)
        for c in range(n_chunk):
            glu = _dot(x, wgu_s[:, c * cw:(c + 1) * cw]) + bgu_ref[:, c * cw:(c + 1) * cw]
            lin = (_dot(x, wgu_s[:, D_FF + c * cw:D_FF + (c + 1) * cw])
                   + bgu_ref[:, D_FF + c * cw:D_FF + (c + 1) * cw])
            glu = jnp.minimum(glu, SWIGLU_LIMIT)
            lin = jnp.clip(lin, -SWIGLU_LIMIT, SWIGLU_LIMIT)
            act = glu * jax.nn.sigmoid(SWIGLU_ALPHA * glu) * (lin + 1.0)
            acc = acc + _dot(act.astype(BF16), wdn_s[c * cw:(c + 1) * cw, :])
        y_ref[...] = _pack_bf16_pairs(acc)

    @pl.when(jnp.logical_not(used))
    def _():
        y_ref[...] = jnp.zeros_like(y_ref)


def _experts(xb, block_e, n_used, w_gu, b_gu, w_dn, b_dn, layer):
    n_rows, dp = xb.shape
    d = 2 * dp
    depth = w_gu.shape[0]
    blk = EXPERT_BLOCK
    n_blocks = n_rows // blk
    wmap = lambda b, be, nu: (layer, be[b], 0, 0)
    grid_spec = pltpu.PrefetchScalarGridSpec(
        num_scalar_prefetch=2,
        grid=(n_blocks,),
        in_specs=[pl.BlockSpec((blk, dp), lambda b, be, nu: (b, 0)),
                  pl.BlockSpec((None, None, d, 2 * D_FF), wmap),
                  pl.BlockSpec((None, None, 1, 2 * D_FF), wmap),
                  pl.BlockSpec((None, None, D_FF, d), wmap),
                  pl.BlockSpec((None, None, 1, d), wmap)],
        out_specs=pl.BlockSpec((blk, dp), lambda b, be, nu: (b, 0)),
        scratch_shapes=[pltpu.VMEM((d, 2 * D_FF), BF16), pltpu.VMEM((D_FF, d), BF16)],
    )
    return pl.pallas_call(
        functools.partial(_expert_kernel, n_chunk=4),
        grid_spec=grid_spec,
        out_shape=jax.ShapeDtypeStruct((n_rows, dp), jnp.int32),
        compiler_params=_cparams(("arbitrary",)),
        name="experts",
    )(block_e, n_used, xb, w_gu, b_gu.reshape(depth, N_EXPERTS, 1, 2 * D_FF), w_dn,
      b_dn.reshape(depth, N_EXPERTS, 1, d))


def _sc_worker_base(per_w):
    return (lax.axis_index("s") * SC_CORES + lax.axis_index("c")) * per_w


def _sc_gather_rows(table_hbm, out_hbm, idx_v, rows_v, gsem, osem, base, n_chunks):
    ch = SC_CHUNK

    def gather(c, slot):
        return pltpu.make_async_copy(table_hbm.at[idx_v.at[pl.ds(c * ch, ch)]], rows_v.at[slot], gsem.at[slot])

    def put(c, slot):
        return pltpu.make_async_copy(rows_v.at[slot], out_hbm.at[pl.ds(base + c * ch, ch)], osem.at[slot])

    gather(0, 0).start()

    @pl.loop(0, n_chunks // 2)
    def _(i):
        c = 2 * i
        gather(c, 0).wait()
        put(c, 0).start()

        @pl.when(i > 0)
        def _():
            put(c - 1, 1).wait()

        gather(c + 1, 1).start()
        gather(c + 1, 1).wait()
        put(c + 1, 1).start()
        put(c, 0).wait()

        @pl.when(c + 2 < n_chunks)
        def _():
            gather(c + 2, 0).start()

    put(n_chunks - 1, 1).wait()


def _sc_scratch(per_w, d, dtype):
    return [pltpu.VMEM((per_w,), jnp.int32), pltpu.VMEM((2, SC_CHUNK, d), dtype),
            pltpu.SemaphoreType.DMA((2,)), pltpu.SemaphoreType.DMA((2,))]


def _sc_gather(table, idx):
    n_idx = idx.shape[0]
    d = table.shape[1]
    n_workers = SC_CORES * SC_SUBCORES
    per_w = n_idx // n_workers
    assert n_idx % (n_workers * SC_CHUNK * 2) == 0
    mesh = plsc.VectorSubcoreMesh(core_axis_name="c", subcore_axis_name="s")

    @functools.partial(
        pl.kernel, mesh=mesh, out_type=jax.ShapeDtypeStruct((n_idx, d), table.dtype),
        scratch_types=_sc_scratch(per_w, d, table.dtype), name="sc_gather")
    def gather(table_hbm, idx_hbm, out_hbm, idx_v, rows_v, gsem, osem):
        base = _sc_worker_base(per_w)
        pltpu.sync_copy(idx_hbm.at[pl.ds(base, per_w)], idx_v)
        _sc_gather_rows(table_hbm, out_hbm, idx_v, rows_v, gsem, osem, base, per_w // SC_CHUNK)

    return gather(table, idx)


def _sc_dispatch(h, dest, n_rows):
    t, d = h.shape
    n_slots = dest.shape[0]
    n_workers = SC_CORES * SC_SUBCORES
    per_w = n_rows // n_workers
    piece = 16384
    unroll = 8
    fill_shift = 3
    assert n_rows % (n_workers * SC_CHUNK * 2) == 0 and n_slots % piece == 0 and (n_rows >> fill_shift) <= t
    mesh = plsc.VectorSubcoreMesh(core_axis_name="c", subcore_axis_name="s")

    @functools.partial(
        pl.kernel, mesh=mesh, out_type=jax.ShapeDtypeStruct((n_rows, d), h.dtype),
        scratch_types=[pltpu.VMEM((piece,), jnp.int32)] + _sc_scratch(per_w, d, h.dtype),
        compiler_params=pltpu.CompilerParams(needs_layout_passes=False), name="sc_dispatch")
    def dispatch(h_hbm, dest_hbm, out_hbm, dest_v, idx_v, rows_v, gsem, osem):
        base = _sc_worker_base(per_w)
        lane = lax.iota(jnp.int32, SC_LANES)

        @pl.loop(0, per_w // SC_LANES)
        def _(j):
            idx_v[pl.ds(j * SC_LANES, SC_LANES)] = (base + j * SC_LANES + lane) >> fill_shift

        @pl.loop(0, n_slots // piece)
        def _(p):
            pltpu.sync_copy(dest_hbm.at[pl.ds(p * piece, piece)], dest_v)

            @pl.loop(0, piece // (SC_LANES * unroll))
            def _(j):
                for u in range(unroll):
                    s0 = (j * unroll + u) * SC_LANES
                    loc = dest_v[pl.ds(s0, SC_LANES)] - base
                    mine = (loc >= 0) & (loc < per_w)
                    tok = (p * piece + s0 + lane) // TOP_K
                    plsc.store_scatter(idx_v, [loc], tok, mask=mine)

        _sc_gather_rows(h_hbm, out_hbm, idx_v, rows_v, gsem, osem, base, per_w // SC_CHUNK)

    return dispatch(h, dest)


def _combine_kernel(x1_ref, ada_ref, yg_ref, g4_ref, fg_ref, o_ref, *, final):
    d = D_MODEL
    g4 = g4_ref[...]
    y = g4[:, 0:1] * _unpack_bf16_pairs(yg_ref[0])
    for k in range(1, TOP_K):
        y = y + g4[:, k:k + 1] * _unpack_bf16_pairs(yg_ref[k])
    x2 = x1_ref[...] + ada_ref[:, 5 * d:6 * d] * y
    if final:
        x2 = _rms(x2, fg_ref[...])
    o_ref[...] = x2


def _combine(x1, ada, yg, g4, final_g, seg, layer, final):
    t, d = x1.shape
    tm = TOKEN_TILE
    return pl.pallas_call(
        functools.partial(_combine_kernel, final=final),
        grid=(t // tm,),
        in_specs=[pl.BlockSpec((tm, d), lambda i: (i, 0)),
                  _ada_spec(ada, layer, seg),
                  pl.BlockSpec((TOP_K, tm, d // 2), lambda i: (0, i, 0)),
                  pl.BlockSpec((tm, TOP_K), lambda i: (i, 0)),
                  pl.BlockSpec((1, d), lambda i: (0, 0))],
        out_specs=pl.BlockSpec((tm, d), lambda i: (i, 0)),
        out_shape=jax.ShapeDtypeStruct((t, d), F32),
        compiler_params=_cparams(("arbitrary",)),
        name="combine",
    )(x1, ada, yg, g4, final_g)


def _rope_tables(n_lat):
    rows = n_lat // GRID_W
    row_ids = jnp.repeat(jnp.arange(rows, dtype=F32), GRID_W)
    col_ids = jnp.tile(jnp.arange(GRID_W, dtype=F32), rows)

    def table(d_rot, lane0):
        d_axis = d_rot // 2
        nf = d_axis // 2
        inv_freq = ROPE_BASE ** (-jnp.arange(0, d_axis, 2, dtype=F32) / d_axis)
        ang_r = row_ids[:, None] * inv_freq
        ang_c = col_ids[:, None] * inv_freq
        ang = jnp.concatenate([ang_r, ang_r, ang_c, ang_c], axis=-1)
        sign = jnp.tile(jnp.concatenate([-jnp.ones((nf,), F32), jnp.ones((nf,), F32)]), 2)
        cos = jnp.ones((n_lat, LANE), F32)
        sin = jnp.zeros((n_lat, LANE), F32)
        for l0 in lane0:
            cos = cos.at[:, l0:l0 + d_rot].set(jnp.cos(ang))
            sin = sin.at[:, l0:l0 + d_rot].set(jnp.sin(ang) * sign)
        ident = (jnp.ones((TOKEN_TILE, LANE), F32), jnp.zeros((TOKEN_TILE, LANE), F32))
        return jnp.concatenate([ident[0], cos], axis=0), jnp.concatenate([ident[1], sin], axis=0)

    ca, sa = table(HD_A, (0, HD_A))
    cb, sb = table(ROPE_B, (NOPE_B,))
    return ca, sa, cb, sb


def _prep_weights(w_in, sink, w_uq, w_ukv, lru_wa, lru_wx, lru_lam, w_pa):
    depth, d, _ = w_in.shape
    cuts = np.cumsum((512, 128, 128, Q_RANK, KV_RANK, ROPE_B, LRU_W, LRU_W, 3 * d))
    perm = np.array(HEAD_PERM_A)
    w_qa = w_in[:, :, :cuts[0]].reshape(depth, d, N_HEADS_A, HD_A)[:, :, perm].reshape(depth, d, 512)
    w_kr = jnp.pad(w_in[:, :, cuts[4]:cuts[5]], ((0, 0), (0, 0), (NOPE_B, LANE - NOPE_B - ROPE_B)))
    w_main = jnp.concatenate([w_qa, w_in[:, :, cuts[0]:cuts[4]], w_in[:, :, cuts[5]:cuts[7]], w_kr],
                             axis=-1).astype(BF16)
    w_g = w_in[:, :, cuts[7]:].astype(BF16)
    sink_p = sink[:, perm]
    w_pa_p = w_pa.reshape(depth, N_HEADS_A, HD_A, d)[:, perm].reshape(depth, 512, d).astype(BF16)
    hb = NOPE_B + ROPE_B
    w_uq_p = jnp.pad(w_uq.reshape(depth, Q_RANK, N_HEADS_B, hb),
                     ((0, 0), (0, 0), (0, 0), (0, HEAD_PAD_B - hb))).reshape(depth, Q_RANK, -1).astype(BF16)
    w_ukv4 = w_ukv.reshape(depth, KV_RANK, N_HEADS_B, NOPE_B + V_B)
    w_uk_p = jnp.pad(w_ukv4[..., :NOPE_B], ((0, 0), (0, 0), (0, 0), (0, HEAD_PAD_B - NOPE_B))
                     ).reshape(depth, KV_RANK, -1).astype(BF16)
    w_uv = w_ukv4[..., NOPE_B:].reshape(depth, KV_RANK, -1).astype(BF16)

    def block_diag(w):
        hpb = LRU_HEADS // 2
        blk = LRU_W // LRU_HEADS
        w = w.reshape(depth, 2, 2, hpb, blk, blk)
        eye = jnp.eye(hpb, dtype=w.dtype)
        out = jnp.einsum('ldghij,hk->ldghikj', w, eye)
        return out.reshape(depth, 2, 2, hpb * blk, hpb * blk).astype(BF16)

    sp = jax.nn.softplus(-lru_lam.astype(F32))
    return w_main, w_g, sink_p, w_pa_p, w_uq_p, w_uk_p, w_uv, block_diag(lru_wa), block_diag(lru_wx), sp


def kernel(x_prompt, x_sample, cache_k_a, cache_v_a, cache_ckv, cache_krope, state_lru, c, c_ctx, ln1_g, ln2_g,
           final_g, w_ada, b_ada, w_in, sink, g_q, w_uq, g_kv, w_ukv, conv_w, conv_b, lru_wa, lru_ba, lru_wx,
           lru_bx, lru_lam, w_pa, w_pb, w_pc, w_out, w_router, b_router, w_gu, b_gu, w_dn, b_dn):
    n_ctx, s_len, d = x_prompt.shape
    n_lat, n_len, _ = x_sample.shape
    depth = w_in.shape[0]
    past = cache_k_a.shape[2]
    t_ctx = n_ctx * s_len
    t_lat = n_lat * n_len
    t = t_ctx + t_lat
    tm = TOKEN_TILE
    ctx_tiles = t_ctx // tm
    lat_tiles = n_len // tm
    merge_tile = min(MERGE_TILE, n_len)
    assert t_ctx % n_len == 0 and t_ctx % tm == 0 and n_len % tm == 0 and n_lat + 1 <= SUBLANE
    assert t_ctx % merge_tile == 0 and n_len % merge_tile == 0

    seg = lambda i: jnp.where(i < ctx_tiles, 0, 1 + (i - ctx_tiles) // lat_tiles)
    seg_m = lambda i: jnp.where(i < t_ctx // merge_tile, 0, 1 + (i - t_ctx // merge_tile) // (n_len // merge_tile))
    tab_blk = lambda i: jnp.where(i < ctx_tiles, 0, 1 + (i - ctx_tiles) % lat_tiles)

    cond8 = jnp.zeros((SUBLANE, d), F32).at[0].set(c_ctx).at[1:1 + n_lat].set(c)
    ada = _ada_all(cond8, w_ada, b_ada)
    tabs = _rope_tables(n_len)
    (w_main, w_g, sink_p, w_pa_p, w_uq_p, w_uk_p, w_uv, wa_bd, wx_bd, sp) = _prep_weights(
        w_in, sink, w_uq, w_ukv, lru_wa, lru_wx, lru_lam, w_pa)
    w_pb_b, w_pc_b, w_out_b = w_pb.astype(BF16), w_pc.astype(BF16), w_out.astype(BF16)
    w_r_hi = w_router.astype(BF16)
    w_r_cat = jnp.concatenate([w_r_hi, (w_router - w_r_hi.astype(F32)).astype(BF16)], axis=-1)
    cache_k2 = cache_k_a.reshape(n_lat, depth, past, N_KV_A * HD_A)
    cache_v2 = cache_v_a.reshape(n_lat, depth, past, N_KV_A * HD_A)
    cache_kr_pad = jnp.pad(cache_krope, ((0, 0), (0, 0), (0, 0), (NOPE_B, LANE - NOPE_B - ROPE_B)))
    h0_ctx = jnp.zeros((n_ctx, 1, 2, LRU_W), F32)
    rows3 = lambda a: a.reshape(depth, 1, -1)
    ada = ada.reshape(depth, SUBLANE, 1, 6 * d)
    ln1_3, ln2_3, g_q3, g_kv3, conv_b3, b_r3 = (rows3(a) for a in (ln1_g, ln2_g, g_q, g_kv, conv_b, b_router))

    x_ctx, x_lat = x_prompt.reshape(t_ctx, d), x_sample.reshape(t_lat, d)
    ks_a, vs_a, ckvs, krs, lrus = [], [], [], [], []
    n_rows = t * TOP_K + N_EXPERTS * EXPERT_BLOCK
    n_blocks = n_rows // EXPERT_BLOCK
    for l in range(depth):
        qa, ka, va, qb, ckvn, kb, vb, kr, xc, yc = _in_proj(
            x_ctx, x_lat, t_ctx, t, ada, ln1_3, w_main, g_q3, w_uq_p, g_kv3, w_uk_p, w_uv, tabs, seg, tab_blk, l)
        ks_a.append(ka[:t_ctx].reshape(n_ctx, s_len, N_KV_A, HD_A))
        vs_a.append(va[:t_ctx].reshape(n_ctx, s_len, N_KV_A, HD_A))
        ckvs.append(ckvn[:t_ctx].reshape(n_ctx, s_len, KV_RANK))
        krs.append(kr[:t_ctx, NOPE_B:NOPE_B + ROPE_B].reshape(n_ctx, s_len, ROPE_B))

        oa_c = _gqa_ctx(qa, ka, va, sink_p, l, n_ctx, s_len)
        oa_l = _gqa_lat(qa, ka, va, cache_k2, cache_v2, sink_p, l, n_lat, n_len, t_ctx)
        kbx, vbx = _cache_kv(cache_ckv, cache_kr_pad, w_uk_p, w_uv, l)
        ob_c = _mla_ctx(qb, kb, vb, n_ctx, s_len)
        ob_l = _mla_lat(qb, kb, vb, kbx, vbx, n_lat, n_len, t_ctx)
        lru_args = (conv_w, conv_b3, wa_bd, lru_ba, wx_bd, lru_bx, sp, l)
        oc_c, st_c = _lru(xc, yc, h0_ctx, 0, *lru_args, n_ctx, s_len, 0)
        oc_l, _ = _lru(xc, yc, state_lru, l, *lru_args, n_lat, n_len, t_ctx)
        lrus.append(st_c)
        x1, h, e4, g4, r4, cnt = _merge(x_ctx, x_lat, t, ada, (oa_c, ob_c, oc_c), (oa_l, ob_l, oc_l), ln1_3, w_g,
                                        w_pa_p, w_pb_b, w_pc_b, w_out_b, ln2_3, w_r_cat, b_r3, seg_m, merge_tile, l)

        counts = cnt[0].astype(jnp.int32)
        padded = (counts + EXPERT_BLOCK - 1) // EXPERT_BLOCK * EXPERT_BLOCK
        pad_end = jnp.cumsum(padded)
        pad_start = pad_end - padded
        dest4 = pad_start[e4] + r4
        blk_row0 = jnp.arange(n_blocks, dtype=jnp.int32) * EXPERT_BLOCK
        block_e = jnp.minimum(jnp.sum((pad_end[None, :] <= blk_row0[:, None]).astype(jnp.int32), axis=1),
                              N_EXPERTS - 1)
        n_used = (pad_end[-1:] // EXPERT_BLOCK).astype(jnp.int32)
        xb = _sc_dispatch(h, dest4.reshape(-1), n_rows)
        yb = _experts(xb, block_e, n_used, w_gu, b_gu, w_dn, b_dn, l)
        yg = _sc_gather(yb, dest4.T.reshape(-1)).reshape(TOP_K, t, d // 2)
        x = _combine(x1, ada, yg, g4, final_g.reshape(1, d), seg, l, l == depth - 1)
        x_ctx = x_lat = x

    y_prompt = x[:t_ctx].reshape(n_ctx, s_len, d)
    y_sample = x[t_ctx:].reshape(n_lat, n_len, d)
    return (y_prompt, y_sample, jnp.stack(ks_a, axis=1), jnp.stack(vs_a, axis=1), jnp.stack(ckvs, axis=1),
            jnp.stack(krs, axis=1), jnp.stack(lrus, axis=1))
```

```python
import functools

import numpy as np
import jax
import jax.numpy as jnp
from jax import lax
from jax.experimental import pallas as pl
from jax.experimental.pallas import tpu as pltpu
from jax.experimental.pallas import tpu_sc as plsc

F32 = jnp.float32
BF16 = jnp.bfloat16

D_MODEL = 1024
GRID_W = 64
EPS = 1e-6
ROPE_BASE = 10000.0
NEG_INF = -1e30
HD_A = 64
N_HEADS_A = 8
N_KV_A = 2
WINDOW = 128
N_HEADS_B = 8
NOPE_B = 64
ROPE_B = 32
V_B = 64
Q_RANK = 512
KV_RANK = 256
LRU_W = 512
LRU_HEADS = 8
CONV_W = 4
LRU_C = 8.0
N_EXPERTS = 32
TOP_K = 4
D_FF = D_MODEL
SWIGLU_LIMIT = 7.0
SWIGLU_ALPHA = 1.702

LANE = 128
SUBLANE = 8
TOKEN_TILE = 512
MLA_Q_BLOCK = 2048
GQA_Q_BLOCK = 256
MERGE_TILE = 512
EXPERT_BLOCK = 512
HEAD_PAD_B = 128
LRU_HALF = LRU_W // 2
VMEM_LIMIT = 56 * 1024 * 1024
SC_CORES = 2
SC_SUBCORES = 16
SC_LANES = 16
SC_CHUNK = 64

HEAD_PERM_A = (0, 4, 1, 5, 2, 6, 3, 7)

C_QA, C_KA, C_VA, C_CQ, C_CKV, C_XC, C_YC, C_KR, C_END = 0, 512, 640, 768, 1280, 1536, 2048, 2560, 2688


def _cparams(sem):
    return pltpu.CompilerParams(dimension_semantics=sem, vmem_limit_bytes=VMEM_LIMIT)


def _layer_spec(a, layer):
    zeros = (0,) * (a.ndim - 1)
    return pl.BlockSpec((None,) + a.shape[1:], lambda *_: (layer,) + zeros)


def _ada_spec(ada, layer, seg):
    return pl.BlockSpec((None, None, 1, ada.shape[-1]), lambda i: (layer, seg(i), 0, 0))


def _stream_specs(x_ctx, x_lat, t_ctx, tm):
    ctx_tiles = t_ctx // tm
    lat0 = 0 if x_lat is not x_ctx else ctx_tiles
    d = x_ctx.shape[1]
    return (pl.BlockSpec((tm, d), lambda i: (jnp.minimum(i, ctx_tiles - 1), 0)),
            pl.BlockSpec((tm, d), lambda i: (jnp.maximum(i - ctx_tiles, 0) + lat0, 0)))


def _rms(x, g):
    return x * lax.rsqrt(jnp.mean(x * x, axis=-1, keepdims=True) + EPS) * g


def _dot(a, b):
    return jnp.dot(a, b, preferred_element_type=F32)


def _dot_t(a, b):
    return lax.dot_general(a, b, (((1,), (1,)), ((), ())), preferred_element_type=F32)


def _sigmoid(x):
    return 0.5 * jnp.tanh(0.5 * x) + 0.5


def _pack_bf16_pairs(x):
    n = x.shape[1] // 2
    bits = lambda v: pltpu.bitcast(v.astype(BF16).astype(F32), jnp.uint32)
    return pltpu.bitcast((bits(x[:, :n]) >> 16) | (bits(x[:, n:]) & jnp.uint32(0xFFFF0000)), jnp.int32)


def _unpack_bf16_pairs(p):
    p = pltpu.bitcast(p, jnp.uint32)
    lo = pltpu.bitcast(p << 16, F32)
    hi = pltpu.bitcast(p & jnp.uint32(0xFFFF0000), F32)
    return jnp.concatenate([lo, hi], axis=1)


def _ada_kernel(c_ref, w_ref, b_ref, o_ref):
    c = c_ref[...]
    s = c * jax.nn.sigmoid(c)
    o_ref[...] = _dot(s.astype(BF16), w_ref[...].astype(BF16)) + b_ref[...]


def _ada_all(cond8, w_ada, b_ada):
    depth, d, n6 = w_ada.shape
    nb = 1536
    return pl.pallas_call(
        _ada_kernel,
        grid=(depth, n6 // nb),
        in_specs=[pl.BlockSpec((SUBLANE, d), lambda l, j: (0, 0)),
                  pl.BlockSpec((None, d, nb), lambda l, j: (l, 0, j)),
                  pl.BlockSpec((None, 1, nb), lambda l, j: (l, 0, j))],
        out_specs=pl.BlockSpec((None, SUBLANE, nb), lambda l, j: (l, 0, j)),
        out_shape=jax.ShapeDtypeStruct((depth, SUBLANE, n6), F32),
        compiler_params=_cparams(("arbitrary", "arbitrary")),
        name="ada",
    )(cond8, w_ada, b_ada.reshape(depth, 1, n6))


def _swap_halves(x, half):
    n = x.shape[-1]
    lane = lax.broadcasted_iota(jnp.int32, x.shape, x.ndim - 1)
    first = (lane % (2 * half)) < half
    return jnp.where(first, pltpu.roll(x, n - half, x.ndim - 1), pltpu.roll(x, half, x.ndim - 1))


def _rope(x, cos, sin_signed, half):
    reps = x.shape[-1] // cos.shape[-1]
    if reps > 1:
        cos = jnp.concatenate([cos] * reps, axis=-1)
        sin_signed = jnp.concatenate([sin_signed] * reps, axis=-1)
    return x * cos + _swap_halves(x, half) * sin_signed


def _in_kernel(xc_in_ref, xl_in_ref, ada_ref, ln1_ref, w_ref, gq_ref, wuq_ref, gkv_ref, wuk_ref, wuv_ref,
               ca_ref, sa_ref, cb_ref, sb_ref,
               qa_ref, ka_ref, va_ref, qb_ref, ckvn_ref, kb_ref, vb_ref, kr_ref, xc_ref, yc_ref, *, ctx_tiles):
    d = D_MODEL
    x = jnp.where(pl.program_id(0) < ctx_tiles, xc_in_ref[...], xl_in_ref[...])
    shift = ada_ref[:, 0:d]
    scale = ada_ref[:, d:2 * d]
    u = _rms(x, ln1_ref[...]) * (1.0 + scale) + shift
    p = _dot(u.astype(BF16), w_ref[...])
    ca, sa, cb, sb = ca_ref[...], sa_ref[...], cb_ref[...], sb_ref[...]

    qa = _rope(p[:, C_QA:C_KA], ca, sa, HD_A // 4) * (HD_A ** -0.5)
    qa_ref[...] = qa.astype(BF16)
    ka_ref[...] = _rope(p[:, C_KA:C_VA], ca, sa, HD_A // 4)
    va_ref[...] = p[:, C_VA:C_CQ]

    cq = _rms(p[:, C_CQ:C_CKV], gq_ref[...])
    qb = _dot(cq.astype(BF16), wuq_ref[...])
    qb = _rope(qb, cb, sb, ROPE_B // 4) * ((NOPE_B + ROPE_B) ** -0.5)
    qb_ref[...] = qb.astype(BF16)

    ckvn = _rms(p[:, C_CKV:C_XC], gkv_ref[...])
    ckvn_ref[...] = ckvn
    ckvn_b = ckvn.astype(BF16)
    kr = _rope(p[:, C_KR:C_END], cb, sb, ROPE_B // 4)
    kr_ref[...] = kr
    kb = _dot(ckvn_b, wuk_ref[...]) + jnp.concatenate([kr] * N_HEADS_B, axis=-1)
    kb_ref[...] = kb.astype(BF16)
    vb_ref[...] = _dot(ckvn_b, wuv_ref[...]).astype(BF16)

    xc_ref[...] = p[:, C_XC:C_YC]
    yc_ref[...] = p[:, C_YC:C_KR]


def _in_proj(x_ctx, x_lat, t_ctx, t, ada, ln1, w_main, g_q, w_uq, g_kv, w_uk, w_uv, tabs, seg, tab_blk, layer):
    tm = TOKEN_TILE
    row = lambda n: pl.BlockSpec((tm, n), lambda i: (i, 0))
    full = lambda a: _layer_spec(a, layer)
    tab = pl.BlockSpec((tm, LANE), lambda i: (tab_blk(i), 0))
    sds = lambda n, dt: jax.ShapeDtypeStruct((t, n), dt)
    nb = N_HEADS_B * HEAD_PAD_B
    return pl.pallas_call(
        functools.partial(_in_kernel, ctx_tiles=t_ctx // tm),
        grid=(t // tm,),
        in_specs=[*_stream_specs(x_ctx, x_lat, t_ctx, tm), _ada_spec(ada, layer, seg), full(ln1), full(w_main),
                  full(g_q), full(w_uq), full(g_kv), full(w_uk), full(w_uv), tab, tab, tab, tab],
        out_specs=[row(512), row(128), row(128), row(nb), row(KV_RANK), row(nb), row(512), row(128),
                   row(LRU_W), row(LRU_W)],
        out_shape=[sds(512, BF16), sds(128, F32), sds(128, F32), sds(nb, BF16), sds(KV_RANK, F32),
                   sds(nb, BF16), sds(512, BF16), sds(128, F32), sds(LRU_W, F32), sds(LRU_W, F32)],
        compiler_params=_cparams(("arbitrary",)),
        name="in_proj",
    )(x_ctx, x_lat, ada, ln1, w_main, g_q, w_uq, g_kv, w_uk, w_uv, *tabs)


def _gqa_heads(q, k_all, v_all, bias, sink_ref, layer, o_ref):
    lane = lax.broadcasted_iota(jnp.int32, (q.shape[0], LANE), 1)
    low = lane < HD_A
    for pair in range(N_HEADS_A // 2):
        qp = q[:, pair * LANE:(pair + 1) * LANE]
        outs = []
        for g in range(N_KV_A):
            qm = jnp.where(low if g == 0 else ~low, qp, jnp.zeros_like(qp))
            s = _dot_t(qm, k_all)
            if bias is not None:
                s = s + bias
            sink = sink_ref[layer, 2 * pair + g]
            m = jnp.maximum(jnp.max(s, axis=-1, keepdims=True), sink)
            e = jnp.exp(s - m)
            l = jnp.sum(e, axis=-1, keepdims=True) + jnp.exp(sink - m)
            outs.append(_dot(e.astype(BF16), v_all) / l)
        o_ref[:, pair * LANE:(pair + 1) * LANE] = jnp.where(low, outs[0], outs[1]).astype(o_ref.dtype)


def _gqa_ctx_kernel(sink_ref, q_ref, k_ref, v_ref, o_ref, *, layer):
    _gqa_heads(q_ref[...], k_ref[...].astype(BF16), v_ref[...].astype(BF16), None, sink_ref, layer, o_ref)


def _gqa_ctx(qa, ka, va, sink_p, layer, n_seq, s_len):
    return pl.pallas_call(
        functools.partial(_gqa_ctx_kernel, layer=layer),
        grid=(n_seq,),
        in_specs=[pl.BlockSpec(memory_space=pltpu.SMEM),
                  pl.BlockSpec((s_len, 512), lambda b: (b, 0)),
                  pl.BlockSpec((s_len, LANE), lambda b: (b, 0)),
                  pl.BlockSpec((s_len, LANE), lambda b: (b, 0))],
        out_specs=pl.BlockSpec((s_len, 512), lambda b: (b, 0)),
        out_shape=jax.ShapeDtypeStruct((n_seq * s_len, 512), BF16),
        compiler_params=_cparams(("arbitrary",)),
        name="gqa_ctx",
    )(sink_p, qa, ka, va)


def _gqa_lat_kernel(sink_ref, q_ref, kp_ref, kc_ref, kn_ref, vp_ref, vc_ref, vn_ref, kx_ref, vx_ref, o_ref, *,
                    layer):
    j = pl.program_id(1)
    last = pl.num_programs(1) - 1
    w = WINDOW
    qb = q_ref.shape[0]
    k_all = jnp.concatenate([kp_ref[...], kc_ref[...], kn_ref[...], kx_ref[...]], axis=0).astype(BF16)
    v_all = jnp.concatenate([vp_ref[...], vc_ref[...], vn_ref[...], vx_ref[...]], axis=0).astype(BF16)
    n_loc = qb + 2 * w
    qi = lax.broadcasted_iota(jnp.int32, (qb, n_loc), 0)
    col = lax.broadcasted_iota(jnp.int32, (qb, n_loc), 1)
    rel = col - w - qi
    zero = jnp.zeros((qb, n_loc), F32)
    neg = jnp.full((qb, n_loc), NEG_INF, F32)
    band = jnp.where(jnp.abs(rel) <= w, zero, neg)
    before = jnp.where(col < w, jnp.where(j > 0, zero, neg), zero)
    after = jnp.where(col >= qb + w, jnp.where(j < last, zero, neg), zero)
    bias = jnp.concatenate([band + before + after, jnp.zeros((qb, kx_ref.shape[0]), F32)], axis=1)
    _gqa_heads(q_ref[...], k_all, v_all, bias, sink_ref, layer, o_ref)


def _gqa_lat(qa, ka, va, kx, vx, sink_p, layer, n_seq, n_len, row0):
    w = WINDOW
    qb = GQA_Q_BLOCK
    r = qb // w
    nqb = n_len // qb
    nwb = n_len // w
    cur = lambda b, j: (row0 // qb + b * nqb + j, 0)
    prev = lambda b, j: (row0 // w + b * nwb + jnp.maximum(r * j - 1, 0), 0)
    nxt = lambda b, j: (row0 // w + b * nwb + jnp.minimum(r * j + r, nwb - 1), 0)
    past = kx.shape[2]
    cache = pl.BlockSpec((None, None, past, LANE), lambda b, j: (b, layer, 0, 0))
    return pl.pallas_call(
        functools.partial(_gqa_lat_kernel, layer=layer),
        grid=(n_seq, nqb),
        in_specs=[pl.BlockSpec(memory_space=pltpu.SMEM),
                  pl.BlockSpec((qb, 512), cur),
                  pl.BlockSpec((w, LANE), prev), pl.BlockSpec((qb, LANE), cur), pl.BlockSpec((w, LANE), nxt),
                  pl.BlockSpec((w, LANE), prev), pl.BlockSpec((qb, LANE), cur), pl.BlockSpec((w, LANE), nxt),
                  cache, cache],
        out_specs=pl.BlockSpec((qb, 512), lambda b, j: (b * nqb + j, 0)),
        out_shape=jax.ShapeDtypeStruct((n_seq * n_len, 512), BF16),
        compiler_params=_cparams(("arbitrary", "arbitrary")),
        name="gqa_lat",
    )(sink_p, qa, ka, ka, ka, va, va, va, kx, vx)


def _mla_kernel(*refs, n_src, chunk):
    q_ref = refs[0]
    kv_refs = refs[1:1 + 2 * n_src]
    o_ref = refs[-1]
    qb = q_ref.shape[0]
    lane = lax.broadcasted_iota(jnp.int32, (qb, LANE), 1)
    for pair in range(q_ref.shape[1] // (2 * HEAD_PAD_B)):
        outs = []
        for hh in range(2):
            hs = slice((2 * pair + hh) * HEAD_PAD_B, (2 * pair + hh + 1) * HEAD_PAD_B)
            q = q_ref[:, hs]
            m = jnp.full((qb, 1), NEG_INF, F32)
            l = jnp.zeros((qb, 1), F32)
            acc = jnp.zeros((qb, LANE), F32)
            for s_i in range(n_src):
                k_ref, v_ref = kv_refs[2 * s_i], kv_refs[2 * s_i + 1]
                nk = k_ref.shape[0]
                for c in range(nk // chunk):
                    k = k_ref[c * chunk:(c + 1) * chunk, hs]
                    v = v_ref[c * chunk:(c + 1) * chunk, pair * LANE:(pair + 1) * LANE]
                    s = _dot_t(q, k)
                    m_new = jnp.maximum(m, jnp.max(s, axis=-1, keepdims=True))
                    alpha = jnp.exp(m - m_new)
                    e = jnp.exp(s - m_new)
                    l = alpha * l + jnp.sum(e, axis=-1, keepdims=True)
                    acc = alpha * acc + _dot(e.astype(BF16), v)
                    m = m_new
            outs.append(acc / l)
        o_ref[:, pair * LANE:(pair + 1) * LANE] = jnp.where(lane < V_B, outs[0], outs[1]).astype(o_ref.dtype)


def _mla_ctx(qb, kb, vb, n_seq, s_len):
    nq = N_HEADS_B * HEAD_PAD_B
    nv = N_HEADS_B * V_B
    return pl.pallas_call(
        functools.partial(_mla_kernel, n_src=1, chunk=s_len),
        grid=(n_seq,),
        in_specs=[pl.BlockSpec((s_len, nq), lambda b: (b, 0)),
                  pl.BlockSpec((s_len, nq), lambda b: (b, 0)),
                  pl.BlockSpec((s_len, nv), lambda b: (b, 0))],
        out_specs=pl.BlockSpec((s_len, nv), lambda b: (b, 0)),
        out_shape=jax.ShapeDtypeStruct((n_seq * s_len, nv), BF16),
        compiler_params=_cparams(("arbitrary",)),
        name="mla_ctx",
    )(qb, kb, vb)


def _mla_lat(qb, kb, vb, kbx, vbx, n_seq, n_len, row0):
    npair = N_HEADS_B // 2
    qblk = min(MLA_Q_BLOCK, n_len)
    nqb = n_len // qblk
    past = kbx.shape[0] // n_seq
    qmap = lambda b, p, j: (row0 // qblk + b * nqb + j, p)
    return pl.pallas_call(
        functools.partial(_mla_kernel, n_src=2, chunk=512),
        grid=(n_seq, npair, nqb),
        in_specs=[pl.BlockSpec((qblk, 2 * HEAD_PAD_B), qmap),
                  pl.BlockSpec((n_len, 2 * HEAD_PAD_B), lambda b, p, j: (row0 // n_len + b, p)),
                  pl.BlockSpec((n_len, LANE), lambda b, p, j: (row0 // n_len + b, p)),
                  pl.BlockSpec((past, 2 * HEAD_PAD_B), lambda b, p, j: (b, p)),
                  pl.BlockSpec((past, LANE), lambda b, p, j: (b, p))],
        out_specs=pl.BlockSpec((qblk, LANE), lambda b, p, j: (b * nqb + j, p)),
        out_shape=jax.ShapeDtypeStruct((n_seq * n_len, 512), BF16),
        compiler_params=_cparams(("arbitrary", "arbitrary", "arbitrary")),
        name="mla_lat",
    )(qb, kb, vb, kbx, vbx)


def _cache_kv_kernel(ckv_ref, kr_ref, wuk_ref, wuv_ref, kb_ref, vb_ref):
    c = ckv_ref[...].astype(BF16)
    kb = _dot(c, wuk_ref[...]) + jnp.concatenate([kr_ref[...]] * N_HEADS_B, axis=-1)
    kb_ref[...] = kb.astype(BF16)
    vb_ref[...] = _dot(c, wuv_ref[...]).astype(BF16)


def _cache_kv(cache_ckv, cache_kr_pad, w_uk, w_uv, layer):
    n_seq, _, past, _ = cache_ckv.shape
    nb = N_HEADS_B * HEAD_PAD_B
    return pl.pallas_call(
        _cache_kv_kernel,
        grid=(n_seq,),
        in_specs=[pl.BlockSpec((None, None, past, KV_RANK), lambda b: (b, layer, 0, 0)),
                  pl.BlockSpec((None, None, past, LANE), lambda b: (b, layer, 0, 0)),
                  _layer_spec(w_uk, layer), _layer_spec(w_uv, layer)],
        out_specs=[pl.BlockSpec((past, nb), lambda b: (b, 0)), pl.BlockSpec((past, 512), lambda b: (b, 0))],
        out_shape=[jax.ShapeDtypeStruct((n_seq * past, nb), BF16), jax.ShapeDtypeStruct((n_seq * past, 512), BF16)],
        compiler_params=_cparams(("arbitrary",)),
        name="cache_kv",
    )(cache_ckv, cache_kr_pad, w_uk, w_uv)


def _lru_kernel(xc_ref, yc_ref, h0_ref, cw_ref, cb_ref, wa_ref, ba_ref, wx_ref, bx_ref, sp_ref,
                o_ref, st_ref, pad_ref, xcv_ref, a_ref, b_ref, *, chunk):
    n = xc_ref.shape[0]
    halo = SUBLANE
    pad_ref[0:halo, :] = jnp.zeros((halo, LRU_W), F32)
    pad_ref[halo + n:2 * halo + n, :] = jnp.zeros((halo, LRU_W), F32)
    pad_ref[halo:halo + n, :] = xc_ref[...]
    left = CONV_W // 2
    for c in range(n // chunk):
        r0 = c * chunk
        acc = jnp.broadcast_to(cb_ref[...], (chunk, LRU_W))
        for j in range(CONV_W):
            off = halo + r0 + j - left
            acc = acc + cw_ref[j:j + 1, :] * pad_ref[off:off + chunk, :]
        xcv_ref[r0:r0 + chunk, :] = acc

    row = lax.broadcasted_iota(jnp.int32, (SUBLANE, LRU_W), 0)
    for d in range(2):
        for c in range(n // chunk):
            r0 = c * chunk
            xv = xcv_ref[r0:r0 + chunk, :]
            xb = xv.astype(BF16)
            for hf in range(2):
                cs = slice(hf * LRU_HALF, (hf + 1) * LRU_HALF)
                r = _sigmoid(_dot(xb[:, cs], wa_ref[d, hf]) + ba_ref[d:d + 1, cs])
                i = _sigmoid(_dot(xb[:, cs], wx_ref[d, hf]) + bx_ref[d:d + 1, cs])
                log_a = (-LRU_C) * r * sp_ref[d:d + 1, cs]
                a = jnp.exp(log_a)
                a_ref[d, r0:r0 + chunk, cs] = a
                b_ref[d, r0:r0 + chunk, cs] = jnp.sqrt(-jnp.tanh(log_a) * (a * a + 1.0)) * (i * xv[:, cs])

    def scan_group(d, grp, h):
        rows = pl.ds(pl.multiple_of(grp * SUBLANE, SUBLANE), SUBLANE)
        a = a_ref[d, rows, :]
        b = b_ref[d, rows, :]
        for sh in (1, 2, 4):
            if d == 0:
                keep = row >= sh
                a_s = jnp.where(keep, pltpu.roll(a, sh, 0), 1.0)
                b_s = jnp.where(keep, pltpu.roll(b, sh, 0), 0.0)
            else:
                keep = row < SUBLANE - sh
                a_s = jnp.where(keep, pltpu.roll(a, SUBLANE - sh, 0), 1.0)
                b_s = jnp.where(keep, pltpu.roll(b, SUBLANE - sh, 0), 0.0)
            b = a * b_s + b
            a = a * a_s
        hrows = a * h + b
        b_ref[d, rows, :] = hrows
        return hrows[SUBLANE - 1:SUBLANE, :] if d == 0 else hrows[0:1, :]

    n_grp = n // SUBLANE

    def body(g, hs):
        return scan_group(0, g, hs[0]), scan_group(1, n_grp - 1 - g, hs[1])

    h_f, h_b = lax.fori_loop(0, n_grp, body, (h0_ref[0:1, :], h0_ref[1:2, :]))
    st_ref[0:1, :] = h_f
    st_ref[1:2, :] = h_b

    for c in range(n // chunk):
        rs = slice(c * chunk, (c + 1) * chunk)
        o_ref[rs, :] = ((b_ref[0, rs, :] + b_ref[1, rs, :]) * jax.nn.gelu(yc_ref[rs, :])).astype(o_ref.dtype)


def _lru(xc, yc, h0, h0_layer, conv_w, conv_b, wa, ba, wx, bx, sp, layer, n_seq, n_len, row0):
    blk0 = row0 // n_len
    full = lambda a: _layer_spec(a, layer)
    seq = pl.BlockSpec((n_len, LRU_W), lambda b: (blk0 + b, 0))
    return pl.pallas_call(
        functools.partial(_lru_kernel, chunk=min(n_len, 256)),
        grid=(n_seq,),
        in_specs=[seq, seq, pl.BlockSpec((None, None, 2, LRU_W), lambda b: (b, h0_layer, 0, 0)),
                  full(conv_w), full(conv_b), full(wa), full(ba), full(wx), full(bx), full(sp)],
        out_specs=[pl.BlockSpec((n_len, LRU_W), lambda b: (b, 0)),
                   pl.BlockSpec((None, 2, LRU_W), lambda b: (b, 0, 0))],
        out_shape=[jax.ShapeDtypeStruct((n_seq * n_len, LRU_W), BF16),
                   jax.ShapeDtypeStruct((n_seq, 2, LRU_W), F32)],
        scratch_shapes=[pltpu.VMEM((n_len + 2 * SUBLANE, LRU_W), F32), pltpu.VMEM((n_len, LRU_W), F32),
                        pltpu.VMEM((2, n_len, LRU_W), F32), pltpu.VMEM((2, n_len, LRU_W), F32)],
        compiler_params=_cparams(("arbitrary",)),
        name="lru",
    )(xc, yc, h0, conv_w, conv_b, wa, ba, wx, bx, sp)


def _merge_kernel(xc_in_ref, xl_in_ref, ada_ref, oac_ref, oal_ref, obc_ref, obl_ref, occ_ref, ocl_ref, ln1_ref,
                  wg_ref, wpa_ref, wpb_ref, wpc_ref, wout_ref, ln2_ref, wr_ref, br_ref,
                  x1_ref, h_ref, e4_ref, g4_ref, r4_ref, cnt_ref, carry_ref, *, ctx_tiles):
    d = D_MODEL
    tm = xc_in_ref.shape[0]
    i = pl.program_id(0)
    is_ctx = i < ctx_tiles
    x = jnp.where(is_ctx, xc_in_ref[...], xl_in_ref[...])
    oa = jnp.where(is_ctx, oac_ref[...], oal_ref[...])
    ob = jnp.where(is_ctx, obc_ref[...], obl_ref[...])
    oc = jnp.where(is_ctx, occ_ref[...], ocl_ref[...])

    @pl.when(i == 0)
    def _():
        carry_ref[...] = jnp.zeros_like(carry_ref)

    ada = ada_ref[...]
    u = _rms(x, ln1_ref[...]) * (1.0 + ada[:, d:2 * d]) + ada[:, 0:d]
    g = jax.nn.sigmoid(_dot(u.astype(BF16), wg_ref[...]))
    m = (g[:, 0:d] * _dot(oa, wpa_ref[...]) + g[:, d:2 * d] * _dot(ob, wpb_ref[...])
         + g[:, 2 * d:3 * d] * _dot(oc, wpc_ref[...]))
    x1 = x + ada[:, 2 * d:3 * d] * _dot(m.astype(BF16), wout_ref[...])
    x1_ref[...] = x1
    h = _rms(x1, ln2_ref[...]) * (1.0 + ada[:, 4 * d:5 * d]) + ada[:, 3 * d:4 * d]
    h_ref[...] = _pack_bf16_pairs(h)

    h_hi = h.astype(BF16)
    h_lo = (h - h_hi.astype(F32)).astype(BF16)
    hw = _dot(h_hi, wr_ref[...])
    logits = (hw[:, :N_EXPERTS] + hw[:, N_EXPERTS:] + _dot(h_lo, wr_ref[:, :N_EXPERTS])) + br_ref[...]
    col = lax.broadcasted_iota(jnp.int32, (tm, N_EXPERTS), 1).astype(F32)
    col4 = lax.broadcasted_iota(jnp.int32, (tm, TOP_K), 1)
    sel_any = jnp.zeros((tm, N_EXPERTS), F32)
    vals, idxs = [], []
    work = logits
    for _k in range(TOP_K):
        mx = jnp.max(work, axis=-1, keepdims=True)
        idx = jnp.min(jnp.where(work == mx, col, float(N_EXPERTS)), axis=-1, keepdims=True)
        sel = col == idx
        vals.append(mx)
        idxs.append(idx)
        sel_any = jnp.where(sel, 1.0, sel_any)
        work = jnp.where(sel, -jnp.inf, work)

    ri = lax.broadcasted_iota(jnp.int32, (tm, tm), 0)
    ci = lax.broadcasted_iota(jnp.int32, (tm, tm), 1)
    tri = jnp.where(ri > ci, 1.0, 0.0).astype(BF16)
    before = _dot(tri, sel_any.astype(BF16)) + carry_ref[...]
    carry = carry_ref[...] + jnp.sum(sel_any, axis=0, keepdims=True)
    carry_ref[...] = carry
    cnt_ref[...] = jnp.broadcast_to(carry, cnt_ref.shape)

    exps = [jnp.exp(v - vals[0]) for v in vals]
    den = exps[0] + exps[1] + exps[2] + exps[3]
    e4 = jnp.zeros((tm, TOP_K), jnp.int32)
    g4 = jnp.zeros((tm, TOP_K), F32)
    r4 = jnp.zeros((tm, TOP_K), jnp.int32)
    for k in range(TOP_K):
        rank = jnp.sum(jnp.where(col == idxs[k], before, 0.0), axis=-1, keepdims=True)
        e4 = jnp.where(col4 == k, idxs[k].astype(jnp.int32), e4)
        g4 = jnp.where(col4 == k, exps[k] / den, g4)
        r4 = jnp.where(col4 == k, rank.astype(jnp.int32), r4)
    e4_ref[...] = e4
    g4_ref[...] = g4
    r4_ref[...] = r4


def _merge(x_ctx, x_lat, t, ada, o_ctx, o_lat, ln1, w_g, w_pa, w_pb, w_pc, w_out, ln2, w_r, b_r, seg, tm, layer):
    d = x_ctx.shape[1]
    t_ctx = o_ctx[0].shape[0]
    ctx_tiles = t_ctx // tm
    row = lambda n: pl.BlockSpec((tm, n), lambda i: (i, 0))
    full = lambda a: _layer_spec(a, layer)
    ctx = pl.BlockSpec((tm, 512), lambda i: (jnp.minimum(i, ctx_tiles - 1), 0))
    lat = pl.BlockSpec((tm, 512), lambda i: (jnp.maximum(i - ctx_tiles, 0), 0))
    return pl.pallas_call(
        functools.partial(_merge_kernel, ctx_tiles=ctx_tiles),
        grid=(t // tm,),
        in_specs=[*_stream_specs(x_ctx, x_lat, t_ctx, tm), _ada_spec(ada, layer, seg), ctx, lat, ctx, lat, ctx, lat,
                  full(ln1), full(w_g), full(w_pa), full(w_pb), full(w_pc), full(w_out), full(ln2), full(w_r),
                  full(b_r)],
        out_specs=[row(d), row(d // 2), row(TOP_K), row(TOP_K), row(TOP_K),
                   pl.BlockSpec((SUBLANE, N_EXPERTS), lambda i: (0, 0))],
        out_shape=[jax.ShapeDtypeStruct((t, d), F32), jax.ShapeDtypeStruct((t, d // 2), jnp.int32),
                   jax.ShapeDtypeStruct((t, TOP_K), jnp.int32), jax.ShapeDtypeStruct((t, TOP_K), F32),
                   jax.ShapeDtypeStruct((t, TOP_K), jnp.int32),
                   jax.ShapeDtypeStruct((SUBLANE, N_EXPERTS), F32)],
        scratch_shapes=[pltpu.VMEM((1, N_EXPERTS), F32)],
        compiler_params=_cparams(("arbitrary",)),
        name="merge",
    )(x_ctx, x_lat, ada, o_ctx[0], o_lat[0], o_ctx[1], o_lat[1], o_ctx[2], o_lat[2], ln1, w_g, w_pa, w_pb, w_pc,
      w_out, ln2, w_r, b_r)


def _expert_kernel(be_ref, nu_ref, x_ref, wgu_ref, bgu_ref, wdn_ref, bdn_ref, y_ref, wgu_s, wdn_s, *, n_chunk):
    b = pl.program_id(0)
    used = b < nu_ref[0]
    new_expert = jnp.logical_or(b == 0, be_ref[b] != be_ref[jnp.maximum(b - 1, 0)])

    @pl.when(jnp.logical_and(used, new_expert))
    def _():
        wgu_s[...] = wgu_ref[...].astype(BF16)
        wdn_s[...] = wdn_ref[...].astype(BF16)

    @pl.when(used)
    def _():
        x = _unpack_bf16_pairs(x_ref[...]).astype(BF16)
        cw = D_FF // n_chunk
        acc = jnp.broadcast_to(bdn_ref[...], x.shape)
        for c in range(n_chunk):
            glu = _dot(x, wgu_s[:, c * cw:(c + 1) * cw]) + bgu_ref[:, c * cw:(c + 1) * cw]
            lin = (_dot(x, wgu_s[:, D_FF + c * cw:D_FF + (c + 1) * cw])
                   + bgu_ref[:, D_FF + c * cw:D_FF + (c + 1) * cw])
            glu = jnp.minimum(glu, SWIGLU_LIMIT)
            lin = jnp.clip(lin, -SWIGLU_LIMIT, SWIGLU_LIMIT)
            act = glu * jax.nn.sigmoid(SWIGLU_ALPHA * glu) * (lin + 1.0)
            acc = acc + _dot(act.astype(BF16), wdn_s[c * cw:(c + 1) * cw, :])
        y_ref[...] = _pack_bf16_pairs(acc)

    @pl.when(jnp.logical_not(used))
    def _():
        y_ref[...] = jnp.zeros_like(y_ref)


def _experts(xb, block_e, n_used, w_gu, b_gu, w_dn, b_dn, layer):
    n_rows, dp = xb.shape
    d = 2 * dp
    depth = w_gu.shape[0]
    blk = EXPERT_BLOCK
    n_blocks = n_rows // blk
    wmap = lambda b, be, nu: (layer, be[b], 0, 0)
    grid_spec = pltpu.PrefetchScalarGridSpec(
        num_scalar_prefetch=2,
        grid=(n_blocks,),
        in_specs=[pl.BlockSpec((blk, dp), lambda b, be, nu: (b, 0)),
                  pl.BlockSpec((None, None, d, 2 * D_FF), wmap),
                  pl.BlockSpec((None, None, 1, 2 * D_FF), wmap),
                  pl.BlockSpec((None, None, D_FF, d), wmap),
                  pl.BlockSpec((None, None, 1, d), wmap)],
        out_specs=pl.BlockSpec((blk, dp), lambda b, be, nu: (b, 0)),
        scratch_shapes=[pltpu.VMEM((d, 2 * D_FF), BF16), pltpu.VMEM((D_FF, d), BF16)],
    )
    return pl.pallas_call(
        functools.partial(_expert_kernel, n_chunk=4),
        grid_spec=grid_spec,
        out_shape=jax.ShapeDtypeStruct((n_rows, dp), jnp.int32),
        compiler_params=_cparams(("arbitrary",)),
        name="experts",
    )(block_e, n_used, xb, w_gu, b_gu.reshape(depth, N_EXPERTS, 1, 2 * D_FF), w_dn,
      b_dn.reshape(depth, N_EXPERTS, 1, d))


def _sc_worker_base(per_w):
    return (lax.axis_index("s") * SC_CORES + lax.axis_index("c")) * per_w


def _sc_gather_rows(table_hbm, out_hbm, idx_v, rows_v, gsem, osem, base, n_chunks):
    ch = SC_CHUNK

    def gather(c, slot):
        return pltpu.make_async_copy(table_hbm.at[idx_v.at[pl.ds(c * ch, ch)]], rows_v.at[slot], gsem.at[slot])

    def put(c, slot):
        return pltpu.make_async_copy(rows_v.at[slot], out_hbm.at[pl.ds(base + c * ch, ch)], osem.at[slot])

    gather(0, 0).start()

    @pl.loop(0, n_chunks // 2)
    def _(i):
        c = 2 * i
        gather(c, 0).wait()
        put(c, 0).start()

        @pl.when(i > 0)
        def _():
            put(c - 1, 1).wait()

        gather(c + 1, 1).start()
        gather(c + 1, 1).wait()
        put(c + 1, 1).start()
        put(c, 0).wait()

        @pl.when(c + 2 < n_chunks)
        def _():
            gather(c + 2, 0).start()

    put(n_chunks - 1, 1).wait()


def _sc_scratch(per_w, d, dtype):
    return [pltpu.VMEM((per_w,), jnp.int32), pltpu.VMEM((2, SC_CHUNK, d), dtype),
            pltpu.SemaphoreType.DMA((2,)), pltpu.SemaphoreType.DMA((2,))]


def _sc_gather(table, idx):
    n_idx = idx.shape[0]
    d = table.shape[1]
    n_workers = SC_CORES * SC_SUBCORES
    per_w = n_idx // n_workers
    assert n_idx % (n_workers * SC_CHUNK * 2) == 0
    mesh = plsc.VectorSubcoreMesh(core_axis_name="c", subcore_axis_name="s")

    @functools.partial(
        pl.kernel, mesh=mesh, out_type=jax.ShapeDtypeStruct((n_idx, d), table.dtype),
        scratch_types=_sc_scratch(per_w, d, table.dtype), name="sc_gather")
    def gather(table_hbm, idx_hbm, out_hbm, idx_v, rows_v, gsem, osem):
        base = _sc_worker_base(per_w)
        pltpu.sync_copy(idx_hbm.at[pl.ds(base, per_w)], idx_v)
        _sc_gather_rows(table_hbm, out_hbm, idx_v, rows_v, gsem, osem, base, per_w // SC_CHUNK)

    return gather(table, idx)


def _sc_dispatch(h, dest, n_rows):
    t, d = h.shape
    n_slots = dest.shape[0]
    n_workers = SC_CORES * SC_SUBCORES
    per_w = n_rows // n_workers
    piece = 16384
    unroll = 8
    fill_shift = 3
    assert n_rows % (n_workers * SC_CHUNK * 2) == 0 and n_slots % piece == 0 and (n_rows >> fill_shift) <= t
    mesh = plsc.VectorSubcoreMesh(core_axis_name="c", subcore_axis_name="s")

    @functools.partial(
        pl.kernel, mesh=mesh, out_type=jax.ShapeDtypeStruct((n_rows, d), h.dtype),
        scratch_types=[pltpu.VMEM((piece,), jnp.int32)] + _sc_scratch(per_w, d, h.dtype),
        compiler_params=pltpu.CompilerParams(needs_layout_passes=False), name="sc_dispatch")
    def dispatch(h_hbm, dest_hbm, out_hbm, dest_v, idx_v, rows_v, gsem, osem):
        base = _sc_worker_base(per_w)
        lane = lax.iota(jnp.int32, SC_LANES)

        @pl.loop(0, per_w // SC_LANES)
        def _(j):
            idx_v[pl.ds(j * SC_LANES, SC_LANES)] = (base + j * SC_LANES + lane) >> fill_shift

        @pl.loop(0, n_slots // piece)
        def _(p):
            pltpu.sync_copy(dest_hbm.at[pl.ds(p * piece, piece)], dest_v)

            @pl.loop(0, piece // (SC_LANES * unroll))
            def _(j):
                for u in range(unroll):
                    s0 = (j * unroll + u) * SC_LANES
                    loc = dest_v[pl.ds(s0, SC_LANES)] - base
                    mine = (loc >= 0) & (loc < per_w)
                    tok = (p * piece + s0 + lane) // TOP_K
                    plsc.store_scatter(idx_v, [loc], tok, mask=mine)

        _sc_gather_rows(h_hbm, out_hbm, idx_v, rows_v, gsem, osem, base, per_w // SC_CHUNK)

    return dispatch(h, dest)


def _combine_kernel(x1_ref, ada_ref, yg_ref, g4_ref, fg_ref, o_ref, *, final):
    d = D_MODEL
    g4 = g4_ref[...]
    y = g4[:, 0:1] * _unpack_bf16_pairs(yg_ref[0])
    for k in range(1, TOP_K):
        y = y + g4[:, k:k + 1] * _unpack_bf16_pairs(yg_ref[k])
    x2 = x1_ref[...] + ada_ref[:, 5 * d:6 * d] * y
    if final:
        x2 = _rms(x2, fg_ref[...])
    o_ref[...] = x2


def _combine(x1, ada, yg, g4, final_g, seg, layer, final):
    t, d = x1.shape
    tm = TOKEN_TILE
    return pl.pallas_call(
        functools.partial(_combine_kernel, final=final),
        grid=(t // tm,),
        in_specs=[pl.BlockSpec((tm, d), lambda i: (i, 0)),
                  _ada_spec(ada, layer, seg),
                  pl.BlockSpec((TOP_K, tm, d // 2), lambda i: (0, i, 0)),
                  pl.BlockSpec((tm, TOP_K), lambda i: (i, 0)),
                  pl.BlockSpec((1, d), lambda i: (0, 0))],
        out_specs=pl.BlockSpec((tm, d), lambda i: (i, 0)),
        out_shape=jax.ShapeDtypeStruct((t, d), F32),
        compiler_params=_cparams(("arbitrary",)),
        name="combine",
    )(x1, ada, yg, g4, final_g)


def _rope_tables(n_lat):
    rows = n_lat // GRID_W
    row_ids = jnp.repeat(jnp.arange(rows, dtype=F32), GRID_W)
    col_ids = jnp.tile(jnp.arange(GRID_W, dtype=F32), rows)

    def table(d_rot, lane0):
        d_axis = d_rot // 2
        nf = d_axis // 2
        inv_freq = ROPE_BASE ** (-jnp.arange(0, d_axis, 2, dtype=F32) / d_axis)
        ang_r = row_ids[:, None] * inv_freq
        ang_c = col_ids[:, None] * inv_freq
        ang = jnp.concatenate([ang_r, ang_r, ang_c, ang_c], axis=-1)
        sign = jnp.tile(jnp.concatenate([-jnp.ones((nf,), F32), jnp.ones((nf,), F32)]), 2)
        cos = jnp.ones((n_lat, LANE), F32)
        sin = jnp.zeros((n_lat, LANE), F32)
        for l0 in lane0:
            cos = cos.at[:, l0:l0 + d_rot].set(jnp.cos(ang))
            sin = sin.at[:, l0:l0 + d_rot].set(jnp.sin(ang) * sign)
        ident = (jnp.ones((TOKEN_TILE, LANE), F32), jnp.zeros((TOKEN_TILE, LANE), F32))
        return jnp.concatenate([ident[0], cos], axis=0), jnp.concatenate([ident[1], sin], axis=0)

    ca, sa = table(HD_A, (0, HD_A))
    cb, sb = table(ROPE_B, (NOPE_B,))
    return ca, sa, cb, sb


def _prep_weights(w_in, sink, w_uq, w_ukv, lru_wa, lru_wx, lru_lam, w_pa):
    depth, d, _ = w_in.shape
    cuts = np.cumsum((512, 128, 128, Q_RANK, KV_RANK, ROPE_B, LRU_W, LRU_W, 3 * d))
    perm = np.array(HEAD_PERM_A)
    w_qa = w_in[:, :, :cuts[0]].reshape(depth, d, N_HEADS_A, HD_A)[:, :, perm].reshape(depth, d, 512)
    w_kr = jnp.pad(w_in[:, :, cuts[4]:cuts[5]], ((0, 0), (0, 0), (NOPE_B, LANE - NOPE_B - ROPE_B)))
    w_main = jnp.concatenate([w_qa, w_in[:, :, cuts[0]:cuts[4]], w_in[:, :, cuts[5]:cuts[7]], w_kr],
                             axis=-1).astype(BF16)
    w_g = w_in[:, :, cuts[7]:].astype(BF16)
    sink_p = sink[:, perm]
    w_pa_p = w_pa.reshape(depth, N_HEADS_A, HD_A, d)[:, perm].reshape(depth, 512, d).astype(BF16)
    hb = NOPE_B + ROPE_B
    w_uq_p = jnp.pad(w_uq.reshape(depth, Q_RANK, N_HEADS_B, hb),
                     ((0, 0), (0, 0), (0, 0), (0, HEAD_PAD_B - hb))).reshape(depth, Q_RANK, -1).astype(BF16)
    w_ukv4 = w_ukv.reshape(depth, KV_RANK, N_HEADS_B, NOPE_B + V_B)
    w_uk_p = jnp.pad(w_ukv4[..., :NOPE_B], ((0, 0), (0, 0), (0, 0), (0, HEAD_PAD_B - NOPE_B))
                     ).reshape(depth, KV_RANK, -1).astype(BF16)
    w_uv = w_ukv4[..., NOPE_B:].reshape(depth, KV_RANK, -1).astype(BF16)

    def block_diag(w):
        hpb = LRU_HEADS // 2
        blk = LRU_W // LRU_HEADS
        w = w.reshape(depth, 2, 2, hpb, blk, blk)
        eye = jnp.eye(hpb, dtype=w.dtype)
        out = jnp.einsum('ldghij,hk->ldghikj', w, eye)
        return out.reshape(depth, 2, 2, hpb * blk, hpb * blk).astype(BF16)

    sp = jax.nn.softplus(-lru_lam.astype(F32))
    return w_main, w_g, sink_p, w_pa_p, w_uq_p, w_uk_p, w_uv, block_diag(lru_wa), block_diag(lru_wx), sp


def kernel(x_prompt, x_sample, cache_k_a, cache_v_a, cache_ckv, cache_krope, state_lru, c, c_ctx, ln1_g, ln2_g,
           final_g, w_ada, b_ada, w_in, sink, g_q, w_uq, g_kv, w_ukv, conv_w, conv_b, lru_wa, lru_ba, lru_wx,
           lru_bx, lru_lam, w_pa, w_pb, w_pc, w_out, w_router, b_router, w_gu, b_gu, w_dn, b_dn):
    n_ctx, s_len, d = x_prompt.shape
    n_lat, n_len, _ = x_sample.shape
    depth = w_in.shape[0]
    past = cache_k_a.shape[2]
    t_ctx = n_ctx * s_len
    t_lat = n_lat * n_len
    t = t_ctx + t_lat
    tm = TOKEN_TILE
    ctx_tiles = t_ctx // tm
    lat_tiles = n_len // tm
    merge_tile = min(MERGE_TILE, n_len)
    assert t_ctx % n_len == 0 and t_ctx % tm == 0 and n_len % tm == 0 and n_lat + 1 <= SUBLANE
    assert t_ctx % merge_tile == 0 and n_len % merge_tile == 0

    seg = lambda i: jnp.where(i < ctx_tiles, 0, 1 + (i - ctx_tiles) // lat_tiles)
    seg_m = lambda i: jnp.where(i < t_ctx // merge_tile, 0, 1 + (i - t_ctx // merge_tile) // (n_len // merge_tile))
    tab_blk = lambda i: jnp.where(i < ctx_tiles, 0, 1 + (i - ctx_tiles) % lat_tiles)

    cond8 = jnp.zeros((SUBLANE, d), F32).at[0].set(c_ctx).at[1:1 + n_lat].set(c)
    ada = _ada_all(cond8, w_ada, b_ada)
    tabs = _rope_tables(n_len)
    (w_main, w_g, sink_p, w_pa_p, w_uq_p, w_uk_p, w_uv, wa_bd, wx_bd, sp) = _prep_weights(
        w_in, sink, w_uq, w_ukv, lru_wa, lru_wx, lru_lam, w_pa)
    w_pb_b, w_pc_b, w_out_b = w_pb.astype(BF16), w_pc.astype(BF16), w_out.astype(BF16)
    w_r_hi = w_router.astype(BF16)
    w_r_cat = jnp.concatenate([w_r_hi, (w_router - w_r_hi.astype(F32)).astype(BF16)], axis=-1)
    cache_k2 = cache_k_a.reshape(n_lat, depth, past, N_KV_A * HD_A)
    cache_v2 = cache_v_a.reshape(n_lat, depth, past, N_KV_A * HD_A)
    cache_kr_pad = jnp.pad(cache_krope, ((0, 0), (0, 0), (0, 0), (NOPE_B, LANE - NOPE_B - ROPE_B)))
    h0_ctx = jnp.zeros((n_ctx, 1, 2, LRU_W), F32)
    rows3 = lambda a: a.reshape(depth, 1, -1)
    ada = ada.reshape(depth, SUBLANE, 1, 6 * d)
    ln1_3, ln2_3, g_q3, g_kv3, conv_b3, b_r3 = (rows3(a) for a in (ln1_g, ln2_g, g_q, g_kv, conv_b, b_router))

    x_ctx, x_lat = x_prompt.reshape(t_ctx, d), x_sample.reshape(t_lat, d)
    ks_a, vs_a, ckvs, krs, lrus = [], [], [], [], []
    n_rows = t * TOP_K + N_EXPERTS * EXPERT_BLOCK
    n_blocks = n_rows // EXPERT_BLOCK
    for l in range(depth):
        qa, ka, va, qb, ckvn, kb, vb, kr, xc, yc = _in_proj(
            x_ctx, x_lat, t_ctx, t, ada, ln1_3, w_main, g_q3, w_uq_p, g_kv3, w_uk_p, w_uv, tabs, seg, tab_blk, l)
        ks_a.append(ka[:t_ctx].reshape(n_ctx, s_len, N_KV_A, HD_A))
        vs_a.append(va[:t_ctx].reshape(n_ctx, s_len, N_KV_A, HD_A))
        ckvs.append(ckvn[:t_ctx].reshape(n_ctx, s_len, KV_RANK))
        krs.append(kr[:t_ctx, NOPE_B:NOPE_B + ROPE_B].reshape(n_ctx, s_len, ROPE_B))

        oa_c = _gqa_ctx(qa, ka, va, sink_p, l, n_ctx, s_len)
        oa_l = _gqa_lat(qa, ka, va, cache_k2, cache_v2, sink_p, l, n_lat, n_len, t_ctx)
        kbx, vbx = _cache_kv(cache_ckv, cache_kr_pad, w_uk_p, w_uv, l)
        ob_c = _mla_ctx(qb, kb, vb, n_ctx, s_len)
        ob_l = _mla_lat(qb, kb, vb, kbx, vbx, n_lat, n_len, t_ctx)
        lru_args = (conv_w, conv_b3, wa_bd, lru_ba, wx_bd, lru_bx, sp, l)
        oc_c, st_c = _lru(xc, yc, h0_ctx, 0, *lru_args, n_ctx, s_len, 0)
        oc_l, _ = _lru(xc, yc, state_lru, l, *lru_args, n_lat, n_len, t_ctx)
        lrus.append(st_c)
        x1, h, e4, g4, r4, cnt = _merge(x_ctx, x_lat, t, ada, (oa_c, ob_c, oc_c), (oa_l, ob_l, oc_l), ln1_3, w_g,
                                        w_pa_p, w_pb_b, w_pc_b, w_out_b, ln2_3, w_r_cat, b_r3, seg_m, merge_tile, l)

        counts = cnt[0].astype(jnp.int32)
        padded = (counts + EXPERT_BLOCK - 1) // EXPERT_BLOCK * EXPERT_BLOCK
        pad_end = jnp.cumsum(padded)
        pad_start = pad_end - padded
        dest_kt = pad_start[e4.T] + r4.T
        blk_row0 = jnp.arange(n_blocks, dtype=jnp.int32) * EXPERT_BLOCK
        block_e = jnp.minimum(jnp.sum((pad_end[None, :] <= blk_row0[:, None]).astype(jnp.int32), axis=1),
                              N_EXPERTS - 1)
        n_used = (pad_end[-1:] // EXPERT_BLOCK).astype(jnp.int32)
        xb = _sc_dispatch(h, dest_kt.T.reshape(-1), n_rows)
        yb = _experts(xb, block_e, n_used, w_gu, b_gu, w_dn, b_dn, l)
        yg = _sc_gather(yb, dest_kt.reshape(-1)).reshape(TOP_K, t, d // 2)
        x = _combine(x1, ada, yg, g4, final_g.reshape(1, d), seg, l, l == depth - 1)
        x_ctx = x_lat = x

    y_prompt = x[:t_ctx].reshape(n_ctx, s_len, d)
    y_sample = x[t_ctx:].reshape(n_lat, n_len, d)
    return (y_prompt, y_sample, jnp.stack(ks_a, axis=1), jnp.stack(vs_a, axis=1), jnp.stack(ckvs, axis=1),
            jnp.stack(krs, axis=1), jnp.stack(lrus, axis=1))
```

```python
import functools

import numpy as np
import jax
import jax.numpy as jnp
from jax import lax
from jax.experimental import pallas as pl
from jax.experimental.pallas import tpu as pltpu
from jax.experimental.pallas import tpu_sc as plsc

F32 = jnp.float32
BF16 = jnp.bfloat16

D_MODEL = 1024
GRID_W = 64
EPS = 1e-6
ROPE_BASE = 10000.0
NEG_INF = -1e30
HD_A = 64
N_HEADS_A = 8
N_KV_A = 2
WINDOW = 128
N_HEADS_B = 8
NOPE_B = 64
ROPE_B = 32
V_B = 64
Q_RANK = 512
KV_RANK = 256
LRU_W = 512
LRU_HEADS = 8
CONV_W = 4
LRU_C = 8.0
N_EXPERTS = 32
TOP_K = 4
D_FF = D_MODEL
SWIGLU_LIMIT = 7.0
SWIGLU_ALPHA = 1.702

LANE = 128
SUBLANE = 8
TOKEN_TILE = 512
MLA_Q_BLOCK = 2048
GQA_Q_BLOCK = 256
MERGE_TILE = 512
EXPERT_BLOCK = 512
HEAD_PAD_B = 128
LRU_HALF = LRU_W // 2
VMEM_LIMIT = 56 * 1024 * 1024
SC_CORES = 2
SC_SUBCORES = 16
SC_LANES = 16
SC_CHUNK = 64

HEAD_PERM_A = (0, 4, 1, 5, 2, 6, 3, 7)

C_QA, C_KA, C_VA, C_CQ, C_CKV, C_XC, C_YC, C_KR, C_END = 0, 512, 640, 768, 1280, 1536, 2048, 2560, 2688


def _cparams(sem):
    return pltpu.CompilerParams(dimension_semantics=sem, vmem_limit_bytes=VMEM_LIMIT)


def _layer_spec(a, layer):
    zeros = (0,) * (a.ndim - 1)
    return pl.BlockSpec((None,) + a.shape[1:], lambda *_: (layer,) + zeros)


def _ada_spec(ada, layer, seg):
    return pl.BlockSpec((None, None, 1, ada.shape[-1]), lambda i: (layer, seg(i), 0, 0))


def _stream_specs(x_ctx, x_lat, t_ctx, tm):
    ctx_tiles = t_ctx // tm
    lat0 = 0 if x_lat is not x_ctx else ctx_tiles
    d = x_ctx.shape[1]
    return (pl.BlockSpec((tm, d), lambda i: (jnp.minimum(i, ctx_tiles - 1), 0)),
            pl.BlockSpec((tm, d), lambda i: (jnp.maximum(i - ctx_tiles, 0) + lat0, 0)))


def _rms(x, g):
    return x * lax.rsqrt(jnp.mean(x * x, axis=-1, keepdims=True) + EPS) * g


def _dot(a, b):
    return jnp.dot(a, b, preferred_element_type=F32)


def _dot_t(a, b):
    return lax.dot_general(a, b, (((1,), (1,)), ((), ())), preferred_element_type=F32)


def _sigmoid(x):
    return 0.5 * jnp.tanh(0.5 * x) + 0.5


def _pack_bf16_pairs(x):
    n = x.shape[1] // 2
    bits = lambda v: pltpu.bitcast(v.astype(BF16).astype(F32), jnp.uint32)
    return pltpu.bitcast((bits(x[:, :n]) >> 16) | (bits(x[:, n:]) & jnp.uint32(0xFFFF0000)), jnp.int32)


def _unpack_bf16_pairs(p):
    p = pltpu.bitcast(p, jnp.uint32)
    lo = pltpu.bitcast(p << 16, F32)
    hi = pltpu.bitcast(p & jnp.uint32(0xFFFF0000), F32)
    return jnp.concatenate([lo, hi], axis=1)


def _ada_kernel(c_ref, w_ref, b_ref, o_ref):
    c = c_ref[...]
    s = c * jax.nn.sigmoid(c)
    o_ref[...] = _dot(s.astype(BF16), w_ref[...].astype(BF16)) + b_ref[...]


def _ada_all(cond8, w_ada, b_ada):
    depth, d, n6 = w_ada.shape
    nb = 1536
    return pl.pallas_call(
        _ada_kernel,
        grid=(depth, n6 // nb),
        in_specs=[pl.BlockSpec((SUBLANE, d), lambda l, j: (0, 0)),
                  pl.BlockSpec((None, d, nb), lambda l, j: (l, 0, j)),
                  pl.BlockSpec((None, 1, nb), lambda l, j: (l, 0, j))],
        out_specs=pl.BlockSpec((None, SUBLANE, nb), lambda l, j: (l, 0, j)),
        out_shape=jax.ShapeDtypeStruct((depth, SUBLANE, n6), F32),
        compiler_params=_cparams(("arbitrary", "arbitrary")),
        name="ada",
    )(cond8, w_ada, b_ada.reshape(depth, 1, n6))


def _swap_halves(x, half):
    n = x.shape[-1]
    lane = lax.broadcasted_iota(jnp.int32, x.shape, x.ndim - 1)
    first = (lane % (2 * half)) < half
    return jnp.where(first, pltpu.roll(x, n - half, x.ndim - 1), pltpu.roll(x, half, x.ndim - 1))


def _rope(x, cos, sin_signed, half):
    reps = x.shape[-1] // cos.shape[-1]
    if reps > 1:
        cos = jnp.concatenate([cos] * reps, axis=-1)
        sin_signed = jnp.concatenate([sin_signed] * reps, axis=-1)
    return x * cos + _swap_halves(x, half) * sin_signed


def _in_kernel(xc_in_ref, xl_in_ref, ada_ref, ln1_ref, w_ref, gq_ref, wuq_ref, gkv_ref, wuk_ref, wuv_ref,
               ca_ref, sa_ref, cb_ref, sb_ref,
               qa_ref, ka_ref, va_ref, qb_ref, ckvn_ref, kb_ref, vb_ref, kr_ref, xc_ref, yc_ref, *, ctx_tiles):
    d = D_MODEL
    x = jnp.where(pl.program_id(0) < ctx_tiles, xc_in_ref[...], xl_in_ref[...])
    shift = ada_ref[:, 0:d]
    scale = ada_ref[:, d:2 * d]
    u = _rms(x, ln1_ref[...]) * (1.0 + scale) + shift
    p = _dot(u.astype(BF16), w_ref[...])
    ca, sa, cb, sb = ca_ref[...], sa_ref[...], cb_ref[...], sb_ref[...]

    qa = _rope(p[:, C_QA:C_KA], ca, sa, HD_A // 4) * (HD_A ** -0.5)
    qa_ref[...] = qa.astype(BF16)
    ka_ref[...] = _rope(p[:, C_KA:C_VA], ca, sa, HD_A // 4)
    va_ref[...] = p[:, C_VA:C_CQ]

    cq = _rms(p[:, C_CQ:C_CKV], gq_ref[...])
    qb = _dot(cq.astype(BF16), wuq_ref[...])
    qb = _rope(qb, cb, sb, ROPE_B // 4) * ((NOPE_B + ROPE_B) ** -0.5)
    qb_ref[...] = qb.astype(BF16)

    ckvn = _rms(p[:, C_CKV:C_XC], gkv_ref[...])
    ckvn_ref[...] = ckvn
    ckvn_b = ckvn.astype(BF16)
    kr = _rope(p[:, C_KR:C_END], cb, sb, ROPE_B // 4)
    kr_ref[...] = kr
    kb = _dot(ckvn_b, wuk_ref[...]) + jnp.concatenate([kr] * N_HEADS_B, axis=-1)
    kb_ref[...] = kb.astype(BF16)
    vb_ref[...] = _dot(ckvn_b, wuv_ref[...]).astype(BF16)

    xc_ref[...] = p[:, C_XC:C_YC]
    yc_ref[...] = p[:, C_YC:C_KR]


def _in_proj(x_ctx, x_lat, t_ctx, t, ada, ln1, w_main, g_q, w_uq, g_kv, w_uk, w_uv, tabs, seg, tab_blk, layer):
    tm = TOKEN_TILE
    row = lambda n: pl.BlockSpec((tm, n), lambda i: (i, 0))
    full = lambda a: _layer_spec(a, layer)
    tab = pl.BlockSpec((tm, LANE), lambda i: (tab_blk(i), 0))
    sds = lambda n, dt: jax.ShapeDtypeStruct((t, n), dt)
    nb = N_HEADS_B * HEAD_PAD_B
    return pl.pallas_call(
        functools.partial(_in_kernel, ctx_tiles=t_ctx // tm),
        grid=(t // tm,),
        in_specs=[*_stream_specs(x_ctx, x_lat, t_ctx, tm), _ada_spec(ada, layer, seg), full(ln1), full(w_main),
                  full(g_q), full(w_uq), full(g_kv), full(w_uk), full(w_uv), tab, tab, tab, tab],
        out_specs=[row(512), row(128), row(128), row(nb), row(KV_RANK), row(nb), row(512), row(128),
                   row(LRU_W), row(LRU_W)],
        out_shape=[sds(512, BF16), sds(128, F32), sds(128, F32), sds(nb, BF16), sds(KV_RANK, F32),
                   sds(nb, BF16), sds(512, BF16), sds(128, F32), sds(LRU_W, F32), sds(LRU_W, F32)],
        compiler_params=_cparams(("arbitrary",)),
        name="in_proj",
    )(x_ctx, x_lat, ada, ln1, w_main, g_q, w_uq, g_kv, w_uk, w_uv, *tabs)


def _gqa_heads(q, k_all, v_all, bias, sink_ref, layer, o_ref):
    lane = lax.broadcasted_iota(jnp.int32, (q.shape[0], LANE), 1)
    low = lane < HD_A
    for pair in range(N_HEADS_A // 2):
        qp = q[:, pair * LANE:(pair + 1) * LANE]
        outs = []
        for g in range(N_KV_A):
            qm = jnp.where(low if g == 0 else ~low, qp, jnp.zeros_like(qp))
            s = _dot_t(qm, k_all)
            if bias is not None:
                s = s + bias
            sink = sink_ref[layer, 2 * pair + g]
            m = jnp.maximum(jnp.max(s, axis=-1, keepdims=True), sink)
            e = jnp.exp(s - m)
            l = jnp.sum(e, axis=-1, keepdims=True) + jnp.exp(sink - m)
            outs.append(_dot(e.astype(BF16), v_all) / l)
        o_ref[:, pair * LANE:(pair + 1) * LANE] = jnp.where(low, outs[0], outs[1]).astype(o_ref.dtype)


def _gqa_ctx_kernel(sink_ref, q_ref, k_ref, v_ref, o_ref, *, layer):
    _gqa_heads(q_ref[...], k_ref[...].astype(BF16), v_ref[...].astype(BF16), None, sink_ref, layer, o_ref)


def _gqa_ctx(qa, ka, va, sink_p, layer, n_seq, s_len):
    return pl.pallas_call(
        functools.partial(_gqa_ctx_kernel, layer=layer),
        grid=(n_seq,),
        in_specs=[pl.BlockSpec(memory_space=pltpu.SMEM),
                  pl.BlockSpec((s_len, 512), lambda b: (b, 0)),
                  pl.BlockSpec((s_len, LANE), lambda b: (b, 0)),
                  pl.BlockSpec((s_len, LANE), lambda b: (b, 0))],
        out_specs=pl.BlockSpec((s_len, 512), lambda b: (b, 0)),
        out_shape=jax.ShapeDtypeStruct((n_seq * s_len, 512), BF16),
        compiler_params=_cparams(("arbitrary",)),
        name="gqa_ctx",
    )(sink_p, qa, ka, va)


def _gqa_lat_kernel(sink_ref, q_ref, kp_ref, kc_ref, kn_ref, vp_ref, vc_ref, vn_ref, kx_ref, vx_ref, o_ref, *,
                    layer):
    j = pl.program_id(1)
    last = pl.num_programs(1) - 1
    w = WINDOW
    qb = q_ref.shape[0]
    k_all = jnp.concatenate([kp_ref[...], kc_ref[...], kn_ref[...], kx_ref[...]], axis=0).astype(BF16)
    v_all = jnp.concatenate([vp_ref[...], vc_ref[...], vn_ref[...], vx_ref[...]], axis=0).astype(BF16)
    n_loc = qb + 2 * w
    qi = lax.broadcasted_iota(jnp.int32, (qb, n_loc), 0)
    col = lax.broadcasted_iota(jnp.int32, (qb, n_loc), 1)
    rel = col - w - qi
    zero = jnp.zeros((qb, n_loc), F32)
    neg = jnp.full((qb, n_loc), NEG_INF, F32)
    band = jnp.where(jnp.abs(rel) <= w, zero, neg)
    before = jnp.where(col < w, jnp.where(j > 0, zero, neg), zero)
    after = jnp.where(col >= qb + w, jnp.where(j < last, zero, neg), zero)
    bias = jnp.concatenate([band + before + after, jnp.zeros((qb, kx_ref.shape[0]), F32)], axis=1)
    _gqa_heads(q_ref[...], k_all, v_all, bias, sink_ref, layer, o_ref)


def _gqa_lat(qa, ka, va, kx, vx, sink_p, layer, n_seq, n_len, row0):
    w = WINDOW
    qb = GQA_Q_BLOCK
    r = qb // w
    nqb = n_len // qb
    nwb = n_len // w
    cur = lambda b, j: (row0 // qb + b * nqb + j, 0)
    prev = lambda b, j: (row0 // w + b * nwb + jnp.maximum(r * j - 1, 0), 0)
    nxt = lambda b, j: (row0 // w + b * nwb + jnp.minimum(r * j + r, nwb - 1), 0)
    past = kx.shape[2]
    cache = pl.BlockSpec((None, None, past, LANE), lambda b, j: (b, layer, 0, 0))
    return pl.pallas_call(
        functools.partial(_gqa_lat_kernel, layer=layer),
        grid=(n_seq, nqb),
        in_specs=[pl.BlockSpec(memory_space=pltpu.SMEM),
                  pl.BlockSpec((qb, 512), cur),
                  pl.BlockSpec((w, LANE), prev), pl.BlockSpec((qb, LANE), cur), pl.BlockSpec((w, LANE), nxt),
                  pl.BlockSpec((w, LANE), prev), pl.BlockSpec((qb, LANE), cur), pl.BlockSpec((w, LANE), nxt),
                  cache, cache],
        out_specs=pl.BlockSpec((qb, 512), lambda b, j: (b * nqb + j, 0)),
        out_shape=jax.ShapeDtypeStruct((n_seq * n_len, 512), BF16),
        compiler_params=_cparams(("arbitrary", "arbitrary")),
        name="gqa_lat",
    )(sink_p, qa, ka, ka, ka, va, va, va, kx, vx)


def _mla_kernel(*refs, n_src, chunk):
    q_ref = refs[0]
    kv_refs = refs[1:1 + 2 * n_src]
    o_ref = refs[-1]
    qb = q_ref.shape[0]
    lane = lax.broadcasted_iota(jnp.int32, (qb, LANE), 1)
    for pair in range(q_ref.shape[1] // (2 * HEAD_PAD_B)):
        outs = []
        for hh in range(2):
            hs = slice((2 * pair + hh) * HEAD_PAD_B, (2 * pair + hh + 1) * HEAD_PAD_B)
            q = q_ref[:, hs]
            m = jnp.full((qb, 1), NEG_INF, F32)
            l = jnp.zeros((qb, 1), F32)
            acc = jnp.zeros((qb, LANE), F32)
            for s_i in range(n_src):
                k_ref, v_ref = kv_refs[2 * s_i], kv_refs[2 * s_i + 1]
                nk = k_ref.shape[0]
                for c in range(nk // chunk):
                    k = k_ref[c * chunk:(c + 1) * chunk, hs]
                    v = v_ref[c * chunk:(c + 1) * chunk, pair * LANE:(pair + 1) * LANE]
                    s = _dot_t(q, k)
                    m_new = jnp.maximum(m, jnp.max(s, axis=-1, keepdims=True))
                    alpha = jnp.exp(m - m_new)
                    e = jnp.exp(s - m_new)
                    l = alpha * l + jnp.sum(e, axis=-1, keepdims=True)
                    acc = alpha * acc + _dot(e.astype(BF16), v)
                    m = m_new
            outs.append(acc / l)
        o_ref[:, pair * LANE:(pair + 1) * LANE] = jnp.where(lane < V_B, outs[0], outs[1]).astype(o_ref.dtype)


def _mla_ctx(qb, kb, vb, n_seq, s_len):
    nq = N_HEADS_B * HEAD_PAD_B
    nv = N_HEADS_B * V_B
    return pl.pallas_call(
        functools.partial(_mla_kernel, n_src=1, chunk=s_len),
        grid=(n_seq,),
        in_specs=[pl.BlockSpec((s_len, nq), lambda b: (b, 0)),
                  pl.BlockSpec((s_len, nq), lambda b: (b, 0)),
                  pl.BlockSpec((s_len, nv), lambda b: (b, 0))],
        out_specs=pl.BlockSpec((s_len, nv), lambda b: (b, 0)),
        out_shape=jax.ShapeDtypeStruct((n_seq * s_len, nv), BF16),
        compiler_params=_cparams(("arbitrary",)),
        name="mla_ctx",
    )(qb, kb, vb)


def _mla_lat(qb, kb, vb, kbx, vbx, n_seq, n_len, row0):
    npair = N_HEADS_B // 2
    qblk = min(MLA_Q_BLOCK, n_len)
    nqb = n_len // qblk
    past = kbx.shape[0] // n_seq
    qmap = lambda b, p, j: (row0 // qblk + b * nqb + j, p)
    return pl.pallas_call(
        functools.partial(_mla_kernel, n_src=2, chunk=512),
        grid=(n_seq, npair, nqb),
        in_specs=[pl.BlockSpec((qblk, 2 * HEAD_PAD_B), qmap),
                  pl.BlockSpec((n_len, 2 * HEAD_PAD_B), lambda b, p, j: (row0 // n_len + b, p)),
                  pl.BlockSpec((n_len, LANE), lambda b, p, j: (row0 // n_len + b, p)),
                  pl.BlockSpec((past, 2 * HEAD_PAD_B), lambda b, p, j: (b, p)),
                  pl.BlockSpec((past, LANE), lambda b, p, j: (b, p))],
        out_specs=pl.BlockSpec((qblk, LANE), lambda b, p, j: (b * nqb + j, p)),
        out_shape=jax.ShapeDtypeStruct((n_seq * n_len, 512), BF16),
        compiler_params=_cparams(("arbitrary", "arbitrary", "arbitrary")),
        name="mla_lat",
    )(qb, kb, vb, kbx, vbx)


def _cache_kv_kernel(ckv_ref, kr_ref, wuk_ref, wuv_ref, kb_ref, vb_ref):
    c = ckv_ref[...].astype(BF16)
    kb = _dot(c, wuk_ref[...]) + jnp.concatenate([kr_ref[...]] * N_HEADS_B, axis=-1)
    kb_ref[...] = kb.astype(BF16)
    vb_ref[...] = _dot(c, wuv_ref[...]).astype(BF16)


def _cache_kv(cache_ckv, cache_kr_pad, w_uk, w_uv, layer):
    n_seq, _, past, _ = cache_ckv.shape
    nb = N_HEADS_B * HEAD_PAD_B
    return pl.pallas_call(
        _cache_kv_kernel,
        grid=(n_seq,),
        in_specs=[pl.BlockSpec((None, None, past, KV_RANK), lambda b: (b, layer, 0, 0)),
                  pl.BlockSpec((None, None, past, LANE), lambda b: (b, layer, 0, 0)),
                  _layer_spec(w_uk, layer), _layer_spec(w_uv, layer)],
        out_specs=[pl.BlockSpec((past, nb), lambda b: (b, 0)), pl.BlockSpec((past, 512), lambda b: (b, 0))],
        out_shape=[jax.ShapeDtypeStruct((n_seq * past, nb), BF16), jax.ShapeDtypeStruct((n_seq * past, 512), BF16)],
        compiler_params=_cparams(("arbitrary",)),
        name="cache_kv",
    )(cache_ckv, cache_kr_pad, w_uk, w_uv)


def _lru_kernel(xc_ref, yc_ref, h0_ref, cw_ref, cb_ref, wa_ref, ba_ref, wx_ref, bx_ref, sp_ref,
                o_ref, st_ref, pad_ref, xcv_ref, a_ref, b_ref, *, chunk):
    n = xc_ref.shape[0]
    halo = SUBLANE
    pad_ref[0:halo, :] = jnp.zeros((halo, LRU_W), F32)
    pad_ref[halo + n:2 * halo + n, :] = jnp.zeros((halo, LRU_W), F32)
    pad_ref[halo:halo + n, :] = xc_ref[...]
    left = CONV_W // 2
    for c in range(n // chunk):
        r0 = c * chunk
        acc = jnp.broadcast_to(cb_ref[...], (chunk, LRU_W))
        for j in range(CONV_W):
            off = halo + r0 + j - left
            acc = acc + cw_ref[j:j + 1, :] * pad_ref[off:off + chunk, :]
        xcv_ref[r0:r0 + chunk, :] = acc

    row = lax.broadcasted_iota(jnp.int32, (SUBLANE, LRU_W), 0)
    for d in range(2):
        for c in range(n // chunk):
            r0 = c * chunk
            xv = xcv_ref[r0:r0 + chunk, :]
            xb = xv.astype(BF16)
            for hf in range(2):
                cs = slice(hf * LRU_HALF, (hf + 1) * LRU_HALF)
                r = _sigmoid(_dot(xb[:, cs], wa_ref[d, hf]) + ba_ref[d:d + 1, cs])
                i = _sigmoid(_dot(xb[:, cs], wx_ref[d, hf]) + bx_ref[d:d + 1, cs])
                log_a = (-LRU_C) * r * sp_ref[d:d + 1, cs]
                a = jnp.exp(log_a)
                a_ref[d, r0:r0 + chunk, cs] = a
                b_ref[d, r0:r0 + chunk, cs] = jnp.sqrt(-jnp.tanh(log_a) * (a * a + 1.0)) * (i * xv[:, cs])

    def scan_group(d, grp, h):
        rows = pl.ds(pl.multiple_of(grp * SUBLANE, SUBLANE), SUBLANE)
        a = a_ref[d, rows, :]
        b = b_ref[d, rows, :]
        for sh in (1, 2, 4):
            if d == 0:
                keep = row >= sh
                a_s = jnp.where(keep, pltpu.roll(a, sh, 0), 1.0)
                b_s = jnp.where(keep, pltpu.roll(b, sh, 0), 0.0)
            else:
                keep = row < SUBLANE - sh
                a_s = jnp.where(keep, pltpu.roll(a, SUBLANE - sh, 0), 1.0)
                b_s = jnp.where(keep, pltpu.roll(b, SUBLANE - sh, 0), 0.0)
            b = a * b_s + b
            a = a * a_s
        hrows = a * h + b
        b_ref[d, rows, :] = hrows
        return hrows[SUBLANE - 1:SUBLANE, :] if d == 0 else hrows[0:1, :]

    n_grp = n // SUBLANE

    def body(g, hs):
        return scan_group(0, g, hs[0]), scan_group(1, n_grp - 1 - g, hs[1])

    h_f, h_b = lax.fori_loop(0, n_grp, body, (h0_ref[0:1, :], h0_ref[1:2, :]))
    st_ref[0:1, :] = h_f
    st_ref[1:2, :] = h_b

    for c in range(n // chunk):
        rs = slice(c * chunk, (c + 1) * chunk)
        o_ref[rs, :] = ((b_ref[0, rs, :] + b_ref[1, rs, :]) * jax.nn.gelu(yc_ref[rs, :])).astype(o_ref.dtype)


def _lru(xc, yc, h0, h0_layer, conv_w, conv_b, wa, ba, wx, bx, sp, layer, n_seq, n_len, row0):
    blk0 = row0 // n_len
    full = lambda a: _layer_spec(a, layer)
    seq = pl.BlockSpec((n_len, LRU_W), lambda b: (blk0 + b, 0))
    return pl.pallas_call(
        functools.partial(_lru_kernel, chunk=min(n_len, 256)),
        grid=(n_seq,),
        in_specs=[seq, seq, pl.BlockSpec((None, None, 2, LRU_W), lambda b: (b, h0_layer, 0, 0)),
                  full(conv_w), full(conv_b), full(wa), full(ba), full(wx), full(bx), full(sp)],
        out_specs=[pl.BlockSpec((n_len, LRU_W), lambda b: (b, 0)),
                   pl.BlockSpec((None, 2, LRU_W), lambda b: (b, 0, 0))],
        out_shape=[jax.ShapeDtypeStruct((n_seq * n_len, LRU_W), BF16),
                   jax.ShapeDtypeStruct((n_seq, 2, LRU_W), F32)],
        scratch_shapes=[pltpu.VMEM((n_len + 2 * SUBLANE, LRU_W), F32), pltpu.VMEM((n_len, LRU_W), F32),
                        pltpu.VMEM((2, n_len, LRU_W), F32), pltpu.VMEM((2, n_len, LRU_W), F32)],
        compiler_params=_cparams(("arbitrary",)),
        name="lru",
    )(xc, yc, h0, conv_w, conv_b, wa, ba, wx, bx, sp)


def _merge_kernel(xc_in_ref, xl_in_ref, ada_ref, oac_ref, oal_ref, obc_ref, obl_ref, occ_ref, ocl_ref, ln1_ref,
                  wg_ref, wpa_ref, wpb_ref, wpc_ref, wout_ref, ln2_ref, wr_ref, br_ref,
                  x1_ref, h_ref, e4_ref, g4_ref, r4_ref, cnt_ref, carry_ref, *, ctx_tiles):
    d = D_MODEL
    tm = xc_in_ref.shape[0]
    i = pl.program_id(0)
    is_ctx = i < ctx_tiles
    x = jnp.where(is_ctx, xc_in_ref[...], xl_in_ref[...])
    oa = jnp.where(is_ctx, oac_ref[...], oal_ref[...])
    ob = jnp.where(is_ctx, obc_ref[...], obl_ref[...])
    oc = jnp.where(is_ctx, occ_ref[...], ocl_ref[...])

    @pl.when(i == 0)
    def _():
        carry_ref[...] = jnp.zeros_like(carry_ref)

    ada = ada_ref[...]
    u = _rms(x, ln1_ref[...]) * (1.0 + ada[:, d:2 * d]) + ada[:, 0:d]
    g = jax.nn.sigmoid(_dot(u.astype(BF16), wg_ref[...]))
    m = (g[:, 0:d] * _dot(oa, wpa_ref[...]) + g[:, d:2 * d] * _dot(ob, wpb_ref[...])
         + g[:, 2 * d:3 * d] * _dot(oc, wpc_ref[...]))
    x1 = x + ada[:, 2 * d:3 * d] * _dot(m.astype(BF16), wout_ref[...])
    x1_ref[...] = x1
    h = _rms(x1, ln2_ref[...]) * (1.0 + ada[:, 4 * d:5 * d]) + ada[:, 3 * d:4 * d]
    h_ref[...] = _pack_bf16_pairs(h)

    h_hi = h.astype(BF16)
    h_lo = (h - h_hi.astype(F32)).astype(BF16)
    hw = _dot(h_hi, wr_ref[...])
    logits = (hw[:, :N_EXPERTS] + hw[:, N_EXPERTS:] + _dot(h_lo, wr_ref[:, :N_EXPERTS])) + br_ref[...]
    col = lax.broadcasted_iota(jnp.int32, (tm, N_EXPERTS), 1).astype(F32)
    col4 = lax.broadcasted_iota(jnp.int32, (tm, TOP_K), 1)
    sel_any = jnp.zeros((tm, N_EXPERTS), F32)
    vals, idxs = [], []
    work = logits
    for _k in range(TOP_K):
        mx = jnp.max(work, axis=-1, keepdims=True)
        idx = jnp.min(jnp.where(work == mx, col, float(N_EXPERTS)), axis=-1, keepdims=True)
        sel = col == idx
        vals.append(mx)
        idxs.append(idx)
        sel_any = jnp.where(sel, 1.0, sel_any)
        work = jnp.where(sel, -jnp.inf, work)

    ri = lax.broadcasted_iota(jnp.int32, (tm, tm), 0)
    ci = lax.broadcasted_iota(jnp.int32, (tm, tm), 1)
    tri = jnp.where(ri > ci, 1.0, 0.0).astype(BF16)
    before = _dot(tri, sel_any.astype(BF16)) + carry_ref[...]
    carry = carry_ref[...] + jnp.sum(sel_any, axis=0, keepdims=True)
    carry_ref[...] = carry
    cnt_ref[...] = jnp.broadcast_to(carry, cnt_ref.shape)

    exps = [jnp.exp(v - vals[0]) for v in vals]
    den = exps[0] + exps[1] + exps[2] + exps[3]
    e4 = jnp.zeros((tm, TOP_K), jnp.int32)
    g4 = jnp.zeros((tm, TOP_K), F32)
    r4 = jnp.zeros((tm, TOP_K), jnp.int32)
    for k in range(TOP_K):
        rank = jnp.sum(jnp.where(col == idxs[k], before, 0.0), axis=-1, keepdims=True)
        e4 = jnp.where(col4 == k, idxs[k].astype(jnp.int32), e4)
        g4 = jnp.where(col4 == k, exps[k] / den, g4)
        r4 = jnp.where(col4 == k, rank.astype(jnp.int32), r4)
    e4_ref[...] = e4
    g4_ref[...] = g4
    r4_ref[...] = r4


def _merge(x_ctx, x_lat, t, ada, o_ctx, o_lat, ln1, w_g, w_pa, w_pb, w_pc, w_out, ln2, w_r, b_r, seg, tm, layer):
    d = x_ctx.shape[1]
    t_ctx = o_ctx[0].shape[0]
    ctx_tiles = t_ctx // tm
    row = lambda n: pl.BlockSpec((tm, n), lambda i: (i, 0))
    full = lambda a: _layer_spec(a, layer)
    ctx = pl.BlockSpec((tm, 512), lambda i: (jnp.minimum(i, ctx_tiles - 1), 0))
    lat = pl.BlockSpec((tm, 512), lambda i: (jnp.maximum(i - ctx_tiles, 0), 0))
    return pl.pallas_call(
        functools.partial(_merge_kernel, ctx_tiles=ctx_tiles),
        grid=(t // tm,),
        in_specs=[*_stream_specs(x_ctx, x_lat, t_ctx, tm), _ada_spec(ada, layer, seg), ctx, lat, ctx, lat, ctx, lat,
                  full(ln1), full(w_g), full(w_pa), full(w_pb), full(w_pc), full(w_out), full(ln2), full(w_r),
                  full(b_r)],
        out_specs=[row(d), row(d // 2), row(TOP_K), row(TOP_K), row(TOP_K),
                   pl.BlockSpec((SUBLANE, N_EXPERTS), lambda i: (0, 0))],
        out_shape=[jax.ShapeDtypeStruct((t, d), F32), jax.ShapeDtypeStruct((t, d // 2), jnp.int32),
                   jax.ShapeDtypeStruct((t, TOP_K), jnp.int32), jax.ShapeDtypeStruct((t, TOP_K), F32),
                   jax.ShapeDtypeStruct((t, TOP_K), jnp.int32),
                   jax.ShapeDtypeStruct((SUBLANE, N_EXPERTS), F32)],
        scratch_shapes=[pltpu.VMEM((1, N_EXPERTS), F32)],
        compiler_params=_cparams(("arbitrary",)),
        name="merge",
    )(x_ctx, x_lat, ada, o_ctx[0], o_lat[0], o_ctx[1], o_lat[1], o_ctx[2], o_lat[2], ln1, w_g, w_pa, w_pb, w_pc,
      w_out, ln2, w_r, b_r)


def _expert_kernel(be_ref, nu_ref, x_ref, wgu_ref, bgu_ref, wdn_ref, bdn_ref, y_ref, wgu_s, wdn_s, *, n_chunk):
    b = pl.program_id(0)
    used = b < nu_ref[0]
    new_expert = jnp.logical_or(b == 0, be_ref[b] != be_ref[jnp.maximum(b - 1, 0)])

    @pl.when(jnp.logical_and(used, new_expert))
    def _():
        wgu_s[...] = wgu_ref[...].astype(BF16)
        wdn_s[...] = wdn_ref[...].astype(BF16)

    @pl.when(used)
    def _():
        x = _unpack_bf16_pairs(x_ref[...]).astype(BF16)
        cw = D_FF // n_chunk
        acc = jnp.broadcast_to(bdn_ref[...], x.shape)
        for c in range(n_chunk):
            glu = _dot(x, wgu_s[:, c * cw:(c + 1) * cw]) + bgu_ref[:, c * cw:(c + 1) * cw]
            lin = (_dot(x, wgu_s[:, D_FF + c * cw:D_FF + (c + 1) * cw])
                   + bgu_ref[:, D_FF + c * cw:D_FF + (c + 1) * cw])
            glu = jnp.minimum(glu, SWIGLU_LIMIT)
            lin = jnp.clip(lin, -SWIGLU_LIMIT, SWIGLU_LIMIT)
            act = glu * jax.nn.sigmoid(SWIGLU_ALPHA * glu) * (lin + 1.0)
            acc = acc + _dot(act.astype(BF16), wdn_s[c * cw:(c + 1) * cw, :])
        y_ref[...] = _pack_bf16_pairs(acc)

    @pl.when(jnp.logical_not(used))
    def _():
        y_ref[...] = jnp.zeros_like(y_ref)


def _experts(xb, block_e, n_used, w_gu, b_gu, w_dn, b_dn, layer):
    n_rows, dp = xb.shape
    d = 2 * dp
    depth = w_gu.shape[0]
    blk = EXPERT_BLOCK
    n_blocks = n_rows // blk
    wmap = lambda b, be, nu: (layer, be[b], 0, 0)
    grid_spec = pltpu.PrefetchScalarGridSpec(
        num_scalar_prefetch=2,
        grid=(n_blocks,),
        in_specs=[pl.BlockSpec((blk, dp), lambda b, be, nu: (b, 0)),
                  pl.BlockSpec((None, None, d, 2 * D_FF), wmap),
                  pl.BlockSpec((None, None, 1, 2 * D_FF), wmap),
                  pl.BlockSpec((None, None, D_FF, d), wmap),
                  pl.BlockSpec((None, None, 1, d), wmap)],
        out_specs=pl.BlockSpec((blk, dp), lambda b, be, nu: (b, 0)),
        scratch_shapes=[pltpu.VMEM((d, 2 * D_FF), BF16), pltpu.VMEM((D_FF, d), BF16)],
    )
    return pl.pallas_call(
        functools.partial(_expert_kernel, n_chunk=4),
        grid_spec=grid_spec,
        out_shape=jax.ShapeDtypeStruct((n_rows, dp), jnp.int32),
        compiler_params=_cparams(("arbitrary",)),
        name="experts",
    )(block_e, n_used, xb, w_gu, b_gu.reshape(depth, N_EXPERTS, 1, 2 * D_FF), w_dn,
      b_dn.reshape(depth, N_EXPERTS, 1, d))


def _sc_worker_base(per_w):
    return (lax.axis_index("s") * SC_CORES + lax.axis_index("c")) * per_w


def _sc_gather_rows(table_hbm, out_hbm, idx_v, rows_v, gsem, osem, base, n_chunks):
    ch = SC_CHUNK

    def gather(c, slot):
        return pltpu.make_async_copy(table_hbm.at[idx_v.at[pl.ds(c * ch, ch)]], rows_v.at[slot], gsem.at[slot])

    def put(c, slot):
        return pltpu.make_async_copy(rows_v.at[slot], out_hbm.at[pl.ds(base + c * ch, ch)], osem.at[slot])

    gather(0, 0).start()

    @pl.loop(0, n_chunks // 2)
    def _(i):
        c = 2 * i
        gather(c, 0).wait()
        put(c, 0).start()

        @pl.when(i > 0)
        def _():
            put(c - 1, 1).wait()

        gather(c + 1, 1).start()
        gather(c + 1, 1).wait()
        put(c + 1, 1).start()
        put(c, 0).wait()

        @pl.when(c + 2 < n_chunks)
        def _():
            gather(c + 2, 0).start()

    put(n_chunks - 1, 1).wait()


def _sc_scratch(per_w, d, dtype):
    return [pltpu.VMEM((per_w,), jnp.int32), pltpu.VMEM((2, SC_CHUNK, d), dtype),
            pltpu.SemaphoreType.DMA((2,)), pltpu.SemaphoreType.DMA((2,))]


def _sc_gather(table, idx):
    n_idx = idx.shape[0]
    d = table.shape[1]
    n_workers = SC_CORES * SC_SUBCORES
    per_w = n_idx // n_workers
    assert n_idx % (n_workers * SC_CHUNK * 2) == 0
    mesh = plsc.VectorSubcoreMesh(core_axis_name="c", subcore_axis_name="s")

    @functools.partial(
        pl.kernel, mesh=mesh, out_type=jax.ShapeDtypeStruct((n_idx, d), table.dtype),
        scratch_types=_sc_scratch(per_w, d, table.dtype), name="sc_gather")
    def gather(table_hbm, idx_hbm, out_hbm, idx_v, rows_v, gsem, osem):
        base = _sc_worker_base(per_w)
        pltpu.sync_copy(idx_hbm.at[pl.ds(base, per_w)], idx_v)
        _sc_gather_rows(table_hbm, out_hbm, idx_v, rows_v, gsem, osem, base, per_w // SC_CHUNK)

    return gather(table, idx)


def _sc_dispatch(h, dest, n_rows):
    t, d = h.shape
    n_slots = dest.shape[0]
    n_workers = SC_CORES * SC_SUBCORES
    per_w = n_rows // n_workers
    piece = 16384
    unroll = 8
    fill_shift = 3
    assert n_rows % (n_workers * SC_CHUNK * 2) == 0 and n_slots % piece == 0 and (n_rows >> fill_shift) <= t
    mesh = plsc.VectorSubcoreMesh(core_axis_name="c", subcore_axis_name="s")

    @functools.partial(
        pl.kernel, mesh=mesh, out_type=jax.ShapeDtypeStruct((n_rows, d), h.dtype),
        scratch_types=[pltpu.VMEM((piece,), jnp.int32)] + _sc_scratch(per_w, d, h.dtype),
        compiler_params=pltpu.CompilerParams(needs_layout_passes=False), name="sc_dispatch")
    def dispatch(h_hbm, dest_hbm, out_hbm, dest_v, idx_v, rows_v, gsem, osem):
        base = _sc_worker_base(per_w)
        lane = lax.iota(jnp.int32, SC_LANES)

        @pl.loop(0, per_w // SC_LANES)
        def _(j):
            idx_v[pl.ds(j * SC_LANES, SC_LANES)] = (base + j * SC_LANES + lane) >> fill_shift

        @pl.loop(0, n_slots // piece)
        def _(p):
            pltpu.sync_copy(dest_hbm.at[pl.ds(p * piece, piece)], dest_v)

            @plsc.parallel_loop(0, piece // SC_LANES, unroll=unroll)
            def _(j):
                s0 = j * SC_LANES
                loc = dest_v[pl.ds(s0, SC_LANES)] - base
                mine = (loc >= 0) & (loc < per_w)
                tok = (p * piece + s0 + lane) // TOP_K
                plsc.store_scatter(idx_v, [loc], tok, mask=mine)

        _sc_gather_rows(h_hbm, out_hbm, idx_v, rows_v, gsem, osem, base, per_w // SC_CHUNK)

    return dispatch(h, dest)


def _combine_kernel(x1_ref, ada_ref, yg_ref, g4_ref, fg_ref, o_ref, *, final):
    d = D_MODEL
    g4 = g4_ref[...]
    y = g4[:, 0:1] * _unpack_bf16_pairs(yg_ref[0])
    for k in range(1, TOP_K):
        y = y + g4[:, k:k + 1] * _unpack_bf16_pairs(yg_ref[k])
    x2 = x1_ref[...] + ada_ref[:, 5 * d:6 * d] * y
    if final:
        x2 = _rms(x2, fg_ref[...])
    o_ref[...] = x2


def _combine(x1, ada, yg, g4, final_g, seg, layer, final):
    t, d = x1.shape
    tm = TOKEN_TILE
    return pl.pallas_call(
        functools.partial(_combine_kernel, final=final),
        grid=(t // tm,),
        in_specs=[pl.BlockSpec((tm, d), lambda i: (i, 0)),
                  _ada_spec(ada, layer, seg),
                  pl.BlockSpec((TOP_K, tm, d // 2), lambda i: (0, i, 0)),
                  pl.BlockSpec((tm, TOP_K), lambda i: (i, 0)),
                  pl.BlockSpec((1, d), lambda i: (0, 0))],
        out_specs=pl.BlockSpec((tm, d), lambda i: (i, 0)),
        out_shape=jax.ShapeDtypeStruct((t, d), F32),
        compiler_params=_cparams(("arbitrary",)),
        name="combine",
    )(x1, ada, yg, g4, final_g)


def _rope_tables(n_lat):
    rows = n_lat // GRID_W
    row_ids = jnp.repeat(jnp.arange(rows, dtype=F32), GRID_W)
    col_ids = jnp.tile(jnp.arange(GRID_W, dtype=F32), rows)

    def table(d_rot, lane0):
        d_axis = d_rot // 2
        nf = d_axis // 2
        inv_freq = ROPE_BASE ** (-jnp.arange(0, d_axis, 2, dtype=F32) / d_axis)
        ang_r = row_ids[:, None] * inv_freq
        ang_c = col_ids[:, None] * inv_freq
        ang = jnp.concatenate([ang_r, ang_r, ang_c, ang_c], axis=-1)
        sign = jnp.tile(jnp.concatenate([-jnp.ones((nf,), F32), jnp.ones((nf,), F32)]), 2)
        cos = jnp.ones((n_lat, LANE), F32)
        sin = jnp.zeros((n_lat, LANE), F32)
        for l0 in lane0:
            cos = cos.at[:, l0:l0 + d_rot].set(jnp.cos(ang))
            sin = sin.at[:, l0:l0 + d_rot].set(jnp.sin(ang) * sign)
        ident = (jnp.ones((TOKEN_TILE, LANE), F32), jnp.zeros((TOKEN_TILE, LANE), F32))
        return jnp.concatenate([ident[0], cos], axis=0), jnp.concatenate([ident[1], sin], axis=0)

    ca, sa = table(HD_A, (0, HD_A))
    cb, sb = table(ROPE_B, (NOPE_B,))
    return ca, sa, cb, sb


def _prep_weights(w_in, sink, w_uq, w_ukv, lru_wa, lru_wx, lru_lam, w_pa):
    depth, d, _ = w_in.shape
    cuts = np.cumsum((512, 128, 128, Q_RANK, KV_RANK, ROPE_B, LRU_W, LRU_W, 3 * d))
    perm = np.array(HEAD_PERM_A)
    w_qa = w_in[:, :, :cuts[0]].reshape(depth, d, N_HEADS_A, HD_A)[:, :, perm].reshape(depth, d, 512)
    w_kr = jnp.pad(w_in[:, :, cuts[4]:cuts[5]], ((0, 0), (0, 0), (NOPE_B, LANE - NOPE_B - ROPE_B)))
    w_main = jnp.concatenate([w_qa, w_in[:, :, cuts[0]:cuts[4]], w_in[:, :, cuts[5]:cuts[7]], w_kr],
                             axis=-1).astype(BF16)
    w_g = w_in[:, :, cuts[7]:].astype(BF16)
    sink_p = sink[:, perm]
    w_pa_p = w_pa.reshape(depth, N_HEADS_A, HD_A, d)[:, perm].reshape(depth, 512, d).astype(BF16)
    hb = NOPE_B + ROPE_B
    w_uq_p = jnp.pad(w_uq.reshape(depth, Q_RANK, N_HEADS_B, hb),
                     ((0, 0), (0, 0), (0, 0), (0, HEAD_PAD_B - hb))).reshape(depth, Q_RANK, -1).astype(BF16)
    w_ukv4 = w_ukv.reshape(depth, KV_RANK, N_HEADS_B, NOPE_B + V_B)
    w_uk_p = jnp.pad(w_ukv4[..., :NOPE_B], ((0, 0), (0, 0), (0, 0), (0, HEAD_PAD_B - NOPE_B))
                     ).reshape(depth, KV_RANK, -1).astype(BF16)
    w_uv = w_ukv4[..., NOPE_B:].reshape(depth, KV_RANK, -1).astype(BF16)

    def block_diag(w):
        hpb = LRU_HEADS // 2
        blk = LRU_W // LRU_HEADS
        w = w.reshape(depth, 2, 2, hpb, blk, blk)
        eye = jnp.eye(hpb, dtype=w.dtype)
        out = jnp.einsum('ldghij,hk->ldghikj', w, eye)
        return out.reshape(depth, 2, 2, hpb * blk, hpb * blk).astype(BF16)

    sp = jax.nn.softplus(-lru_lam.astype(F32))
    return w_main, w_g, sink_p, w_pa_p, w_uq_p, w_uk_p, w_uv, block_diag(lru_wa), block_diag(lru_wx), sp


def kernel(x_prompt, x_sample, cache_k_a, cache_v_a, cache_ckv, cache_krope, state_lru, c, c_ctx, ln1_g, ln2_g,
           final_g, w_ada, b_ada, w_in, sink, g_q, w_uq, g_kv, w_ukv, conv_w, conv_b, lru_wa, lru_ba, lru_wx,
           lru_bx, lru_lam, w_pa, w_pb, w_pc, w_out, w_router, b_router, w_gu, b_gu, w_dn, b_dn):
    n_ctx, s_len, d = x_prompt.shape
    n_lat, n_len, _ = x_sample.shape
    depth = w_in.shape[0]
    past = cache_k_a.shape[2]
    t_ctx = n_ctx * s_len
    t_lat = n_lat * n_len
    t = t_ctx + t_lat
    tm = TOKEN_TILE
    ctx_tiles = t_ctx // tm
    lat_tiles = n_len // tm
    merge_tile = min(MERGE_TILE, n_len)
    assert t_ctx % n_len == 0 and t_ctx % tm == 0 and n_len % tm == 0 and n_lat + 1 <= SUBLANE
    assert t_ctx % merge_tile == 0 and n_len % merge_tile == 0

    seg = lambda i: jnp.where(i < ctx_tiles, 0, 1 + (i - ctx_tiles) // lat_tiles)
    seg_m = lambda i: jnp.where(i < t_ctx // merge_tile, 0, 1 + (i - t_ctx // merge_tile) // (n_len // merge_tile))
    tab_blk = lambda i: jnp.where(i < ctx_tiles, 0, 1 + (i - ctx_tiles) % lat_tiles)

    cond8 = jnp.zeros((SUBLANE, d), F32).at[0].set(c_ctx).at[1:1 + n_lat].set(c)
    ada = _ada_all(cond8, w_ada, b_ada)
    tabs = _rope_tables(n_len)
    (w_main, w_g, sink_p, w_pa_p, w_uq_p, w_uk_p, w_uv, wa_bd, wx_bd, sp) = _prep_weights(
        w_in, sink, w_uq, w_ukv, lru_wa, lru_wx, lru_lam, w_pa)
    w_pb_b, w_pc_b, w_out_b = w_pb.astype(BF16), w_pc.astype(BF16), w_out.astype(BF16)
    w_r_hi = w_router.astype(BF16)
    w_r_cat = jnp.concatenate([w_r_hi, (w_router - w_r_hi.astype(F32)).astype(BF16)], axis=-1)
    cache_k2 = cache_k_a.reshape(n_lat, depth, past, N_KV_A * HD_A)
    cache_v2 = cache_v_a.reshape(n_lat, depth, past, N_KV_A * HD_A)
    cache_kr_pad = jnp.pad(cache_krope, ((0, 0), (0, 0), (0, 0), (NOPE_B, LANE - NOPE_B - ROPE_B)))
    h0_ctx = jnp.zeros((n_ctx, 1, 2, LRU_W), F32)
    rows3 = lambda a: a.reshape(depth, 1, -1)
    ada = ada.reshape(depth, SUBLANE, 1, 6 * d)
    ln1_3, ln2_3, g_q3, g_kv3, conv_b3, b_r3 = (rows3(a) for a in (ln1_g, ln2_g, g_q, g_kv, conv_b, b_router))

    x_ctx, x_lat = x_prompt.reshape(t_ctx, d), x_sample.reshape(t_lat, d)
    ks_a, vs_a, ckvs, krs, lrus = [], [], [], [], []
    n_rows = t * TOP_K + N_EXPERTS * EXPERT_BLOCK
    n_blocks = n_rows // EXPERT_BLOCK
    for l in range(depth):
        qa, ka, va, qb, ckvn, kb, vb, kr, xc, yc = _in_proj(
            x_ctx, x_lat, t_ctx, t, ada, ln1_3, w_main, g_q3, w_uq_p, g_kv3, w_uk_p, w_uv, tabs, seg, tab_blk, l)
        ks_a.append(ka[:t_ctx].reshape(n_ctx, s_len, N_KV_A, HD_A))
        vs_a.append(va[:t_ctx].reshape(n_ctx, s_len, N_KV_A, HD_A))
        ckvs.append(ckvn[:t_ctx].reshape(n_ctx, s_len, KV_RANK))
        krs.append(kr[:t_ctx, NOPE_B:NOPE_B + ROPE_B].reshape(n_ctx, s_len, ROPE_B))

        oa_c = _gqa_ctx(qa, ka, va, sink_p, l, n_ctx, s_len)
        oa_l = _gqa_lat(qa, ka, va, cache_k2, cache_v2, sink_p, l, n_lat, n_len, t_ctx)
        kbx, vbx = _cache_kv(cache_ckv, cache_kr_pad, w_uk_p, w_uv, l)
        ob_c = _mla_ctx(qb, kb, vb, n_ctx, s_len)
        ob_l = _mla_lat(qb, kb, vb, kbx, vbx, n_lat, n_len, t_ctx)
        lru_args = (conv_w, conv_b3, wa_bd, lru_ba, wx_bd, lru_bx, sp, l)
        oc_c, st_c = _lru(xc, yc, h0_ctx, 0, *lru_args, n_ctx, s_len, 0)
        oc_l, _ = _lru(xc, yc, state_lru, l, *lru_args, n_lat, n_len, t_ctx)
        lrus.append(st_c)
        x1, h, e4, g4, r4, cnt = _merge(x_ctx, x_lat, t, ada, (oa_c, ob_c, oc_c), (oa_l, ob_l, oc_l), ln1_3, w_g,
                                        w_pa_p, w_pb_b, w_pc_b, w_out_b, ln2_3, w_r_cat, b_r3, seg_m, merge_tile, l)

        counts = cnt[0].astype(jnp.int32)
        padded = (counts + EXPERT_BLOCK - 1) // EXPERT_BLOCK * EXPERT_BLOCK
        pad_end = jnp.cumsum(padded)
        pad_start = pad_end - padded
        dest4 = pad_start[e4] + r4
        blk_row0 = jnp.arange(n_blocks, dtype=jnp.int32) * EXPERT_BLOCK
        block_e = jnp.minimum(jnp.sum((pad_end[None, :] <= blk_row0[:, None]).astype(jnp.int32), axis=1),
                              N_EXPERTS - 1)
        n_used = (pad_end[-1:] // EXPERT_BLOCK).astype(jnp.int32)
        xb = _sc_dispatch(h, dest4.reshape(-1), n_rows)
        yb = _experts(xb, block_e, n_used, w_gu, b_gu, w_dn, b_dn, l)
        yg = _sc_gather(yb, dest4.T.reshape(-1)).reshape(TOP_K, t, d // 2)
        x = _combine(x1, ada, yg, g4, final_g.reshape(1, d), seg, l, l == depth - 1)
        x_ctx = x_lat = x

    y_prompt = x[:t_ctx].reshape(n_ctx, s_len, d)
    y_sample = x[t_ctx:].reshape(n_lat, n_len, d)
    return (y_prompt, y_sample, jnp.stack(ks_a, axis=1), jnp.stack(vs_a, axis=1), jnp.stack(ckvs, axis=1),
            jnp.stack(krs, axis=1), jnp.stack(lrus, axis=1))
```

```python
import functools

import numpy as np
import jax
import jax.numpy as jnp
from jax import lax
from jax.experimental import pallas as pl
from jax.experimental.pallas import tpu as pltpu
from jax.experimental.pallas import tpu_sc as plsc

F32 = jnp.float32
BF16 = jnp.bfloat16

D_MODEL = 1024
GRID_W = 64
EPS = 1e-6
ROPE_BASE = 10000.0
NEG_INF = -1e30
HD_A = 64
N_HEADS_A = 8
N_KV_A = 2
WINDOW = 128
N_HEADS_B = 8
NOPE_B = 64
ROPE_B = 32
V_B = 64
Q_RANK = 512
KV_RANK = 256
LRU_W = 512
LRU_HEADS = 8
CONV_W = 4
LRU_C = 8.0
N_EXPERTS = 32
TOP_K = 4
D_FF = D_MODEL
SWIGLU_LIMIT = 7.0
SWIGLU_ALPHA = 1.702

LANE = 128
SUBLANE = 8
TOKEN_TILE = 512
MLA_Q_BLOCK = 2048
GQA_Q_BLOCK = 512
MERGE_TILE = 512
EXPERT_BLOCK = 512
HEAD_PAD_B = 128
LRU_HALF = LRU_W // 2
VMEM_LIMIT = 56 * 1024 * 1024
SC_CORES = 2
SC_SUBCORES = 16
SC_LANES = 16
SC_CHUNK = 64

HEAD_PERM_A = (0, 4, 1, 5, 2, 6, 3, 7)

C_QA, C_KA, C_VA, C_CQ, C_CKV, C_XC, C_YC, C_KR, C_END = 0, 512, 640, 768, 1280, 1536, 2048, 2560, 2688


def _cparams(sem):
    return pltpu.CompilerParams(dimension_semantics=sem, vmem_limit_bytes=VMEM_LIMIT)


def _layer_spec(a, layer):
    zeros = (0,) * (a.ndim - 1)
    return pl.BlockSpec((None,) + a.shape[1:], lambda *_: (layer,) + zeros)


def _ada_spec(ada, layer, seg):
    return pl.BlockSpec((None, None, 1, ada.shape[-1]), lambda i: (layer, seg(i), 0, 0))


def _stream_specs(x_ctx, x_lat, t_ctx, tm):
    ctx_tiles = t_ctx // tm
    lat0 = 0 if x_lat is not x_ctx else ctx_tiles
    d = x_ctx.shape[1]
    return (pl.BlockSpec((tm, d), lambda i: (jnp.minimum(i, ctx_tiles - 1), 0)),
            pl.BlockSpec((tm, d), lambda i: (jnp.maximum(i - ctx_tiles, 0) + lat0, 0)))


def _rms(x, g):
    return x * lax.rsqrt(jnp.mean(x * x, axis=-1, keepdims=True) + EPS) * g


def _dot(a, b):
    return jnp.dot(a, b, preferred_element_type=F32)


def _dot_t(a, b):
    return lax.dot_general(a, b, (((1,), (1,)), ((), ())), preferred_element_type=F32)


def _sigmoid(x):
    return 0.5 * jnp.tanh(0.5 * x) + 0.5


def _pack_bf16_pairs(x):
    n = x.shape[1] // 2
    bits = lambda v: pltpu.bitcast(v.astype(BF16).astype(F32), jnp.uint32)
    return pltpu.bitcast((bits(x[:, :n]) >> 16) | (bits(x[:, n:]) & jnp.uint32(0xFFFF0000)), jnp.int32)


def _unpack_bf16_pairs(p):
    p = pltpu.bitcast(p, jnp.uint32)
    lo = pltpu.bitcast(p << 16, F32)
    hi = pltpu.bitcast(p & jnp.uint32(0xFFFF0000), F32)
    return jnp.concatenate([lo, hi], axis=1)


def _ada_kernel(c_ref, w_ref, b_ref, o_ref):
    c = c_ref[...]
    s = c * jax.nn.sigmoid(c)
    o_ref[...] = _dot(s.astype(BF16), w_ref[...].astype(BF16)) + b_ref[...]


def _ada_all(cond8, w_ada, b_ada):
    depth, d, n6 = w_ada.shape
    nb = 1536
    return pl.pallas_call(
        _ada_kernel,
        grid=(depth, n6 // nb),
        in_specs=[pl.BlockSpec((SUBLANE, d), lambda l, j: (0, 0)),
                  pl.BlockSpec((None, d, nb), lambda l, j: (l, 0, j)),
                  pl.BlockSpec((None, 1, nb), lambda l, j: (l, 0, j))],
        out_specs=pl.BlockSpec((None, SUBLANE, nb), lambda l, j: (l, 0, j)),
        out_shape=jax.ShapeDtypeStruct((depth, SUBLANE, n6), F32),
        compiler_params=_cparams(("arbitrary", "arbitrary")),
        name="ada",
    )(cond8, w_ada, b_ada.reshape(depth, 1, n6))


def _swap_halves(x, half):
    n = x.shape[-1]
    lane = lax.broadcasted_iota(jnp.int32, x.shape, x.ndim - 1)
    first = (lane % (2 * half)) < half
    return jnp.where(first, pltpu.roll(x, n - half, x.ndim - 1), pltpu.roll(x, half, x.ndim - 1))


def _rope(x, cos, sin_signed, half):
    reps = x.shape[-1] // cos.shape[-1]
    if reps > 1:
        cos = jnp.concatenate([cos] * reps, axis=-1)
        sin_signed = jnp.concatenate([sin_signed] * reps, axis=-1)
    return x * cos + _swap_halves(x, half) * sin_signed


def _in_kernel(xc_in_ref, xl_in_ref, ada_ref, ln1_ref, w_ref, gq_ref, wuq_ref, gkv_ref, wuk_ref, wuv_ref,
               ca_ref, sa_ref, cb_ref, sb_ref,
               qa_ref, ka_ref, va_ref, qb_ref, ckvn_ref, kb_ref, vb_ref, kr_ref, xc_ref, yc_ref, *, ctx_tiles):
    d = D_MODEL
    x = jnp.where(pl.program_id(0) < ctx_tiles, xc_in_ref[...], xl_in_ref[...])
    shift = ada_ref[:, 0:d]
    scale = ada_ref[:, d:2 * d]
    u = _rms(x, ln1_ref[...]) * (1.0 + scale) + shift
    p = _dot(u.astype(BF16), w_ref[...])
    ca, sa, cb, sb = ca_ref[...], sa_ref[...], cb_ref[...], sb_ref[...]

    qa = _rope(p[:, C_QA:C_KA], ca, sa, HD_A // 4) * (HD_A ** -0.5)
    qa_ref[...] = qa.astype(BF16)
    ka_ref[...] = _rope(p[:, C_KA:C_VA], ca, sa, HD_A // 4)
    va_ref[...] = p[:, C_VA:C_CQ]

    cq = _rms(p[:, C_CQ:C_CKV], gq_ref[...])
    qb = _dot(cq.astype(BF16), wuq_ref[...])
    qb = _rope(qb, cb, sb, ROPE_B // 4) * ((NOPE_B + ROPE_B) ** -0.5)
    qb_ref[...] = qb.astype(BF16)

    ckvn = _rms(p[:, C_CKV:C_XC], gkv_ref[...])
    ckvn_ref[...] = ckvn
    ckvn_b = ckvn.astype(BF16)
    kr = _rope(p[:, C_KR:C_END], cb, sb, ROPE_B // 4)
    kr_ref[...] = kr
    kb = _dot(ckvn_b, wuk_ref[...]) + jnp.concatenate([kr] * N_HEADS_B, axis=-1)
    kb_ref[...] = kb.astype(BF16)
    vb_ref[...] = _dot(ckvn_b, wuv_ref[...]).astype(BF16)

    xc_ref[...] = p[:, C_XC:C_YC]
    yc_ref[...] = p[:, C_YC:C_KR]


def _in_proj(x_ctx, x_lat, t_ctx, t, ada, ln1, w_main, g_q, w_uq, g_kv, w_uk, w_uv, tabs, seg, tab_blk, layer):
    tm = TOKEN_TILE
    row = lambda n: pl.BlockSpec((tm, n), lambda i: (i, 0))
    full = lambda a: _layer_spec(a, layer)
    tab = pl.BlockSpec((tm, LANE), lambda i: (tab_blk(i), 0))
    sds = lambda n, dt: jax.ShapeDtypeStruct((t, n), dt)
    nb = N_HEADS_B * HEAD_PAD_B
    return pl.pallas_call(
        functools.partial(_in_kernel, ctx_tiles=t_ctx // tm),
        grid=(t // tm,),
        in_specs=[*_stream_specs(x_ctx, x_lat, t_ctx, tm), _ada_spec(ada, layer, seg), full(ln1), full(w_main),
                  full(g_q), full(w_uq), full(g_kv), full(w_uk), full(w_uv), tab, tab, tab, tab],
        out_specs=[row(512), row(128), row(128), row(nb), row(KV_RANK), row(nb), row(512), row(128),
                   row(LRU_W), row(LRU_W)],
        out_shape=[sds(512, BF16), sds(128, F32), sds(128, F32), sds(nb, BF16), sds(KV_RANK, F32),
                   sds(nb, BF16), sds(512, BF16), sds(128, F32), sds(LRU_W, F32), sds(LRU_W, F32)],
        compiler_params=_cparams(("arbitrary",)),
        name="in_proj",
    )(x_ctx, x_lat, ada, ln1, w_main, g_q, w_uq, g_kv, w_uk, w_uv, *tabs)


def _gqa_heads(q, k_all, v_all, bias, sink_ref, layer, o_ref):
    lane = lax.broadcasted_iota(jnp.int32, (q.shape[0], LANE), 1)
    low = lane < HD_A
    for pair in range(N_HEADS_A // 2):
        qp = q[:, pair * LANE:(pair + 1) * LANE]
        outs = []
        for g in range(N_KV_A):
            qm = jnp.where(low if g == 0 else ~low, qp, jnp.zeros_like(qp))
            s = _dot_t(qm, k_all)
            if bias is not None:
                s = s + bias
            sink = sink_ref[layer, 2 * pair + g]
            m = jnp.maximum(jnp.max(s, axis=-1, keepdims=True), sink)
            e = jnp.exp(s - m)
            l = jnp.sum(e, axis=-1, keepdims=True) + jnp.exp(sink - m)
            outs.append(_dot(e.astype(BF16), v_all) / l)
        o_ref[:, pair * LANE:(pair + 1) * LANE] = jnp.where(low, outs[0], outs[1]).astype(o_ref.dtype)


def _gqa_ctx_kernel(sink_ref, q_ref, k_ref, v_ref, o_ref, *, layer):
    _gqa_heads(q_ref[...], k_ref[...].astype(BF16), v_ref[...].astype(BF16), None, sink_ref, layer, o_ref)


def _gqa_ctx(qa, ka, va, sink_p, layer, n_seq, s_len):
    return pl.pallas_call(
        functools.partial(_gqa_ctx_kernel, layer=layer),
        grid=(n_seq,),
        in_specs=[pl.BlockSpec(memory_space=pltpu.SMEM),
                  pl.BlockSpec((s_len, 512), lambda b: (b, 0)),
                  pl.BlockSpec((s_len, LANE), lambda b: (b, 0)),
                  pl.BlockSpec((s_len, LANE), lambda b: (b, 0))],
        out_specs=pl.BlockSpec((s_len, 512), lambda b: (b, 0)),
        out_shape=jax.ShapeDtypeStruct((n_seq * s_len, 512), BF16),
        compiler_params=_cparams(("arbitrary",)),
        name="gqa_ctx",
    )(sink_p, qa, ka, va)


def _gqa_lat_kernel(sink_ref, q_ref, kp_ref, kc_ref, kn_ref, vp_ref, vc_ref, vn_ref, kx_ref, vx_ref, o_ref, *,
                    layer):
    j = pl.program_id(1)
    last = pl.num_programs(1) - 1
    w = WINDOW
    qb = q_ref.shape[0]
    k_all = jnp.concatenate([kp_ref[...], kc_ref[...], kn_ref[...], kx_ref[...]], axis=0).astype(BF16)
    v_all = jnp.concatenate([vp_ref[...], vc_ref[...], vn_ref[...], vx_ref[...]], axis=0).astype(BF16)
    n_loc = qb + 2 * w
    qi = lax.broadcasted_iota(jnp.int32, (qb, n_loc), 0)
    col = lax.broadcasted_iota(jnp.int32, (qb, n_loc), 1)
    rel = col - w - qi
    zero = jnp.zeros((qb, n_loc), F32)
    neg = jnp.full((qb, n_loc), NEG_INF, F32)
    band = jnp.where(jnp.abs(rel) <= w, zero, neg)
    before = jnp.where(col < w, jnp.where(j > 0, zero, neg), zero)
    after = jnp.where(col >= qb + w, jnp.where(j < last, zero, neg), zero)
    bias = jnp.concatenate([band + before + after, jnp.zeros((qb, kx_ref.shape[0]), F32)], axis=1)
    _gqa_heads(q_ref[...], k_all, v_all, bias, sink_ref, layer, o_ref)


def _gqa_lat(qa, ka, va, kx, vx, sink_p, layer, n_seq, n_len, row0):
    w = WINDOW
    qb = GQA_Q_BLOCK
    r = qb // w
    nqb = n_len // qb
    nwb = n_len // w
    cur = lambda b, j: (row0 // qb + b * nqb + j, 0)
    prev = lambda b, j: (row0 // w + b * nwb + jnp.maximum(r * j - 1, 0), 0)
    nxt = lambda b, j: (row0 // w + b * nwb + jnp.minimum(r * j + r, nwb - 1), 0)
    past = kx.shape[2]
    cache = pl.BlockSpec((None, None, past, LANE), lambda b, j: (b, layer, 0, 0))
    return pl.pallas_call(
        functools.partial(_gqa_lat_kernel, layer=layer),
        grid=(n_seq, nqb),
        in_specs=[pl.BlockSpec(memory_space=pltpu.SMEM),
                  pl.BlockSpec((qb, 512), cur),
                  pl.BlockSpec((w, LANE), prev), pl.BlockSpec((qb, LANE), cur), pl.BlockSpec((w, LANE), nxt),
                  pl.BlockSpec((w, LANE), prev), pl.BlockSpec((qb, LANE), cur), pl.BlockSpec((w, LANE), nxt),
                  cache, cache],
        out_specs=pl.BlockSpec((qb, 512), lambda b, j: (b * nqb + j, 0)),
        out_shape=jax.ShapeDtypeStruct((n_seq * n_len, 512), BF16),
        compiler_params=_cparams(("arbitrary", "arbitrary")),
        name="gqa_lat",
    )(sink_p, qa, ka, ka, ka, va, va, va, kx, vx)


def _mla_kernel(*refs, n_src, chunk):
    q_ref = refs[0]
    kv_refs = refs[1:1 + 2 * n_src]
    o_ref = refs[-1]
    qb = q_ref.shape[0]
    lane = lax.broadcasted_iota(jnp.int32, (qb, LANE), 1)
    for pair in range(q_ref.shape[1] // (2 * HEAD_PAD_B)):
        outs = []
        for hh in range(2):
            hs = slice((2 * pair + hh) * HEAD_PAD_B, (2 * pair + hh + 1) * HEAD_PAD_B)
            q = q_ref[:, hs]
            m = jnp.full((qb, 1), NEG_INF, F32)
            l = jnp.zeros((qb, 1), F32)
            acc = jnp.zeros((qb, LANE), F32)
            for s_i in range(n_src):
                k_ref, v_ref = kv_refs[2 * s_i], kv_refs[2 * s_i + 1]
                nk = k_ref.shape[0]
                for c in range(nk // chunk):
                    k = k_ref[c * chunk:(c + 1) * chunk, hs]
                    v = v_ref[c * chunk:(c + 1) * chunk, pair * LANE:(pair + 1) * LANE]
                    s = _dot_t(q, k)
                    m_new = jnp.maximum(m, jnp.max(s, axis=-1, keepdims=True))
                    alpha = jnp.exp(m - m_new)
                    e = jnp.exp(s - m_new)
                    l = alpha * l + jnp.sum(e, axis=-1, keepdims=True)
                    acc = alpha * acc + _dot(e.astype(BF16), v)
                    m = m_new
            outs.append(acc / l)
        o_ref[:, pair * LANE:(pair + 1) * LANE] = jnp.where(lane < V_B, outs[0], outs[1]).astype(o_ref.dtype)


def _mla_ctx(qb, kb, vb, n_seq, s_len):
    nq = N_HEADS_B * HEAD_PAD_B
    nv = N_HEADS_B * V_B
    return pl.pallas_call(
        functools.partial(_mla_kernel, n_src=1, chunk=s_len),
        grid=(n_seq,),
        in_specs=[pl.BlockSpec((s_len, nq), lambda b: (b, 0)),
                  pl.BlockSpec((s_len, nq), lambda b: (b, 0)),
                  pl.BlockSpec((s_len, nv), lambda b: (b, 0))],
        out_specs=pl.BlockSpec((s_len, nv), lambda b: (b, 0)),
        out_shape=jax.ShapeDtypeStruct((n_seq * s_len, nv), BF16),
        compiler_params=_cparams(("arbitrary",)),
        name="mla_ctx",
    )(qb, kb, vb)


def _mla_lat(qb, kb, vb, kbx, vbx, n_seq, n_len, row0):
    npair = N_HEADS_B // 2
    qblk = min(MLA_Q_BLOCK, n_len)
    nqb = n_len // qblk
    past = kbx.shape[0] // n_seq
    qmap = lambda b, p, j: (row0 // qblk + b * nqb + j, p)
    return pl.pallas_call(
        functools.partial(_mla_kernel, n_src=2, chunk=512),
        grid=(n_seq, npair, nqb),
        in_specs=[pl.BlockSpec((qblk, 2 * HEAD_PAD_B), qmap),
                  pl.BlockSpec((n_len, 2 * HEAD_PAD_B), lambda b, p, j: (row0 // n_len + b, p)),
                  pl.BlockSpec((n_len, LANE), lambda b, p, j: (row0 // n_len + b, p)),
                  pl.BlockSpec((past, 2 * HEAD_PAD_B), lambda b, p, j: (b, p)),
                  pl.BlockSpec((past, LANE), lambda b, p, j: (b, p))],
        out_specs=pl.BlockSpec((qblk, LANE), lambda b, p, j: (b * nqb + j, p)),
        out_shape=jax.ShapeDtypeStruct((n_seq * n_len, 512), BF16),
        compiler_params=_cparams(("arbitrary", "arbitrary", "arbitrary")),
        name="mla_lat",
    )(qb, kb, vb, kbx, vbx)


def _cache_kv_kernel(ckv_ref, kr_ref, wuk_ref, wuv_ref, kb_ref, vb_ref):
    c = ckv_ref[...].astype(BF16)
    kb = _dot(c, wuk_ref[...]) + jnp.concatenate([kr_ref[...]] * N_HEADS_B, axis=-1)
    kb_ref[...] = kb.astype(BF16)
    vb_ref[...] = _dot(c, wuv_ref[...]).astype(BF16)


def _cache_kv(cache_ckv, cache_kr_pad, w_uk, w_uv, layer):
    n_seq, _, past, _ = cache_ckv.shape
    nb = N_HEADS_B * HEAD_PAD_B
    return pl.pallas_call(
        _cache_kv_kernel,
        grid=(n_seq,),
        in_specs=[pl.BlockSpec((None, None, past, KV_RANK), lambda b: (b, layer, 0, 0)),
                  pl.BlockSpec((None, None, past, LANE), lambda b: (b, layer, 0, 0)),
                  _layer_spec(w_uk, layer), _layer_spec(w_uv, layer)],
        out_specs=[pl.BlockSpec((past, nb), lambda b: (b, 0)), pl.BlockSpec((past, 512), lambda b: (b, 0))],
        out_shape=[jax.ShapeDtypeStruct((n_seq * past, nb), BF16), jax.ShapeDtypeStruct((n_seq * past, 512), BF16)],
        compiler_params=_cparams(("arbitrary",)),
        name="cache_kv",
    )(cache_ckv, cache_kr_pad, w_uk, w_uv)


def _lru_kernel(xc_ref, yc_ref, h0_ref, cw_ref, cb_ref, wa_ref, ba_ref, wx_ref, bx_ref, sp_ref,
                o_ref, st_ref, pad_ref, xcv_ref, a_ref, b_ref, *, chunk):
    n = xc_ref.shape[0]
    halo = SUBLANE
    pad_ref[0:halo, :] = jnp.zeros((halo, LRU_W), F32)
    pad_ref[halo + n:2 * halo + n, :] = jnp.zeros((halo, LRU_W), F32)
    pad_ref[halo:halo + n, :] = xc_ref[...]
    left = CONV_W // 2
    for c in range(n // chunk):
        r0 = c * chunk
        acc = jnp.broadcast_to(cb_ref[...], (chunk, LRU_W))
        for j in range(CONV_W):
            off = halo + r0 + j - left
            acc = acc + cw_ref[j:j + 1, :] * pad_ref[off:off + chunk, :]
        xcv_ref[r0:r0 + chunk, :] = acc

    row = lax.broadcasted_iota(jnp.int32, (SUBLANE, LRU_W), 0)
    for d in range(2):
        for c in range(n // chunk):
            r0 = c * chunk
            xv = xcv_ref[r0:r0 + chunk, :]
            xb = xv.astype(BF16)
            for hf in range(2):
                cs = slice(hf * LRU_HALF, (hf + 1) * LRU_HALF)
                r = _sigmoid(_dot(xb[:, cs], wa_ref[d, hf]) + ba_ref[d:d + 1, cs])
                i = _sigmoid(_dot(xb[:, cs], wx_ref[d, hf]) + bx_ref[d:d + 1, cs])
                log_a = (-LRU_C) * r * sp_ref[d:d + 1, cs]
                a = jnp.exp(log_a)
                a_ref[d, r0:r0 + chunk, cs] = a
                b_ref[d, r0:r0 + chunk, cs] = jnp.sqrt(-jnp.tanh(log_a) * (a * a + 1.0)) * (i * xv[:, cs])

    def scan_group(d, grp, h):
        rows = pl.ds(pl.multiple_of(grp * SUBLANE, SUBLANE), SUBLANE)
        a = a_ref[d, rows, :]
        b = b_ref[d, rows, :]
        for sh in (1, 2, 4):
            if d == 0:
                keep = row >= sh
                a_s = jnp.where(keep, pltpu.roll(a, sh, 0), 1.0)
                b_s = jnp.where(keep, pltpu.roll(b, sh, 0), 0.0)
            else:
                keep = row < SUBLANE - sh
                a_s = jnp.where(keep, pltpu.roll(a, SUBLANE - sh, 0), 1.0)
                b_s = jnp.where(keep, pltpu.roll(b, SUBLANE - sh, 0), 0.0)
            b = a * b_s + b
            a = a * a_s
        hrows = a * h + b
        b_ref[d, rows, :] = hrows
        return hrows[SUBLANE - 1:SUBLANE, :] if d == 0 else hrows[0:1, :]

    n_grp = n // SUBLANE

    def body(g, hs):
        return scan_group(0, g, hs[0]), scan_group(1, n_grp - 1 - g, hs[1])

    h_f, h_b = lax.fori_loop(0, n_grp, body, (h0_ref[0:1, :], h0_ref[1:2, :]))
    st_ref[0:1, :] = h_f
    st_ref[1:2, :] = h_b

    for c in range(n // chunk):
        rs = slice(c * chunk, (c + 1) * chunk)
        o_ref[rs, :] = ((b_ref[0, rs, :] + b_ref[1, rs, :]) * jax.nn.gelu(yc_ref[rs, :])).astype(o_ref.dtype)


def _lru(xc, yc, h0, h0_layer, conv_w, conv_b, wa, ba, wx, bx, sp, layer, n_seq, n_len, row0):
    blk0 = row0 // n_len
    full = lambda a: _layer_spec(a, layer)
    seq = pl.BlockSpec((n_len, LRU_W), lambda b: (blk0 + b, 0))
    return pl.pallas_call(
        functools.partial(_lru_kernel, chunk=min(n_len, 256)),
        grid=(n_seq,),
        in_specs=[seq, seq, pl.BlockSpec((None, None, 2, LRU_W), lambda b: (b, h0_layer, 0, 0)),
                  full(conv_w), full(conv_b), full(wa), full(ba), full(wx), full(bx), full(sp)],
        out_specs=[pl.BlockSpec((n_len, LRU_W), lambda b: (b, 0)),
                   pl.BlockSpec((None, 2, LRU_W), lambda b: (b, 0, 0))],
        out_shape=[jax.ShapeDtypeStruct((n_seq * n_len, LRU_W), BF16),
                   jax.ShapeDtypeStruct((n_seq, 2, LRU_W), F32)],
        scratch_shapes=[pltpu.VMEM((n_len + 2 * SUBLANE, LRU_W), F32), pltpu.VMEM((n_len, LRU_W), F32),
                        pltpu.VMEM((2, n_len, LRU_W), F32), pltpu.VMEM((2, n_len, LRU_W), F32)],
        compiler_params=_cparams(("arbitrary",)),
        name="lru",
    )(xc, yc, h0, conv_w, conv_b, wa, ba, wx, bx, sp)


def _merge_kernel(xc_in_ref, xl_in_ref, ada_ref, oac_ref, oal_ref, obc_ref, obl_ref, occ_ref, ocl_ref, ln1_ref,
                  wg_ref, wpa_ref, wpb_ref, wpc_ref, wout_ref, ln2_ref, wr_ref, br_ref,
                  x1_ref, h_ref, e4_ref, g4_ref, r4_ref, cnt_ref, carry_ref, *, ctx_tiles):
    d = D_MODEL
    tm = xc_in_ref.shape[0]
    i = pl.program_id(0)
    is_ctx = i < ctx_tiles
    x = jnp.where(is_ctx, xc_in_ref[...], xl_in_ref[...])
    oa = jnp.where(is_ctx, oac_ref[...], oal_ref[...])
    ob = jnp.where(is_ctx, obc_ref[...], obl_ref[...])
    oc = jnp.where(is_ctx, occ_ref[...], ocl_ref[...])

    @pl.when(i == 0)
    def _():
        carry_ref[...] = jnp.zeros_like(carry_ref)

    ada = ada_ref[...]
    u = _rms(x, ln1_ref[...]) * (1.0 + ada[:, d:2 * d]) + ada[:, 0:d]
    g = jax.nn.sigmoid(_dot(u.astype(BF16), wg_ref[...]))
    m = (g[:, 0:d] * _dot(oa, wpa_ref[...]) + g[:, d:2 * d] * _dot(ob, wpb_ref[...])
         + g[:, 2 * d:3 * d] * _dot(oc, wpc_ref[...]))
    x1 = x + ada[:, 2 * d:3 * d] * _dot(m.astype(BF16), wout_ref[...])
    x1_ref[...] = x1
    h = _rms(x1, ln2_ref[...]) * (1.0 + ada[:, 4 * d:5 * d]) + ada[:, 3 * d:4 * d]
    h_ref[...] = _pack_bf16_pairs(h)

    h_hi = h.astype(BF16)
    h_lo = (h - h_hi.astype(F32)).astype(BF16)
    hw = _dot(h_hi, wr_ref[...])
    logits = (hw[:, :N_EXPERTS] + hw[:, N_EXPERTS:] + _dot(h_lo, wr_ref[:, :N_EXPERTS])) + br_ref[...]
    col = lax.broadcasted_iota(jnp.int32, (tm, N_EXPERTS), 1).astype(F32)
    col4 = lax.broadcasted_iota(jnp.int32, (tm, TOP_K), 1)
    sel_any = jnp.zeros((tm, N_EXPERTS), F32)
    vals, idxs = [], []
    work = logits
    for _k in range(TOP_K):
        mx = jnp.max(work, axis=-1, keepdims=True)
        idx = jnp.min(jnp.where(work == mx, col, float(N_EXPERTS)), axis=-1, keepdims=True)
        sel = col == idx
        vals.append(mx)
        idxs.append(idx)
        sel_any = jnp.where(sel, 1.0, sel_any)
        work = jnp.where(sel, -jnp.inf, work)

    ri = lax.broadcasted_iota(jnp.int32, (tm, tm), 0)
    ci = lax.broadcasted_iota(jnp.int32, (tm, tm), 1)
    tri = jnp.where(ri > ci, 1.0, 0.0).astype(BF16)
    before = _dot(tri, sel_any.astype(BF16)) + carry_ref[...]
    carry = carry_ref[...] + jnp.sum(sel_any, axis=0, keepdims=True)
    carry_ref[...] = carry
    cnt_ref[...] = jnp.broadcast_to(carry, cnt_ref.shape)

    exps = [jnp.exp(v - vals[0]) for v in vals]
    den = exps[0] + exps[1] + exps[2] + exps[3]
    e4 = jnp.zeros((tm, TOP_K), jnp.int32)
    g4 = jnp.zeros((tm, TOP_K), F32)
    r4 = jnp.zeros((tm, TOP_K), jnp.int32)
    for k in range(TOP_K):
        rank = jnp.sum(jnp.where(col == idxs[k], before, 0.0), axis=-1, keepdims=True)
        e4 = jnp.where(col4 == k, idxs[k].astype(jnp.int32), e4)
        g4 = jnp.where(col4 == k, exps[k] / den, g4)
        r4 = jnp.where(col4 == k, rank.astype(jnp.int32), r4)
    e4_ref[...] = e4
    g4_ref[...] = g4
    r4_ref[...] = r4


def _merge(x_ctx, x_lat, t, ada, o_ctx, o_lat, ln1, w_g, w_pa, w_pb, w_pc, w_out, ln2, w_r, b_r, seg, tm, layer):
    d = x_ctx.shape[1]
    t_ctx = o_ctx[0].shape[0]
    ctx_tiles = t_ctx // tm
    row = lambda n: pl.BlockSpec((tm, n), lambda i: (i, 0))
    full = lambda a: _layer_spec(a, layer)
    ctx = pl.BlockSpec((tm, 512), lambda i: (jnp.minimum(i, ctx_tiles - 1), 0))
    lat = pl.BlockSpec((tm, 512), lambda i: (jnp.maximum(i - ctx_tiles, 0), 0))
    return pl.pallas_call(
        functools.partial(_merge_kernel, ctx_tiles=ctx_tiles),
        grid=(t // tm,),
        in_specs=[*_stream_specs(x_ctx, x_lat, t_ctx, tm), _ada_spec(ada, layer, seg), ctx, lat, ctx, lat, ctx, lat,
                  full(ln1), full(w_g), full(w_pa), full(w_pb), full(w_pc), full(w_out), full(ln2), full(w_r),
                  full(b_r)],
        out_specs=[row(d), row(d // 2), row(TOP_K), row(TOP_K), row(TOP_K),
                   pl.BlockSpec((SUBLANE, N_EXPERTS), lambda i: (0, 0))],
        out_shape=[jax.ShapeDtypeStruct((t, d), F32), jax.ShapeDtypeStruct((t, d // 2), jnp.int32),
                   jax.ShapeDtypeStruct((t, TOP_K), jnp.int32), jax.ShapeDtypeStruct((t, TOP_K), F32),
                   jax.ShapeDtypeStruct((t, TOP_K), jnp.int32),
                   jax.ShapeDtypeStruct((SUBLANE, N_EXPERTS), F32)],
        scratch_shapes=[pltpu.VMEM((1, N_EXPERTS), F32)],
        compiler_params=_cparams(("arbitrary",)),
        name="merge",
    )(x_ctx, x_lat, ada, o_ctx[0], o_lat[0], o_ctx[1], o_lat[1], o_ctx[2], o_lat[2], ln1, w_g, w_pa, w_pb, w_pc,
      w_out, ln2, w_r, b_r)


def _expert_kernel(be_ref, nu_ref, x_ref, wgu_ref, bgu_ref, wdn_ref, bdn_ref, y_ref, wgu_s, wdn_s, *, n_chunk):
    b = pl.program_id(0)
    used = b < nu_ref[0]
    new_expert = jnp.logical_or(b == 0, be_ref[b] != be_ref[jnp.maximum(b - 1, 0)])

    @pl.when(jnp.logical_and(used, new_expert))
    def _():
        wgu_s[...] = wgu_ref[...].astype(BF16)
        wdn_s[...] = wdn_ref[...].astype(BF16)

    @pl.when(used)
    def _():
        x = _unpack_bf16_pairs(x_ref[...]).astype(BF16)
        cw = D_FF // n_chunk
        acc = jnp.broadcast_to(bdn_ref[...], x.shape)
        for c in range(n_chunk):
            glu = _dot(x, wgu_s[:, c * cw:(c + 1) * cw]) + bgu_ref[:, c * cw:(c + 1) * cw]
            lin = (_dot(x, wgu_s[:, D_FF + c * cw:D_FF + (c + 1) * cw])
                   + bgu_ref[:, D_FF + c * cw:D_FF + (c + 1) * cw])
            glu = jnp.minimum(glu, SWIGLU_LIMIT)
            lin = jnp.clip(lin, -SWIGLU_LIMIT, SWIGLU_LIMIT)
            act = glu * jax.nn.sigmoid(SWIGLU_ALPHA * glu) * (lin + 1.0)
            acc = acc + _dot(act.astype(BF16), wdn_s[c * cw:(c + 1) * cw, :])
        y_ref[...] = _pack_bf16_pairs(acc)

    @pl.when(jnp.logical_not(used))
    def _():
        y_ref[...] = jnp.zeros_like(y_ref)


def _experts(xb, block_e, n_used, w_gu, b_gu, w_dn, b_dn, layer):
    n_rows, dp = xb.shape
    d = 2 * dp
    depth = w_gu.shape[0]
    blk = EXPERT_BLOCK
    n_blocks = n_rows // blk
    wmap = lambda b, be, nu: (layer, be[b], 0, 0)
    grid_spec = pltpu.PrefetchScalarGridSpec(
        num_scalar_prefetch=2,
        grid=(n_blocks,),
        in_specs=[pl.BlockSpec((blk, dp), lambda b, be, nu: (b, 0)),
                  pl.BlockSpec((None, None, d, 2 * D_FF), wmap),
                  pl.BlockSpec((None, None, 1, 2 * D_FF), wmap),
                  pl.BlockSpec((None, None, D_FF, d), wmap),
                  pl.BlockSpec((None, None, 1, d), wmap)],
        out_specs=pl.BlockSpec((blk, dp), lambda b, be, nu: (b, 0)),
        scratch_shapes=[pltpu.VMEM((d, 2 * D_FF), BF16), pltpu.VMEM((D_FF, d), BF16)],
    )
    return pl.pallas_call(
        functools.partial(_expert_kernel, n_chunk=4),
        grid_spec=grid_spec,
        out_shape=jax.ShapeDtypeStruct((n_rows, dp), jnp.int32),
        compiler_params=_cparams(("arbitrary",)),
        name="experts",
    )(block_e, n_used, xb, w_gu, b_gu.reshape(depth, N_EXPERTS, 1, 2 * D_FF), w_dn,
      b_dn.reshape(depth, N_EXPERTS, 1, d))


def _sc_worker_base(per_w):
    return (lax.axis_index("s") * SC_CORES + lax.axis_index("c")) * per_w


def _sc_gather_rows(table_hbm, out_hbm, idx_v, rows_v, gsem, osem, base, n_chunks):
    ch = SC_CHUNK

    def gather(c, slot):
        return pltpu.make_async_copy(table_hbm.at[idx_v.at[pl.ds(c * ch, ch)]], rows_v.at[slot], gsem.at[slot])

    def put(c, slot):
        return pltpu.make_async_copy(rows_v.at[slot], out_hbm.at[pl.ds(base + c * ch, ch)], osem.at[slot])

    gather(0, 0).start()

    @pl.loop(0, n_chunks // 2)
    def _(i):
        c = 2 * i
        gather(c, 0).wait()
        put(c, 0).start()

        @pl.when(i > 0)
        def _():
            put(c - 1, 1).wait()

        gather(c + 1, 1).start()
        gather(c + 1, 1).wait()
        put(c + 1, 1).start()
        put(c, 0).wait()

        @pl.when(c + 2 < n_chunks)
        def _():
            gather(c + 2, 0).start()

    put(n_chunks - 1, 1).wait()


def _sc_scratch(per_w, d, dtype):
    return [pltpu.VMEM((per_w,), jnp.int32), pltpu.VMEM((2, SC_CHUNK, d), dtype),
            pltpu.SemaphoreType.DMA((2,)), pltpu.SemaphoreType.DMA((2,))]


def _sc_gather(table, idx):
    n_idx = idx.shape[0]
    d = table.shape[1]
    n_workers = SC_CORES * SC_SUBCORES
    per_w = n_idx // n_workers
    assert n_idx % (n_workers * SC_CHUNK * 2) == 0
    mesh = plsc.VectorSubcoreMesh(core_axis_name="c", subcore_axis_name="s")

    @functools.partial(
        pl.kernel, mesh=mesh, out_type=jax.ShapeDtypeStruct((n_idx, d), table.dtype),
        scratch_types=_sc_scratch(per_w, d, table.dtype), name="sc_gather")
    def gather(table_hbm, idx_hbm, out_hbm, idx_v, rows_v, gsem, osem):
        base = _sc_worker_base(per_w)
        pltpu.sync_copy(idx_hbm.at[pl.ds(base, per_w)], idx_v)
        _sc_gather_rows(table_hbm, out_hbm, idx_v, rows_v, gsem, osem, base, per_w // SC_CHUNK)

    return gather(table, idx)


def _sc_dispatch(h, dest, n_rows):
    t, d = h.shape
    n_slots = dest.shape[0]
    n_workers = SC_CORES * SC_SUBCORES
    per_w = n_rows // n_workers
    piece = 16384
    unroll = 8
    fill_shift = 3
    assert n_rows % (n_workers * SC_CHUNK * 2) == 0 and n_slots % piece == 0 and (n_rows >> fill_shift) <= t
    mesh = plsc.VectorSubcoreMesh(core_axis_name="c", subcore_axis_name="s")

    @functools.partial(
        pl.kernel, mesh=mesh, out_type=jax.ShapeDtypeStruct((n_rows, d), h.dtype),
        scratch_types=[pltpu.VMEM((piece,), jnp.int32)] + _sc_scratch(per_w, d, h.dtype),
        compiler_params=pltpu.CompilerParams(needs_layout_passes=False), name="sc_dispatch")
    def dispatch(h_hbm, dest_hbm, out_hbm, dest_v, idx_v, rows_v, gsem, osem):
        base = _sc_worker_base(per_w)
        lane = lax.iota(jnp.int32, SC_LANES)

        @pl.loop(0, per_w // SC_LANES)
        def _(j):
            idx_v[pl.ds(j * SC_LANES, SC_LANES)] = (base + j * SC_LANES + lane) >> fill_shift

        @pl.loop(0, n_slots // piece)
        def _(p):
            pltpu.sync_copy(dest_hbm.at[pl.ds(p * piece, piece)], dest_v)

            @plsc.parallel_loop(0, piece // SC_LANES, unroll=unroll)
            def _(j):
                s0 = j * SC_LANES
                loc = dest_v[pl.ds(s0, SC_LANES)] - base
                mine = (loc >= 0) & (loc < per_w)
                tok = (p * piece + s0 + lane) // TOP_K
                plsc.store_scatter(idx_v, [loc], tok, mask=mine)

        _sc_gather_rows(h_hbm, out_hbm, idx_v, rows_v, gsem, osem, base, per_w // SC_CHUNK)

    return dispatch(h, dest)


def _combine_kernel(x1_ref, ada_ref, yg_ref, g4_ref, fg_ref, o_ref, *, final):
    d = D_MODEL
    g4 = g4_ref[...]
    y = g4[:, 0:1] * _unpack_bf16_pairs(yg_ref[0])
    for k in range(1, TOP_K):
        y = y + g4[:, k:k + 1] * _unpack_bf16_pairs(yg_ref[k])
    x2 = x1_ref[...] + ada_ref[:, 5 * d:6 * d] * y
    if final:
        x2 = _rms(x2, fg_ref[...])
    o_ref[...] = x2


def _combine(x1, ada, yg, g4, final_g, seg, layer, final):
    t, d = x1.shape
    tm = TOKEN_TILE
    return pl.pallas_call(
        functools.partial(_combine_kernel, final=final),
        grid=(t // tm,),
        in_specs=[pl.BlockSpec((tm, d), lambda i: (i, 0)),
                  _ada_spec(ada, layer, seg),
                  pl.BlockSpec((TOP_K, tm, d // 2), lambda i: (0, i, 0)),
                  pl.BlockSpec((tm, TOP_K), lambda i: (i, 0)),
                  pl.BlockSpec((1, d), lambda i: (0, 0))],
        out_specs=pl.BlockSpec((tm, d), lambda i: (i, 0)),
        out_shape=jax.ShapeDtypeStruct((t, d), F32),
        compiler_params=_cparams(("arbitrary",)),
        name="combine",
    )(x1, ada, yg, g4, final_g)


def _rope_tables(n_lat):
    rows = n_lat // GRID_W
    row_ids = jnp.repeat(jnp.arange(rows, dtype=F32), GRID_W)
    col_ids = jnp.tile(jnp.arange(GRID_W, dtype=F32), rows)

    def table(d_rot, lane0):
        d_axis = d_rot // 2
        nf = d_axis // 2
        inv_freq = ROPE_BASE ** (-jnp.arange(0, d_axis, 2, dtype=F32) / d_axis)
        ang_r = row_ids[:, None] * inv_freq
        ang_c = col_ids[:, None] * inv_freq
        ang = jnp.concatenate([ang_r, ang_r, ang_c, ang_c], axis=-1)
        sign = jnp.tile(jnp.concatenate([-jnp.ones((nf,), F32), jnp.ones((nf,), F32)]), 2)
        cos = jnp.ones((n_lat, LANE), F32)
        sin = jnp.zeros((n_lat, LANE), F32)
        for l0 in lane0:
            cos = cos.at[:, l0:l0 + d_rot].set(jnp.cos(ang))
            sin = sin.at[:, l0:l0 + d_rot].set(jnp.sin(ang) * sign)
        ident = (jnp.ones((TOKEN_TILE, LANE), F32), jnp.zeros((TOKEN_TILE, LANE), F32))
        return jnp.concatenate([ident[0], cos], axis=0), jnp.concatenate([ident[1], sin], axis=0)

    ca, sa = table(HD_A, (0, HD_A))
    cb, sb = table(ROPE_B, (NOPE_B,))
    return ca, sa, cb, sb


def _prep_weights(w_in, sink, w_uq, w_ukv, lru_wa, lru_wx, lru_lam, w_pa):
    depth, d, _ = w_in.shape
    cuts = np.cumsum((512, 128, 128, Q_RANK, KV_RANK, ROPE_B, LRU_W, LRU_W, 3 * d))
    perm = np.array(HEAD_PERM_A)
    w_qa = w_in[:, :, :cuts[0]].reshape(depth, d, N_HEADS_A, HD_A)[:, :, perm].reshape(depth, d, 512)
    w_kr = jnp.pad(w_in[:, :, cuts[4]:cuts[5]], ((0, 0), (0, 0), (NOPE_B, LANE - NOPE_B - ROPE_B)))
    w_main = jnp.concatenate([w_qa, w_in[:, :, cuts[0]:cuts[4]], w_in[:, :, cuts[5]:cuts[7]], w_kr],
                             axis=-1).astype(BF16)
    w_g = w_in[:, :, cuts[7]:].astype(BF16)
    sink_p = sink[:, perm]
    w_pa_p = w_pa.reshape(depth, N_HEADS_A, HD_A, d)[:, perm].reshape(depth, 512, d).astype(BF16)
    hb = NOPE_B + ROPE_B
    w_uq_p = jnp.pad(w_uq.reshape(depth, Q_RANK, N_HEADS_B, hb),
                     ((0, 0), (0, 0), (0, 0), (0, HEAD_PAD_B - hb))).reshape(depth, Q_RANK, -1).astype(BF16)
    w_ukv4 = w_ukv.reshape(depth, KV_RANK, N_HEADS_B, NOPE_B + V_B)
    w_uk_p = jnp.pad(w_ukv4[..., :NOPE_B], ((0, 0), (0, 0), (0, 0), (0, HEAD_PAD_B - NOPE_B))
                     ).reshape(depth, KV_RANK, -1).astype(BF16)
    w_uv = w_ukv4[..., NOPE_B:].reshape(depth, KV_RANK, -1).astype(BF16)

    def block_diag(w):
        hpb = LRU_HEADS // 2
        blk = LRU_W // LRU_HEADS
        w = w.reshape(depth, 2, 2, hpb, blk, blk)
        eye = jnp.eye(hpb, dtype=w.dtype)
        out = jnp.einsum('ldghij,hk->ldghikj', w, eye)
        return out.reshape(depth, 2, 2, hpb * blk, hpb * blk).astype(BF16)

    sp = jax.nn.softplus(-lru_lam.astype(F32))
    return w_main, w_g, sink_p, w_pa_p, w_uq_p, w_uk_p, w_uv, block_diag(lru_wa), block_diag(lru_wx), sp


def kernel(x_prompt, x_sample, cache_k_a, cache_v_a, cache_ckv, cache_krope, state_lru, c, c_ctx, ln1_g, ln2_g,
           final_g, w_ada, b_ada, w_in, sink, g_q, w_uq, g_kv, w_ukv, conv_w, conv_b, lru_wa, lru_ba, lru_wx,
           lru_bx, lru_lam, w_pa, w_pb, w_pc, w_out, w_router, b_router, w_gu, b_gu, w_dn, b_dn):
    n_ctx, s_len, d = x_prompt.shape
    n_lat, n_len, _ = x_sample.shape
    depth = w_in.shape[0]
    past = cache_k_a.shape[2]
    t_ctx = n_ctx * s_len
    t_lat = n_lat * n_len
    t = t_ctx + t_lat
    tm = TOKEN_TILE
    ctx_tiles = t_ctx // tm
    lat_tiles = n_len // tm
    merge_tile = min(MERGE_TILE, n_len)
    assert t_ctx % n_len == 0 and t_ctx % tm == 0 and n_len % tm == 0 and n_lat + 1 <= SUBLANE
    assert t_ctx % merge_tile == 0 and n_len % merge_tile == 0

    seg = lambda i: jnp.where(i < ctx_tiles, 0, 1 + (i - ctx_tiles) // lat_tiles)
    seg_m = lambda i: jnp.where(i < t_ctx // merge_tile, 0, 1 + (i - t_ctx // merge_tile) // (n_len // merge_tile))
    tab_blk = lambda i: jnp.where(i < ctx_tiles, 0, 1 + (i - ctx_tiles) % lat_tiles)

    cond8 = jnp.zeros((SUBLANE, d), F32).at[0].set(c_ctx).at[1:1 + n_lat].set(c)
    ada = _ada_all(cond8, w_ada, b_ada)
    tabs = _rope_tables(n_len)
    (w_main, w_g, sink_p, w_pa_p, w_uq_p, w_uk_p, w_uv, wa_bd, wx_bd, sp) = _prep_weights(
        w_in, sink, w_uq, w_ukv, lru_wa, lru_wx, lru_lam, w_pa)
    w_pb_b, w_pc_b, w_out_b = w_pb.astype(BF16), w_pc.astype(BF16), w_out.astype(BF16)
    w_r_hi = w_router.astype(BF16)
    w_r_cat = jnp.concatenate([w_r_hi, (w_router - w_r_hi.astype(F32)).astype(BF16)], axis=-1)
    cache_k2 = cache_k_a.reshape(n_lat, depth, past, N_KV_A * HD_A)
    cache_v2 = cache_v_a.reshape(n_lat, depth, past, N_KV_A * HD_A)
    cache_kr_pad = jnp.pad(cache_krope, ((0, 0), (0, 0), (0, 0), (NOPE_B, LANE - NOPE_B - ROPE_B)))
    h0_ctx = jnp.zeros((n_ctx, 1, 2, LRU_W), F32)
    rows3 = lambda a: a.reshape(depth, 1, -1)
    ada = ada.reshape(depth, SUBLANE, 1, 6 * d)
    ln1_3, ln2_3, g_q3, g_kv3, conv_b3, b_r3 = (rows3(a) for a in (ln1_g, ln2_g, g_q, g_kv, conv_b, b_router))

    x_ctx, x_lat = x_prompt.reshape(t_ctx, d), x_sample.reshape(t_lat, d)
    ks_a, vs_a, ckvs, krs, lrus = [], [], [], [], []
    n_rows = t * TOP_K + N_EXPERTS * EXPERT_BLOCK
    n_blocks = n_rows // EXPERT_BLOCK
    for l in range(depth):
        qa, ka, va, qb, ckvn, kb, vb, kr, xc, yc = _in_proj(
            x_ctx, x_lat, t_ctx, t, ada, ln1_3, w_main, g_q3, w_uq_p, g_kv3, w_uk_p, w_uv, tabs, seg, tab_blk, l)
        ks_a.append(ka[:t_ctx].reshape(n_ctx, s_len, N_KV_A, HD_A))
        vs_a.append(va[:t_ctx].reshape(n_ctx, s_len, N_KV_A, HD_A))
        ckvs.append(ckvn[:t_ctx].reshape(n_ctx, s_len, KV_RANK))
        krs.append(kr[:t_ctx, NOPE_B:NOPE_B + ROPE_B].reshape(n_ctx, s_len, ROPE_B))

        oa_c = _gqa_ctx(qa, ka, va, sink_p, l, n_ctx, s_len)
        oa_l = _gqa_lat(qa, ka, va, cache_k2, cache_v2, sink_p, l, n_lat, n_len, t_ctx)
        kbx, vbx = _cache_kv(cache_ckv, cache_kr_pad, w_uk_p, w_uv, l)
        ob_c = _mla_ctx(qb, kb, vb, n_ctx, s_len)
        ob_l = _mla_lat(qb, kb, vb, kbx, vbx, n_lat, n_len, t_ctx)
        lru_args = (conv_w, conv_b3, wa_bd, lru_ba, wx_bd, lru_bx, sp, l)
        oc_c, st_c = _lru(xc, yc, h0_ctx, 0, *lru_args, n_ctx, s_len, 0)
        oc_l, _ = _lru(xc, yc, state_lru, l, *lru_args, n_lat, n_len, t_ctx)
        lrus.append(st_c)
        x1, h, e4, g4, r4, cnt = _merge(x_ctx, x_lat, t, ada, (oa_c, ob_c, oc_c), (oa_l, ob_l, oc_l), ln1_3, w_g,
                                        w_pa_p, w_pb_b, w_pc_b, w_out_b, ln2_3, w_r_cat, b_r3, seg_m, merge_tile, l)

        counts = cnt[0].astype(jnp.int32)
        padded = (counts + EXPERT_BLOCK - 1) // EXPERT_BLOCK * EXPERT_BLOCK
        pad_end = jnp.cumsum(padded)
        pad_start = pad_end - padded
        dest4 = pad_start[e4] + r4
        blk_row0 = jnp.arange(n_blocks, dtype=jnp.int32) * EXPERT_BLOCK
        block_e = jnp.minimum(jnp.sum((pad_end[None, :] <= blk_row0[:, None]).astype(jnp.int32), axis=1),
                              N_EXPERTS - 1)
        n_used = (pad_end[-1:] // EXPERT_BLOCK).astype(jnp.int32)
        xb = _sc_dispatch(h, dest4.reshape(-1), n_rows)
        yb = _experts(xb, block_e, n_used, w_gu, b_gu, w_dn, b_dn, l)
        yg = _sc_gather(yb, dest4.T.reshape(-1)).reshape(TOP_K, t, d // 2)
        x = _combine(x1, ada, yg, g4, final_g.reshape(1, d), seg, l, l == depth - 1)
        x_ctx = x_lat = x

    y_prompt = x[:t_ctx].reshape(n_ctx, s_len, d)
    y_sample = x[t_ctx:].reshape(n_lat, n_len, d)
    return (y_prompt, y_sample, jnp.stack(ks_a, axis=1), jnp.stack(vs_a, axis=1), jnp.stack(ckvs, axis=1),
            jnp.stack(krs, axis=1), jnp.stack(lrus, axis=1))
```

```python
import functools

import numpy as np
import jax
import jax.numpy as jnp
from jax import lax
from jax.experimental import pallas as pl
from jax.experimental.pallas import tpu as pltpu
from jax.experimental.pallas import tpu_sc as plsc

F32 = jnp.float32
BF16 = jnp.bfloat16

D_MODEL = 1024
GRID_W = 64
EPS = 1e-6
ROPE_BASE = 10000.0
NEG_INF = -1e30
HD_A = 64
N_HEADS_A = 8
N_KV_A = 2
WINDOW = 128
N_HEADS_B = 8
NOPE_B = 64
ROPE_B = 32
V_B = 64
Q_RANK = 512
KV_RANK = 256
LRU_W = 512
LRU_HEADS = 8
CONV_W = 4
LRU_C = 8.0
N_EXPERTS = 32
TOP_K = 4
D_FF = D_MODEL
SWIGLU_LIMIT = 7.0
SWIGLU_ALPHA = 1.702

LANE = 128
SUBLANE = 8
TOKEN_TILE = 512
MLA_Q_BLOCK = 2048
GQA_Q_BLOCK = 512
MERGE_TILE = 512
EXPERT_BLOCK = 512
HEAD_PAD_B = 128
LRU_HALF = LRU_W // 2
VMEM_LIMIT = 56 * 1024 * 1024
SC_CORES = 2
SC_SUBCORES = 16
SC_LANES = 16
SC_CHUNK = 64

HEAD_PERM_A = (0, 4, 1, 5, 2, 6, 3, 7)

C_QA, C_KA, C_VA, C_CQ, C_CKV, C_XC, C_YC, C_KR, C_END = 0, 512, 640, 768, 1280, 1536, 2048, 2560, 2688


def _cparams(sem):
    return pltpu.CompilerParams(dimension_semantics=sem, vmem_limit_bytes=VMEM_LIMIT)


def _layer_spec(a, layer):
    zeros = (0,) * (a.ndim - 1)
    return pl.BlockSpec((None,) + a.shape[1:], lambda *_: (layer,) + zeros)


def _ada_spec(ada, layer, seg):
    return pl.BlockSpec((None, None, 1, ada.shape[-1]), lambda i: (layer, seg(i), 0, 0))


def _stream_specs(x_ctx, x_lat, t_ctx, tm):
    ctx_tiles = t_ctx // tm
    lat0 = 0 if x_lat is not x_ctx else ctx_tiles
    d = x_ctx.shape[1]
    return (pl.BlockSpec((tm, d), lambda i: (jnp.minimum(i, ctx_tiles - 1), 0)),
            pl.BlockSpec((tm, d), lambda i: (jnp.maximum(i - ctx_tiles, 0) + lat0, 0)))


def _rms(x, g):
    return x * lax.rsqrt(jnp.mean(x * x, axis=-1, keepdims=True) + EPS) * g


def _dot(a, b):
    return jnp.dot(a, b, preferred_element_type=F32)


def _dot_t(a, b):
    return lax.dot_general(a, b, (((1,), (1,)), ((), ())), preferred_element_type=F32)


def _sigmoid(x):
    return 0.5 * jnp.tanh(0.5 * x) + 0.5


def _pack_bf16_pairs(x):
    n = x.shape[1] // 2
    bits = lambda v: pltpu.bitcast(v.astype(BF16).astype(F32), jnp.uint32)
    return pltpu.bitcast((bits(x[:, :n]) >> 16) | (bits(x[:, n:]) & jnp.uint32(0xFFFF0000)), jnp.int32)


def _unpack_bf16_pairs(p):
    p = pltpu.bitcast(p, jnp.uint32)
    lo = pltpu.bitcast(p << 16, F32)
    hi = pltpu.bitcast(p & jnp.uint32(0xFFFF0000), F32)
    return jnp.concatenate([lo, hi], axis=1)


def _ada_kernel(c_ref, w_ref, b_ref, o_ref):
    c = c_ref[...]
    s = c * jax.nn.sigmoid(c)
    o_ref[...] = _dot(s.astype(BF16), w_ref[...].astype(BF16)) + b_ref[...]


def _ada_all(cond8, w_ada, b_ada):
    depth, d, n6 = w_ada.shape
    nb = 1536
    return pl.pallas_call(
        _ada_kernel,
        grid=(depth, n6 // nb),
        in_specs=[pl.BlockSpec((SUBLANE, d), lambda l, j: (0, 0)),
                  pl.BlockSpec((None, d, nb), lambda l, j: (l, 0, j)),
                  pl.BlockSpec((None, 1, nb), lambda l, j: (l, 0, j))],
        out_specs=pl.BlockSpec((None, SUBLANE, nb), lambda l, j: (l, 0, j)),
        out_shape=jax.ShapeDtypeStruct((depth, SUBLANE, n6), F32),
        compiler_params=_cparams(("arbitrary", "arbitrary")),
        name="ada",
    )(cond8, w_ada, b_ada.reshape(depth, 1, n6))


def _swap_halves(x, half):
    n = x.shape[-1]
    lane = lax.broadcasted_iota(jnp.int32, x.shape, x.ndim - 1)
    first = (lane % (2 * half)) < half
    return jnp.where(first, pltpu.roll(x, n - half, x.ndim - 1), pltpu.roll(x, half, x.ndim - 1))


def _rope(x, cos, sin_signed, half):
    reps = x.shape[-1] // cos.shape[-1]
    if reps > 1:
        cos = jnp.concatenate([cos] * reps, axis=-1)
        sin_signed = jnp.concatenate([sin_signed] * reps, axis=-1)
    return x * cos + _swap_halves(x, half) * sin_signed


def _in_kernel(xc_in_ref, xl_in_ref, ada_ref, ln1_ref, w_ref, gq_ref, wuq_ref, gkv_ref, wuk_ref, wuv_ref,
               ca_ref, sa_ref, cb_ref, sb_ref,
               qa_ref, ka_ref, va_ref, qb_ref, ckvn_ref, kb_ref, vb_ref, kr_ref, xc_ref, yc_ref, *, ctx_tiles):
    d = D_MODEL
    x = jnp.where(pl.program_id(0) < ctx_tiles, xc_in_ref[...], xl_in_ref[...])
    shift = ada_ref[:, 0:d]
    scale = ada_ref[:, d:2 * d]
    u = _rms(x, ln1_ref[...]) * (1.0 + scale) + shift
    p = _dot(u.astype(BF16), w_ref[...])
    ca, sa, cb, sb = ca_ref[...], sa_ref[...], cb_ref[...], sb_ref[...]

    qa = _rope(p[:, C_QA:C_KA], ca, sa, HD_A // 4) * (HD_A ** -0.5)
    qa_ref[...] = qa.astype(BF16)
    ka_ref[...] = _rope(p[:, C_KA:C_VA], ca, sa, HD_A // 4)
    va_ref[...] = p[:, C_VA:C_CQ]

    cq = _rms(p[:, C_CQ:C_CKV], gq_ref[...])
    qb = _dot(cq.astype(BF16), wuq_ref[...])
    qb = _rope(qb, cb, sb, ROPE_B // 4) * ((NOPE_B + ROPE_B) ** -0.5)
    qb_ref[...] = qb.astype(BF16)

    ckvn = _rms(p[:, C_CKV:C_XC], gkv_ref[...])
    ckvn_ref[...] = ckvn
    ckvn_b = ckvn.astype(BF16)
    kr = _rope(p[:, C_KR:C_END], cb, sb, ROPE_B // 4)
    kr_ref[...] = kr
    kb = _dot(ckvn_b, wuk_ref[...]) + jnp.concatenate([kr] * N_HEADS_B, axis=-1)
    kb_ref[...] = kb.astype(BF16)
    vb_ref[...] = _dot(ckvn_b, wuv_ref[...]).astype(BF16)

    xc_ref[...] = p[:, C_XC:C_YC]
    yc_ref[...] = p[:, C_YC:C_KR]


def _in_proj(x_ctx, x_lat, t_ctx, t, ada, ln1, w_main, g_q, w_uq, g_kv, w_uk, w_uv, tabs, seg, tab_blk, layer):
    tm = TOKEN_TILE
    row = lambda n: pl.BlockSpec((tm, n), lambda i: (i, 0))
    full = lambda a: _layer_spec(a, layer)
    tab = pl.BlockSpec((tm, LANE), lambda i: (tab_blk(i), 0))
    sds = lambda n, dt: jax.ShapeDtypeStruct((t, n), dt)
    nb = N_HEADS_B * HEAD_PAD_B
    return pl.pallas_call(
        functools.partial(_in_kernel, ctx_tiles=t_ctx // tm),
        grid=(t // tm,),
        in_specs=[*_stream_specs(x_ctx, x_lat, t_ctx, tm), _ada_spec(ada, layer, seg), full(ln1), full(w_main),
                  full(g_q), full(w_uq), full(g_kv), full(w_uk), full(w_uv), tab, tab, tab, tab],
        out_specs=[row(512), row(128), row(128), row(nb), row(KV_RANK), row(nb), row(512), row(128),
                   row(LRU_W), row(LRU_W)],
        out_shape=[sds(512, BF16), sds(128, F32), sds(128, F32), sds(nb, BF16), sds(KV_RANK, F32),
                   sds(nb, BF16), sds(512, BF16), sds(128, F32), sds(LRU_W, F32), sds(LRU_W, F32)],
        compiler_params=_cparams(("arbitrary",)),
        name="in_proj",
    )(x_ctx, x_lat, ada, ln1, w_main, g_q, w_uq, g_kv, w_uk, w_uv, *tabs)


def _gqa_heads(q, k_all, v_all, bias, sink_ref, layer, o_ref):
    lane = lax.broadcasted_iota(jnp.int32, (q.shape[0], LANE), 1)
    low = lane < HD_A
    for pair in range(N_HEADS_A // 2):
        qp = q[:, pair * LANE:(pair + 1) * LANE]
        outs = []
        for g in range(N_KV_A):
            qm = jnp.where(low if g == 0 else ~low, qp, jnp.zeros_like(qp))
            s = _dot_t(qm, k_all)
            if bias is not None:
                s = s + bias
            sink = sink_ref[layer, 2 * pair + g]
            m = jnp.maximum(jnp.max(s, axis=-1, keepdims=True), sink)
            e = jnp.exp(s - m)
            l = jnp.sum(e, axis=-1, keepdims=True) + jnp.exp(sink - m)
            outs.append(_dot(e.astype(BF16), v_all) / l)
        o_ref[:, pair * LANE:(pair + 1) * LANE] = jnp.where(low, outs[0], outs[1]).astype(o_ref.dtype)


def _gqa_ctx_kernel(sink_ref, q_ref, k_ref, v_ref, o_ref, *, layer):
    _gqa_heads(q_ref[...], k_ref[...].astype(BF16), v_ref[...].astype(BF16), None, sink_ref, layer, o_ref)


def _gqa_ctx(qa, ka, va, sink_p, layer, n_seq, s_len):
    return pl.pallas_call(
        functools.partial(_gqa_ctx_kernel, layer=layer),
        grid=(n_seq,),
        in_specs=[pl.BlockSpec(memory_space=pltpu.SMEM),
                  pl.BlockSpec((s_len, 512), lambda b: (b, 0)),
                  pl.BlockSpec((s_len, LANE), lambda b: (b, 0)),
                  pl.BlockSpec((s_len, LANE), lambda b: (b, 0))],
        out_specs=pl.BlockSpec((s_len, 512), lambda b: (b, 0)),
        out_shape=jax.ShapeDtypeStruct((n_seq * s_len, 512), BF16),
        compiler_params=_cparams(("arbitrary",)),
        name="gqa_ctx",
    )(sink_p, qa, ka, va)


def _gqa_lat_kernel(sink_ref, q_ref, kp_ref, kc_ref, kn_ref, vp_ref, vc_ref, vn_ref, kx_ref, vx_ref, o_ref, *,
                    layer):
    j = pl.program_id(1)
    last = pl.num_programs(1) - 1
    w = WINDOW
    qb = q_ref.shape[0]
    k_all = jnp.concatenate([kp_ref[...], kc_ref[...], kn_ref[...], kx_ref[...]], axis=0).astype(BF16)
    v_all = jnp.concatenate([vp_ref[...], vc_ref[...], vn_ref[...], vx_ref[...]], axis=0).astype(BF16)
    n_loc = qb + 2 * w
    qi = lax.broadcasted_iota(jnp.int32, (qb, n_loc), 0)
    col = lax.broadcasted_iota(jnp.int32, (qb, n_loc), 1)
    rel = col - w - qi
    zero = jnp.zeros((qb, n_loc), F32)
    neg = jnp.full((qb, n_loc), NEG_INF, F32)
    band = jnp.where(jnp.abs(rel) <= w, zero, neg)
    before = jnp.where(col < w, jnp.where(j > 0, zero, neg), zero)
    after = jnp.where(col >= qb + w, jnp.where(j < last, zero, neg), zero)
    bias = jnp.concatenate([band + before + after, jnp.zeros((qb, kx_ref.shape[0]), F32)], axis=1)
    _gqa_heads(q_ref[...], k_all, v_all, bias, sink_ref, layer, o_ref)


def _gqa_lat(qa, ka, va, kx, vx, sink_p, layer, n_seq, n_len, row0):
    w = WINDOW
    qb = GQA_Q_BLOCK
    r = qb // w
    nqb = n_len // qb
    nwb = n_len // w
    cur = lambda b, j: (row0 // qb + b * nqb + j, 0)
    prev = lambda b, j: (row0 // w + b * nwb + jnp.maximum(r * j - 1, 0), 0)
    nxt = lambda b, j: (row0 // w + b * nwb + jnp.minimum(r * j + r, nwb - 1), 0)
    past = kx.shape[2]
    cache = pl.BlockSpec((None, None, past, LANE), lambda b, j: (b, layer, 0, 0))
    return pl.pallas_call(
        functools.partial(_gqa_lat_kernel, layer=layer),
        grid=(n_seq, nqb),
        in_specs=[pl.BlockSpec(memory_space=pltpu.SMEM),
                  pl.BlockSpec((qb, 512), cur),
                  pl.BlockSpec((w, LANE), prev), pl.BlockSpec((qb, LANE), cur), pl.BlockSpec((w, LANE), nxt),
                  pl.BlockSpec((w, LANE), prev), pl.BlockSpec((qb, LANE), cur), pl.BlockSpec((w, LANE), nxt),
                  cache, cache],
        out_specs=pl.BlockSpec((qb, 512), lambda b, j: (b * nqb + j, 0)),
        out_shape=jax.ShapeDtypeStruct((n_seq * n_len, 512), BF16),
        compiler_params=_cparams(("arbitrary", "arbitrary")),
        name="gqa_lat",
    )(sink_p, qa, ka, ka, ka, va, va, va, kx, vx)


def _mla_kernel(*refs, n_src, chunk):
    q_ref = refs[0]
    kv_refs = refs[1:1 + 2 * n_src]
    o_ref = refs[-1]
    qb = q_ref.shape[0]
    lane = lax.broadcasted_iota(jnp.int32, (qb, LANE), 1)
    for pair in range(q_ref.shape[1] // (2 * HEAD_PAD_B)):
        outs = []
        for hh in range(2):
            hs = slice((2 * pair + hh) * HEAD_PAD_B, (2 * pair + hh + 1) * HEAD_PAD_B)
            q = q_ref[:, hs]
            m = jnp.full((qb, 1), NEG_INF, F32)
            l = jnp.zeros((qb, 1), F32)
            acc = jnp.zeros((qb, LANE), F32)
            for s_i in range(n_src):
                k_ref, v_ref = kv_refs[2 * s_i], kv_refs[2 * s_i + 1]
                nk = k_ref.shape[0]
                ck = min(chunk, nk)
                for c in range(nk // ck):
                    k = k_ref[c * ck:(c + 1) * ck, hs]
                    v = v_ref[c * ck:(c + 1) * ck, pair * LANE:(pair + 1) * LANE]
                    s = _dot_t(q, k)
                    m_new = jnp.maximum(m, jnp.max(s, axis=-1, keepdims=True))
                    alpha = jnp.exp(m - m_new)
                    e = jnp.exp(s - m_new)
                    l = alpha * l + jnp.sum(e, axis=-1, keepdims=True)
                    acc = alpha * acc + _dot(e.astype(BF16), v)
                    m = m_new
            outs.append(acc / l)
        o_ref[:, pair * LANE:(pair + 1) * LANE] = jnp.where(lane < V_B, outs[0], outs[1]).astype(o_ref.dtype)


def _mla_ctx(qb, kb, vb, n_seq, s_len):
    nq = N_HEADS_B * HEAD_PAD_B
    nv = N_HEADS_B * V_B
    return pl.pallas_call(
        functools.partial(_mla_kernel, n_src=1, chunk=s_len),
        grid=(n_seq,),
        in_specs=[pl.BlockSpec((s_len, nq), lambda b: (b, 0)),
                  pl.BlockSpec((s_len, nq), lambda b: (b, 0)),
                  pl.BlockSpec((s_len, nv), lambda b: (b, 0))],
        out_specs=pl.BlockSpec((s_len, nv), lambda b: (b, 0)),
        out_shape=jax.ShapeDtypeStruct((n_seq * s_len, nv), BF16),
        compiler_params=_cparams(("arbitrary",)),
        name="mla_ctx",
    )(qb, kb, vb)


def _mla_lat(qb, kb, vb, kbx, vbx, n_seq, n_len, row0):
    npair = N_HEADS_B // 2
    qblk = min(MLA_Q_BLOCK, n_len)
    nqb = n_len // qblk
    past = kbx.shape[0] // n_seq
    qmap = lambda b, p, j: (row0 // qblk + b * nqb + j, p)
    return pl.pallas_call(
        functools.partial(_mla_kernel, n_src=2, chunk=1024),
        grid=(n_seq, npair, nqb),
        in_specs=[pl.BlockSpec((qblk, 2 * HEAD_PAD_B), qmap),
                  pl.BlockSpec((n_len, 2 * HEAD_PAD_B), lambda b, p, j: (row0 // n_len + b, p)),
                  pl.BlockSpec((n_len, LANE), lambda b, p, j: (row0 // n_len + b, p)),
                  pl.BlockSpec((past, 2 * HEAD_PAD_B), lambda b, p, j: (b, p)),
                  pl.BlockSpec((past, LANE), lambda b, p, j: (b, p))],
        out_specs=pl.BlockSpec((qblk, LANE), lambda b, p, j: (b * nqb + j, p)),
        out_shape=jax.ShapeDtypeStruct((n_seq * n_len, 512), BF16),
        compiler_params=_cparams(("arbitrary", "arbitrary", "arbitrary")),
        name="mla_lat",
    )(qb, kb, vb, kbx, vbx)


def _cache_kv_kernel(ckv_ref, kr_ref, wuk_ref, wuv_ref, kb_ref, vb_ref):
    c = ckv_ref[...].astype(BF16)
    kb = _dot(c, wuk_ref[...]) + jnp.concatenate([kr_ref[...]] * N_HEADS_B, axis=-1)
    kb_ref[...] = kb.astype(BF16)
    vb_ref[...] = _dot(c, wuv_ref[...]).astype(BF16)


def _cache_kv(cache_ckv, cache_kr_pad, w_uk, w_uv, layer):
    n_seq, _, past, _ = cache_ckv.shape
    nb = N_HEADS_B * HEAD_PAD_B
    return pl.pallas_call(
        _cache_kv_kernel,
        grid=(n_seq,),
        in_specs=[pl.BlockSpec((None, None, past, KV_RANK), lambda b: (b, layer, 0, 0)),
                  pl.BlockSpec((None, None, past, LANE), lambda b: (b, layer, 0, 0)),
                  _layer_spec(w_uk, layer), _layer_spec(w_uv, layer)],
        out_specs=[pl.BlockSpec((past, nb), lambda b: (b, 0)), pl.BlockSpec((past, 512), lambda b: (b, 0))],
        out_shape=[jax.ShapeDtypeStruct((n_seq * past, nb), BF16), jax.ShapeDtypeStruct((n_seq * past, 512), BF16)],
        compiler_params=_cparams(("arbitrary",)),
        name="cache_kv",
    )(cache_ckv, cache_kr_pad, w_uk, w_uv)


def _lru_kernel(xc_ref, yc_ref, h0_ref, cw_ref, cb_ref, wa_ref, ba_ref, wx_ref, bx_ref, sp_ref,
                o_ref, st_ref, pad_ref, xcv_ref, a_ref, b_ref, *, chunk):
    n = xc_ref.shape[0]
    halo = SUBLANE
    pad_ref[0:halo, :] = jnp.zeros((halo, LRU_W), F32)
    pad_ref[halo + n:2 * halo + n, :] = jnp.zeros((halo, LRU_W), F32)
    pad_ref[halo:halo + n, :] = xc_ref[...]
    left = CONV_W // 2
    for c in range(n // chunk):
        r0 = c * chunk
        acc = jnp.broadcast_to(cb_ref[...], (chunk, LRU_W))
        for j in range(CONV_W):
            off = halo + r0 + j - left
            acc = acc + cw_ref[j:j + 1, :] * pad_ref[off:off + chunk, :]
        xcv_ref[r0:r0 + chunk, :] = acc

    row = lax.broadcasted_iota(jnp.int32, (SUBLANE, LRU_W), 0)
    for d in range(2):
        for c in range(n // chunk):
            r0 = c * chunk
            xv = xcv_ref[r0:r0 + chunk, :]
            xb = xv.astype(BF16)
            for hf in range(2):
                cs = slice(hf * LRU_HALF, (hf + 1) * LRU_HALF)
                r = _sigmoid(_dot(xb[:, cs], wa_ref[d, hf]) + ba_ref[d:d + 1, cs])
                i = _sigmoid(_dot(xb[:, cs], wx_ref[d, hf]) + bx_ref[d:d + 1, cs])
                log_a = (-LRU_C) * r * sp_ref[d:d + 1, cs]
                a = jnp.exp(log_a)
                a_ref[d, r0:r0 + chunk, cs] = a
                b_ref[d, r0:r0 + chunk, cs] = jnp.sqrt(-jnp.tanh(log_a) * (a * a + 1.0)) * (i * xv[:, cs])

    def scan_group(d, grp, h):
        rows = pl.ds(pl.multiple_of(grp * SUBLANE, SUBLANE), SUBLANE)
        a = a_ref[d, rows, :]
        b = b_ref[d, rows, :]
        for sh in (1, 2, 4):
            if d == 0:
                keep = row >= sh
                a_s = jnp.where(keep, pltpu.roll(a, sh, 0), 1.0)
                b_s = jnp.where(keep, pltpu.roll(b, sh, 0), 0.0)
            else:
                keep = row < SUBLANE - sh
                a_s = jnp.where(keep, pltpu.roll(a, SUBLANE - sh, 0), 1.0)
                b_s = jnp.where(keep, pltpu.roll(b, SUBLANE - sh, 0), 0.0)
            b = a * b_s + b
            a = a * a_s
        hrows = a * h + b
        b_ref[d, rows, :] = hrows
        return hrows[SUBLANE - 1:SUBLANE, :] if d == 0 else hrows[0:1, :]

    n_grp = n // SUBLANE

    def body(g, hs):
        return scan_group(0, g, hs[0]), scan_group(1, n_grp - 1 - g, hs[1])

    h_f, h_b = lax.fori_loop(0, n_grp, body, (h0_ref[0:1, :], h0_ref[1:2, :]))
    st_ref[0:1, :] = h_f
    st_ref[1:2, :] = h_b

    for c in range(n // chunk):
        rs = slice(c * chunk, (c + 1) * chunk)
        o_ref[rs, :] = ((b_ref[0, rs, :] + b_ref[1, rs, :]) * jax.nn.gelu(yc_ref[rs, :])).astype(o_ref.dtype)


def _lru(xc, yc, h0, h0_layer, conv_w, conv_b, wa, ba, wx, bx, sp, layer, n_seq, n_len, row0):
    blk0 = row0 // n_len
    full = lambda a: _layer_spec(a, layer)
    seq = pl.BlockSpec((n_len, LRU_W), lambda b: (blk0 + b, 0))
    return pl.pallas_call(
        functools.partial(_lru_kernel, chunk=min(n_len, 256)),
        grid=(n_seq,),
        in_specs=[seq, seq, pl.BlockSpec((None, None, 2, LRU_W), lambda b: (b, h0_layer, 0, 0)),
                  full(conv_w), full(conv_b), full(wa), full(ba), full(wx), full(bx), full(sp)],
        out_specs=[pl.BlockSpec((n_len, LRU_W), lambda b: (b, 0)),
                   pl.BlockSpec((None, 2, LRU_W), lambda b: (b, 0, 0))],
        out_shape=[jax.ShapeDtypeStruct((n_seq * n_len, LRU_W), BF16),
                   jax.ShapeDtypeStruct((n_seq, 2, LRU_W), F32)],
        scratch_shapes=[pltpu.VMEM((n_len + 2 * SUBLANE, LRU_W), F32), pltpu.VMEM((n_len, LRU_W), F32),
                        pltpu.VMEM((2, n_len, LRU_W), F32), pltpu.VMEM((2, n_len, LRU_W), F32)],
        compiler_params=_cparams(("arbitrary",)),
        name="lru",
    )(xc, yc, h0, conv_w, conv_b, wa, ba, wx, bx, sp)


def _merge_kernel(xc_in_ref, xl_in_ref, ada_ref, oac_ref, oal_ref, obc_ref, obl_ref, occ_ref, ocl_ref, ln1_ref,
                  wg_ref, wpa_ref, wpb_ref, wpc_ref, wout_ref, ln2_ref, wr_ref, br_ref,
                  x1_ref, h_ref, e4_ref, g4_ref, r4_ref, cnt_ref, carry_ref, *, ctx_tiles):
    d = D_MODEL
    tm = xc_in_ref.shape[0]
    i = pl.program_id(0)
    is_ctx = i < ctx_tiles
    x = jnp.where(is_ctx, xc_in_ref[...], xl_in_ref[...])
    oa = jnp.where(is_ctx, oac_ref[...], oal_ref[...])
    ob = jnp.where(is_ctx, obc_ref[...], obl_ref[...])
    oc = jnp.where(is_ctx, occ_ref[...], ocl_ref[...])

    @pl.when(i == 0)
    def _():
        carry_ref[...] = jnp.zeros_like(carry_ref)

    ada = ada_ref[...]
    u = _rms(x, ln1_ref[...]) * (1.0 + ada[:, d:2 * d]) + ada[:, 0:d]
    g = jax.nn.sigmoid(_dot(u.astype(BF16), wg_ref[...]))
    m = (g[:, 0:d] * _dot(oa, wpa_ref[...]) + g[:, d:2 * d] * _dot(ob, wpb_ref[...])
         + g[:, 2 * d:3 * d] * _dot(oc, wpc_ref[...]))
    x1 = x + ada[:, 2 * d:3 * d] * _dot(m.astype(BF16), wout_ref[...])
    x1_ref[...] = x1
    h = _rms(x1, ln2_ref[...]) * (1.0 + ada[:, 4 * d:5 * d]) + ada[:, 3 * d:4 * d]
    h_ref[...] = _pack_bf16_pairs(h)

    h_hi = h.astype(BF16)
    h_lo = (h - h_hi.astype(F32)).astype(BF16)
    hw = _dot(h_hi, wr_ref[...])
    logits = (hw[:, :N_EXPERTS] + hw[:, N_EXPERTS:] + _dot(h_lo, wr_ref[:, :N_EXPERTS])) + br_ref[...]
    col = lax.broadcasted_iota(jnp.int32, (tm, N_EXPERTS), 1).astype(F32)
    col4 = lax.broadcasted_iota(jnp.int32, (tm, TOP_K), 1)
    sel_any = jnp.zeros((tm, N_EXPERTS), F32)
    vals, idxs = [], []
    work = logits
    for _k in range(TOP_K):
        mx = jnp.max(work, axis=-1, keepdims=True)
        idx = jnp.min(jnp.where(work == mx, col, float(N_EXPERTS)), axis=-1, keepdims=True)
        sel = col == idx
        vals.append(mx)
        idxs.append(idx)
        sel_any = jnp.where(sel, 1.0, sel_any)
        work = jnp.where(sel, -jnp.inf, work)

    ri = lax.broadcasted_iota(jnp.int32, (tm, tm), 0)
    ci = lax.broadcasted_iota(jnp.int32, (tm, tm), 1)
    tri = jnp.where(ri > ci, 1.0, 0.0).astype(BF16)
    before = _dot(tri, sel_any.astype(BF16)) + carry_ref[...]
    carry = carry_ref[...] + jnp.sum(sel_any, axis=0, keepdims=True)
    carry_ref[...] = carry
    cnt_ref[...] = jnp.broadcast_to(carry, cnt_ref.shape)

    exps = [jnp.exp(v - vals[0]) for v in vals]
    den = exps[0] + exps[1] + exps[2] + exps[3]
    e4 = jnp.zeros((tm, TOP_K), jnp.int32)
    g4 = jnp.zeros((tm, TOP_K), F32)
    r4 = jnp.zeros((tm, TOP_K), jnp.int32)
    for k in range(TOP_K):
        rank = jnp.sum(jnp.where(col == idxs[k], before, 0.0), axis=-1, keepdims=True)
        e4 = jnp.where(col4 == k, idxs[k].astype(jnp.int32), e4)
        g4 = jnp.where(col4 == k, exps[k] / den, g4)
        r4 = jnp.where(col4 == k, rank.astype(jnp.int32), r4)
    e4_ref[...] = e4
    g4_ref[...] = g4
    r4_ref[...] = r4


def _merge(x_ctx, x_lat, t, ada, o_ctx, o_lat, ln1, w_g, w_pa, w_pb, w_pc, w_out, ln2, w_r, b_r, seg, tm, layer):
    d = x_ctx.shape[1]
    t_ctx = o_ctx[0].shape[0]
    ctx_tiles = t_ctx // tm
    row = lambda n: pl.BlockSpec((tm, n), lambda i: (i, 0))
    full = lambda a: _layer_spec(a, layer)
    ctx = pl.BlockSpec((tm, 512), lambda i: (jnp.minimum(i, ctx_tiles - 1), 0))
    lat = pl.BlockSpec((tm, 512), lambda i: (jnp.maximum(i - ctx_tiles, 0), 0))
    return pl.pallas_call(
        functools.partial(_merge_kernel, ctx_tiles=ctx_tiles),
        grid=(t // tm,),
        in_specs=[*_stream_specs(x_ctx, x_lat, t_ctx, tm), _ada_spec(ada, layer, seg), ctx, lat, ctx, lat, ctx, lat,
                  full(ln1), full(w_g), full(w_pa), full(w_pb), full(w_pc), full(w_out), full(ln2), full(w_r),
                  full(b_r)],
        out_specs=[row(d), row(d // 2), row(TOP_K), row(TOP_K), row(TOP_K),
                   pl.BlockSpec((SUBLANE, N_EXPERTS), lambda i: (0, 0))],
        out_shape=[jax.ShapeDtypeStruct((t, d), F32), jax.ShapeDtypeStruct((t, d // 2), jnp.int32),
                   jax.ShapeDtypeStruct((t, TOP_K), jnp.int32), jax.ShapeDtypeStruct((t, TOP_K), F32),
                   jax.ShapeDtypeStruct((t, TOP_K), jnp.int32),
                   jax.ShapeDtypeStruct((SUBLANE, N_EXPERTS), F32)],
        scratch_shapes=[pltpu.VMEM((1, N_EXPERTS), F32)],
        compiler_params=_cparams(("arbitrary",)),
        name="merge",
    )(x_ctx, x_lat, ada, o_ctx[0], o_lat[0], o_ctx[1], o_lat[1], o_ctx[2], o_lat[2], ln1, w_g, w_pa, w_pb, w_pc,
      w_out, ln2, w_r, b_r)


def _expert_kernel(be_ref, nu_ref, x_ref, wgu_ref, bgu_ref, wdn_ref, bdn_ref, y_ref, wgu_s, wdn_s, *, n_chunk):
    b = pl.program_id(0)
    used = b < nu_ref[0]
    new_expert = jnp.logical_or(b == 0, be_ref[b] != be_ref[jnp.maximum(b - 1, 0)])

    @pl.when(jnp.logical_and(used, new_expert))
    def _():
        wgu_s[...] = wgu_ref[...].astype(BF16)
        wdn_s[...] = wdn_ref[...].astype(BF16)

    @pl.when(used)
    def _():
        x = _unpack_bf16_pairs(x_ref[...]).astype(BF16)
        cw = D_FF // n_chunk
        acc = jnp.broadcast_to(bdn_ref[...], x.shape)
        for c in range(n_chunk):
            glu = _dot(x, wgu_s[:, c * cw:(c + 1) * cw]) + bgu_ref[:, c * cw:(c + 1) * cw]
            lin = (_dot(x, wgu_s[:, D_FF + c * cw:D_FF + (c + 1) * cw])
                   + bgu_ref[:, D_FF + c * cw:D_FF + (c + 1) * cw])
            glu = jnp.minimum(glu, SWIGLU_LIMIT)
            lin = jnp.clip(lin, -SWIGLU_LIMIT, SWIGLU_LIMIT)
            act = glu * jax.nn.sigmoid(SWIGLU_ALPHA * glu) * (lin + 1.0)
            acc = acc + _dot(act.astype(BF16), wdn_s[c * cw:(c + 1) * cw, :])
        y_ref[...] = _pack_bf16_pairs(acc)

    @pl.when(jnp.logical_not(used))
    def _():
        y_ref[...] = jnp.zeros_like(y_ref)


def _experts(xb, block_e, n_used, w_gu, b_gu, w_dn, b_dn, layer):
    n_rows, dp = xb.shape
    d = 2 * dp
    depth = w_gu.shape[0]
    blk = EXPERT_BLOCK
    n_blocks = n_rows // blk
    wmap = lambda b, be, nu: (layer, be[b], 0, 0)
    grid_spec = pltpu.PrefetchScalarGridSpec(
        num_scalar_prefetch=2,
        grid=(n_blocks,),
        in_specs=[pl.BlockSpec((blk, dp), lambda b, be, nu: (b, 0)),
                  pl.BlockSpec((None, None, d, 2 * D_FF), wmap),
                  pl.BlockSpec((None, None, 1, 2 * D_FF), wmap),
                  pl.BlockSpec((None, None, D_FF, d), wmap),
                  pl.BlockSpec((None, None, 1, d), wmap)],
        out_specs=pl.BlockSpec((blk, dp), lambda b, be, nu: (b, 0)),
        scratch_shapes=[pltpu.VMEM((d, 2 * D_FF), BF16), pltpu.VMEM((D_FF, d), BF16)],
    )
    return pl.pallas_call(
        functools.partial(_expert_kernel, n_chunk=4),
        grid_spec=grid_spec,
        out_shape=jax.ShapeDtypeStruct((n_rows, dp), jnp.int32),
        compiler_params=_cparams(("arbitrary",)),
        name="experts",
    )(block_e, n_used, xb, w_gu, b_gu.reshape(depth, N_EXPERTS, 1, 2 * D_FF), w_dn,
      b_dn.reshape(depth, N_EXPERTS, 1, d))


def _sc_worker_base(per_w):
    return (lax.axis_index("s") * SC_CORES + lax.axis_index("c")) * per_w


def _sc_gather_rows(table_hbm, out_hbm, idx_v, rows_v, gsem, osem, base, n_chunks):
    ch = SC_CHUNK

    def gather(c, slot):
        return pltpu.make_async_copy(table_hbm.at[idx_v.at[pl.ds(c * ch, ch)]], rows_v.at[slot], gsem.at[slot])

    def put(c, slot):
        return pltpu.make_async_copy(rows_v.at[slot], out_hbm.at[pl.ds(base + c * ch, ch)], osem.at[slot])

    gather(0, 0).start()

    @pl.loop(0, n_chunks // 2)
    def _(i):
        c = 2 * i
        gather(c, 0).wait()
        put(c, 0).start()

        @pl.when(i > 0)
        def _():
            put(c - 1, 1).wait()

        gather(c + 1, 1).start()
        gather(c + 1, 1).wait()
        put(c + 1, 1).start()
        put(c, 0).wait()

        @pl.when(c + 2 < n_chunks)
        def _():
            gather(c + 2, 0).start()

    put(n_chunks - 1, 1).wait()


def _sc_scratch(per_w, d, dtype):
    return [pltpu.VMEM((per_w,), jnp.int32), pltpu.VMEM((2, SC_CHUNK, d), dtype),
            pltpu.SemaphoreType.DMA((2,)), pltpu.SemaphoreType.DMA((2,))]


def _sc_gather(table, idx):
    n_idx = idx.shape[0]
    d = table.shape[1]
    n_workers = SC_CORES * SC_SUBCORES
    per_w = n_idx // n_workers
    assert n_idx % (n_workers * SC_CHUNK * 2) == 0
    mesh = plsc.VectorSubcoreMesh(core_axis_name="c", subcore_axis_name="s")

    @functools.partial(
        pl.kernel, mesh=mesh, out_type=jax.ShapeDtypeStruct((n_idx, d), table.dtype),
        scratch_types=_sc_scratch(per_w, d, table.dtype), name="sc_gather")
    def gather(table_hbm, idx_hbm, out_hbm, idx_v, rows_v, gsem, osem):
        base = _sc_worker_base(per_w)
        pltpu.sync_copy(idx_hbm.at[pl.ds(base, per_w)], idx_v)
        _sc_gather_rows(table_hbm, out_hbm, idx_v, rows_v, gsem, osem, base, per_w // SC_CHUNK)

    return gather(table, idx)


def _sc_dispatch(h, dest, n_rows):
    t, d = h.shape
    n_slots = dest.shape[0]
    n_workers = SC_CORES * SC_SUBCORES
    per_w = n_rows // n_workers
    piece = 16384
    unroll = 8
    fill_shift = 3
    assert n_rows % (n_workers * SC_CHUNK * 2) == 0 and n_slots % piece == 0 and (n_rows >> fill_shift) <= t
    mesh = plsc.VectorSubcoreMesh(core_axis_name="c", subcore_axis_name="s")

    @functools.partial(
        pl.kernel, mesh=mesh, out_type=jax.ShapeDtypeStruct((n_rows, d), h.dtype),
        scratch_types=[pltpu.VMEM((piece,), jnp.int32)] + _sc_scratch(per_w, d, h.dtype),
        compiler_params=pltpu.CompilerParams(needs_layout_passes=False), name="sc_dispatch")
    def dispatch(h_hbm, dest_hbm, out_hbm, dest_v, idx_v, rows_v, gsem, osem):
        base = _sc_worker_base(per_w)
        lane = lax.iota(jnp.int32, SC_LANES)

        @pl.loop(0, per_w // SC_LANES)
        def _(j):
            idx_v[pl.ds(j * SC_LANES, SC_LANES)] = (base + j * SC_LANES + lane) >> fill_shift

        @pl.loop(0, n_slots // piece)
        def _(p):
            pltpu.sync_copy(dest_hbm.at[pl.ds(p * piece, piece)], dest_v)

            @plsc.parallel_loop(0, piece // SC_LANES, unroll=unroll)
            def _(j):
                s0 = j * SC_LANES
                loc = dest_v[pl.ds(s0, SC_LANES)] - base
                mine = (loc >= 0) & (loc < per_w)
                tok = (p * piece + s0 + lane) // TOP_K
                plsc.store_scatter(idx_v, [loc], tok, mask=mine)

        _sc_gather_rows(h_hbm, out_hbm, idx_v, rows_v, gsem, osem, base, per_w // SC_CHUNK)

    return dispatch(h, dest)


def _combine_kernel(x1_ref, ada_ref, yg_ref, g4_ref, fg_ref, o_ref, *, final):
    d = D_MODEL
    g4 = g4_ref[...]
    y = g4[:, 0:1] * _unpack_bf16_pairs(yg_ref[0])
    for k in range(1, TOP_K):
        y = y + g4[:, k:k + 1] * _unpack_bf16_pairs(yg_ref[k])
    x2 = x1_ref[...] + ada_ref[:, 5 * d:6 * d] * y
    if final:
        x2 = _rms(x2, fg_ref[...])
    o_ref[...] = x2


def _combine(x1, ada, yg, g4, final_g, seg, layer, final):
    t, d = x1.shape
    tm = TOKEN_TILE
    return pl.pallas_call(
        functools.partial(_combine_kernel, final=final),
        grid=(t // tm,),
        in_specs=[pl.BlockSpec((tm, d), lambda i: (i, 0)),
                  _ada_spec(ada, layer, seg),
                  pl.BlockSpec((TOP_K, tm, d // 2), lambda i: (0, i, 0)),
                  pl.BlockSpec((tm, TOP_K), lambda i: (i, 0)),
                  pl.BlockSpec((1, d), lambda i: (0, 0))],
        out_specs=pl.BlockSpec((tm, d), lambda i: (i, 0)),
        out_shape=jax.ShapeDtypeStruct((t, d), F32),
        compiler_params=_cparams(("arbitrary",)),
        name="combine",
    )(x1, ada, yg, g4, final_g)


def _rope_tables(n_lat):
    rows = n_lat // GRID_W
    row_ids = jnp.repeat(jnp.arange(rows, dtype=F32), GRID_W)
    col_ids = jnp.tile(jnp.arange(GRID_W, dtype=F32), rows)

    def table(d_rot, lane0):
        d_axis = d_rot // 2
        nf = d_axis // 2
        inv_freq = ROPE_BASE ** (-jnp.arange(0, d_axis, 2, dtype=F32) / d_axis)
        ang_r = row_ids[:, None] * inv_freq
        ang_c = col_ids[:, None] * inv_freq
        ang = jnp.concatenate([ang_r, ang_r, ang_c, ang_c], axis=-1)
        sign = jnp.tile(jnp.concatenate([-jnp.ones((nf,), F32), jnp.ones((nf,), F32)]), 2)
        cos = jnp.ones((n_lat, LANE), F32)
        sin = jnp.zeros((n_lat, LANE), F32)
        for l0 in lane0:
            cos = cos.at[:, l0:l0 + d_rot].set(jnp.cos(ang))
            sin = sin.at[:, l0:l0 + d_rot].set(jnp.sin(ang) * sign)
        ident = (jnp.ones((TOKEN_TILE, LANE), F32), jnp.zeros((TOKEN_TILE, LANE), F32))
        return jnp.concatenate([ident[0], cos], axis=0), jnp.concatenate([ident[1], sin], axis=0)

    ca, sa = table(HD_A, (0, HD_A))
    cb, sb = table(ROPE_B, (NOPE_B,))
    return ca, sa, cb, sb


def _prep_weights(w_in, sink, w_uq, w_ukv, lru_wa, lru_wx, lru_lam, w_pa):
    depth, d, _ = w_in.shape
    cuts = np.cumsum((512, 128, 128, Q_RANK, KV_RANK, ROPE_B, LRU_W, LRU_W, 3 * d))
    perm = np.array(HEAD_PERM_A)
    w_qa = w_in[:, :, :cuts[0]].reshape(depth, d, N_HEADS_A, HD_A)[:, :, perm].reshape(depth, d, 512)
    w_kr = jnp.pad(w_in[:, :, cuts[4]:cuts[5]], ((0, 0), (0, 0), (NOPE_B, LANE - NOPE_B - ROPE_B)))
    w_main = jnp.concatenate([w_qa, w_in[:, :, cuts[0]:cuts[4]], w_in[:, :, cuts[5]:cuts[7]], w_kr],
                             axis=-1).astype(BF16)
    w_g = w_in[:, :, cuts[7]:].astype(BF16)
    sink_p = sink[:, perm]
    w_pa_p = w_pa.reshape(depth, N_HEADS_A, HD_A, d)[:, perm].reshape(depth, 512, d).astype(BF16)
    hb = NOPE_B + ROPE_B
    w_uq_p = jnp.pad(w_uq.reshape(depth, Q_RANK, N_HEADS_B, hb),
                     ((0, 0), (0, 0), (0, 0), (0, HEAD_PAD_B - hb))).reshape(depth, Q_RANK, -1).astype(BF16)
    w_ukv4 = w_ukv.reshape(depth, KV_RANK, N_HEADS_B, NOPE_B + V_B)
    w_uk_p = jnp.pad(w_ukv4[..., :NOPE_B], ((0, 0), (0, 0), (0, 0), (0, HEAD_PAD_B - NOPE_B))
                     ).reshape(depth, KV_RANK, -1).astype(BF16)
    w_uv = w_ukv4[..., NOPE_B:].reshape(depth, KV_RANK, -1).astype(BF16)

    def block_diag(w):
        hpb = LRU_HEADS // 2
        blk = LRU_W // LRU_HEADS
        w = w.reshape(depth, 2, 2, hpb, blk, blk)
        eye = jnp.eye(hpb, dtype=w.dtype)
        out = jnp.einsum('ldghij,hk->ldghikj', w, eye)
        return out.reshape(depth, 2, 2, hpb * blk, hpb * blk).astype(BF16)

    sp = jax.nn.softplus(-lru_lam.astype(F32))
    return w_main, w_g, sink_p, w_pa_p, w_uq_p, w_uk_p, w_uv, block_diag(lru_wa), block_diag(lru_wx), sp


def kernel(x_prompt, x_sample, cache_k_a, cache_v_a, cache_ckv, cache_krope, state_lru, c, c_ctx, ln1_g, ln2_g,
           final_g, w_ada, b_ada, w_in, sink, g_q, w_uq, g_kv, w_ukv, conv_w, conv_b, lru_wa, lru_ba, lru_wx,
           lru_bx, lru_lam, w_pa, w_pb, w_pc, w_out, w_router, b_router, w_gu, b_gu, w_dn, b_dn):
    n_ctx, s_len, d = x_prompt.shape
    n_lat, n_len, _ = x_sample.shape
    depth = w_in.shape[0]
    past = cache_k_a.shape[2]
    t_ctx = n_ctx * s_len
    t_lat = n_lat * n_len
    t = t_ctx + t_lat
    tm = TOKEN_TILE
    ctx_tiles = t_ctx // tm
    lat_tiles = n_len // tm
    merge_tile = min(MERGE_TILE, n_len)
    assert t_ctx % n_len == 0 and t_ctx % tm == 0 and n_len % tm == 0 and n_lat + 1 <= SUBLANE
    assert t_ctx % merge_tile == 0 and n_len % merge_tile == 0

    seg = lambda i: jnp.where(i < ctx_tiles, 0, 1 + (i - ctx_tiles) // lat_tiles)
    seg_m = lambda i: jnp.where(i < t_ctx // merge_tile, 0, 1 + (i - t_ctx // merge_tile) // (n_len // merge_tile))
    tab_blk = lambda i: jnp.where(i < ctx_tiles, 0, 1 + (i - ctx_tiles) % lat_tiles)

    cond8 = jnp.zeros((SUBLANE, d), F32).at[0].set(c_ctx).at[1:1 + n_lat].set(c)
    ada = _ada_all(cond8, w_ada, b_ada)
    tabs = _rope_tables(n_len)
    (w_main, w_g, sink_p, w_pa_p, w_uq_p, w_uk_p, w_uv, wa_bd, wx_bd, sp) = _prep_weights(
        w_in, sink, w_uq, w_ukv, lru_wa, lru_wx, lru_lam, w_pa)
    w_pb_b, w_pc_b, w_out_b = w_pb.astype(BF16), w_pc.astype(BF16), w_out.astype(BF16)
    w_r_hi = w_router.astype(BF16)
    w_r_cat = jnp.concatenate([w_r_hi, (w_router - w_r_hi.astype(F32)).astype(BF16)], axis=-1)
    cache_k2 = cache_k_a.reshape(n_lat, depth, past, N_KV_A * HD_A)
    cache_v2 = cache_v_a.reshape(n_lat, depth, past, N_KV_A * HD_A)
    cache_kr_pad = jnp.pad(cache_krope, ((0, 0), (0, 0), (0, 0), (NOPE_B, LANE - NOPE_B - ROPE_B)))
    h0_ctx = jnp.zeros((n_ctx, 1, 2, LRU_W), F32)
    rows3 = lambda a: a.reshape(depth, 1, -1)
    ada = ada.reshape(depth, SUBLANE, 1, 6 * d)
    ln1_3, ln2_3, g_q3, g_kv3, conv_b3, b_r3 = (rows3(a) for a in (ln1_g, ln2_g, g_q, g_kv, conv_b, b_router))

    x_ctx, x_lat = x_prompt.reshape(t_ctx, d), x_sample.reshape(t_lat, d)
    ks_a, vs_a, ckvs, krs, lrus = [], [], [], [], []
    n_rows = t * TOP_K + N_EXPERTS * EXPERT_BLOCK
    n_blocks = n_rows // EXPERT_BLOCK
    for l in range(depth):
        qa, ka, va, qb, ckvn, kb, vb, kr, xc, yc = _in_proj(
            x_ctx, x_lat, t_ctx, t, ada, ln1_3, w_main, g_q3, w_uq_p, g_kv3, w_uk_p, w_uv, tabs, seg, tab_blk, l)
        ks_a.append(ka[:t_ctx].reshape(n_ctx, s_len, N_KV_A, HD_A))
        vs_a.append(va[:t_ctx].reshape(n_ctx, s_len, N_KV_A, HD_A))
        ckvs.append(ckvn[:t_ctx].reshape(n_ctx, s_len, KV_RANK))
        krs.append(kr[:t_ctx, NOPE_B:NOPE_B + ROPE_B].reshape(n_ctx, s_len, ROPE_B))

        oa_c = _gqa_ctx(qa, ka, va, sink_p, l, n_ctx, s_len)
        oa_l = _gqa_lat(qa, ka, va, cache_k2, cache_v2, sink_p, l, n_lat, n_len, t_ctx)
        kbx, vbx = _cache_kv(cache_ckv, cache_kr_pad, w_uk_p, w_uv, l)
        ob_c = _mla_ctx(qb, kb, vb, n_ctx, s_len)
        ob_l = _mla_lat(qb, kb, vb, kbx, vbx, n_lat, n_len, t_ctx)
        lru_args = (conv_w, conv_b3, wa_bd, lru_ba, wx_bd, lru_bx, sp, l)
        oc_c, st_c = _lru(xc, yc, h0_ctx, 0, *lru_args, n_ctx, s_len, 0)
        oc_l, _ = _lru(xc, yc, state_lru, l, *lru_args, n_lat, n_len, t_ctx)
        lrus.append(st_c)
        x1, h, e4, g4, r4, cnt = _merge(x_ctx, x_lat, t, ada, (oa_c, ob_c, oc_c), (oa_l, ob_l, oc_l), ln1_3, w_g,
                                        w_pa_p, w_pb_b, w_pc_b, w_out_b, ln2_3, w_r_cat, b_r3, seg_m, merge_tile, l)

        counts = cnt[0].astype(jnp.int32)
        padded = (counts + EXPERT_BLOCK - 1) // EXPERT_BLOCK * EXPERT_BLOCK
        pad_end = jnp.cumsum(padded)
        pad_start = pad_end - padded
        dest4 = pad_start[e4] + r4
        blk_row0 = jnp.arange(n_blocks, dtype=jnp.int32) * EXPERT_BLOCK
        block_e = jnp.minimum(jnp.sum((pad_end[None, :] <= blk_row0[:, None]).astype(jnp.int32), axis=1),
                              N_EXPERTS - 1)
        n_used = (pad_end[-1:] // EXPERT_BLOCK).astype(jnp.int32)
        xb = _sc_dispatch(h, dest4.reshape(-1), n_rows)
        yb = _experts(xb, block_e, n_used, w_gu, b_gu, w_dn, b_dn, l)
        yg = _sc_gather(yb, dest4.T.reshape(-1)).reshape(TOP_K, t, d // 2)
        x = _combine(x1, ada, yg, g4, final_g.reshape(1, d), seg, l, l == depth - 1)
        x_ctx = x_lat = x

    y_prompt = x[:t_ctx].reshape(n_ctx, s_len, d)
    y_sample = x[t_ctx:].reshape(n_lat, n_len, d)
    return (y_prompt, y_sample, jnp.stack(ks_a, axis=1), jnp.stack(vs_a, axis=1), jnp.stack(ckvs, axis=1),
            jnp.stack(krs, axis=1), jnp.stack(lrus, axis=1))
```

```python
import functools

import numpy as np
import jax
import jax.numpy as jnp
from jax import lax
from jax.experimental import pallas as pl
from jax.experimental.pallas import tpu as pltpu
from jax.experimental.pallas import tpu_sc as plsc

F32 = jnp.float32
BF16 = jnp.bfloat16

D_MODEL = 1024
GRID_W = 64
EPS = 1e-6
ROPE_BASE = 10000.0
NEG_INF = -1e30
HD_A = 64
N_HEADS_A = 8
N_KV_A = 2
WINDOW = 128
N_HEADS_B = 8
NOPE_B = 64
ROPE_B = 32
V_B = 64
Q_RANK = 512
KV_RANK = 256
LRU_W = 512
LRU_HEADS = 8
CONV_W = 4
LRU_C = 8.0
N_EXPERTS = 32
TOP_K = 4
D_FF = D_MODEL
SWIGLU_LIMIT = 7.0
SWIGLU_ALPHA = 1.702

LANE = 128
SUBLANE = 8
TOKEN_TILE = 512
MLA_Q_BLOCK = 2048
GQA_Q_BLOCK = 512
MERGE_TILE = 512
COMBINE_TILE = 1024
EXPERT_BLOCK = 512
HEAD_PAD_B = 128
LRU_HALF = LRU_W // 2
VMEM_LIMIT = 56 * 1024 * 1024
SC_CORES = 2
SC_SUBCORES = 16
SC_LANES = 16
SC_CHUNK = 64

HEAD_PERM_A = (0, 4, 1, 5, 2, 6, 3, 7)

C_QA, C_KA, C_VA, C_CQ, C_CKV, C_XC, C_YC, C_KR, C_END = 0, 512, 640, 768, 1280, 1536, 2048, 2560, 2688


def _cparams(sem):
    return pltpu.CompilerParams(dimension_semantics=sem, vmem_limit_bytes=VMEM_LIMIT)


def _layer_spec(a, layer):
    zeros = (0,) * (a.ndim - 1)
    return pl.BlockSpec((None,) + a.shape[1:], lambda *_: (layer,) + zeros)


def _ada_spec(ada, layer, seg):
    return pl.BlockSpec((None, None, 1, ada.shape[-1]), lambda i: (layer, seg(i), 0, 0))


def _stream_specs(x_ctx, x_lat, t_ctx, tm):
    ctx_tiles = t_ctx // tm
    lat0 = 0 if x_lat is not x_ctx else ctx_tiles
    d = x_ctx.shape[1]
    return (pl.BlockSpec((tm, d), lambda i: (jnp.minimum(i, ctx_tiles - 1), 0)),
            pl.BlockSpec((tm, d), lambda i: (jnp.maximum(i - ctx_tiles, 0) + lat0, 0)))


def _rms(x, g):
    return x * lax.rsqrt(jnp.mean(x * x, axis=-1, keepdims=True) + EPS) * g


def _dot(a, b):
    return jnp.dot(a, b, preferred_element_type=F32)


def _dot_t(a, b):
    return lax.dot_general(a, b, (((1,), (1,)), ((), ())), preferred_element_type=F32)


def _sigmoid(x):
    return 0.5 * jnp.tanh(0.5 * x) + 0.5


def _pack_bf16_pairs(x):
    n = x.shape[1] // 2
    bits = lambda v: pltpu.bitcast(v.astype(BF16).astype(F32), jnp.uint32)
    return pltpu.bitcast((bits(x[:, :n]) >> 16) | (bits(x[:, n:]) & jnp.uint32(0xFFFF0000)), jnp.int32)


def _unpack_bf16_pairs(p):
    p = pltpu.bitcast(p, jnp.uint32)
    lo = pltpu.bitcast(p << 16, F32)
    hi = pltpu.bitcast(p & jnp.uint32(0xFFFF0000), F32)
    return jnp.concatenate([lo, hi], axis=1)


def _ada_kernel(c_ref, w_ref, b_ref, o_ref):
    c = c_ref[...]
    s = c * jax.nn.sigmoid(c)
    o_ref[...] = _dot(s.astype(BF16), w_ref[...].astype(BF16)) + b_ref[...]


def _ada_all(cond8, w_ada, b_ada):
    depth, d, n6 = w_ada.shape
    nb = 1536
    return pl.pallas_call(
        _ada_kernel,
        grid=(depth, n6 // nb),
        in_specs=[pl.BlockSpec((SUBLANE, d), lambda l, j: (0, 0)),
                  pl.BlockSpec((None, d, nb), lambda l, j: (l, 0, j)),
                  pl.BlockSpec((None, 1, nb), lambda l, j: (l, 0, j))],
        out_specs=pl.BlockSpec((None, SUBLANE, nb), lambda l, j: (l, 0, j)),
        out_shape=jax.ShapeDtypeStruct((depth, SUBLANE, n6), F32),
        compiler_params=_cparams(("arbitrary", "arbitrary")),
        name="ada",
    )(cond8, w_ada, b_ada.reshape(depth, 1, n6))


def _swap_halves(x, half):
    n = x.shape[-1]
    lane = lax.broadcasted_iota(jnp.int32, x.shape, x.ndim - 1)
    first = (lane % (2 * half)) < half
    return jnp.where(first, pltpu.roll(x, n - half, x.ndim - 1), pltpu.roll(x, half, x.ndim - 1))


def _rope(x, cos, sin_signed, half):
    reps = x.shape[-1] // cos.shape[-1]
    if reps > 1:
        cos = jnp.concatenate([cos] * reps, axis=-1)
        sin_signed = jnp.concatenate([sin_signed] * reps, axis=-1)
    return x * cos + _swap_halves(x, half) * sin_signed


def _in_kernel(xc_in_ref, xl_in_ref, ada_ref, ln1_ref, w_ref, gq_ref, wuq_ref, gkv_ref, wuk_ref, wuv_ref,
               ca_ref, sa_ref, cb_ref, sb_ref,
               qa_ref, ka_ref, va_ref, qb_ref, ckvn_ref, kb_ref, vb_ref, kr_ref, xc_ref, yc_ref, *, ctx_tiles):
    d = D_MODEL
    x = jnp.where(pl.program_id(0) < ctx_tiles, xc_in_ref[...], xl_in_ref[...])
    shift = ada_ref[:, 0:d]
    scale = ada_ref[:, d:2 * d]
    u = _rms(x, ln1_ref[...]) * (1.0 + scale) + shift
    p = _dot(u.astype(BF16), w_ref[...])
    ca, sa, cb, sb = ca_ref[...], sa_ref[...], cb_ref[...], sb_ref[...]

    qa = _rope(p[:, C_QA:C_KA], ca, sa, HD_A // 4) * (HD_A ** -0.5)
    qa_ref[...] = qa.astype(BF16)
    ka_ref[...] = _rope(p[:, C_KA:C_VA], ca, sa, HD_A // 4)
    va_ref[...] = p[:, C_VA:C_CQ]

    cq = _rms(p[:, C_CQ:C_CKV], gq_ref[...])
    qb = _dot(cq.astype(BF16), wuq_ref[...])
    qb = _rope(qb, cb, sb, ROPE_B // 4) * ((NOPE_B + ROPE_B) ** -0.5)
    qb_ref[...] = qb.astype(BF16)

    ckvn = _rms(p[:, C_CKV:C_XC], gkv_ref[...])
    ckvn_ref[...] = ckvn
    ckvn_b = ckvn.astype(BF16)
    kr = _rope(p[:, C_KR:C_END], cb, sb, ROPE_B // 4)
    kr_ref[...] = kr
    kb = _dot(ckvn_b, wuk_ref[...]) + jnp.concatenate([kr] * N_HEADS_B, axis=-1)
    kb_ref[...] = kb.astype(BF16)
    vb_ref[...] = _dot(ckvn_b, wuv_ref[...]).astype(BF16)

    xc_ref[...] = p[:, C_XC:C_YC]
    yc_ref[...] = p[:, C_YC:C_KR]


def _in_proj(x_ctx, x_lat, t_ctx, t, ada, ln1, w_main, g_q, w_uq, g_kv, w_uk, w_uv, tabs, seg, tab_blk, layer):
    tm = TOKEN_TILE
    row = lambda n: pl.BlockSpec((tm, n), lambda i: (i, 0))
    full = lambda a: _layer_spec(a, layer)
    tab = pl.BlockSpec((tm, LANE), lambda i: (tab_blk(i), 0))
    sds = lambda n, dt: jax.ShapeDtypeStruct((t, n), dt)
    nb = N_HEADS_B * HEAD_PAD_B
    return pl.pallas_call(
        functools.partial(_in_kernel, ctx_tiles=t_ctx // tm),
        grid=(t // tm,),
        in_specs=[*_stream_specs(x_ctx, x_lat, t_ctx, tm), _ada_spec(ada, layer, seg), full(ln1), full(w_main),
                  full(g_q), full(w_uq), full(g_kv), full(w_uk), full(w_uv), tab, tab, tab, tab],
        out_specs=[row(512), row(128), row(128), row(nb), row(KV_RANK), row(nb), row(512), row(128),
                   row(LRU_W), row(LRU_W)],
        out_shape=[sds(512, BF16), sds(128, F32), sds(128, F32), sds(nb, BF16), sds(KV_RANK, F32),
                   sds(nb, BF16), sds(512, BF16), sds(128, F32), sds(LRU_W, F32), sds(LRU_W, F32)],
        compiler_params=_cparams(("arbitrary",)),
        name="in_proj",
    )(x_ctx, x_lat, ada, ln1, w_main, g_q, w_uq, g_kv, w_uk, w_uv, *tabs)


def _gqa_heads(q, k_all, v_all, bias, sink_ref, layer, o_ref):
    lane = lax.broadcasted_iota(jnp.int32, (q.shape[0], LANE), 1)
    low = lane < HD_A
    for pair in range(N_HEADS_A // 2):
        qp = q[:, pair * LANE:(pair + 1) * LANE]
        outs = []
        for g in range(N_KV_A):
            qm = jnp.where(low if g == 0 else ~low, qp, jnp.zeros_like(qp))
            s = _dot_t(qm, k_all)
            if bias is not None:
                s = s + bias
            sink = sink_ref[layer, 2 * pair + g]
            m = jnp.maximum(jnp.max(s, axis=-1, keepdims=True), sink)
            e = jnp.exp(s - m)
            l = jnp.sum(e, axis=-1, keepdims=True) + jnp.exp(sink - m)
            outs.append(_dot(e.astype(BF16), v_all) / l)
        o_ref[:, pair * LANE:(pair + 1) * LANE] = jnp.where(low, outs[0], outs[1]).astype(o_ref.dtype)


def _gqa_ctx_kernel(sink_ref, q_ref, k_ref, v_ref, o_ref, *, layer):
    _gqa_heads(q_ref[...], k_ref[...].astype(BF16), v_ref[...].astype(BF16), None, sink_ref, layer, o_ref)


def _gqa_ctx(qa, ka, va, sink_p, layer, n_seq, s_len):
    return pl.pallas_call(
        functools.partial(_gqa_ctx_kernel, layer=layer),
        grid=(n_seq,),
        in_specs=[pl.BlockSpec(memory_space=pltpu.SMEM),
                  pl.BlockSpec((s_len, 512), lambda b: (b, 0)),
                  pl.BlockSpec((s_len, LANE), lambda b: (b, 0)),
                  pl.BlockSpec((s_len, LANE), lambda b: (b, 0))],
        out_specs=pl.BlockSpec((s_len, 512), lambda b: (b, 0)),
        out_shape=jax.ShapeDtypeStruct((n_seq * s_len, 512), BF16),
        compiler_params=_cparams(("arbitrary",)),
        name="gqa_ctx",
    )(sink_p, qa, ka, va)


def _gqa_lat_kernel(sink_ref, q_ref, kp_ref, kc_ref, kn_ref, vp_ref, vc_ref, vn_ref, kx_ref, vx_ref, o_ref, *,
                    layer):
    j = pl.program_id(1)
    last = pl.num_programs(1) - 1
    w = WINDOW
    qb = q_ref.shape[0]
    k_all = jnp.concatenate([kp_ref[...], kc_ref[...], kn_ref[...], kx_ref[...]], axis=0).astype(BF16)
    v_all = jnp.concatenate([vp_ref[...], vc_ref[...], vn_ref[...], vx_ref[...]], axis=0).astype(BF16)
    n_loc = qb + 2 * w
    qi = lax.broadcasted_iota(jnp.int32, (qb, n_loc), 0)
    col = lax.broadcasted_iota(jnp.int32, (qb, n_loc), 1)
    rel = col - w - qi
    zero = jnp.zeros((qb, n_loc), F32)
    neg = jnp.full((qb, n_loc), NEG_INF, F32)
    band = jnp.where(jnp.abs(rel) <= w, zero, neg)
    before = jnp.where(col < w, jnp.where(j > 0, zero, neg), zero)
    after = jnp.where(col >= qb + w, jnp.where(j < last, zero, neg), zero)
    bias = jnp.concatenate([band + before + after, jnp.zeros((qb, kx_ref.shape[0]), F32)], axis=1)
    _gqa_heads(q_ref[...], k_all, v_all, bias, sink_ref, layer, o_ref)


def _gqa_lat(qa, ka, va, kx, vx, sink_p, layer, n_seq, n_len, row0):
    w = WINDOW
    qb = GQA_Q_BLOCK
    r = qb // w
    nqb = n_len // qb
    nwb = n_len // w
    cur = lambda b, j: (row0 // qb + b * nqb + j, 0)
    prev = lambda b, j: (row0 // w + b * nwb + jnp.maximum(r * j - 1, 0), 0)
    nxt = lambda b, j: (row0 // w + b * nwb + jnp.minimum(r * j + r, nwb - 1), 0)
    past = kx.shape[2]
    cache = pl.BlockSpec((None, None, past, LANE), lambda b, j: (b, layer, 0, 0))
    return pl.pallas_call(
        functools.partial(_gqa_lat_kernel, layer=layer),
        grid=(n_seq, nqb),
        in_specs=[pl.BlockSpec(memory_space=pltpu.SMEM),
                  pl.BlockSpec((qb, 512), cur),
                  pl.BlockSpec((w, LANE), prev), pl.BlockSpec((qb, LANE), cur), pl.BlockSpec((w, LANE), nxt),
                  pl.BlockSpec((w, LANE), prev), pl.BlockSpec((qb, LANE), cur), pl.BlockSpec((w, LANE), nxt),
                  cache, cache],
        out_specs=pl.BlockSpec((qb, 512), lambda b, j: (b * nqb + j, 0)),
        out_shape=jax.ShapeDtypeStruct((n_seq * n_len, 512), BF16),
        compiler_params=_cparams(("arbitrary", "arbitrary")),
        name="gqa_lat",
    )(sink_p, qa, ka, ka, ka, va, va, va, kx, vx)


def _mla_kernel(*refs, n_src, chunk):
    q_ref = refs[0]
    kv_refs = refs[1:1 + 2 * n_src]
    o_ref = refs[-1]
    qb = q_ref.shape[0]
    lane = lax.broadcasted_iota(jnp.int32, (qb, LANE), 1)
    for pair in range(q_ref.shape[1] // (2 * HEAD_PAD_B)):
        outs = []
        for hh in range(2):
            hs = slice((2 * pair + hh) * HEAD_PAD_B, (2 * pair + hh + 1) * HEAD_PAD_B)
            q = q_ref[:, hs]
            m = jnp.full((qb, 1), NEG_INF, F32)
            l = jnp.zeros((qb, 1), F32)
            acc = jnp.zeros((qb, LANE), F32)
            for s_i in range(n_src):
                k_ref, v_ref = kv_refs[2 * s_i], kv_refs[2 * s_i + 1]
                nk = k_ref.shape[0]
                ck = min(chunk, nk)
                for c in range(nk // ck):
                    k = k_ref[c * ck:(c + 1) * ck, hs]
                    v = v_ref[c * ck:(c + 1) * ck, pair * LANE:(pair + 1) * LANE]
                    s = _dot_t(q, k)
                    m_new = jnp.maximum(m, jnp.max(s, axis=-1, keepdims=True))
                    alpha = jnp.exp(m - m_new)
                    e = jnp.exp(s - m_new)
                    l = alpha * l + jnp.sum(e, axis=-1, keepdims=True)
                    acc = alpha * acc + _dot(e.astype(BF16), v)
                    m = m_new
            outs.append(acc / l)
        o_ref[:, pair * LANE:(pair + 1) * LANE] = jnp.where(lane < V_B, outs[0], outs[1]).astype(o_ref.dtype)


def _mla_ctx(qb, kb, vb, n_seq, s_len):
    nq = N_HEADS_B * HEAD_PAD_B
    nv = N_HEADS_B * V_B
    return pl.pallas_call(
        functools.partial(_mla_kernel, n_src=1, chunk=s_len),
        grid=(n_seq,),
        in_specs=[pl.BlockSpec((s_len, nq), lambda b: (b, 0)),
                  pl.BlockSpec((s_len, nq), lambda b: (b, 0)),
                  pl.BlockSpec((s_len, nv), lambda b: (b, 0))],
        out_specs=pl.BlockSpec((s_len, nv), lambda b: (b, 0)),
        out_shape=jax.ShapeDtypeStruct((n_seq * s_len, nv), BF16),
        compiler_params=_cparams(("arbitrary",)),
        name="mla_ctx",
    )(qb, kb, vb)


def _mla_lat(qb, kb, vb, kbx, vbx, n_seq, n_len, row0):
    npair = N_HEADS_B // 2
    qblk = min(MLA_Q_BLOCK, n_len)
    nqb = n_len // qblk
    past = kbx.shape[0] // n_seq
    qmap = lambda b, p, j: (row0 // qblk + b * nqb + j, p)
    return pl.pallas_call(
        functools.partial(_mla_kernel, n_src=2, chunk=1024),
        grid=(n_seq, npair, nqb),
        in_specs=[pl.BlockSpec((qblk, 2 * HEAD_PAD_B), qmap),
                  pl.BlockSpec((n_len, 2 * HEAD_PAD_B), lambda b, p, j: (row0 // n_len + b, p)),
                  pl.BlockSpec((n_len, LANE), lambda b, p, j: (row0 // n_len + b, p)),
                  pl.BlockSpec((past, 2 * HEAD_PAD_B), lambda b, p, j: (b, p)),
                  pl.BlockSpec((past, LANE), lambda b, p, j: (b, p))],
        out_specs=pl.BlockSpec((qblk, LANE), lambda b, p, j: (b * nqb + j, p)),
        out_shape=jax.ShapeDtypeStruct((n_seq * n_len, 512), BF16),
        compiler_params=_cparams(("arbitrary", "arbitrary", "arbitrary")),
        name="mla_lat",
    )(qb, kb, vb, kbx, vbx)


def _cache_kv_kernel(ckv_ref, kr_ref, wuk_ref, wuv_ref, kb_ref, vb_ref):
    c = ckv_ref[...].astype(BF16)
    kb = _dot(c, wuk_ref[...]) + jnp.concatenate([kr_ref[...]] * N_HEADS_B, axis=-1)
    kb_ref[...] = kb.astype(BF16)
    vb_ref[...] = _dot(c, wuv_ref[...]).astype(BF16)


def _cache_kv(cache_ckv, cache_kr_pad, w_uk, w_uv, layer):
    n_seq, _, past, _ = cache_ckv.shape
    nb = N_HEADS_B * HEAD_PAD_B
    return pl.pallas_call(
        _cache_kv_kernel,
        grid=(n_seq,),
        in_specs=[pl.BlockSpec((None, None, past, KV_RANK), lambda b: (b, layer, 0, 0)),
                  pl.BlockSpec((None, None, past, LANE), lambda b: (b, layer, 0, 0)),
                  _layer_spec(w_uk, layer), _layer_spec(w_uv, layer)],
        out_specs=[pl.BlockSpec((past, nb), lambda b: (b, 0)), pl.BlockSpec((past, 512), lambda b: (b, 0))],
        out_shape=[jax.ShapeDtypeStruct((n_seq * past, nb), BF16), jax.ShapeDtypeStruct((n_seq * past, 512), BF16)],
        compiler_params=_cparams(("arbitrary",)),
        name="cache_kv",
    )(cache_ckv, cache_kr_pad, w_uk, w_uv)


def _lru_kernel(xc_ref, yc_ref, h0_ref, cw_ref, cb_ref, wa_ref, ba_ref, wx_ref, bx_ref, sp_ref,
                o_ref, st_ref, pad_ref, xcv_ref, a_ref, b_ref, *, chunk):
    n = xc_ref.shape[0]
    halo = SUBLANE
    pad_ref[0:halo, :] = jnp.zeros((halo, LRU_W), F32)
    pad_ref[halo + n:2 * halo + n, :] = jnp.zeros((halo, LRU_W), F32)
    pad_ref[halo:halo + n, :] = xc_ref[...]
    left = CONV_W // 2
    for c in range(n // chunk):
        r0 = c * chunk
        acc = jnp.broadcast_to(cb_ref[...], (chunk, LRU_W))
        for j in range(CONV_W):
            off = halo + r0 + j - left
            acc = acc + cw_ref[j:j + 1, :] * pad_ref[off:off + chunk, :]
        xcv_ref[r0:r0 + chunk, :] = acc

    row = lax.broadcasted_iota(jnp.int32, (SUBLANE, LRU_W), 0)
    for d in range(2):
        for c in range(n // chunk):
            r0 = c * chunk
            xv = xcv_ref[r0:r0 + chunk, :]
            xb = xv.astype(BF16)
            for hf in range(2):
                cs = slice(hf * LRU_HALF, (hf + 1) * LRU_HALF)
                r = _sigmoid(_dot(xb[:, cs], wa_ref[d, hf]) + ba_ref[d:d + 1, cs])
                i = _sigmoid(_dot(xb[:, cs], wx_ref[d, hf]) + bx_ref[d:d + 1, cs])
                log_a = (-LRU_C) * r * sp_ref[d:d + 1, cs]
                a = jnp.exp(log_a)
                a_ref[d, r0:r0 + chunk, cs] = a
                b_ref[d, r0:r0 + chunk, cs] = jnp.sqrt(-jnp.tanh(log_a) * (a * a + 1.0)) * (i * xv[:, cs])

    def scan_group(d, grp, h):
        rows = pl.ds(pl.multiple_of(grp * SUBLANE, SUBLANE), SUBLANE)
        a = a_ref[d, rows, :]
        b = b_ref[d, rows, :]
        for sh in (1, 2, 4):
            if d == 0:
                keep = row >= sh
                a_s = jnp.where(keep, pltpu.roll(a, sh, 0), 1.0)
                b_s = jnp.where(keep, pltpu.roll(b, sh, 0), 0.0)
            else:
                keep = row < SUBLANE - sh
                a_s = jnp.where(keep, pltpu.roll(a, SUBLANE - sh, 0), 1.0)
                b_s = jnp.where(keep, pltpu.roll(b, SUBLANE - sh, 0), 0.0)
            b = a * b_s + b
            a = a * a_s
        hrows = a * h + b
        b_ref[d, rows, :] = hrows
        return hrows[SUBLANE - 1:SUBLANE, :] if d == 0 else hrows[0:1, :]

    n_grp = n // SUBLANE

    def body(g, hs):
        return scan_group(0, g, hs[0]), scan_group(1, n_grp - 1 - g, hs[1])

    h_f, h_b = lax.fori_loop(0, n_grp, body, (h0_ref[0:1, :], h0_ref[1:2, :]))
    st_ref[0:1, :] = h_f
    st_ref[1:2, :] = h_b

    for c in range(n // chunk):
        rs = slice(c * chunk, (c + 1) * chunk)
        o_ref[rs, :] = ((b_ref[0, rs, :] + b_ref[1, rs, :]) * jax.nn.gelu(yc_ref[rs, :])).astype(o_ref.dtype)


def _lru(xc, yc, h0, h0_layer, conv_w, conv_b, wa, ba, wx, bx, sp, layer, n_seq, n_len, row0):
    blk0 = row0 // n_len
    full = lambda a: _layer_spec(a, layer)
    seq = pl.BlockSpec((n_len, LRU_W), lambda b: (blk0 + b, 0))
    return pl.pallas_call(
        functools.partial(_lru_kernel, chunk=min(n_len, 256)),
        grid=(n_seq,),
        in_specs=[seq, seq, pl.BlockSpec((None, None, 2, LRU_W), lambda b: (b, h0_layer, 0, 0)),
                  full(conv_w), full(conv_b), full(wa), full(ba), full(wx), full(bx), full(sp)],
        out_specs=[pl.BlockSpec((n_len, LRU_W), lambda b: (b, 0)),
                   pl.BlockSpec((None, 2, LRU_W), lambda b: (b, 0, 0))],
        out_shape=[jax.ShapeDtypeStruct((n_seq * n_len, LRU_W), BF16),
                   jax.ShapeDtypeStruct((n_seq, 2, LRU_W), F32)],
        scratch_shapes=[pltpu.VMEM((n_len + 2 * SUBLANE, LRU_W), F32), pltpu.VMEM((n_len, LRU_W), F32),
                        pltpu.VMEM((2, n_len, LRU_W), F32), pltpu.VMEM((2, n_len, LRU_W), F32)],
        compiler_params=_cparams(("arbitrary",)),
        name="lru",
    )(xc, yc, h0, conv_w, conv_b, wa, ba, wx, bx, sp)


def _merge_kernel(xc_in_ref, xl_in_ref, ada_ref, oac_ref, oal_ref, obc_ref, obl_ref, occ_ref, ocl_ref, ln1_ref,
                  wg_ref, wpa_ref, wpb_ref, wpc_ref, wout_ref, ln2_ref, wr_ref, br_ref,
                  x1_ref, h_ref, e4_ref, g4_ref, r4_ref, cnt_ref, carry_ref, *, ctx_tiles):
    d = D_MODEL
    tm = xc_in_ref.shape[0]
    i = pl.program_id(0)
    is_ctx = i < ctx_tiles
    x = jnp.where(is_ctx, xc_in_ref[...], xl_in_ref[...])
    oa = jnp.where(is_ctx, oac_ref[...], oal_ref[...])
    ob = jnp.where(is_ctx, obc_ref[...], obl_ref[...])
    oc = jnp.where(is_ctx, occ_ref[...], ocl_ref[...])

    @pl.when(i == 0)
    def _():
        carry_ref[...] = jnp.zeros_like(carry_ref)

    ada = ada_ref[...]
    u = _rms(x, ln1_ref[...]) * (1.0 + ada[:, d:2 * d]) + ada[:, 0:d]
    g = jax.nn.sigmoid(_dot(u.astype(BF16), wg_ref[...]))
    m = (g[:, 0:d] * _dot(oa, wpa_ref[...]) + g[:, d:2 * d] * _dot(ob, wpb_ref[...])
         + g[:, 2 * d:3 * d] * _dot(oc, wpc_ref[...]))
    x1 = x + ada[:, 2 * d:3 * d] * _dot(m.astype(BF16), wout_ref[...])
    x1_ref[...] = x1
    h = _rms(x1, ln2_ref[...]) * (1.0 + ada[:, 4 * d:5 * d]) + ada[:, 3 * d:4 * d]
    h_ref[...] = _pack_bf16_pairs(h)

    h_hi = h.astype(BF16)
    h_lo = (h - h_hi.astype(F32)).astype(BF16)
    hw = _dot(h_hi, wr_ref[...])
    logits = (hw[:, :N_EXPERTS] + hw[:, N_EXPERTS:] + _dot(h_lo, wr_ref[:, :N_EXPERTS])) + br_ref[...]
    col = lax.broadcasted_iota(jnp.int32, (tm, N_EXPERTS), 1).astype(F32)
    col4 = lax.broadcasted_iota(jnp.int32, (tm, TOP_K), 1)
    sel_any = jnp.zeros((tm, N_EXPERTS), F32)
    vals, idxs = [], []
    work = logits
    for _k in range(TOP_K):
        mx = jnp.max(work, axis=-1, keepdims=True)
        idx = jnp.min(jnp.where(work == mx, col, float(N_EXPERTS)), axis=-1, keepdims=True)
        sel = col == idx
        vals.append(mx)
        idxs.append(idx)
        sel_any = jnp.where(sel, 1.0, sel_any)
        work = jnp.where(sel, -jnp.inf, work)

    ri = lax.broadcasted_iota(jnp.int32, (tm, tm), 0)
    ci = lax.broadcasted_iota(jnp.int32, (tm, tm), 1)
    tri = jnp.where(ri > ci, 1.0, 0.0).astype(BF16)
    before = _dot(tri, sel_any.astype(BF16)) + carry_ref[...]
    carry = carry_ref[...] + jnp.sum(sel_any, axis=0, keepdims=True)
    carry_ref[...] = carry
    cnt_ref[...] = jnp.broadcast_to(carry, cnt_ref.shape)

    exps = [jnp.exp(v - vals[0]) for v in vals]
    den = exps[0] + exps[1] + exps[2] + exps[3]
    e4 = jnp.zeros((tm, TOP_K), jnp.int32)
    g4 = jnp.zeros((tm, TOP_K), F32)
    r4 = jnp.zeros((tm, TOP_K), jnp.int32)
    for k in range(TOP_K):
        rank = jnp.sum(jnp.where(col == idxs[k], before, 0.0), axis=-1, keepdims=True)
        e4 = jnp.where(col4 == k, idxs[k].astype(jnp.int32), e4)
        g4 = jnp.where(col4 == k, exps[k] / den, g4)
        r4 = jnp.where(col4 == k, rank.astype(jnp.int32), r4)
    e4_ref[...] = e4
    g4_ref[...] = g4
    r4_ref[...] = r4


def _merge(x_ctx, x_lat, t, ada, o_ctx, o_lat, ln1, w_g, w_pa, w_pb, w_pc, w_out, ln2, w_r, b_r, seg, tm, layer):
    d = x_ctx.shape[1]
    t_ctx = o_ctx[0].shape[0]
    ctx_tiles = t_ctx // tm
    row = lambda n: pl.BlockSpec((tm, n), lambda i: (i, 0))
    full = lambda a: _layer_spec(a, layer)
    ctx = pl.BlockSpec((tm, 512), lambda i: (jnp.minimum(i, ctx_tiles - 1), 0))
    lat = pl.BlockSpec((tm, 512), lambda i: (jnp.maximum(i - ctx_tiles, 0), 0))
    return pl.pallas_call(
        functools.partial(_merge_kernel, ctx_tiles=ctx_tiles),
        grid=(t // tm,),
        in_specs=[*_stream_specs(x_ctx, x_lat, t_ctx, tm), _ada_spec(ada, layer, seg), ctx, lat, ctx, lat, ctx, lat,
                  full(ln1), full(w_g), full(w_pa), full(w_pb), full(w_pc), full(w_out), full(ln2), full(w_r),
                  full(b_r)],
        out_specs=[row(d), row(d // 2), row(TOP_K), row(TOP_K), row(TOP_K),
                   pl.BlockSpec((SUBLANE, N_EXPERTS), lambda i: (0, 0))],
        out_shape=[jax.ShapeDtypeStruct((t, d), F32), jax.ShapeDtypeStruct((t, d // 2), jnp.int32),
                   jax.ShapeDtypeStruct((t, TOP_K), jnp.int32), jax.ShapeDtypeStruct((t, TOP_K), F32),
                   jax.ShapeDtypeStruct((t, TOP_K), jnp.int32),
                   jax.ShapeDtypeStruct((SUBLANE, N_EXPERTS), F32)],
        scratch_shapes=[pltpu.VMEM((1, N_EXPERTS), F32)],
        compiler_params=_cparams(("arbitrary",)),
        name="merge",
    )(x_ctx, x_lat, ada, o_ctx[0], o_lat[0], o_ctx[1], o_lat[1], o_ctx[2], o_lat[2], ln1, w_g, w_pa, w_pb, w_pc,
      w_out, ln2, w_r, b_r)


def _expert_kernel(be_ref, nu_ref, x_ref, wgu_ref, bgu_ref, wdn_ref, bdn_ref, y_ref, wgu_s, wdn_s, *, n_chunk):
    b = pl.program_id(0)
    used = b < nu_ref[0]
    new_expert = jnp.logical_or(b == 0, be_ref[b] != be_ref[jnp.maximum(b - 1, 0)])

    @pl.when(jnp.logical_and(used, new_expert))
    def _():
        wgu_s[...] = wgu_ref[...].astype(BF16)
        wdn_s[...] = wdn_ref[...].astype(BF16)

    @pl.when(used)
    def _():
        x = _unpack_bf16_pairs(x_ref[...]).astype(BF16)
        cw = D_FF // n_chunk
        acc = jnp.broadcast_to(bdn_ref[...], x.shape)
        for c in range(n_chunk):
            glu = _dot(x, wgu_s[:, c * cw:(c + 1) * cw]) + bgu_ref[:, c * cw:(c + 1) * cw]
            lin = (_dot(x, wgu_s[:, D_FF + c * cw:D_FF + (c + 1) * cw])
                   + bgu_ref[:, D_FF + c * cw:D_FF + (c + 1) * cw])
            glu = jnp.minimum(glu, SWIGLU_LIMIT)
            lin = jnp.clip(lin, -SWIGLU_LIMIT, SWIGLU_LIMIT)
            act = glu * jax.nn.sigmoid(SWIGLU_ALPHA * glu) * (lin + 1.0)
            acc = acc + _dot(act.astype(BF16), wdn_s[c * cw:(c + 1) * cw, :])
        y_ref[...] = _pack_bf16_pairs(acc)

    @pl.when(jnp.logical_not(used))
    def _():
        y_ref[...] = jnp.zeros_like(y_ref)


def _experts(xb, block_e, n_used, w_gu, b_gu, w_dn, b_dn, layer):
    n_rows, dp = xb.shape
    d = 2 * dp
    depth = w_gu.shape[0]
    blk = EXPERT_BLOCK
    n_blocks = n_rows // blk
    wmap = lambda b, be, nu: (layer, be[b], 0, 0)
    grid_spec = pltpu.PrefetchScalarGridSpec(
        num_scalar_prefetch=2,
        grid=(n_blocks,),
        in_specs=[pl.BlockSpec((blk, dp), lambda b, be, nu: (b, 0)),
                  pl.BlockSpec((None, None, d, 2 * D_FF), wmap),
                  pl.BlockSpec((None, None, 1, 2 * D_FF), wmap),
                  pl.BlockSpec((None, None, D_FF, d), wmap),
                  pl.BlockSpec((None, None, 1, d), wmap)],
        out_specs=pl.BlockSpec((blk, dp), lambda b, be, nu: (b, 0)),
        scratch_shapes=[pltpu.VMEM((d, 2 * D_FF), BF16), pltpu.VMEM((D_FF, d), BF16)],
    )
    return pl.pallas_call(
        functools.partial(_expert_kernel, n_chunk=4),
        grid_spec=grid_spec,
        out_shape=jax.ShapeDtypeStruct((n_rows, dp), jnp.int32),
        compiler_params=_cparams(("arbitrary",)),
        name="experts",
    )(block_e, n_used, xb, w_gu, b_gu.reshape(depth, N_EXPERTS, 1, 2 * D_FF), w_dn,
      b_dn.reshape(depth, N_EXPERTS, 1, d))


def _sc_worker_base(per_w):
    return (lax.axis_index("s") * SC_CORES + lax.axis_index("c")) * per_w


def _sc_gather_rows(table_hbm, out_hbm, idx_v, rows_v, gsem, osem, base, n_chunks):
    ch = SC_CHUNK

    def gather(c, slot):
        return pltpu.make_async_copy(table_hbm.at[idx_v.at[pl.ds(c * ch, ch)]], rows_v.at[slot], gsem.at[slot])

    def put(c, slot):
        return pltpu.make_async_copy(rows_v.at[slot], out_hbm.at[pl.ds(base + c * ch, ch)], osem.at[slot])

    gather(0, 0).start()

    @pl.loop(0, n_chunks // 2)
    def _(i):
        c = 2 * i
        gather(c, 0).wait()
        put(c, 0).start()

        @pl.when(i > 0)
        def _():
            put(c - 1, 1).wait()

        gather(c + 1, 1).start()
        gather(c + 1, 1).wait()
        put(c + 1, 1).start()
        put(c, 0).wait()

        @pl.when(c + 2 < n_chunks)
        def _():
            gather(c + 2, 0).start()

    put(n_chunks - 1, 1).wait()


def _sc_scratch(per_w, d, dtype):
    return [pltpu.VMEM((per_w,), jnp.int32), pltpu.VMEM((2, SC_CHUNK, d), dtype),
            pltpu.SemaphoreType.DMA((2,)), pltpu.SemaphoreType.DMA((2,))]


def _sc_gather(table, idx):
    n_idx = idx.shape[0]
    d = table.shape[1]
    n_workers = SC_CORES * SC_SUBCORES
    per_w = n_idx // n_workers
    assert n_idx % (n_workers * SC_CHUNK * 2) == 0
    mesh = plsc.VectorSubcoreMesh(core_axis_name="c", subcore_axis_name="s")

    @functools.partial(
        pl.kernel, mesh=mesh, out_type=jax.ShapeDtypeStruct((n_idx, d), table.dtype),
        scratch_types=_sc_scratch(per_w, d, table.dtype), name="sc_gather")
    def gather(table_hbm, idx_hbm, out_hbm, idx_v, rows_v, gsem, osem):
        base = _sc_worker_base(per_w)
        pltpu.sync_copy(idx_hbm.at[pl.ds(base, per_w)], idx_v)
        _sc_gather_rows(table_hbm, out_hbm, idx_v, rows_v, gsem, osem, base, per_w // SC_CHUNK)

    return gather(table, idx)


def _sc_dispatch(h, dest, n_rows):
    t, d = h.shape
    n_slots = dest.shape[0]
    n_workers = SC_CORES * SC_SUBCORES
    per_w = n_rows // n_workers
    piece = 16384
    unroll = 8
    fill_shift = 3
    assert n_rows % (n_workers * SC_CHUNK * 2) == 0 and n_slots % piece == 0 and (n_rows >> fill_shift) <= t
    mesh = plsc.VectorSubcoreMesh(core_axis_name="c", subcore_axis_name="s")

    @functools.partial(
        pl.kernel, mesh=mesh, out_type=jax.ShapeDtypeStruct((n_rows, d), h.dtype),
        scratch_types=[pltpu.VMEM((piece,), jnp.int32)] + _sc_scratch(per_w, d, h.dtype),
        compiler_params=pltpu.CompilerParams(needs_layout_passes=False), name="sc_dispatch")
    def dispatch(h_hbm, dest_hbm, out_hbm, dest_v, idx_v, rows_v, gsem, osem):
        base = _sc_worker_base(per_w)
        lane = lax.iota(jnp.int32, SC_LANES)

        @pl.loop(0, per_w // SC_LANES)
        def _(j):
            idx_v[pl.ds(j * SC_LANES, SC_LANES)] = (base + j * SC_LANES + lane) >> fill_shift

        @pl.loop(0, n_slots // piece)
        def _(p):
            pltpu.sync_copy(dest_hbm.at[pl.ds(p * piece, piece)], dest_v)

            @plsc.parallel_loop(0, piece // SC_LANES, unroll=unroll)
            def _(j):
                s0 = j * SC_LANES
                loc = dest_v[pl.ds(s0, SC_LANES)] - base
                mine = (loc >= 0) & (loc < per_w)
                tok = (p * piece + s0 + lane) // TOP_K
                plsc.store_scatter(idx_v, [loc], tok, mask=mine)

        _sc_gather_rows(h_hbm, out_hbm, idx_v, rows_v, gsem, osem, base, per_w // SC_CHUNK)

    return dispatch(h, dest)


def _combine_kernel(x1_ref, ada_ref, yg_ref, g4_ref, fg_ref, o_ref, *, final):
    d = D_MODEL
    g4 = g4_ref[...]
    y = g4[:, 0:1] * _unpack_bf16_pairs(yg_ref[0])
    for k in range(1, TOP_K):
        y = y + g4[:, k:k + 1] * _unpack_bf16_pairs(yg_ref[k])
    x2 = x1_ref[...] + ada_ref[:, 5 * d:6 * d] * y
    if final:
        x2 = _rms(x2, fg_ref[...])
    o_ref[...] = x2


def _combine(x1, ada, yg, g4, final_g, seg, tm, layer, final):
    t, d = x1.shape
    return pl.pallas_call(
        functools.partial(_combine_kernel, final=final),
        grid=(t // tm,),
        in_specs=[pl.BlockSpec((tm, d), lambda i: (i, 0)),
                  _ada_spec(ada, layer, seg),
                  pl.BlockSpec((TOP_K, tm, d // 2), lambda i: (0, i, 0)),
                  pl.BlockSpec((tm, TOP_K), lambda i: (i, 0)),
                  pl.BlockSpec((1, d), lambda i: (0, 0))],
        out_specs=pl.BlockSpec((tm, d), lambda i: (i, 0)),
        out_shape=jax.ShapeDtypeStruct((t, d), F32),
        compiler_params=_cparams(("arbitrary",)),
        name="combine",
    )(x1, ada, yg, g4, final_g)


def _rope_tables(n_lat):
    rows = n_lat // GRID_W
    row_ids = jnp.repeat(jnp.arange(rows, dtype=F32), GRID_W)
    col_ids = jnp.tile(jnp.arange(GRID_W, dtype=F32), rows)

    def table(d_rot, lane0):
        d_axis = d_rot // 2
        nf = d_axis // 2
        inv_freq = ROPE_BASE ** (-jnp.arange(0, d_axis, 2, dtype=F32) / d_axis)
        ang_r = row_ids[:, None] * inv_freq
        ang_c = col_ids[:, None] * inv_freq
        ang = jnp.concatenate([ang_r, ang_r, ang_c, ang_c], axis=-1)
        sign = jnp.tile(jnp.concatenate([-jnp.ones((nf,), F32), jnp.ones((nf,), F32)]), 2)
        cos = jnp.ones((n_lat, LANE), F32)
        sin = jnp.zeros((n_lat, LANE), F32)
        for l0 in lane0:
            cos = cos.at[:, l0:l0 + d_rot].set(jnp.cos(ang))
            sin = sin.at[:, l0:l0 + d_rot].set(jnp.sin(ang) * sign)
        ident = (jnp.ones((TOKEN_TILE, LANE), F32), jnp.zeros((TOKEN_TILE, LANE), F32))
        return jnp.concatenate([ident[0], cos], axis=0), jnp.concatenate([ident[1], sin], axis=0)

    ca, sa = table(HD_A, (0, HD_A))
    cb, sb = table(ROPE_B, (NOPE_B,))
    return ca, sa, cb, sb


def _prep_weights(w_in, sink, w_uq, w_ukv, lru_wa, lru_wx, lru_lam, w_pa):
    depth, d, _ = w_in.shape
    cuts = np.cumsum((512, 128, 128, Q_RANK, KV_RANK, ROPE_B, LRU_W, LRU_W, 3 * d))
    perm = np.array(HEAD_PERM_A)
    w_qa = w_in[:, :, :cuts[0]].reshape(depth, d, N_HEADS_A, HD_A)[:, :, perm].reshape(depth, d, 512)
    w_kr = jnp.pad(w_in[:, :, cuts[4]:cuts[5]], ((0, 0), (0, 0), (NOPE_B, LANE - NOPE_B - ROPE_B)))
    w_main = jnp.concatenate([w_qa, w_in[:, :, cuts[0]:cuts[4]], w_in[:, :, cuts[5]:cuts[7]], w_kr],
                             axis=-1).astype(BF16)
    w_g = w_in[:, :, cuts[7]:].astype(BF16)
    sink_p = sink[:, perm]
    w_pa_p = w_pa.reshape(depth, N_HEADS_A, HD_A, d)[:, perm].reshape(depth, 512, d).astype(BF16)
    hb = NOPE_B + ROPE_B
    w_uq_p = jnp.pad(w_uq.reshape(depth, Q_RANK, N_HEADS_B, hb),
                     ((0, 0), (0, 0), (0, 0), (0, HEAD_PAD_B - hb))).reshape(depth, Q_RANK, -1).astype(BF16)
    w_ukv4 = w_ukv.reshape(depth, KV_RANK, N_HEADS_B, NOPE_B + V_B)
    w_uk_p = jnp.pad(w_ukv4[..., :NOPE_B], ((0, 0), (0, 0), (0, 0), (0, HEAD_PAD_B - NOPE_B))
                     ).reshape(depth, KV_RANK, -1).astype(BF16)
    w_uv = w_ukv4[..., NOPE_B:].reshape(depth, KV_RANK, -1).astype(BF16)

    def block_diag(w):
        hpb = LRU_HEADS // 2
        blk = LRU_W // LRU_HEADS
        w = w.reshape(depth, 2, 2, hpb, blk, blk)
        eye = jnp.eye(hpb, dtype=w.dtype)
        out = jnp.einsum('ldghij,hk->ldghikj', w, eye)
        return out.reshape(depth, 2, 2, hpb * blk, hpb * blk).astype(BF16)

    sp = jax.nn.softplus(-lru_lam.astype(F32))
    return w_main, w_g, sink_p, w_pa_p, w_uq_p, w_uk_p, w_uv, block_diag(lru_wa), block_diag(lru_wx), sp


def kernel(x_prompt, x_sample, cache_k_a, cache_v_a, cache_ckv, cache_krope, state_lru, c, c_ctx, ln1_g, ln2_g,
           final_g, w_ada, b_ada, w_in, sink, g_q, w_uq, g_kv, w_ukv, conv_w, conv_b, lru_wa, lru_ba, lru_wx,
           lru_bx, lru_lam, w_pa, w_pb, w_pc, w_out, w_router, b_router, w_gu, b_gu, w_dn, b_dn):
    n_ctx, s_len, d = x_prompt.shape
    n_lat, n_len, _ = x_sample.shape
    depth = w_in.shape[0]
    past = cache_k_a.shape[2]
    t_ctx = n_ctx * s_len
    t_lat = n_lat * n_len
    t = t_ctx + t_lat
    tm = TOKEN_TILE
    ctx_tiles = t_ctx // tm
    lat_tiles = n_len // tm
    merge_tile = min(MERGE_TILE, n_len)
    assert t_ctx % n_len == 0 and t_ctx % tm == 0 and n_len % tm == 0 and n_lat + 1 <= SUBLANE
    assert t_ctx % merge_tile == 0 and n_len % merge_tile == 0

    seg = lambda i: jnp.where(i < ctx_tiles, 0, 1 + (i - ctx_tiles) // lat_tiles)
    seg_m = lambda i: jnp.where(i < t_ctx // merge_tile, 0, 1 + (i - t_ctx // merge_tile) // (n_len // merge_tile))
    combine_tile = min(COMBINE_TILE, n_len)
    assert t_ctx % combine_tile == 0 and n_len % combine_tile == 0
    seg_c = lambda i: jnp.where(i < t_ctx // combine_tile, 0,
                                1 + (i - t_ctx // combine_tile) // (n_len // combine_tile))
    tab_blk = lambda i: jnp.where(i < ctx_tiles, 0, 1 + (i - ctx_tiles) % lat_tiles)

    cond8 = jnp.zeros((SUBLANE, d), F32).at[0].set(c_ctx).at[1:1 + n_lat].set(c)
    ada = _ada_all(cond8, w_ada, b_ada)
    tabs = _rope_tables(n_len)
    (w_main, w_g, sink_p, w_pa_p, w_uq_p, w_uk_p, w_uv, wa_bd, wx_bd, sp) = _prep_weights(
        w_in, sink, w_uq, w_ukv, lru_wa, lru_wx, lru_lam, w_pa)
    w_pb_b, w_pc_b, w_out_b = w_pb.astype(BF16), w_pc.astype(BF16), w_out.astype(BF16)
    w_r_hi = w_router.astype(BF16)
    w_r_cat = jnp.concatenate([w_r_hi, (w_router - w_r_hi.astype(F32)).astype(BF16)], axis=-1)
    cache_k2 = cache_k_a.reshape(n_lat, depth, past, N_KV_A * HD_A)
    cache_v2 = cache_v_a.reshape(n_lat, depth, past, N_KV_A * HD_A)
    cache_kr_pad = jnp.pad(cache_krope, ((0, 0), (0, 0), (0, 0), (NOPE_B, LANE - NOPE_B - ROPE_B)))
    h0_ctx = jnp.zeros((n_ctx, 1, 2, LRU_W), F32)
    rows3 = lambda a: a.reshape(depth, 1, -1)
    ada = ada.reshape(depth, SUBLANE, 1, 6 * d)
    ln1_3, ln2_3, g_q3, g_kv3, conv_b3, b_r3 = (rows3(a) for a in (ln1_g, ln2_g, g_q, g_kv, conv_b, b_router))

    x_ctx, x_lat = x_prompt.reshape(t_ctx, d), x_sample.reshape(t_lat, d)
    ks_a, vs_a, ckvs, krs, lrus = [], [], [], [], []
    n_rows = t * TOP_K + N_EXPERTS * EXPERT_BLOCK
    n_blocks = n_rows // EXPERT_BLOCK
    for l in range(depth):
        qa, ka, va, qb, ckvn, kb, vb, kr, xc, yc = _in_proj(
            x_ctx, x_lat, t_ctx, t, ada, ln1_3, w_main, g_q3, w_uq_p, g_kv3, w_uk_p, w_uv, tabs, seg, tab_blk, l)
        ks_a.append(ka[:t_ctx].reshape(n_ctx, s_len, N_KV_A, HD_A))
        vs_a.append(va[:t_ctx].reshape(n_ctx, s_len, N_KV_A, HD_A))
        ckvs.append(ckvn[:t_ctx].reshape(n_ctx, s_len, KV_RANK))
        krs.append(kr[:t_ctx, NOPE_B:NOPE_B + ROPE_B].reshape(n_ctx, s_len, ROPE_B))

        oa_c = _gqa_ctx(qa, ka, va, sink_p, l, n_ctx, s_len)
        oa_l = _gqa_lat(qa, ka, va, cache_k2, cache_v2, sink_p, l, n_lat, n_len, t_ctx)
        kbx, vbx = _cache_kv(cache_ckv, cache_kr_pad, w_uk_p, w_uv, l)
        ob_c = _mla_ctx(qb, kb, vb, n_ctx, s_len)
        ob_l = _mla_lat(qb, kb, vb, kbx, vbx, n_lat, n_len, t_ctx)
        lru_args = (conv_w, conv_b3, wa_bd, lru_ba, wx_bd, lru_bx, sp, l)
        oc_c, st_c = _lru(xc, yc, h0_ctx, 0, *lru_args, n_ctx, s_len, 0)
        oc_l, _ = _lru(xc, yc, state_lru, l, *lru_args, n_lat, n_len, t_ctx)
        lrus.append(st_c)
        x1, h, e4, g4, r4, cnt = _merge(x_ctx, x_lat, t, ada, (oa_c, ob_c, oc_c), (oa_l, ob_l, oc_l), ln1_3, w_g,
                                        w_pa_p, w_pb_b, w_pc_b, w_out_b, ln2_3, w_r_cat, b_r3, seg_m, merge_tile, l)

        counts = cnt[0].astype(jnp.int32)
        padded = (counts + EXPERT_BLOCK - 1) // EXPERT_BLOCK * EXPERT_BLOCK
        pad_end = jnp.cumsum(padded)
        pad_start = pad_end - padded
        dest4 = pad_start[e4] + r4
        blk_row0 = jnp.arange(n_blocks, dtype=jnp.int32) * EXPERT_BLOCK
        block_e = jnp.minimum(jnp.sum((pad_end[None, :] <= blk_row0[:, None]).astype(jnp.int32), axis=1),
                              N_EXPERTS - 1)
        n_used = (pad_end[-1:] // EXPERT_BLOCK).astype(jnp.int32)
        xb = _sc_dispatch(h, dest4.reshape(-1), n_rows)
        yb = _experts(xb, block_e, n_used, w_gu, b_gu, w_dn, b_dn, l)
        yg = _sc_gather(yb, dest4.T.reshape(-1)).reshape(TOP_K, t, d // 2)
        x = _combine(x1, ada, yg, g4, final_g.reshape(1, d), seg_c, combine_tile, l, l == depth - 1)
        x_ctx = x_lat = x

    y_prompt = x[:t_ctx].reshape(n_ctx, s_len, d)
    y_sample = x[t_ctx:].reshape(n_lat, n_len, d)
    return (y_prompt, y_sample, jnp.stack(ks_a, axis=1), jnp.stack(vs_a, axis=1), jnp.stack(ckvs, axis=1),
            jnp.stack(krs, axis=1), jnp.stack(lrus, axis=1))
```
